```python
import jax, jax.numpy as jnp
from jax import lax
import numpy as np

D_MODEL = 1024
BATCH = 2
SEQ = 8192
DEPTH = 1

D_MIX = D_MODEL
HEAD_DIM = 64
N_Q_HEADS = 8
N_KV_HEADS = 2
Q_PER_KV = N_Q_HEADS // N_KV_HEADS
D_ATT = N_Q_HEADS * HEAD_DIM
D_CONV = D_MIX - D_ATT
KV_W = N_KV_HEADS * HEAD_DIM
N_BRANCH = 3
CONV_TAPS = 3
ROT_DIM = HEAD_DIM // 4
ROPE_THETA = 500000.0
CMP_BLOCK = 32
CMP_STRIDE = 16
CMP_HIDDEN = 256
SLC_BLOCK = 64
N_SELECT = 16
WINDOW = 512
Q_BLOCK = 128
D_FF = 2816
D_PLE = 256
EPS = 1e-6
NEG = -1e30
D_IN = D_ATT + 6 * KV_W + N_BRANCH * N_Q_HEADS + 3 * D_CONV

kernel_name = "hybrid_nsa_shortconv_convglu_ple"


def rms_norm(x, g):
    xf = x.astype(jnp.float32)
    y = xf * lax.rsqrt(jnp.mean(xf * xf, axis=-1, keepdims=True) + EPS)
    return (y * g.astype(jnp.float32)).astype(x.dtype)


def partial_rope(x, pos):
    half = ROT_DIM // 2
    inv_freq = ROPE_THETA ** (-jnp.arange(half, dtype=jnp.float32) * 2.0 / ROT_DIM)
    ang = pos.astype(jnp.float32)[:, None] * inv_freq[None, :]
    ang = ang.reshape(ang.shape[:1] + (1,) * (x.ndim - 3) + (half,))
    cos, sin = jnp.cos(ang), jnp.sin(ang)
    xf = x.astype(jnp.float32)
    x1, x2, rest = xf[..., :half], xf[..., half:ROT_DIM], xf[..., ROT_DIM:]
    out = jnp.concatenate([x1 * cos - x2 * sin, x2 * cos + x1 * sin, rest], axis=-1)
    return out.astype(x.dtype)


def causal_dwconv(u, w):
    c = u.shape[-1]
    return lax.conv_general_dilated(u, w[:, None, :].astype(u.dtype), window_strides=(1,),
                                    padding=[(CONV_TAPS - 1, 0)],
                                    dimension_numbers=('NWC', 'WIO', 'NWC'),
                                    feature_group_count=c)


def masked_softmax(s, valid):
    p = jax.nn.softmax(jnp.where(valid, s, NEG), axis=-1)
    return jnp.where(valid, p, 0.0)


def compress_blocks(tok, pe, w1, w2):
    t = tok.shape[1]
    n_cmp = (t - CMP_BLOCK) // CMP_STRIDE + 1
    idx = CMP_STRIDE * jnp.arange(n_cmp)[:, None] + jnp.arange(CMP_BLOCK)[None, :]
    blk = tok[:, idx] + pe[None, None, :, None, :]
    hid = jax.nn.gelu(jnp.einsum('bnlgd,ldh->bngh', blk, w1))
    return jnp.einsum('bngh,hd->bngd', hid, w2)


def cmp_to_slc_weights(n_cmp, n_slc):
    cs = CMP_STRIDE * jnp.arange(n_cmp)[:, None]
    ss = SLC_BLOCK * jnp.arange(n_slc)[None, :]
    ov = jnp.clip(jnp.minimum(cs + CMP_BLOCK, ss + SLC_BLOCK) - jnp.maximum(cs, ss), 0)
    return ov.astype(jnp.float32) / CMP_BLOCK


def nsa_attention(q, kc, vc, ks, vs, kw, vw, gates, qn_g, kn_g, pe_k, pe_v, w_ck1, w_ck2, w_cv1, w_cv2):
    b, t = q.shape[:2]
    pos = jnp.arange(t)
    scale = HEAD_DIM ** -0.5
    q = partial_rope(rms_norm(q, qn_g), pos)
    kcmp = compress_blocks(kc, pe_k, w_ck1, w_ck2)
    vcmp = compress_blocks(vc, pe_v, w_cv1, w_cv2)
    n_cmp = kcmp.shape[1]
    cmp_end = CMP_STRIDE * jnp.arange(n_cmp) + CMP_BLOCK - 1
    kcmp = partial_rope(rms_norm(kcmp, kn_g[0]), cmp_end)
    n_slc = t // SLC_BLOCK
    n_sel = min(N_SELECT, n_slc)
    ks = partial_rope(rms_norm(ks, kn_g[1]), pos)
    ks_blk = ks.reshape(b, n_slc, SLC_BLOCK, N_KV_HEADS, HEAD_DIM).transpose(0, 3, 1, 2, 4)
    vs_blk = vs.reshape(b, n_slc, SLC_BLOCK, N_KV_HEADS, HEAD_DIM).transpose(0, 3, 1, 2, 4)
    cmp2slc = cmp_to_slc_weights(n_cmp, n_slc)
    blk_ids = jnp.arange(n_slc)
    kw = partial_rope(rms_norm(kw, kn_g[2]), pos)
    kw_pad = jnp.pad(kw, ((0, 0), (WINDOW, 0), (0, 0), (0, 0)))
    vw_pad = jnp.pad(vw, ((0, 0), (WINDOW, 0), (0, 0), (0, 0)))
    gather_blocks = jax.vmap(jax.vmap(lambda blocks, idx: blocks[idx]))

    def block_fn(start):
        tq = start + jnp.arange(Q_BLOCK)
        qb = lax.dynamic_slice_in_dim(q, start, Q_BLOCK, axis=1)
        gb = lax.dynamic_slice_in_dim(gates, start, Q_BLOCK, axis=1)
        s_c = jnp.einsum('bqgrd,bngd->bgrqn', qb, kcmp).astype(jnp.float32) * scale
        p_c = masked_softmax(s_c, cmp_end[None, :] <= tq[:, None])
        o_c = jnp.einsum('bgrqn,bngd->bqgrd', p_c.astype(vcmp.dtype), vcmp)
        imp = jnp.einsum('bgrqn,ns->bgqs', p_c, cmp2slc)
        cur = tq // SLC_BLOCK
        forced = (blk_ids[None, :] == 0) | (blk_ids[None, :] == cur[:, None]) | (blk_ids[None, :] == cur[:, None] - 1)
        future = blk_ids[None, :] * SLC_BLOCK > tq[:, None]
        imp = jnp.where(forced, 1e9, jnp.where(future, -1e9, imp))
        _, sel = lax.top_k(imp, n_sel)
        k_sel = gather_blocks(ks_blk, sel)
        v_sel = gather_blocks(vs_blk, sel)
        kpos = sel[..., None] * SLC_BLOCK + jnp.arange(SLC_BLOCK)
        valid_s = (kpos <= tq[:, None, None])[:, :, None]
        s_s = jnp.einsum('bqgrd,bgqnkd->bgrqnk', qb, k_sel).astype(jnp.float32) * scale
        flat = s_s.shape[:4] + (n_sel * SLC_BLOCK,)
        p_s = masked_softmax(s_s.reshape(flat), valid_s.reshape(b, N_KV_HEADS, 1, Q_BLOCK, n_sel * SLC_BLOCK)).reshape(s_s.shape)
        o_s = jnp.einsum('bgrqnk,bgqnkd->bqgrd', p_s.astype(v_sel.dtype), v_sel)
        kwb = lax.dynamic_slice_in_dim(kw_pad, start, WINDOW + Q_BLOCK, axis=1)
        vwb = lax.dynamic_slice_in_dim(vw_pad, start, WINDOW + Q_BLOCK, axis=1)
        kpos_w = start - WINDOW + jnp.arange(WINDOW + Q_BLOCK)
        valid_w = (kpos_w[None, :] <= tq[:, None]) & (kpos_w[None, :] > tq[:, None] - WINDOW) & (kpos_w[None, :] >= 0)
        s_w = jnp.einsum('bqgrd,bkgd->bgrqk', qb, kwb).astype(jnp.float32) * scale
        p_w = masked_softmax(s_w, valid_w)
        o_w = jnp.einsum('bgrqk,bkgd->bqgrd', p_w.astype(vwb.dtype), vwb)
        return gb[..., 0:1] * o_c + gb[..., 1:2] * o_s + gb[..., 2:3] * o_w

    starts = jnp.arange(t // Q_BLOCK) * Q_BLOCK
    out = lax.map(block_fn, starts)
    return out.transpose(1, 0, 2, 3, 4, 5).reshape(b, t, D_ATT)


def hybrid_layer(h, p_l, ln_mix_g, w_in, qn_g, kn_g, pe_k, pe_v, w_ck1, w_ck2, w_cv1, w_cv2,
                 conv_w, on_att_g, on_conv_g, w_o, ln_ffn_g, w_up, ffn_conv_w, ffn_conv_b, w_down,
                 ln_ple_g, w_pg, w_pe):
    b, t, _ = h.shape
    u = rms_norm(h, ln_mix_g) @ w_in
    offs = np.cumsum([D_ATT] + [KV_W] * 6 + [N_BRANCH * N_Q_HEADS, D_CONV, D_CONV]).tolist()
    q, kc, vc, ks, vs, kw, vw, g_logit, cb, cc, cx = jnp.split(u, offs, axis=-1)
    q = q.reshape(b, t, N_KV_HEADS, Q_PER_KV, HEAD_DIM)
    kv_shape = (b, t, N_KV_HEADS, HEAD_DIM)
    gates = jax.nn.sigmoid(g_logit).reshape(b, t, N_KV_HEADS, Q_PER_KV, N_BRANCH)
    o_att = nsa_attention(q, kc.reshape(kv_shape), vc.reshape(kv_shape), ks.reshape(kv_shape),
                          vs.reshape(kv_shape), kw.reshape(kv_shape), vw.reshape(kv_shape), gates,
                          qn_g, kn_g, pe_k, pe_v, w_ck1, w_ck2, w_cv1, w_cv2)
    o_conv = cb * causal_dwconv(cc * cx, conv_w)
    mixed = jnp.concatenate([rms_norm(o_att, on_att_g), rms_norm(o_conv, on_conv_g)], axis=-1)
    h = h + mixed @ w_o
    a = rms_norm(h, ln_ffn_g) @ w_up
    gate, up = jnp.split(a, [D_FF], axis=-1)
    gate = causal_dwconv(gate, ffn_conv_w) + ffn_conv_b
    h = h + (jax.nn.silu(gate) * up) @ w_down
    h = h + jax.nn.sigmoid(rms_norm(h, ln_ple_g) @ w_pg) * (p_l @ w_pe)
    return h


def setup_inputs(seed: int = 0) -> dict:
    key = jax.random.key(seed)
    ks = jax.random.split(key, 32)
    f32 = jnp.float32

    def nrm(k, shape, scale):
        return jax.random.normal(k, shape, f32) * scale

    def gain(k, shape):
        return 1.0 + 0.01 * jax.random.normal(k, shape, f32)

    L = DEPTH
    return {
        'x': nrm(ks[0], (BATCH, SEQ, D_MODEL), 1.0),
        'p': nrm(ks[1], (DEPTH, BATCH, SEQ, D_PLE), 1.0),
        'ln_mix_g': gain(ks[2], (L, D_MODEL)),
        'w_in': nrm(ks[3], (L, D_MODEL, D_IN), D_MODEL ** -0.5),
        'qn_g': gain(ks[4], (L, HEAD_DIM)),
        'kn_g': gain(ks[5], (L, N_BRANCH, HEAD_DIM)),
        'pe_k': nrm(ks[6], (L, CMP_BLOCK, HEAD_DIM), 0.2),
        'pe_v': nrm(ks[7], (L, CMP_BLOCK, HEAD_DIM), 0.2),
        'w_ck1': nrm(ks[8], (L, CMP_BLOCK, HEAD_DIM, CMP_HIDDEN), (CMP_BLOCK * HEAD_DIM) ** -0.5),
        'w_ck2': nrm(ks[9], (L, CMP_HIDDEN, HEAD_DIM), CMP_HIDDEN ** -0.5),
        'w_cv1': nrm(ks[10], (L, CMP_BLOCK, HEAD_DIM, CMP_HIDDEN), (CMP_BLOCK * HEAD_DIM) ** -0.5),
        'w_cv2': nrm(ks[11], (L, CMP_HIDDEN, HEAD_DIM), CMP_HIDDEN ** -0.5),
        'conv_w': nrm(ks[12], (L, CONV_TAPS, D_CONV), CONV_TAPS ** -0.5),
        'on_att_g': gain(ks[13], (L, D_ATT)),
        'on_conv_g': gain(ks[14], (L, D_CONV)),
        'w_o': nrm(ks[15], (L, D_MIX, D_MODEL), D_MIX ** -0.5),
        'ln_ffn_g': gain(ks[16], (L, D_MODEL)),
        'w_up': nrm(ks[17], (L, D_MODEL, 2 * D_FF), D_MODEL ** -0.5),
        'ffn_conv_w': nrm(ks[18], (L, CONV_TAPS, D_FF), CONV_TAPS ** -0.5),
        'ffn_conv_b': nrm(ks[19], (L, D_FF), 0.01),
        'w_down': nrm(ks[20], (L, D_FF, D_MODEL), D_FF ** -0.5),
        'ln_ple_g': gain(ks[21], (L, D_MODEL)),
        'w_pg': nrm(ks[22], (L, D_MODEL, D_MODEL), D_MODEL ** -0.5),
        'w_pe': nrm(ks[23], (L, D_PLE, D_MODEL), D_PLE ** -0.5),
    }


def reference(x, p, ln_mix_g, w_in, qn_g, kn_g, pe_k, pe_v, w_ck1, w_ck2, w_cv1, w_cv2,
              conv_w, on_att_g, on_conv_g, w_o, ln_ffn_g, w_up, ffn_conv_w, ffn_conv_b, w_down,
              ln_ple_g, w_pg, w_pe):
    h = x
    for i in range(DEPTH):
        h = hybrid_layer(h, p[i], ln_mix_g[i], w_in[i], qn_g[i], kn_g[i], pe_k[i], pe_v[i],
                         w_ck1[i], w_ck2[i], w_cv1[i], w_cv2[i], conv_w[i], on_att_g[i], on_conv_g[i],
                         w_o[i], ln_ffn_g[i], w_up[i], ffn_conv_w[i], ffn_conv_b[i], w_down[i],
                         ln_ple_g[i], w_pg[i], w_pe[i])
    return h
```

```python
import functools

import jax
import jax.numpy as jnp
import numpy as np
from jax import lax
from jax.experimental import pallas as pl
from jax.experimental.pallas import tpu as pltpu

D_MODEL = 1024
HEAD_DIM = 64
N_Q_HEADS = 8
N_KV_HEADS = 2
Q_PER_KV = N_Q_HEADS // N_KV_HEADS
D_ATT = N_Q_HEADS * HEAD_DIM
D_CONV = D_MODEL - D_ATT
KV_W = N_KV_HEADS * HEAD_DIM
N_BRANCH = 3
CONV_TAPS = 3
ROT_DIM = HEAD_DIM // 4
ROPE_THETA = 500000.0
CMP_BLOCK = 32
CMP_STRIDE = 16
CMP_HIDDEN = 256
SLC_BLOCK = 64
N_SELECT = 16
WINDOW = 512
Q_BLOCK = 128
D_FF = 2816
D_PLE = 256
EPS = 1e-6
NEG = -1e30

LANES = 128
SUBLANES = 8
N_SLC_LANES = LANES
VMEM_LIMIT_BYTES = 56 * 1024 * 1024

F32 = jnp.float32
BF16 = jnp.bfloat16
_NT = (((1,), (1,)), ((), ()))


def _dot(a, b):
    return jnp.dot(a, b, preferred_element_type=F32)


def _dot_nt(a, b):
    return lax.dot_general(a, b, _NT, preferred_element_type=F32)


def _rms(x, g):
    return x * lax.rsqrt(jnp.mean(x * x, axis=-1, keepdims=True) + EPS) * g


def _head_rms_rope(x, g, bd, cos, sa, sb):
    w = x.shape[-1]
    msq = _dot((x * x).astype(BF16), bd)
    xn = x * lax.rsqrt(msq + EPS) * g
    return xn * cos + pltpu.roll(xn, w - ROT_DIM // 2, 1) * sa + pltpu.roll(xn, ROT_DIM // 2, 1) * sb


def _dup_halves(x):
    r = pltpu.roll(x, HEAD_DIM, 1)
    lane = lax.broadcasted_iota(jnp.int32, x.shape, 1)
    lo = lane < HEAD_DIM
    return jnp.where(lo, x, r), jnp.where(lo, r, x)


def _inproj_kernel(x_ref, lng_ref, wq_ref, wkv_ref, wg_ref, wcv_ref, qng_ref, kng_ref, bd512_ref, bd128_ref,
                   cos_ref, sa_ref, sb_ref, convw_ref, oncg_ref,
                   q_out, kc_out, vc_out, ksel_out, vs_out, kw_out, vw_out, gate_out, mconv_out,
                   zbuf, *, tiles_per_seq):
    tm = x_ref.shape[0]
    it = pl.program_id(0) % tiles_per_seq
    x = x_ref[...]
    xn = _rms(x, lng_ref[...]).astype(BF16)

    cos, sa, sb = cos_ref[...], sa_ref[...], sb_ref[...]
    cos4, sa4, sb4 = (jnp.concatenate([t] * 4, axis=1) for t in (cos, sa, sb))
    q = _dot(xn, wq_ref[...])
    qr = _head_rms_rope(q, qng_ref[...], bd512_ref[...], cos4, sa4, sb4)
    q_out[...] = (qr * (HEAD_DIM ** -0.5)).astype(BF16)

    kv = _dot(xn, wkv_ref[...])
    kc_out[...] = kv[:, 0 * KV_W:1 * KV_W]
    vc_out[...] = kv[:, 1 * KV_W:2 * KV_W]
    ks = _head_rms_rope(kv[:, 2 * KV_W:3 * KV_W], kng_ref[0:1, :], bd128_ref[...], cos, sa, sb)
    kw = _head_rms_rope(kv[:, 4 * KV_W:5 * KV_W], kng_ref[1:2, :], bd128_ref[...], cos, sa, sb)
    vs = kv[:, 3 * KV_W:4 * KV_W]
    vw = kv[:, 5 * KV_W:6 * KV_W]
    tpos = it * tm + lax.broadcasted_iota(jnp.int32, (tm, N_SLC_LANES), 0)
    blk = lax.broadcasted_iota(jnp.int32, (tm, N_SLC_LANES), 1)
    onehot = jnp.where(lax.shift_right_logical(tpos, 6) == blk, 1.0, 0.0).astype(BF16)
    ks_d, vs_d, kw_d, vw_d = (_dup_halves(t) for t in (ks, vs, kw, vw))
    for g in range(N_KV_HEADS):
        ksel_out[0, g] = jnp.concatenate([ks_d[g].astype(BF16), onehot], axis=1)
        vs_out[0, g] = vs_d[g].astype(BF16)
        kw_out[0, g] = kw_d[g].astype(BF16)
        vw_out[0, g] = vw_d[g].astype(BF16)

    gate_out[...] = jax.nn.sigmoid(_dot(xn, wg_ref[...]))

    cv = _dot(xn, wcv_ref[...])
    cb, cc, cx = cv[:, :D_CONV], cv[:, D_CONV:2 * D_CONV], cv[:, 2 * D_CONV:]
    z = cc * cx

    @pl.when(it == 0)
    def _():
        zbuf[0:SUBLANES, :] = jnp.zeros((SUBLANES, D_CONV), F32)

    zbuf[SUBLANES:SUBLANES + tm, :] = z
    y = (convw_ref[2:3, :] * z + convw_ref[1:2, :] * zbuf[SUBLANES - 1:SUBLANES - 1 + tm, :]
         + convw_ref[0:1, :] * zbuf[SUBLANES - 2:SUBLANES - 2 + tm, :])
    zbuf[0:SUBLANES, :] = zbuf[tm:tm + SUBLANES, :]
    mconv_out[...] = _rms(cb * y, oncg_ref[...]).astype(BF16)


def _inproj(x2, lng, wq, wkv, wg, wcv, qng, kng, bd512, bd128, cos, sa, sb, convw, oncg, *, batch, seq, tm):
    n = batch * seq
    tps = seq // tm
    row = lambda w: pl.BlockSpec((tm, w), lambda i: (i, 0))
    full = lambda a: pl.BlockSpec(a.shape, lambda i: (0,) * a.ndim)
    tab = pl.BlockSpec((tm, LANES), lambda i: (i % tps, 0))
    grp = lambda w: pl.BlockSpec((1, N_KV_HEADS, tm, w), lambda i: (i // tps, 0, i % tps, 0))
    gshape = lambda w: jax.ShapeDtypeStruct((batch, N_KV_HEADS, seq, w), BF16)
    return pl.pallas_call(
        functools.partial(_inproj_kernel, tiles_per_seq=tps),
        grid=(n // tm,),
        in_specs=[row(D_MODEL), full(lng), full(wq), full(wkv), full(wg), full(wcv), full(qng), full(kng),
                  full(bd512), full(bd128), tab, tab, tab, full(convw), full(oncg)],
        out_specs=[row(D_ATT), row(KV_W), row(KV_W), grp(2 * LANES), grp(LANES), grp(LANES), grp(LANES),
                   row(N_KV_HEADS * LANES), row(D_CONV)],
        out_shape=[jax.ShapeDtypeStruct((n, D_ATT), BF16), jax.ShapeDtypeStruct((n, KV_W), F32),
                   jax.ShapeDtypeStruct((n, KV_W), F32), gshape(2 * LANES), gshape(LANES), gshape(LANES),
                   gshape(LANES), jax.ShapeDtypeStruct((n, N_KV_HEADS * LANES), F32),
                   jax.ShapeDtypeStruct((n, D_CONV), BF16)],
        scratch_shapes=[pltpu.VMEM((tm + SUBLANES, D_CONV), F32)],
        compiler_params=pltpu.CompilerParams(dimension_semantics=("arbitrary",),
                                             vmem_limit_bytes=VMEM_LIMIT_BYTES),
        name="inproj",
    )(x2, lng, wq, wkv, wg, wcv, qng, kng, bd512, bd128, cos, sa, sb, convw, oncg)


def _gelu_tanh(x):
    return 0.5 * x * (1.0 + jnp.tanh(np.sqrt(2.0 / np.pi).astype(np.float32) * (x + 0.044715 * (x * x * x))))


def _compress_kernel(kc_ref, vc_ref, w1ak_ref, w1bk_ref, w1av_ref, w1bv_ref, peak_ref, pebk_ref, peav_ref,
                     pebv_ref, w2k_ref, w2v_ref, kng_ref, bd128_ref, cos_ref, sa_ref, sb_ref,
                     kcmp_out, vcmp_out):
    nch = kc_ref.shape[1]

    def compress(x, w1a, w1b, pea, peb, w2):
        a = _dot((x + pea).astype(BF16), w1a)
        b = _dot((x + peb).astype(BF16), w1b)
        hid = _gelu_tanh(a + pltpu.roll(b, nch - 1, 0))
        return _dot(hid.astype(BF16), w2)

    kc = compress(kc_ref[0], w1ak_ref[...], w1bk_ref[...], peak_ref[...], pebk_ref[...], w2k_ref[...])
    vc = compress(vc_ref[0], w1av_ref[...], w1bv_ref[...], peav_ref[...], pebv_ref[...], w2v_ref[...])
    kc = _head_rms_rope(kc, kng_ref[...], bd128_ref[...], cos_ref[...], sa_ref[...], sb_ref[...])
    kd, vd = _dup_halves(kc), _dup_halves(vc)
    for g in range(N_KV_HEADS):
        kcmp_out[0, g] = kd[g].astype(BF16)
        vcmp_out[0, g] = vd[g].astype(BF16)


def _compress(kc3, vc3, w1ak, w1bk, w1av, w1bv, peak, pebk, peav, pebv, w2k, w2v, kng0, bd128, cos, sa, sb):
    batch, nch, wide = kc3.shape
    full = lambda a: pl.BlockSpec(a.shape, lambda b: (0,) * a.ndim)
    tok = pl.BlockSpec((1, nch, wide), lambda b: (b, 0, 0))
    out = pl.BlockSpec((1, N_KV_HEADS, nch, LANES), lambda b: (b, 0, 0, 0))
    oshape = jax.ShapeDtypeStruct((batch, N_KV_HEADS, nch, LANES), BF16)
    consts = (w1ak, w1bk, w1av, w1bv, peak, pebk, peav, pebv, w2k, w2v, kng0, bd128, cos, sa, sb)
    return pl.pallas_call(
        _compress_kernel,
        grid=(batch,),
        in_specs=[tok, tok] + [full(a) for a in consts],
        out_specs=[out, out],
        out_shape=[oshape, oshape],
        compiler_params=pltpu.CompilerParams(dimension_semantics=("arbitrary",),
                                             vmem_limit_bytes=VMEM_LIMIT_BYTES),
        name="compress",
    )(kc3, vc3, *consts)


SEL_TILE = 256
ROWS = Q_PER_KV * Q_BLOCK


def _split3(x):
    hi = x.astype(BF16)
    r1 = x - hi.astype(F32)
    mid = r1.astype(BF16)
    lo = (r1 - mid.astype(F32)).astype(BF16)
    return hi, mid, lo


def _attn_kernel(q_ref, ksel_ref, vs_ref, kw_ref, vw_ref, kcmp_ref, vcmp_ref, c2s_ref, gate_ref, o_ref,
                 impt_scr, m_scr, l_scr, acc_scr):
    i = pl.program_id(2)
    start = i * Q_BLOCK
    nch = kcmp_ref.shape[2]

    lane = lax.broadcasted_iota(jnp.int32, (Q_BLOCK, LANES), 1)
    lo_half = lane < HEAD_DIM
    qblk = q_ref[0]
    zero = jnp.zeros((Q_BLOCK, LANES), BF16)
    heads = []
    for r in range(Q_PER_KV):
        pair = qblk[:, LANES * (r // 2):LANES * (r // 2 + 1)]
        heads.append(jnp.where(lo_half if r % 2 == 0 else ~lo_half, pair, zero))
    qs = jnp.concatenate(heads, axis=0)

    def row_tq(ncols):
        rr = lax.broadcasted_iota(jnp.int32, (ROWS, ncols), 0)
        return start + (rr & (Q_BLOCK - 1))

    s_c = _dot_nt(qs, kcmp_ref[0, 0])
    ncol = lax.broadcasted_iota(jnp.int32, (ROWS, nch), 1)
    valid_c = (CMP_STRIDE * ncol + (CMP_BLOCK - 1)) <= row_tq(nch)
    s_c = jnp.where(valid_c, s_c, NEG)
    m_c = jnp.max(s_c, axis=1, keepdims=True)
    e_c = jnp.where(valid_c, jnp.exp(s_c - m_c), 0.0)
    l_c = jnp.sum(e_c, axis=1, keepdims=True)
    p_c = e_c * (1.0 / jnp.maximum(l_c, 1e-30))
    o_c = _dot(p_c.astype(BF16), vcmp_ref[0, 0])

    psum = p_c[0:Q_BLOCK] + p_c[Q_BLOCK:2 * Q_BLOCK] + p_c[2 * Q_BLOCK:3 * Q_BLOCK] + p_c[3 * Q_BLOCK:]
    c2s = c2s_ref[...]
    imp = sum(_dot(t, c2s) for t in _split3(psum))
    tq = start + lax.broadcasted_iota(jnp.int32, (Q_BLOCK, N_SLC_LANES), 0)
    cur = lax.shift_right_logical(tq, 6)
    forced = (lane == 0) | (lane == cur) | (lane == cur - 1)
    future = lane * SLC_BLOCK > tq
    imp = jnp.where(forced, 1e9, jnp.where(future, -1e9, imp))
    impt = imp.T
    impt_scr[...] = impt
    sub = lax.broadcasted_iota(jnp.int32, (SUBLANES, Q_BLOCK), 0)
    cnt = jnp.zeros((N_SLC_LANES, Q_BLOCK), jnp.int32)
    for sp in range(N_SLC_LANES):
        lo, hi = sp - sp % SUBLANES, sp - sp % SUBLANES + SUBLANES
        rowb = jnp.broadcast_to(impt_scr[sp:sp + 1, :], (N_SLC_LANES, Q_BLOCK))
        mid_r, mid_x = rowb[lo:hi], impt[lo:hi]
        mid = (mid_r > mid_x) | ((mid_r == mid_x) & (sub > sp % SUBLANES))
        parts = [jnp.where(rowb[:lo] > impt[:lo], 1, 0)] if lo else []
        parts.append(jnp.where(mid, 1, 0))
        if hi < N_SLC_LANES:
            parts.append(jnp.where(rowb[hi:] >= impt[hi:], 1, 0))
        cnt = cnt + jnp.concatenate(parts, axis=0)
    bias = jnp.where(cnt < N_SELECT, 0.0, NEG).T.astype(BF16)
    qsel = jnp.concatenate([qs, jnp.concatenate([bias] * Q_PER_KV, axis=0)], axis=1)

    m_scr[...] = jnp.full((ROWS, LANES), NEG, F32)
    l_scr[...] = jnp.zeros((ROWS, LANES), F32)
    acc_scr[...] = jnp.zeros((ROWS, LANES), F32)
    tq_s = row_tq(SEL_TILE)
    kcol = lax.broadcasted_iota(jnp.int32, (ROWS, SEL_TILE), 1)

    def sel_step(j, carry):
        k0 = pl.multiple_of(j * SEL_TILE, SEL_TILE)
        s = _dot_nt(qsel, ksel_ref[0, 0, pl.ds(k0, SEL_TILE), :])
        s = jnp.where(k0 + kcol <= tq_s, s, NEG)
        m_prev = m_scr[...]
        m_new = jnp.maximum(m_prev, jnp.max(s, axis=1, keepdims=True))
        alpha = jnp.exp(m_prev - m_new)
        p = jnp.exp(s - jnp.concatenate([m_new] * (SEL_TILE // LANES), axis=1))
        l_scr[...] = alpha * l_scr[...] + jnp.sum(p, axis=1, keepdims=True)
        acc_scr[...] = alpha * acc_scr[...] + _dot(p.astype(BF16), vs_ref[0, 0, pl.ds(k0, SEL_TILE), :])
        m_scr[...] = m_new
        return carry

    lax.fori_loop(0, (start + Q_BLOCK + SEL_TILE - 1) // SEL_TILE, sel_step, 0)
    o_s = acc_scr[...] * (1.0 / l_scr[...])

    wk = WINDOW + Q_BLOCK
    base = pl.multiple_of(jnp.maximum(start - WINDOW, 0), Q_BLOCK)
    s_w = _dot_nt(qs, kw_ref[0, 0, pl.ds(base, wk), :])
    kpos = base + lax.broadcasted_iota(jnp.int32, (ROWS, wk), 1)
    tq_w = row_tq(wk)
    valid_w = (kpos <= tq_w) & (kpos > tq_w - WINDOW)
    s_w = jnp.where(valid_w, s_w, NEG)
    m_w = jnp.max(s_w, axis=1, keepdims=True)
    e_w = jnp.exp(s_w - m_w)
    p_w = e_w * (1.0 / jnp.sum(e_w, axis=1, keepdims=True))
    o_w = _dot(p_w.astype(BF16), vw_ref[0, 0, pl.ds(base, wk), :])

    gates = gate_ref[0]
    outs = []
    for r in range(Q_PER_KV):
        rows = slice(r * Q_BLOCK, (r + 1) * Q_BLOCK)
        gc, gs, gw = (gates[:, r * N_BRANCH + b:r * N_BRANCH + b + 1] for b in range(N_BRANCH))
        outs.append(gc * o_c[rows] + gs * o_s[rows] + gw * o_w[rows])
    o_ref[0] = jnp.concatenate([jnp.where(lo_half, outs[0], outs[1]), jnp.where(lo_half, outs[2], outs[3])],
                               axis=1)


def _attn(q3, ksel, vs, kw, vw, kcmp, vcmp, c2s, gates3):
    batch, seq, _ = q3.shape
    nch = kcmp.shape[2]
    seqblk = lambda w: pl.BlockSpec((1, 1, seq, w), lambda b, g, i: (b, g, 0, 0))
    cmpblk = pl.BlockSpec((1, 1, nch, LANES), lambda b, g, i: (b, g, 0, 0))
    return pl.pallas_call(
        _attn_kernel,
        grid=(batch, N_KV_HEADS, seq // Q_BLOCK),
        in_specs=[pl.BlockSpec((1, Q_BLOCK, Q_PER_KV * HEAD_DIM), lambda b, g, i: (b, i, g)),
                  seqblk(2 * LANES), seqblk(LANES), seqblk(LANES), seqblk(LANES), cmpblk, cmpblk,
                  pl.BlockSpec(c2s.shape, lambda b, g, i: (0, 0)),
                  pl.BlockSpec((1, Q_BLOCK, LANES), lambda b, g, i: (b, i, g))],
        out_specs=pl.BlockSpec((1, Q_BLOCK, Q_PER_KV * HEAD_DIM), lambda b, g, i: (b, i, g)),
        out_shape=jax.ShapeDtypeStruct((batch, seq, D_ATT), F32),
        scratch_shapes=[pltpu.VMEM((N_SLC_LANES, Q_BLOCK), F32), pltpu.VMEM((ROWS, LANES), F32),
                        pltpu.VMEM((ROWS, LANES), F32), pltpu.VMEM((ROWS, LANES), F32)],
        compiler_params=pltpu.CompilerParams(dimension_semantics=("arbitrary", "arbitrary", "arbitrary"),
                                             vmem_limit_bytes=VMEM_LIMIT_BYTES),
        name="attn",
    )(q3, ksel, vs, kw, vw, kcmp, vcmp, c2s, gates3)


FF_CHUNKS = 2
FF_CHUNK = D_FF // FF_CHUNKS


def _post_kernel(x_ref, oatt_ref, mconv_ref, p_ref, onag_ref, wo_ref, lnf_ref, wupg_ref, wupu_ref, fcw_ref,
                 fcb_ref, wdn_ref, lnp_ref, wpg_ref, wpe_ref, out_ref,
                 h1_scr, xn_scr, acc_scr, gbuf, *, tiles_per_seq):
    tm = x_ref.shape[0]
    it = pl.program_id(0) % tiles_per_seq
    j = pl.program_id(1)

    @pl.when(j == 0)
    def _():
        mixed = jnp.concatenate([_rms(oatt_ref[...], onag_ref[...]).astype(BF16), mconv_ref[...]], axis=1)
        h1 = x_ref[...] + _dot(mixed, wo_ref[...])
        h1_scr[...] = h1
        xn_scr[...] = _rms(h1, lnf_ref[...]).astype(BF16)
        acc_scr[...] = jnp.zeros_like(acc_scr)

    @pl.when(it == 0)
    def _():
        gbuf[j, 0:SUBLANES, :] = jnp.zeros((SUBLANES, FF_CHUNK), F32)

    xn = xn_scr[...]
    gpre = _dot(xn, wupg_ref[...])
    up = _dot(xn, wupu_ref[...])
    gbuf[j, SUBLANES:SUBLANES + tm, :] = gpre
    gate = (fcw_ref[2:3, :] * gpre + fcw_ref[1:2, :] * gbuf[j, SUBLANES - 1:SUBLANES - 1 + tm, :]
            + fcw_ref[0:1, :] * gbuf[j, SUBLANES - 2:SUBLANES - 2 + tm, :]) + fcb_ref[...]
    gbuf[j, 0:SUBLANES, :] = gbuf[j, tm:tm + SUBLANES, :]
    act = (gate * jax.nn.sigmoid(gate) * up).astype(BF16)
    acc_scr[...] += _dot(act, wdn_ref[...])

    @pl.when(j == FF_CHUNKS - 1)
    def _():
        h2 = h1_scr[...] + acc_scr[...]
        xn2 = _rms(h2, lnp_ref[...]).astype(BF16)
        out_ref[...] = h2 + jax.nn.sigmoid(_dot(xn2, wpg_ref[...])) * _dot(p_ref[...].astype(BF16), wpe_ref[...])


def _post(x2, oatt, mconv, p2, onag, wo, lnf, wup, fcw, fcb, wdn, lnp, wpg, wpe, *, seq, tm):
    n = x2.shape[0]
    tps = seq // tm
    row = lambda w: pl.BlockSpec((tm, w), lambda i, j: (i, 0))
    full = lambda a: pl.BlockSpec(a.shape, lambda i, j: (0,) * a.ndim)
    return pl.pallas_call(
        functools.partial(_post_kernel, tiles_per_seq=tps),
        grid=(n // tm, FF_CHUNKS),
        in_specs=[row(D_MODEL), row(D_ATT), row(D_CONV), row(D_PLE), full(onag), full(wo), full(lnf),
                  pl.BlockSpec((D_MODEL, FF_CHUNK), lambda i, j: (0, j)),
                  pl.BlockSpec((D_MODEL, FF_CHUNK), lambda i, j: (0, FF_CHUNKS + j)),
                  pl.BlockSpec((CONV_TAPS, FF_CHUNK), lambda i, j: (0, j)),
                  pl.BlockSpec((1, FF_CHUNK), lambda i, j: (0, j)),
                  pl.BlockSpec((FF_CHUNK, D_MODEL), lambda i, j: (j, 0)),
                  full(lnp), full(wpg), full(wpe)],
        out_specs=row(D_MODEL),
        out_shape=jax.ShapeDtypeStruct((n, D_MODEL), F32),
        scratch_shapes=[pltpu.VMEM((tm, D_MODEL), F32), pltpu.VMEM((tm, D_MODEL), BF16),
                        pltpu.VMEM((tm, D_MODEL), F32), pltpu.VMEM((FF_CHUNKS, tm + SUBLANES, FF_CHUNK), F32)],
        compiler_params=pltpu.CompilerParams(dimension_semantics=("arbitrary", "arbitrary"),
                                             vmem_limit_bytes=VMEM_LIMIT_BYTES),
        name="post",
    )(x2, oatt, mconv, p2, onag, wo, lnf, wup, wup, fcw, fcb, wdn, lnp, wpg, wpe)


def _rope_tables(pos):
    half = ROT_DIM // 2
    inv_freq = ROPE_THETA ** (-jnp.arange(half, dtype=F32) * 2.0 / ROT_DIM)
    ang = pos.astype(F32)[:, None] * inv_freq[None, :]
    c, s = jnp.cos(ang), jnp.sin(ang)
    z = jnp.zeros((pos.shape[0], HEAD_DIM - ROT_DIM), F32)
    zh = jnp.zeros_like(c)
    cos = jnp.concatenate([c, c, z + 1.0], axis=1)
    sa = jnp.concatenate([-s, zh, z], axis=1)
    sb = jnp.concatenate([zh, s, z], axis=1)
    return tuple(jnp.concatenate([t, t], axis=1) for t in (cos, sa, sb))


def _block_diag_mean(width):
    idx = np.arange(width) // HEAD_DIM
    return jnp.asarray((idx[:, None] == idx[None, :]).astype(np.float32) / HEAD_DIM, BF16)


def _cmp_to_slc(nch):
    cs = CMP_STRIDE * np.arange(nch)[:, None]
    ss = SLC_BLOCK * np.arange(N_SLC_LANES)[None, :]
    ov = np.clip(np.minimum(cs + CMP_BLOCK, ss + SLC_BLOCK) - np.maximum(cs, ss), 0, None)
    return jnp.asarray(ov.astype(np.float32) / CMP_BLOCK, BF16)


def _layer(h, p_l, ln_mix_g, w_in, qn_g, kn_g, pe_k, pe_v, w_ck1, w_ck2, w_cv1, w_cv2, conv_w, on_att_g,
           on_conv_g, w_o, ln_ffn_g, w_up, ffn_conv_w, ffn_conv_b, w_down, ln_ple_g, w_pg, w_pe):
    batch, seq, _ = h.shape
    assert seq % Q_BLOCK == 0 and seq // SLC_BLOCK <= N_SLC_LANES and seq // SLC_BLOCK >= N_SELECT
    assert seq >= WINDOW + Q_BLOCK
    n = batch * seq
    nch = seq // CMP_STRIDE
    x2 = h.reshape(n, D_MODEL)
    row = lambda v: v.reshape(1, -1).astype(F32)

    o_q, o_kv, o_g, o_cv = 0, D_ATT, D_ATT + 6 * KV_W, D_ATT + 6 * KV_W + N_BRANCH * N_Q_HEADS
    wq = w_in[:, o_q:o_kv].astype(BF16)
    wkv = w_in[:, o_kv:o_g].astype(BF16)
    wcv = w_in[:, o_cv:].astype(BF16)
    per_g = Q_PER_KV * N_BRANCH
    wg = jnp.concatenate(
        [jnp.pad(w_in[:, o_g + g * per_g:o_g + (g + 1) * per_g], ((0, 0), (0, LANES - per_g)))
         for g in range(N_KV_HEADS)], axis=1).astype(BF16)
    cos, sa, sb = _rope_tables(jnp.arange(seq))
    bd512, bd128 = _block_diag_mean(D_ATT), _block_diag_mean(KV_W)
    tile_heads = lambda v, k: jnp.tile(v.astype(F32), k).reshape(1, -1)

    tm = 256
    q2, kc2, vc2, ksel, vs, kw, vw, gates2, mconv = _inproj(
        x2, row(ln_mix_g), wq, wkv, wg, wcv, tile_heads(qn_g, N_Q_HEADS),
        jnp.stack([jnp.tile(kn_g[1], N_KV_HEADS), jnp.tile(kn_g[2], N_KV_HEADS)]).astype(F32),
        bd512, bd128, cos, sa, sb, conv_w.astype(F32), row(on_conv_g), batch=batch, seq=seq, tm=tm)

    eye = jnp.eye(N_KV_HEADS, dtype=F32)
    half = CMP_BLOCK // 2

    def w1_parts(w1):
        wfull = jnp.einsum('ldh,gk->lgdkh', w1, eye).reshape(CMP_BLOCK, KV_W, N_KV_HEADS * CMP_HIDDEN)
        return (wfull[:half].reshape(half * KV_W, -1).astype(BF16),
                wfull[half:].reshape(half * KV_W, -1).astype(BF16))

    def pe_parts(pe):
        pf = jnp.broadcast_to(pe[:, None, :], (CMP_BLOCK, N_KV_HEADS, HEAD_DIM)).astype(F32)
        return pf[:half].reshape(1, -1), pf[half:].reshape(1, -1)

    w2bd = lambda w2: jnp.einsum('hd,gk->ghkd', w2, eye).reshape(N_KV_HEADS * CMP_HIDDEN, KV_W).astype(BF16)
    w1ak, w1bk = w1_parts(w_ck1)
    w1av, w1bv = w1_parts(w_cv1)
    peak, pebk = pe_parts(pe_k)
    peav, pebv = pe_parts(pe_v)
    ccos, csa, csb = _rope_tables(CMP_STRIDE * jnp.arange(nch) + CMP_BLOCK - 1)
    kcmp, vcmp = _compress(kc2.reshape(batch, nch, CMP_STRIDE * KV_W), vc2.reshape(batch, nch, CMP_STRIDE * KV_W),
                           w1ak, w1bk, w1av, w1bv, peak, pebk, peav, pebv, w2bd(w_ck2), w2bd(w_cv2),
                           tile_heads(kn_g[0], N_KV_HEADS), bd128, ccos, csa, csb)

    oatt = _attn(q2.reshape(batch, seq, D_ATT), ksel, vs, kw, vw, kcmp, vcmp, _cmp_to_slc(nch),
                 gates2.reshape(batch, seq, N_KV_HEADS * LANES))

    out = _post(x2, oatt.reshape(n, D_ATT), mconv, p_l.reshape(n, D_PLE), row(on_att_g), w_o.astype(BF16),
                row(ln_ffn_g), w_up.astype(BF16), ffn_conv_w.astype(F32), row(ffn_conv_b), w_down.astype(BF16),
                row(ln_ple_g), w_pg.astype(BF16), w_pe.astype(BF16), seq=seq, tm=tm)
    return out.reshape(batch, seq, D_MODEL)


def kernel(x, p, ln_mix_g, w_in, qn_g, kn_g, pe_k, pe_v, w_ck1, w_ck2, w_cv1, w_cv2, conv_w, on_att_g,
           on_conv_g, w_o, ln_ffn_g, w_up, ffn_conv_w, ffn_conv_b, w_down, ln_ple_g, w_pg, w_pe):
    h = x
    for i in range(p.shape[0]):
        h = _layer(h, p[i], ln_mix_g[i], w_in[i], qn_g[i], kn_g[i], pe_k[i], pe_v[i], w_ck1[i], w_ck2[i],
                   w_cv1[i], w_cv2[i], conv_w[i], on_att_g[i], on_conv_g[i], w_o[i], ln_ffn_g[i], w_up[i],
                   ffn_conv_w[i], ffn_conv_b[i], w_down[i], ln_ple_g[i], w_pg[i], w_pe[i])
    return h
```

```python
import functools

import jax
import jax.numpy as jnp
import numpy as np
from jax import lax
from jax.experimental import pallas as pl
from jax.experimental.pallas import tpu as pltpu

D_MODEL = 1024
HEAD_DIM = 64
N_Q_HEADS = 8
N_KV_HEADS = 2
Q_PER_KV = N_Q_HEADS // N_KV_HEADS
D_ATT = N_Q_HEADS * HEAD_DIM
D_CONV = D_MODEL - D_ATT
KV_W = N_KV_HEADS * HEAD_DIM
N_BRANCH = 3
CONV_TAPS = 3
ROT_DIM = HEAD_DIM // 4
ROPE_THETA = 500000.0
CMP_BLOCK = 32
CMP_STRIDE = 16
CMP_HIDDEN = 256
SLC_BLOCK = 64
N_SELECT = 16
WINDOW = 512
Q_BLOCK = 128
D_FF = 2816
D_PLE = 256
EPS = 1e-6
NEG = -1e30

LANES = 128
SUBLANES = 8
N_SLC_LANES = LANES
VMEM_LIMIT_BYTES = 56 * 1024 * 1024

F32 = jnp.float32
BF16 = jnp.bfloat16
_NT = (((1,), (1,)), ((), ()))


def _dot(a, b):
    return jnp.dot(a, b, preferred_element_type=F32)


def _dot_nt(a, b):
    return lax.dot_general(a, b, _NT, preferred_element_type=F32)


def _rms(x, g):
    return x * lax.rsqrt(jnp.mean(x * x, axis=-1, keepdims=True) + EPS) * g


def _head_rms_rope(x, g, bd, cos, sa, sb):
    w = x.shape[-1]
    msq = _dot((x * x).astype(BF16), bd)
    xn = x * lax.rsqrt(msq + EPS) * g
    return xn * cos + pltpu.roll(xn, w - ROT_DIM // 2, 1) * sa + pltpu.roll(xn, ROT_DIM // 2, 1) * sb


def _dup_halves(x):
    r = pltpu.roll(x, HEAD_DIM, 1)
    lane = lax.broadcasted_iota(jnp.int32, x.shape, 1)
    lo = lane < HEAD_DIM
    return jnp.where(lo, x, r), jnp.where(lo, r, x)


def _inproj_kernel(x_ref, lng_ref, wq_ref, wkv_ref, wg_ref, wcv_ref, qng_ref, kng_ref, bd512_ref, bd128_ref,
                   cos_ref, sa_ref, sb_ref, convw_ref, oncg_ref,
                   q_out, kc_out, vc_out, ksel_out, vs_out, kw_out, vw_out, gate_out, mconv_out,
                   zbuf, *, tiles_per_seq):
    tm = x_ref.shape[0]
    it = pl.program_id(0) % tiles_per_seq
    x = x_ref[...]
    xn = _rms(x, lng_ref[...]).astype(BF16)

    cos, sa, sb = cos_ref[...], sa_ref[...], sb_ref[...]
    cos4, sa4, sb4 = (jnp.concatenate([t] * 4, axis=1) for t in (cos, sa, sb))
    q = _dot(xn, wq_ref[...])
    qr = _head_rms_rope(q, qng_ref[...], bd512_ref[...], cos4, sa4, sb4)
    q_out[...] = (qr * (HEAD_DIM ** -0.5)).astype(BF16)

    kv = _dot(xn, wkv_ref[...])
    kc_out[...] = kv[:, 0 * KV_W:1 * KV_W]
    vc_out[...] = kv[:, 1 * KV_W:2 * KV_W]
    ks = _head_rms_rope(kv[:, 2 * KV_W:3 * KV_W], kng_ref[0:1, :], bd128_ref[...], cos, sa, sb)
    kw = _head_rms_rope(kv[:, 4 * KV_W:5 * KV_W], kng_ref[1:2, :], bd128_ref[...], cos, sa, sb)
    vs = kv[:, 3 * KV_W:4 * KV_W]
    vw = kv[:, 5 * KV_W:6 * KV_W]
    tpos = it * tm + lax.broadcasted_iota(jnp.int32, (tm, N_SLC_LANES), 0)
    blk = lax.broadcasted_iota(jnp.int32, (tm, N_SLC_LANES), 1)
    onehot = jnp.where(lax.shift_right_logical(tpos, 6) == blk, 1.0, 0.0).astype(BF16)
    ks_d, vs_d, kw_d, vw_d = (_dup_halves(t) for t in (ks, vs, kw, vw))
    for g in range(N_KV_HEADS):
        ksel_out[0, g] = jnp.concatenate([ks_d[g].astype(BF16), onehot], axis=1)
        vs_out[0, g] = jnp.where(blk < HEAD_DIM, vs_d[g], 1.0).astype(BF16)
        kw_out[0, g] = kw_d[g].astype(BF16)
        vw_out[0, g] = vw_d[g].astype(BF16)

    gate_out[...] = jax.nn.sigmoid(_dot(xn, wg_ref[...]))

    cv = _dot(xn, wcv_ref[...])
    cb, cc, cx = cv[:, :D_CONV], cv[:, D_CONV:2 * D_CONV], cv[:, 2 * D_CONV:]
    z = cc * cx

    @pl.when(it == 0)
    def _():
        zbuf[0:SUBLANES, :] = jnp.zeros((SUBLANES, D_CONV), F32)

    zbuf[SUBLANES:SUBLANES + tm, :] = z
    y = (convw_ref[2:3, :] * z + convw_ref[1:2, :] * zbuf[SUBLANES - 1:SUBLANES - 1 + tm, :]
         + convw_ref[0:1, :] * zbuf[SUBLANES - 2:SUBLANES - 2 + tm, :])
    zbuf[0:SUBLANES, :] = zbuf[tm:tm + SUBLANES, :]
    mconv_out[...] = _rms(cb * y, oncg_ref[...]).astype(BF16)


def _inproj(x2, lng, wq, wkv, wg, wcv, qng, kng, bd512, bd128, cos, sa, sb, convw, oncg, *, batch, seq, tm):
    n = batch * seq
    tps = seq // tm
    row = lambda w: pl.BlockSpec((tm, w), lambda i: (i, 0))
    full = lambda a: pl.BlockSpec(a.shape, lambda i: (0,) * a.ndim)
    tab = pl.BlockSpec((tm, LANES), lambda i: (i % tps, 0))
    grp = lambda w: pl.BlockSpec((1, N_KV_HEADS, tm, w), lambda i: (i // tps, 0, i % tps, 0))
    gshape = lambda w: jax.ShapeDtypeStruct((batch, N_KV_HEADS, seq, w), BF16)
    return pl.pallas_call(
        functools.partial(_inproj_kernel, tiles_per_seq=tps),
        grid=(n // tm,),
        in_specs=[row(D_MODEL), full(lng), full(wq), full(wkv), full(wg), full(wcv), full(qng), full(kng),
                  full(bd512), full(bd128), tab, tab, tab, full(convw), full(oncg)],
        out_specs=[row(D_ATT), row(KV_W), row(KV_W), grp(2 * LANES), grp(LANES), grp(LANES), grp(LANES),
                   row(N_KV_HEADS * LANES), row(D_CONV)],
        out_shape=[jax.ShapeDtypeStruct((n, D_ATT), BF16), jax.ShapeDtypeStruct((n, KV_W), F32),
                   jax.ShapeDtypeStruct((n, KV_W), F32), gshape(2 * LANES), gshape(LANES), gshape(LANES),
                   gshape(LANES), jax.ShapeDtypeStruct((n, N_KV_HEADS * LANES), F32),
                   jax.ShapeDtypeStruct((n, D_CONV), BF16)],
        scratch_shapes=[pltpu.VMEM((tm + SUBLANES, D_CONV), F32)],
        compiler_params=pltpu.CompilerParams(dimension_semantics=("arbitrary",),
                                             vmem_limit_bytes=VMEM_LIMIT_BYTES),
        name="inproj",
    )(x2, lng, wq, wkv, wg, wcv, qng, kng, bd512, bd128, cos, sa, sb, convw, oncg)


def _gelu_tanh(x):
    return 0.5 * x * (1.0 + jnp.tanh(np.sqrt(2.0 / np.pi).astype(np.float32) * (x + 0.044715 * (x * x * x))))


def _compress_kernel(kc_ref, vc_ref, w1ak_ref, w1bk_ref, w1av_ref, w1bv_ref, peak_ref, pebk_ref, peav_ref,
                     pebv_ref, w2k_ref, w2v_ref, kng_ref, bd128_ref, cos_ref, sa_ref, sb_ref,
                     kcmp_out, vcmp_out):
    nch = kc_ref.shape[1]

    def compress(x, w1a, w1b, pea, peb, w2):
        a = _dot((x + pea).astype(BF16), w1a)
        b = _dot((x + peb).astype(BF16), w1b)
        hid = _gelu_tanh(a + pltpu.roll(b, nch - 1, 0))
        return _dot(hid.astype(BF16), w2)

    kc = compress(kc_ref[0], w1ak_ref[...], w1bk_ref[...], peak_ref[...], pebk_ref[...], w2k_ref[...])
    vc = compress(vc_ref[0], w1av_ref[...], w1bv_ref[...], peav_ref[...], pebv_ref[...], w2v_ref[...])
    kc = _head_rms_rope(kc, kng_ref[...], bd128_ref[...], cos_ref[...], sa_ref[...], sb_ref[...])
    kd, vd = _dup_halves(kc), _dup_halves(vc)
    for g in range(N_KV_HEADS):
        kcmp_out[0, g] = kd[g].astype(BF16)
        vcmp_out[0, g] = vd[g].astype(BF16)


def _compress(kc3, vc3, w1ak, w1bk, w1av, w1bv, peak, pebk, peav, pebv, w2k, w2v, kng0, bd128, cos, sa, sb):
    batch, nch, wide = kc3.shape
    full = lambda a: pl.BlockSpec(a.shape, lambda b: (0,) * a.ndim)
    tok = pl.BlockSpec((1, nch, wide), lambda b: (b, 0, 0))
    out = pl.BlockSpec((1, N_KV_HEADS, nch, LANES), lambda b: (b, 0, 0, 0))
    oshape = jax.ShapeDtypeStruct((batch, N_KV_HEADS, nch, LANES), BF16)
    consts = (w1ak, w1bk, w1av, w1bv, peak, pebk, peav, pebv, w2k, w2v, kng0, bd128, cos, sa, sb)
    return pl.pallas_call(
        _compress_kernel,
        grid=(batch,),
        in_specs=[tok, tok] + [full(a) for a in consts],
        out_specs=[out, out],
        out_shape=[oshape, oshape],
        compiler_params=pltpu.CompilerParams(dimension_semantics=("arbitrary",),
                                             vmem_limit_bytes=VMEM_LIMIT_BYTES),
        name="compress",
    )(kc3, vc3, *consts)


SEL_TILE = 256
SEL_STREAMS = 2
ROWS = Q_PER_KV * Q_BLOCK


def _split3(x):
    hi = x.astype(BF16)
    r1 = x - hi.astype(F32)
    mid = r1.astype(BF16)
    lo = (r1 - mid.astype(F32)).astype(BF16)
    return hi, mid, lo


def _attn_kernel(q_ref, ksel_ref, vs_ref, kw_ref, vw_ref, kcmp_ref, vcmp_ref, c2s_ref, gate_ref, o_ref,
                 impt_scr, m_scr, acc_scr):
    i = pl.program_id(2)
    start = i * Q_BLOCK
    nch = kcmp_ref.shape[2]

    lane = lax.broadcasted_iota(jnp.int32, (Q_BLOCK, LANES), 1)
    lo_half = lane < HEAD_DIM
    qblk = q_ref[0]
    zero = jnp.zeros((Q_BLOCK, LANES), BF16)
    heads = []
    for r in range(Q_PER_KV):
        pair = qblk[:, LANES * (r // 2):LANES * (r // 2 + 1)]
        heads.append(jnp.where(lo_half if r % 2 == 0 else ~lo_half, pair, zero))
    qs = jnp.concatenate(heads, axis=0)

    def row_tq(ncols):
        rr = lax.broadcasted_iota(jnp.int32, (ROWS, ncols), 0)
        return start + (rr & (Q_BLOCK - 1))

    s_c = _dot_nt(qs, kcmp_ref[0, 0])
    ncol = lax.broadcasted_iota(jnp.int32, (ROWS, nch), 1)
    valid_c = (CMP_STRIDE * ncol + (CMP_BLOCK - 1)) <= row_tq(nch)
    s_c = jnp.where(valid_c, s_c, NEG)
    m_c = jnp.max(s_c, axis=1, keepdims=True)
    e_c = jnp.where(valid_c, jnp.exp(s_c - m_c), 0.0)
    l_c = jnp.sum(e_c, axis=1, keepdims=True)
    p_c = e_c * (1.0 / jnp.maximum(l_c, 1e-30))
    o_c = _dot(p_c.astype(BF16), vcmp_ref[0, 0])

    psum = p_c[0:Q_BLOCK] + p_c[Q_BLOCK:2 * Q_BLOCK] + p_c[2 * Q_BLOCK:3 * Q_BLOCK] + p_c[3 * Q_BLOCK:]
    c2s = c2s_ref[...]
    imp = sum(_dot(t, c2s) for t in _split3(psum))
    tq = start + lax.broadcasted_iota(jnp.int32, (Q_BLOCK, N_SLC_LANES), 0)
    cur = lax.shift_right_logical(tq, 6)
    forced = (lane == 0) | (lane == cur) | (lane == cur - 1)
    future = lane * SLC_BLOCK > tq
    imp = jnp.where(forced, 1e9, jnp.where(future, -1e9, imp))
    impt = imp.T
    impt_scr[...] = impt
    sub = lax.broadcasted_iota(jnp.int32, (SUBLANES, Q_BLOCK), 0)
    cnt = jnp.zeros((N_SLC_LANES, Q_BLOCK), jnp.int32)
    for sp in range(N_SLC_LANES):
        lo, hi = sp - sp % SUBLANES, sp - sp % SUBLANES + SUBLANES
        rowb = jnp.broadcast_to(impt_scr[sp:sp + 1, :], (N_SLC_LANES, Q_BLOCK))
        mid_r, mid_x = rowb[lo:hi], impt[lo:hi]
        mid = (mid_r > mid_x) | ((mid_r == mid_x) & (sub > sp % SUBLANES))
        parts = [jnp.where(rowb[:lo] > impt[:lo], 1, 0)] if lo else []
        parts.append(jnp.where(mid, 1, 0))
        if hi < N_SLC_LANES:
            parts.append(jnp.where(rowb[hi:] >= impt[hi:], 1, 0))
        cnt = cnt + jnp.concatenate(parts, axis=0)
    bias = jnp.where(cnt < N_SELECT, 0.0, NEG).T.astype(BF16)
    qsel = jnp.concatenate([qs, jnp.concatenate([bias] * Q_PER_KV, axis=0)], axis=1)

    for st in range(SEL_STREAMS):
        m_scr[st] = jnp.full((ROWS, LANES), NEG, F32)
        acc_scr[st] = jnp.zeros((ROWS, LANES), F32)
    tq_s = row_tq(SEL_TILE)
    kcol = lax.broadcasted_iota(jnp.int32, (ROWS, SEL_TILE), 1)

    def sel_tile(st, j, masked):
        k0 = pl.multiple_of(j * SEL_TILE, SEL_TILE)
        s = _dot_nt(qsel, ksel_ref[0, 0, pl.ds(k0, SEL_TILE), :])
        if masked:
            s = jnp.where(k0 + kcol <= tq_s, s, NEG)
        m_prev = m_scr[st]
        m_new = jnp.maximum(m_prev, jnp.max(s, axis=1, keepdims=True))
        p = jnp.exp(s - jnp.concatenate([m_new] * (SEL_TILE // LANES), axis=1))
        acc_scr[st] = (jnp.exp(m_prev - m_new) * acc_scr[st]
                       + _dot(p.astype(BF16), vs_ref[0, 0, pl.ds(k0, SEL_TILE), :]))
        m_scr[st] = m_new

    n_main = (start // SEL_TILE) // SEL_STREAMS

    def sel_main(t, carry):
        for st in range(SEL_STREAMS):
            sel_tile(st, t * SEL_STREAMS + st, False)
        return carry

    lax.fori_loop(0, n_main, sel_main, 0)
    for st in range(SEL_STREAMS):
        sel_tile(st, n_main * SEL_STREAMS + st, True)
    m_tot = m_scr[0]
    for st in range(1, SEL_STREAMS):
        m_tot = jnp.maximum(m_tot, m_scr[st])
    acc_s = sum(jnp.exp(m_scr[st] - m_tot) * acc_scr[st] for st in range(SEL_STREAMS))
    lo_rows = lax.broadcasted_iota(jnp.int32, (ROWS, LANES), 1) < HEAD_DIM
    o_s = acc_s / jnp.where(lo_rows, pltpu.roll(acc_s, HEAD_DIM, 1), 1.0)
    o_s = jnp.where(lo_rows, o_s, pltpu.roll(o_s, HEAD_DIM, 1))

    wk = WINDOW + Q_BLOCK
    base = pl.multiple_of(jnp.maximum(start - WINDOW, 0), Q_BLOCK)
    s_w = _dot_nt(qs, kw_ref[0, 0, pl.ds(base, wk), :])
    kpos = base + lax.broadcasted_iota(jnp.int32, (ROWS, wk), 1)
    tq_w = row_tq(wk)
    valid_w = (kpos <= tq_w) & (kpos > tq_w - WINDOW)
    s_w = jnp.where(valid_w, s_w, NEG)
    m_w = jnp.max(s_w, axis=1, keepdims=True)
    e_w = jnp.exp(s_w - m_w)
    p_w = e_w * (1.0 / jnp.sum(e_w, axis=1, keepdims=True))
    o_w = _dot(p_w.astype(BF16), vw_ref[0, 0, pl.ds(base, wk), :])

    gates = gate_ref[0]
    outs = []
    for r in range(Q_PER_KV):
        rows = slice(r * Q_BLOCK, (r + 1) * Q_BLOCK)
        gc, gs, gw = (gates[:, r * N_BRANCH + b:r * N_BRANCH + b + 1] for b in range(N_BRANCH))
        outs.append(gc * o_c[rows] + gs * o_s[rows] + gw * o_w[rows])
    o_ref[0] = jnp.concatenate([jnp.where(lo_half, outs[0], outs[1]), jnp.where(lo_half, outs[2], outs[3])],
                               axis=1)


def _attn(q3, ksel, vs, kw, vw, kcmp, vcmp, c2s, gates3):
    batch, seq, _ = q3.shape
    nch = kcmp.shape[2]
    seqblk = lambda w: pl.BlockSpec((1, 1, seq, w), lambda b, g, i: (b, g, 0, 0))
    cmpblk = pl.BlockSpec((1, 1, nch, LANES), lambda b, g, i: (b, g, 0, 0))
    return pl.pallas_call(
        _attn_kernel,
        grid=(batch, N_KV_HEADS, seq // Q_BLOCK),
        in_specs=[pl.BlockSpec((1, Q_BLOCK, Q_PER_KV * HEAD_DIM), lambda b, g, i: (b, i, g)),
                  seqblk(2 * LANES), seqblk(LANES), seqblk(LANES), seqblk(LANES), cmpblk, cmpblk,
                  pl.BlockSpec(c2s.shape, lambda b, g, i: (0, 0)),
                  pl.BlockSpec((1, Q_BLOCK, LANES), lambda b, g, i: (b, i, g))],
        out_specs=pl.BlockSpec((1, Q_BLOCK, Q_PER_KV * HEAD_DIM), lambda b, g, i: (b, i, g)),
        out_shape=jax.ShapeDtypeStruct((batch, seq, D_ATT), F32),
        scratch_shapes=[pltpu.VMEM((N_SLC_LANES, Q_BLOCK), F32), pltpu.VMEM((SEL_STREAMS, ROWS, LANES), F32),
                        pltpu.VMEM((SEL_STREAMS, ROWS, LANES), F32)],
        compiler_params=pltpu.CompilerParams(dimension_semantics=("arbitrary", "arbitrary", "arbitrary"),
                                             vmem_limit_bytes=VMEM_LIMIT_BYTES),
        name="attn",
    )(q3, ksel, vs, kw, vw, kcmp, vcmp, c2s, gates3)


FF_CHUNKS = 2
FF_CHUNK = D_FF // FF_CHUNKS


def _post_kernel(x_ref, oatt_ref, mconv_ref, p_ref, onag_ref, wo_ref, lnf_ref, wupg_ref, wupu_ref, fcw_ref,
                 fcb_ref, wdn_ref, lnp_ref, wpg_ref, wpe_ref, out_ref,
                 h1_scr, xn_scr, acc_scr, gbuf, *, tiles_per_seq):
    tm = x_ref.shape[0]
    it = pl.program_id(0) % tiles_per_seq
    j = pl.program_id(1)

    @pl.when(j == 0)
    def _():
        mixed = jnp.concatenate([_rms(oatt_ref[...], onag_ref[...]).astype(BF16), mconv_ref[...]], axis=1)
        h1 = x_ref[...] + _dot(mixed, wo_ref[...])
        h1_scr[...] = h1
        xn_scr[...] = _rms(h1, lnf_ref[...]).astype(BF16)
        acc_scr[...] = jnp.zeros_like(acc_scr)

    @pl.when(it == 0)
    def _():
        gbuf[j, 0:SUBLANES, :] = jnp.zeros((SUBLANES, FF_CHUNK), F32)

    xn = xn_scr[...]
    gpre = _dot(xn, wupg_ref[...])
    up = _dot(xn, wupu_ref[...])
    gbuf[j, SUBLANES:SUBLANES + tm, :] = gpre
    gate = (fcw_ref[2:3, :] * gpre + fcw_ref[1:2, :] * gbuf[j, SUBLANES - 1:SUBLANES - 1 + tm, :]
            + fcw_ref[0:1, :] * gbuf[j, SUBLANES - 2:SUBLANES - 2 + tm, :]) + fcb_ref[...]
    gbuf[j, 0:SUBLANES, :] = gbuf[j, tm:tm + SUBLANES, :]
    act = (gate * jax.nn.sigmoid(gate) * up).astype(BF16)
    acc_scr[...] += _dot(act, wdn_ref[...])

    @pl.when(j == FF_CHUNKS - 1)
    def _():
        h2 = h1_scr[...] + acc_scr[...]
        xn2 = _rms(h2, lnp_ref[...]).astype(BF16)
        out_ref[...] = h2 + jax.nn.sigmoid(_dot(xn2, wpg_ref[...])) * _dot(p_ref[...].astype(BF16), wpe_ref[...])


def _post(x2, oatt, mconv, p2, onag, wo, lnf, wup, fcw, fcb, wdn, lnp, wpg, wpe, *, seq, tm):
    n = x2.shape[0]
    tps = seq // tm
    row = lambda w: pl.BlockSpec((tm, w), lambda i, j: (i, 0))
    full = lambda a: pl.BlockSpec(a.shape, lambda i, j: (0,) * a.ndim)
    return pl.pallas_call(
        functools.partial(_post_kernel, tiles_per_seq=tps),
        grid=(n // tm, FF_CHUNKS),
        in_specs=[row(D_MODEL), row(D_ATT), row(D_CONV), row(D_PLE), full(onag), full(wo), full(lnf),
                  pl.BlockSpec((D_MODEL, FF_CHUNK), lambda i, j: (0, j)),
                  pl.BlockSpec((D_MODEL, FF_CHUNK), lambda i, j: (0, FF_CHUNKS + j)),
                  pl.BlockSpec((CONV_TAPS, FF_CHUNK), lambda i, j: (0, j)),
                  pl.BlockSpec((1, FF_CHUNK), lambda i, j: (0, j)),
                  pl.BlockSpec((FF_CHUNK, D_MODEL), lambda i, j: (j, 0)),
                  full(lnp), full(wpg), full(wpe)],
        out_specs=row(D_MODEL),
        out_shape=jax.ShapeDtypeStruct((n, D_MODEL), F32),
        scratch_shapes=[pltpu.VMEM((tm, D_MODEL), F32), pltpu.VMEM((tm, D_MODEL), BF16),
                        pltpu.VMEM((tm, D_MODEL), F32), pltpu.VMEM((FF_CHUNKS, tm + SUBLANES, FF_CHUNK), F32)],
        compiler_params=pltpu.CompilerParams(dimension_semantics=("arbitrary", "arbitrary"),
                                             vmem_limit_bytes=VMEM_LIMIT_BYTES),
        name="post",
    )(x2, oatt, mconv, p2, onag, wo, lnf, wup, wup, fcw, fcb, wdn, lnp, wpg, wpe)


def _rope_tables(pos):
    half = ROT_DIM // 2
    inv_freq = ROPE_THETA ** (-jnp.arange(half, dtype=F32) * 2.0 / ROT_DIM)
    ang = pos.astype(F32)[:, None] * inv_freq[None, :]
    c, s = jnp.cos(ang), jnp.sin(ang)
    z = jnp.zeros((pos.shape[0], HEAD_DIM - ROT_DIM), F32)
    zh = jnp.zeros_like(c)
    cos = jnp.concatenate([c, c, z + 1.0], axis=1)
    sa = jnp.concatenate([-s, zh, z], axis=1)
    sb = jnp.concatenate([zh, s, z], axis=1)
    return tuple(jnp.concatenate([t, t], axis=1) for t in (cos, sa, sb))


def _block_diag_mean(width):
    idx = np.arange(width) // HEAD_DIM
    return jnp.asarray((idx[:, None] == idx[None, :]).astype(np.float32) / HEAD_DIM, BF16)


def _cmp_to_slc(nch):
    cs = CMP_STRIDE * np.arange(nch)[:, None]
    ss = SLC_BLOCK * np.arange(N_SLC_LANES)[None, :]
    ov = np.clip(np.minimum(cs + CMP_BLOCK, ss + SLC_BLOCK) - np.maximum(cs, ss), 0, None)
    return jnp.asarray(ov.astype(np.float32) / CMP_BLOCK, BF16)


def _layer(h, p_l, ln_mix_g, w_in, qn_g, kn_g, pe_k, pe_v, w_ck1, w_ck2, w_cv1, w_cv2, conv_w, on_att_g,
           on_conv_g, w_o, ln_ffn_g, w_up, ffn_conv_w, ffn_conv_b, w_down, ln_ple_g, w_pg, w_pe):
    batch, seq, _ = h.shape
    assert seq % Q_BLOCK == 0 and seq // SLC_BLOCK <= N_SLC_LANES and seq // SLC_BLOCK >= N_SELECT
    assert seq >= WINDOW + Q_BLOCK
    n = batch * seq
    nch = seq // CMP_STRIDE
    x2 = h.reshape(n, D_MODEL)
    row = lambda v: v.reshape(1, -1).astype(F32)

    o_q, o_kv, o_g, o_cv = 0, D_ATT, D_ATT + 6 * KV_W, D_ATT + 6 * KV_W + N_BRANCH * N_Q_HEADS
    wq = w_in[:, o_q:o_kv].astype(BF16)
    wkv = w_in[:, o_kv:o_g].astype(BF16)
    wcv = w_in[:, o_cv:].astype(BF16)
    per_g = Q_PER_KV * N_BRANCH
    wg = jnp.concatenate(
        [jnp.pad(w_in[:, o_g + g * per_g:o_g + (g + 1) * per_g], ((0, 0), (0, LANES - per_g)))
         for g in range(N_KV_HEADS)], axis=1).astype(BF16)
    cos, sa, sb = _rope_tables(jnp.arange(seq))
    bd512, bd128 = _block_diag_mean(D_ATT), _block_diag_mean(KV_W)
    tile_heads = lambda v, k: jnp.tile(v.astype(F32), k).reshape(1, -1)

    tm = 256
    q2, kc2, vc2, ksel, vs, kw, vw, gates2, mconv = _inproj(
        x2, row(ln_mix_g), wq, wkv, wg, wcv, tile_heads(qn_g, N_Q_HEADS),
        jnp.stack([jnp.tile(kn_g[1], N_KV_HEADS), jnp.tile(kn_g[2], N_KV_HEADS)]).astype(F32),
        bd512, bd128, cos, sa, sb, conv_w.astype(F32), row(on_conv_g), batch=batch, seq=seq, tm=tm)

    eye = jnp.eye(N_KV_HEADS, dtype=F32)
    half = CMP_BLOCK // 2

    def w1_parts(w1):
        wfull = jnp.einsum('ldh,gk->lgdkh', w1, eye).reshape(CMP_BLOCK, KV_W, N_KV_HEADS * CMP_HIDDEN)
        return (wfull[:half].reshape(half * KV_W, -1).astype(BF16),
                wfull[half:].reshape(half * KV_W, -1).astype(BF16))

    def pe_parts(pe):
        pf = jnp.broadcast_to(pe[:, None, :], (CMP_BLOCK, N_KV_HEADS, HEAD_DIM)).astype(F32)
        return pf[:half].reshape(1, -1), pf[half:].reshape(1, -1)

    w2bd = lambda w2: jnp.einsum('hd,gk->ghkd', w2, eye).reshape(N_KV_HEADS * CMP_HIDDEN, KV_W).astype(BF16)
    w1ak, w1bk = w1_parts(w_ck1)
    w1av, w1bv = w1_parts(w_cv1)
    peak, pebk = pe_parts(pe_k)
    peav, pebv = pe_parts(pe_v)
    ccos, csa, csb = _rope_tables(CMP_STRIDE * jnp.arange(nch) + CMP_BLOCK - 1)
    kcmp, vcmp = _compress(kc2.reshape(batch, nch, CMP_STRIDE * KV_W), vc2.reshape(batch, nch, CMP_STRIDE * KV_W),
                           w1ak, w1bk, w1av, w1bv, peak, pebk, peav, pebv, w2bd(w_ck2), w2bd(w_cv2),
                           tile_heads(kn_g[0], N_KV_HEADS), bd128, ccos, csa, csb)

    oatt = _attn(q2.reshape(batch, seq, D_ATT), ksel, vs, kw, vw, kcmp, vcmp, _cmp_to_slc(nch),
                 gates2.reshape(batch, seq, N_KV_HEADS * LANES))

    out = _post(x2, oatt.reshape(n, D_ATT), mconv, p_l.reshape(n, D_PLE), row(on_att_g), w_o.astype(BF16),
                row(ln_ffn_g), w_up.astype(BF16), ffn_conv_w.astype(F32), row(ffn_conv_b), w_down.astype(BF16),
                row(ln_ple_g), w_pg.astype(BF16), w_pe.astype(BF16), seq=seq, tm=tm)
    return out.reshape(batch, seq, D_MODEL)


def kernel(x, p, ln_mix_g, w_in, qn_g, kn_g, pe_k, pe_v, w_ck1, w_ck2, w_cv1, w_cv2, conv_w, on_att_g,
           on_conv_g, w_o, ln_ffn_g, w_up, ffn_conv_w, ffn_conv_b, w_down, ln_ple_g, w_pg, w_pe):
    h = x
    for i in range(p.shape[0]):
        h = _layer(h, p[i], ln_mix_g[i], w_in[i], qn_g[i], kn_g[i], pe_k[i], pe_v[i], w_ck1[i], w_ck2[i],
                   w_cv1[i], w_cv2[i], conv_w[i], on_att_g[i], on_conv_g[i], w_o[i], ln_ffn_g[i], w_up[i],
                   ffn_conv_w[i], ffn_conv_b[i], w_down[i], ln_ple_g[i], w_pg[i], w_pe[i])
    return h
```

```python
import functools

import jax
import jax.numpy as jnp
import numpy as np
from jax import lax
from jax.experimental import pallas as pl
from jax.experimental.pallas import tpu as pltpu

D_MODEL = 1024
HEAD_DIM = 64
N_Q_HEADS = 8
N_KV_HEADS = 2
Q_PER_KV = N_Q_HEADS // N_KV_HEADS
D_ATT = N_Q_HEADS * HEAD_DIM
D_CONV = D_MODEL - D_ATT
KV_W = N_KV_HEADS * HEAD_DIM
N_BRANCH = 3
CONV_TAPS = 3
ROT_DIM = HEAD_DIM // 4
ROPE_THETA = 500000.0
CMP_BLOCK = 32
CMP_STRIDE = 16
CMP_HIDDEN = 256
SLC_BLOCK = 64
N_SELECT = 16
WINDOW = 512
Q_BLOCK = 128
D_FF = 2816
D_PLE = 256
EPS = 1e-6
NEG = -1e30

LANES = 128
SUBLANES = 8
N_SLC_LANES = LANES
VMEM_LIMIT_BYTES = 56 * 1024 * 1024

F32 = jnp.float32
BF16 = jnp.bfloat16


def _dot(a, b):
    return jnp.dot(a, b, preferred_element_type=F32)


def _rms(x, g):
    return x * lax.rsqrt(jnp.mean(x * x, axis=-1, keepdims=True) + EPS) * g


def _head_rms_rope(x, g, bd, cos, sa, sb):
    w = x.shape[-1]
    msq = _dot((x * x).astype(BF16), bd)
    xn = x * lax.rsqrt(msq + EPS) * g
    return xn * cos + pltpu.roll(xn, w - ROT_DIM // 2, 1) * sa + pltpu.roll(xn, ROT_DIM // 2, 1) * sb


def _dup_halves(x):
    r = pltpu.roll(x, HEAD_DIM, 1)
    lane = lax.broadcasted_iota(jnp.int32, x.shape, 1)
    lo = lane < HEAD_DIM
    return jnp.where(lo, x, r), jnp.where(lo, r, x)


def _inproj_kernel(x_ref, lng_ref, wq_ref, wkv_ref, wg_ref, wcv_ref, qng_ref, kng_ref, bd512_ref, bd128_ref,
                   cos_ref, sa_ref, sb_ref, convw_ref, oncg_ref,
                   q_out, kc_out, vc_out, ksel_out, vs_out, kw_out, vw_out, gate_out, mconv_out,
                   zbuf, *, tiles_per_seq):
    tm = x_ref.shape[0]
    it = pl.program_id(0) % tiles_per_seq
    x = x_ref[...]
    xn = _rms(x, lng_ref[...]).astype(BF16)

    cos, sa, sb = cos_ref[...], sa_ref[...], sb_ref[...]
    cos4, sa4, sb4 = (jnp.concatenate([t] * 4, axis=1) for t in (cos, sa, sb))
    q = _dot(xn, wq_ref[...])
    qr = _head_rms_rope(q, qng_ref[...], bd512_ref[...], cos4, sa4, sb4)
    q_out[...] = (qr * (HEAD_DIM ** -0.5)).astype(BF16)

    kv = _dot(xn, wkv_ref[...])
    kc_out[...] = kv[:, 0 * KV_W:1 * KV_W]
    vc_out[...] = kv[:, 1 * KV_W:2 * KV_W]
    ks = _head_rms_rope(kv[:, 2 * KV_W:3 * KV_W], kng_ref[0:1, :], bd128_ref[...], cos, sa, sb)
    kw = _head_rms_rope(kv[:, 4 * KV_W:5 * KV_W], kng_ref[1:2, :], bd128_ref[...], cos, sa, sb)
    vs = kv[:, 3 * KV_W:4 * KV_W]
    vw = kv[:, 5 * KV_W:6 * KV_W]
    tpos = it * tm + lax.broadcasted_iota(jnp.int32, (tm, N_SLC_LANES), 0)
    blk = lax.broadcasted_iota(jnp.int32, (tm, N_SLC_LANES), 1)
    onehot = jnp.where(lax.shift_right_logical(tpos, 6) == blk, 1.0, 0.0).astype(BF16)
    ks_d, vs_d, kw_d, vw_d = (_dup_halves(t) for t in (ks, vs, kw, vw))
    vrow = lax.broadcasted_iota(jnp.int32, (LANES, tm), 0)
    for g in range(N_KV_HEADS):
        ksel_out[0, g] = jnp.concatenate([ks_d[g].astype(BF16), onehot], axis=1)
        kw_out[0, g] = kw_d[g].astype(BF16)
        vs_out[0, g, 0] = jnp.where(vrow < HEAD_DIM, vs_d[g].T, 1.0).astype(BF16)
        vwt = jnp.where(vrow < HEAD_DIM, vw_d[g].T, 1.0).astype(BF16)
        for c in range(tm // LANES):
            vw_out[0, g, c] = vwt[:, c * LANES:(c + 1) * LANES]

    gate_out[...] = jax.nn.sigmoid(_dot(xn, wg_ref[...]))

    cv = _dot(xn, wcv_ref[...])
    cb, cc, cx = cv[:, :D_CONV], cv[:, D_CONV:2 * D_CONV], cv[:, 2 * D_CONV:]
    z = cc * cx

    @pl.when(it == 0)
    def _():
        zbuf[0:SUBLANES, :] = jnp.zeros((SUBLANES, D_CONV), F32)

    zbuf[SUBLANES:SUBLANES + tm, :] = z
    y = (convw_ref[2:3, :] * z + convw_ref[1:2, :] * zbuf[SUBLANES - 1:SUBLANES - 1 + tm, :]
         + convw_ref[0:1, :] * zbuf[SUBLANES - 2:SUBLANES - 2 + tm, :])
    zbuf[0:SUBLANES, :] = zbuf[tm:tm + SUBLANES, :]
    mconv_out[...] = _rms(cb * y, oncg_ref[...]).astype(BF16)


def _inproj(x2, lng, wq, wkv, wg, wcv, qng, kng, bd512, bd128, cos, sa, sb, convw, oncg, *, batch, seq, tm):
    n = batch * seq
    tps = seq // tm
    row = lambda w: pl.BlockSpec((tm, w), lambda i: (i, 0))
    full = lambda a: pl.BlockSpec(a.shape, lambda i: (0,) * a.ndim)
    tab = pl.BlockSpec((tm, LANES), lambda i: (i % tps, 0))
    grp = lambda w: pl.BlockSpec((1, N_KV_HEADS, tm, w), lambda i: (i // tps, 0, i % tps, 0))
    gshape = lambda w: jax.ShapeDtypeStruct((batch, N_KV_HEADS, seq, w), BF16)
    assert tm == SEL_TILE
    vtile = lambda keys: pl.BlockSpec((1, N_KV_HEADS, tm // keys, LANES, keys),
                                      lambda i: (i // tps, 0, i % tps, 0, 0))
    vshape = lambda keys: jax.ShapeDtypeStruct((batch, N_KV_HEADS, seq // keys, LANES, keys), BF16)
    return pl.pallas_call(
        functools.partial(_inproj_kernel, tiles_per_seq=tps),
        grid=(n // tm,),
        in_specs=[row(D_MODEL), full(lng), full(wq), full(wkv), full(wg), full(wcv), full(qng), full(kng),
                  full(bd512), full(bd128), tab, tab, tab, full(convw), full(oncg)],
        out_specs=[row(D_ATT), row(KV_W), row(KV_W), grp(2 * LANES), vtile(SEL_TILE), grp(LANES), vtile(LANES),
                   row(N_KV_HEADS * LANES), row(D_CONV)],
        out_shape=[jax.ShapeDtypeStruct((n, D_ATT), BF16), jax.ShapeDtypeStruct((n, KV_W), F32),
                   jax.ShapeDtypeStruct((n, KV_W), F32), gshape(2 * LANES), vshape(SEL_TILE), gshape(LANES),
                   vshape(LANES), jax.ShapeDtypeStruct((n, N_KV_HEADS * LANES), F32),
                   jax.ShapeDtypeStruct((n, D_CONV), BF16)],
        scratch_shapes=[pltpu.VMEM((tm + SUBLANES, D_CONV), F32)],
        compiler_params=pltpu.CompilerParams(dimension_semantics=("arbitrary",),
                                             vmem_limit_bytes=VMEM_LIMIT_BYTES),
        name="inproj",
    )(x2, lng, wq, wkv, wg, wcv, qng, kng, bd512, bd128, cos, sa, sb, convw, oncg)


def _gelu_tanh(x):
    return 0.5 * x * (1.0 + jnp.tanh(np.sqrt(2.0 / np.pi).astype(np.float32) * (x + 0.044715 * (x * x * x))))


def _compress_kernel(kc_ref, vc_ref, w1ak_ref, w1bk_ref, w1av_ref, w1bv_ref, peak_ref, pebk_ref, peav_ref,
                     pebv_ref, w2k_ref, w2v_ref, kng_ref, bd128_ref, cos_ref, sa_ref, sb_ref,
                     kcmp_out, vcmp_out):
    nch = kc_ref.shape[1]

    def compress(x, w1a, w1b, pea, peb, w2):
        a = _dot((x + pea).astype(BF16), w1a)
        b = _dot((x + peb).astype(BF16), w1b)
        hid = _gelu_tanh(a + pltpu.roll(b, nch - 1, 0))
        return _dot(hid.astype(BF16), w2)

    kc = compress(kc_ref[0], w1ak_ref[...], w1bk_ref[...], peak_ref[...], pebk_ref[...], w2k_ref[...])
    vc = compress(vc_ref[0], w1av_ref[...], w1bv_ref[...], peav_ref[...], pebv_ref[...], w2v_ref[...])
    kc = _head_rms_rope(kc, kng_ref[...], bd128_ref[...], cos_ref[...], sa_ref[...], sb_ref[...])
    kd, vd = _dup_halves(kc), _dup_halves(vc)
    for g in range(N_KV_HEADS):
        kcmp_out[0, g] = kd[g].astype(BF16)
        vcmp_out[0, g] = vd[g].T.astype(BF16)


def _compress(kc3, vc3, w1ak, w1bk, w1av, w1bv, peak, pebk, peav, pebv, w2k, w2v, kng0, bd128, cos, sa, sb):
    batch, nch, wide = kc3.shape
    full = lambda a: pl.BlockSpec(a.shape, lambda b: (0,) * a.ndim)
    tok = pl.BlockSpec((1, nch, wide), lambda b: (b, 0, 0))
    kout = pl.BlockSpec((1, N_KV_HEADS, nch, LANES), lambda b: (b, 0, 0, 0))
    vout = pl.BlockSpec((1, N_KV_HEADS, LANES, nch), lambda b: (b, 0, 0, 0))
    consts = (w1ak, w1bk, w1av, w1bv, peak, pebk, peav, pebv, w2k, w2v, kng0, bd128, cos, sa, sb)
    return pl.pallas_call(
        _compress_kernel,
        grid=(batch,),
        in_specs=[tok, tok] + [full(a) for a in consts],
        out_specs=[kout, vout],
        out_shape=[jax.ShapeDtypeStruct((batch, N_KV_HEADS, nch, LANES), BF16),
                   jax.ShapeDtypeStruct((batch, N_KV_HEADS, LANES, nch), BF16)],
        compiler_params=pltpu.CompilerParams(dimension_semantics=("arbitrary",),
                                             vmem_limit_bytes=VMEM_LIMIT_BYTES),
        name="compress",
    )(kc3, vc3, *consts)


SEL_TILE = 256
COLS = Q_PER_KV * Q_BLOCK


def _split3(x):
    hi = x.astype(BF16)
    r1 = x - hi.astype(F32)
    mid = r1.astype(BF16)
    lo = (r1 - mid.astype(F32)).astype(BF16)
    return hi, mid, lo


def _normalize_dup(acc):
    o = acc[:HEAD_DIM] / acc[HEAD_DIM:HEAD_DIM + 1]
    return jnp.concatenate([o, o], axis=0)


def _attn_kernel(q_ref, ksel_ref, vst_ref, kw_ref, vwt_ref, kcmp_ref, vcmpt_ref, c2st_ref, gate_ref, o_ref,
                 impt_scr, s0, s1, p0, p1, a0, a1, m_scr, acc_scr):
    i = pl.program_id(2)
    start = i * Q_BLOCK
    nch = kcmp_ref.shape[2]
    last_tile = ksel_ref.shape[2] // SEL_TILE - 1

    lane = lax.broadcasted_iota(jnp.int32, (Q_BLOCK, LANES), 1)
    lo_half = lane < HEAD_DIM
    qblk = q_ref[0]
    qt_parts = []
    for r in range(Q_PER_KV):
        pair = qblk[:, LANES * (r // 2):LANES * (r // 2 + 1)].astype(F32)
        qt_parts.append(jnp.where(lo_half if r % 2 == 0 else ~lo_half, pair, 0.0).T)
    qt = jnp.concatenate(qt_parts, axis=1).astype(BF16)
    tq = start + (lax.broadcasted_iota(jnp.int32, (1, COLS), 1) & (Q_BLOCK - 1))

    s_c = _dot(kcmp_ref[0, 0], qt)
    nrow = lax.broadcasted_iota(jnp.int32, (nch, COLS), 0)
    valid_c = (CMP_STRIDE * nrow + (CMP_BLOCK - 1)) <= tq
    s_c = jnp.where(valid_c, s_c, NEG)
    m_c = jnp.max(s_c, axis=0, keepdims=True)
    e_c = jnp.where(valid_c, jnp.exp(s_c - m_c), 0.0)
    l_c = jnp.sum(e_c, axis=0, keepdims=True)
    p_c = e_c * (1.0 / jnp.maximum(l_c, 1e-30))
    o_c = _dot(vcmpt_ref[0, 0], p_c.astype(BF16))

    psum = sum(p_c[:, r * Q_BLOCK:(r + 1) * Q_BLOCK] for r in range(Q_PER_KV))
    c2st = c2st_ref[...]
    impt = sum(_dot(c2st, t) for t in _split3(psum))
    srow = lax.broadcasted_iota(jnp.int32, (N_SLC_LANES, Q_BLOCK), 0)
    tqq = start + lax.broadcasted_iota(jnp.int32, (N_SLC_LANES, Q_BLOCK), 1)
    cur = lax.shift_right_logical(tqq, 6)
    forced = (srow == 0) | (srow == cur) | (srow == cur - 1)
    future = srow * SLC_BLOCK > tqq
    impt = jnp.where(forced, 1e9, jnp.where(future, -1e9, impt))
    impt_scr[...] = impt
    sub = lax.broadcasted_iota(jnp.int32, (SUBLANES, Q_BLOCK), 0)
    cnt = jnp.zeros((N_SLC_LANES, Q_BLOCK), jnp.int32)
    for sp in range(N_SLC_LANES):
        lo, hi = sp - sp % SUBLANES, sp - sp % SUBLANES + SUBLANES
        rowb = jnp.broadcast_to(impt_scr[sp:sp + 1, :], (N_SLC_LANES, Q_BLOCK))
        mid_r, mid_x = rowb[lo:hi], impt[lo:hi]
        mid = (mid_r > mid_x) | ((mid_r == mid_x) & (sub > sp % SUBLANES))
        parts = [jnp.where(rowb[:lo] > impt[:lo], 1, 0)] if lo else []
        parts.append(jnp.where(mid, 1, 0))
        if hi < N_SLC_LANES:
            parts.append(jnp.where(rowb[hi:] >= impt[hi:], 1, 0))
        cnt = cnt + jnp.concatenate(parts, axis=0)
    bias = jnp.where(cnt < N_SELECT, 0.0, NEG)
    n_main = start // SEL_TILE
    bias_main = jnp.where(srow >= n_main * (SEL_TILE // SLC_BLOCK), NEG, bias)
    widen = lambda b: jnp.concatenate([b.astype(BF16)] * Q_PER_KV, axis=1)
    qt_tail = jnp.concatenate([qt, widen(bias)], axis=0)
    qt_main = jnp.concatenate([qt, widen(bias_main)], axis=0)

    kt = pl.multiple_of(n_main * SEL_TILE, SEL_TILE)
    s_t = _dot(ksel_ref[0, 0, pl.ds(kt, SEL_TILE), :], qt_tail)
    krow = lax.broadcasted_iota(jnp.int32, (SEL_TILE, COLS), 0)
    s_t = jnp.where(kt + krow <= tq, s_t, NEG)
    m_t = jnp.max(s_t, axis=0, keepdims=True)
    acc_t = _dot(vst_ref[0, 0, n_main], jnp.exp(s_t - m_t).astype(BF16))

    def scores(j, s_ref):
        k0 = pl.multiple_of(jnp.minimum(j, last_tile) * SEL_TILE, SEL_TILE)
        s_ref[...] = _dot(ksel_ref[0, 0, pl.ds(k0, SEL_TILE), :], qt_main)

    def softmax(s_ref, p_ref, a_ref):
        s = s_ref[...]
        m_prev = m_scr[0:1, :]
        m_new = jnp.maximum(m_prev, jnp.max(s, axis=0, keepdims=True))
        a_ref[...] = jnp.broadcast_to(jnp.exp(m_prev - m_new), a_ref.shape)
        p_ref[...] = jnp.exp(s - m_new).astype(BF16)
        m_scr[...] = jnp.broadcast_to(m_new, m_scr.shape)

    def values(j, p_ref, a_ref):
        acc_scr[...] = a_ref[0:1, :] * acc_scr[...] + _dot(vst_ref[0, 0, jnp.minimum(j, last_tile)], p_ref[...])

    m_scr[...] = jnp.full(m_scr.shape, NEG, F32)
    acc_scr[...] = jnp.zeros(acc_scr.shape, F32)
    scores(0, s0)
    softmax(s0, p0, a0)
    scores(1, s1)

    def sweep(t, carry):
        j = 2 * t
        values(j, p0, a0)
        softmax(s1, p1, a1)
        scores(j + 2, s0)
        values(j + 1, p1, a1)
        softmax(s0, p0, a0)
        scores(j + 3, s1)
        return carry

    lax.fori_loop(0, (n_main + 1) // 2, sweep, 0)
    m_p = m_scr[0:1, :]
    m_tot = jnp.maximum(m_p, m_t)
    o_s = _normalize_dup(jnp.exp(m_p - m_tot) * acc_scr[...] + jnp.exp(m_t - m_tot) * acc_t)

    wk = WINDOW + Q_BLOCK
    base = pl.multiple_of(jnp.maximum(start - WINDOW, 0), Q_BLOCK)
    s_w = _dot(kw_ref[0, 0, pl.ds(base, wk), :], qt)
    kpos = base + lax.broadcasted_iota(jnp.int32, (wk, COLS), 0)
    s_w = jnp.where((kpos <= tq) & (kpos > tq - WINDOW), s_w, NEG)
    p_w = jnp.exp(s_w - jnp.max(s_w, axis=0, keepdims=True)).astype(BF16)
    wtile = base // LANES
    vw = jnp.concatenate([vwt_ref[0, 0, wtile + c] for c in range(wk // LANES)], axis=1)
    o_w = _normalize_dup(_dot(vw, p_w))

    gt = gate_ref[0].T
    grow = lambda b: jnp.concatenate([gt[r * N_BRANCH + b:r * N_BRANCH + b + 1, :] for r in range(Q_PER_KV)],
                                     axis=1)
    comb = grow(0) * o_c + grow(1) * o_s + grow(2) * o_w
    outs = [comb[:, r * Q_BLOCK:(r + 1) * Q_BLOCK].T for r in range(Q_PER_KV)]
    o_ref[0] = jnp.concatenate([jnp.where(lo_half, outs[0], outs[1]), jnp.where(lo_half, outs[2], outs[3])],
                               axis=1)


def _attn(q3, ksel, vst, kw, vwt, kcmp, vcmpt, c2st, gates3):
    batch, seq, _ = q3.shape
    per_group = lambda a: pl.BlockSpec((1, 1) + a.shape[2:], lambda b, g, i: (b, g) + (0,) * (a.ndim - 2))
    return pl.pallas_call(
        _attn_kernel,
        grid=(batch, N_KV_HEADS, seq // Q_BLOCK),
        in_specs=[pl.BlockSpec((1, Q_BLOCK, Q_PER_KV * HEAD_DIM), lambda b, g, i: (b, i, g)),
                  per_group(ksel), per_group(vst), per_group(kw), per_group(vwt), per_group(kcmp),
                  per_group(vcmpt), pl.BlockSpec(c2st.shape, lambda b, g, i: (0, 0)),
                  pl.BlockSpec((1, Q_BLOCK, LANES), lambda b, g, i: (b, i, g))],
        out_specs=pl.BlockSpec((1, Q_BLOCK, Q_PER_KV * HEAD_DIM), lambda b, g, i: (b, i, g)),
        out_shape=jax.ShapeDtypeStruct((batch, seq, D_ATT), F32),
        scratch_shapes=[pltpu.VMEM((N_SLC_LANES, Q_BLOCK), F32),
                        pltpu.VMEM((SEL_TILE, COLS), F32), pltpu.VMEM((SEL_TILE, COLS), F32),
                        pltpu.VMEM((SEL_TILE, COLS), BF16), pltpu.VMEM((SEL_TILE, COLS), BF16),
                        pltpu.VMEM((SUBLANES, COLS), F32), pltpu.VMEM((SUBLANES, COLS), F32),
                        pltpu.VMEM((SUBLANES, COLS), F32), pltpu.VMEM((LANES, COLS), F32)],
        compiler_params=pltpu.CompilerParams(dimension_semantics=("arbitrary", "arbitrary", "arbitrary"),
                                             vmem_limit_bytes=VMEM_LIMIT_BYTES),
        name="attn",
    )(q3, ksel, vst, kw, vwt, kcmp, vcmpt, c2st, gates3)


FF_CHUNKS = 2
FF_CHUNK = D_FF // FF_CHUNKS


def _post_kernel(x_ref, oatt_ref, mconv_ref, p_ref, onag_ref, wo_ref, lnf_ref, wupg_ref, wupu_ref, fcw_ref,
                 fcb_ref, wdn_ref, lnp_ref, wpg_ref, wpe_ref, out_ref,
                 h1_scr, xn_scr, acc_scr, gbuf, *, tiles_per_seq):
    tm = x_ref.shape[0]
    it = pl.program_id(0) % tiles_per_seq
    j = pl.program_id(1)

    @pl.when(j == 0)
    def _():
        mixed = jnp.concatenate([_rms(oatt_ref[...], onag_ref[...]).astype(BF16), mconv_ref[...]], axis=1)
        h1 = x_ref[...] + _dot(mixed, wo_ref[...])
        h1_scr[...] = h1
        xn_scr[...] = _rms(h1, lnf_ref[...]).astype(BF16)
        acc_scr[...] = jnp.zeros_like(acc_scr)

    @pl.when(it == 0)
    def _():
        gbuf[j, 0:SUBLANES, :] = jnp.zeros((SUBLANES, FF_CHUNK), F32)

    xn = xn_scr[...]
    gpre = _dot(xn, wupg_ref[...])
    up = _dot(xn, wupu_ref[...])
    gbuf[j, SUBLANES:SUBLANES + tm, :] = gpre
    gate = (fcw_ref[2:3, :] * gpre + fcw_ref[1:2, :] * gbuf[j, SUBLANES - 1:SUBLANES - 1 + tm, :]
            + fcw_ref[0:1, :] * gbuf[j, SUBLANES - 2:SUBLANES - 2 + tm, :]) + fcb_ref[...]
    gbuf[j, 0:SUBLANES, :] = gbuf[j, tm:tm + SUBLANES, :]
    act = (gate * jax.nn.sigmoid(gate) * up).astype(BF16)
    acc_scr[...] += _dot(act, wdn_ref[...])

    @pl.when(j == FF_CHUNKS - 1)
    def _():
        h2 = h1_scr[...] + acc_scr[...]
        xn2 = _rms(h2, lnp_ref[...]).astype(BF16)
        out_ref[...] = h2 + jax.nn.sigmoid(_dot(xn2, wpg_ref[...])) * _dot(p_ref[...].astype(BF16), wpe_ref[...])


def _post(x2, oatt, mconv, p2, onag, wo, lnf, wup, fcw, fcb, wdn, lnp, wpg, wpe, *, seq, tm):
    n = x2.shape[0]
    tps = seq // tm
    row = lambda w: pl.BlockSpec((tm, w), lambda i, j: (i, 0))
    full = lambda a: pl.BlockSpec(a.shape, lambda i, j: (0,) * a.ndim)
    return pl.pallas_call(
        functools.partial(_post_kernel, tiles_per_seq=tps),
        grid=(n // tm, FF_CHUNKS),
        in_specs=[row(D_MODEL), row(D_ATT), row(D_CONV), row(D_PLE), full(onag), full(wo), full(lnf),
                  pl.BlockSpec((D_MODEL, FF_CHUNK), lambda i, j: (0, j)),
                  pl.BlockSpec((D_MODEL, FF_CHUNK), lambda i, j: (0, FF_CHUNKS + j)),
                  pl.BlockSpec((CONV_TAPS, FF_CHUNK), lambda i, j: (0, j)),
                  pl.BlockSpec((1, FF_CHUNK), lambda i, j: (0, j)),
                  pl.BlockSpec((FF_CHUNK, D_MODEL), lambda i, j: (j, 0)),
                  full(lnp), full(wpg), full(wpe)],
        out_specs=row(D_MODEL),
        out_shape=jax.ShapeDtypeStruct((n, D_MODEL), F32),
        scratch_shapes=[pltpu.VMEM((tm, D_MODEL), F32), pltpu.VMEM((tm, D_MODEL), BF16),
                        pltpu.VMEM((tm, D_MODEL), F32), pltpu.VMEM((FF_CHUNKS, tm + SUBLANES, FF_CHUNK), F32)],
        compiler_params=pltpu.CompilerParams(dimension_semantics=("arbitrary", "arbitrary"),
                                             vmem_limit_bytes=VMEM_LIMIT_BYTES),
        name="post",
    )(x2, oatt, mconv, p2, onag, wo, lnf, wup, wup, fcw, fcb, wdn, lnp, wpg, wpe)


def _rope_tables(pos):
    half = ROT_DIM // 2
    inv_freq = ROPE_THETA ** (-jnp.arange(half, dtype=F32) * 2.0 / ROT_DIM)
    ang = pos.astype(F32)[:, None] * inv_freq[None, :]
    c, s = jnp.cos(ang), jnp.sin(ang)
    z = jnp.zeros((pos.shape[0], HEAD_DIM - ROT_DIM), F32)
    zh = jnp.zeros_like(c)
    cos = jnp.concatenate([c, c, z + 1.0], axis=1)
    sa = jnp.concatenate([-s, zh, z], axis=1)
    sb = jnp.concatenate([zh, s, z], axis=1)
    return tuple(jnp.concatenate([t, t], axis=1) for t in (cos, sa, sb))


def _block_diag_mean(width):
    idx = np.arange(width) // HEAD_DIM
    return jnp.asarray((idx[:, None] == idx[None, :]).astype(np.float32) / HEAD_DIM, BF16)


def _cmp_to_slc_t(nch):
    cs = CMP_STRIDE * np.arange(nch)[None, :]
    ss = SLC_BLOCK * np.arange(N_SLC_LANES)[:, None]
    ov = np.clip(np.minimum(cs + CMP_BLOCK, ss + SLC_BLOCK) - np.maximum(cs, ss), 0, None)
    return jnp.asarray(ov.astype(np.float32) / CMP_BLOCK, BF16)


def _layer(h, p_l, ln_mix_g, w_in, qn_g, kn_g, pe_k, pe_v, w_ck1, w_ck2, w_cv1, w_cv2, conv_w, on_att_g,
           on_conv_g, w_o, ln_ffn_g, w_up, ffn_conv_w, ffn_conv_b, w_down, ln_ple_g, w_pg, w_pe):
    batch, seq, _ = h.shape
    assert seq % SEL_TILE == 0 and seq // SLC_BLOCK <= N_SLC_LANES and seq // SLC_BLOCK >= N_SELECT
    assert seq >= WINDOW + Q_BLOCK
    n = batch * seq
    nch = seq // CMP_STRIDE
    x2 = h.reshape(n, D_MODEL)
    row = lambda v: v.reshape(1, -1).astype(F32)

    o_q, o_kv, o_g, o_cv = 0, D_ATT, D_ATT + 6 * KV_W, D_ATT + 6 * KV_W + N_BRANCH * N_Q_HEADS
    wq = w_in[:, o_q:o_kv].astype(BF16)
    wkv = w_in[:, o_kv:o_g].astype(BF16)
    wcv = w_in[:, o_cv:].astype(BF16)
    per_g = Q_PER_KV * N_BRANCH
    wg = jnp.concatenate(
        [jnp.pad(w_in[:, o_g + g * per_g:o_g + (g + 1) * per_g], ((0, 0), (0, LANES - per_g)))
         for g in range(N_KV_HEADS)], axis=1).astype(BF16)
    cos, sa, sb = _rope_tables(jnp.arange(seq))
    bd512, bd128 = _block_diag_mean(D_ATT), _block_diag_mean(KV_W)
    tile_heads = lambda v, k: jnp.tile(v.astype(F32), k).reshape(1, -1)

    tm = SEL_TILE
    q2, kc2, vc2, ksel, vst, kw, vwt, gates2, mconv = _inproj(
        x2, row(ln_mix_g), wq, wkv, wg, wcv, tile_heads(qn_g, N_Q_HEADS),
        jnp.stack([jnp.tile(kn_g[1], N_KV_HEADS), jnp.tile(kn_g[2], N_KV_HEADS)]).astype(F32),
        bd512, bd128, cos, sa, sb, conv_w.astype(F32), row(on_conv_g), batch=batch, seq=seq, tm=tm)

    eye = jnp.eye(N_KV_HEADS, dtype=F32)
    half = CMP_BLOCK // 2

    def w1_parts(w1):
        wfull = jnp.einsum('ldh,gk->lgdkh', w1, eye).reshape(CMP_BLOCK, KV_W, N_KV_HEADS * CMP_HIDDEN)
        return (wfull[:half].reshape(half * KV_W, -1).astype(BF16),
                wfull[half:].reshape(half * KV_W, -1).astype(BF16))

    def pe_parts(pe):
        pf = jnp.broadcast_to(pe[:, None, :], (CMP_BLOCK, N_KV_HEADS, HEAD_DIM)).astype(F32)
        return pf[:half].reshape(1, -1), pf[half:].reshape(1, -1)

    w2bd = lambda w2: jnp.einsum('hd,gk->ghkd', w2, eye).reshape(N_KV_HEADS * CMP_HIDDEN, KV_W).astype(BF16)
    w1ak, w1bk = w1_parts(w_ck1)
    w1av, w1bv = w1_parts(w_cv1)
    peak, pebk = pe_parts(pe_k)
    peav, pebv = pe_parts(pe_v)
    ccos, csa, csb = _rope_tables(CMP_STRIDE * jnp.arange(nch) + CMP_BLOCK - 1)
    kcmp, vcmpt = _compress(kc2.reshape(batch, nch, CMP_STRIDE * KV_W), vc2.reshape(batch, nch, CMP_STRIDE * KV_W),
                            w1ak, w1bk, w1av, w1bv, peak, pebk, peav, pebv, w2bd(w_ck2), w2bd(w_cv2),
                            tile_heads(kn_g[0], N_KV_HEADS), bd128, ccos, csa, csb)

    oatt = _attn(q2.reshape(batch, seq, D_ATT), ksel, vst, kw, vwt, kcmp, vcmpt, _cmp_to_slc_t(nch),
                 gates2.reshape(batch, seq, N_KV_HEADS * LANES))

    out = _post(x2, oatt.reshape(n, D_ATT), mconv, p_l.reshape(n, D_PLE), row(on_att_g), w_o.astype(BF16),
                row(ln_ffn_g), w_up.astype(BF16), ffn_conv_w.astype(F32), row(ffn_conv_b), w_down.astype(BF16),
                row(ln_ple_g), w_pg.astype(BF16), w_pe.astype(BF16), seq=seq, tm=tm)
    return out.reshape(batch, seq, D_MODEL)


def kernel(x, p, ln_mix_g, w_in, qn_g, kn_g, pe_k, pe_v, w_ck1, w_ck2, w_cv1, w_cv2, conv_w, on_att_g,
           on_conv_g, w_o, ln_ffn_g, w_up, ffn_conv_w, ffn_conv_b, w_down, ln_ple_g, w_pg, w_pe):
    h = x
    for i in range(p.shape[0]):
        h = _layer(h, p[i], ln_mix_g[i], w_in[i], qn_g[i], kn_g[i], pe_k[i], pe_v[i], w_ck1[i], w_ck2[i],
                   w_cv1[i], w_cv2[i], conv_w[i], on_att_g[i], on_conv_g[i], w_o[i], ln_ffn_g[i], w_up[i],
                   ffn_conv_w[i], ffn_conv_b[i], w_down[i], ln_ple_g[i], w_pg[i], w_pe[i])
    return h
```

```python
import functools

import jax
import jax.numpy as jnp
import numpy as np
from jax import lax
from jax.experimental import pallas as pl
from jax.experimental.pallas import tpu as pltpu

D_MODEL = 1024
HEAD_DIM = 64
N_Q_HEADS = 8
N_KV_HEADS = 2
Q_PER_KV = N_Q_HEADS // N_KV_HEADS
D_ATT = N_Q_HEADS * HEAD_DIM
D_CONV = D_MODEL - D_ATT
KV_W = N_KV_HEADS * HEAD_DIM
N_BRANCH = 3
CONV_TAPS = 3
ROT_DIM = HEAD_DIM // 4
ROPE_THETA = 500000.0
CMP_BLOCK = 32
CMP_STRIDE = 16
CMP_HIDDEN = 256
SLC_BLOCK = 64
N_SELECT = 16
WINDOW = 512
Q_BLOCK = 128
D_FF = 2816
D_PLE = 256
EPS = 1e-6
NEG = -1e30
INT32_MIN = -2 ** 31
GATE_ROWS = 16
SLC_SHIFT = SLC_BLOCK.bit_length() - 1
CMP_SHIFT = CMP_STRIDE.bit_length() - 1
assert 1 << SLC_SHIFT == SLC_BLOCK and 1 << CMP_SHIFT == CMP_STRIDE

LANES = 128
SUBLANES = 8
N_SLC_LANES = LANES
VMEM_LIMIT_BYTES = 56 * 1024 * 1024

F32 = jnp.float32
BF16 = jnp.bfloat16


def _dot(a, b):
    return jnp.dot(a, b, preferred_element_type=F32)


def _rms(x, g):
    return x * lax.rsqrt(jnp.mean(x * x, axis=-1, keepdims=True) + EPS) * g


def _head_rms_rope(x, g, bd, cos, sa, sb):
    w = x.shape[-1]
    msq = _dot((x * x).astype(BF16), bd)
    xn = x * lax.rsqrt(msq + EPS) * g
    return xn * cos + pltpu.roll(xn, w - ROT_DIM // 2, 1) * sa + pltpu.roll(xn, ROT_DIM // 2, 1) * sb


def _dup_halves(x):
    r = pltpu.roll(x, HEAD_DIM, 1)
    lane = lax.broadcasted_iota(jnp.int32, x.shape, 1)
    lo = lane < HEAD_DIM
    return jnp.where(lo, x, r), jnp.where(lo, r, x)


def _inproj_kernel(x_ref, lng_ref, wq_ref, wkv_ref, wg_ref, wcv_ref, qng_ref, kng_ref, bd512_ref, bd128_ref,
                   cos_ref, sa_ref, sb_ref, convw_ref, oncg_ref,
                   q_out, kc_out, vc_out, ksel_out, vs_out, kw_out, vw_out, gate_out, mconv_out,
                   zbuf, *, tiles_per_seq):
    tm = x_ref.shape[0]
    it = pl.program_id(0) % tiles_per_seq
    x = x_ref[...]
    xn = _rms(x, lng_ref[...]).astype(BF16)

    cos, sa, sb = cos_ref[...], sa_ref[...], sb_ref[...]
    cos4, sa4, sb4 = (jnp.concatenate([t] * 4, axis=1) for t in (cos, sa, sb))
    q = _dot(xn, wq_ref[...])
    qr = _head_rms_rope(q, qng_ref[...], bd512_ref[...], cos4, sa4, sb4)
    qs = qr * (HEAD_DIM ** -0.5)
    for blk_i in range(tm // Q_BLOCK):
        rows = slice(blk_i * Q_BLOCK, (blk_i + 1) * Q_BLOCK)
        for pr in range(N_Q_HEADS // 2):
            q_out[0, blk_i, pr] = qs[rows, pr * LANES:(pr + 1) * LANES].T.astype(BF16)

    kv = _dot(xn, wkv_ref[...])
    kc_out[...] = kv[:, 0 * KV_W:1 * KV_W]
    vc_out[...] = kv[:, 1 * KV_W:2 * KV_W]
    ks = _head_rms_rope(kv[:, 2 * KV_W:3 * KV_W], kng_ref[0:1, :], bd128_ref[...], cos, sa, sb)
    kw = _head_rms_rope(kv[:, 4 * KV_W:5 * KV_W], kng_ref[1:2, :], bd128_ref[...], cos, sa, sb)
    vs = kv[:, 3 * KV_W:4 * KV_W]
    vw = kv[:, 5 * KV_W:6 * KV_W]
    tpos = it * tm + lax.broadcasted_iota(jnp.int32, (tm, N_SLC_LANES), 0)
    blk = lax.broadcasted_iota(jnp.int32, (tm, N_SLC_LANES), 1)
    onehot = jnp.where(lax.shift_right_logical(tpos, SLC_SHIFT) == blk, 1.0, 0.0).astype(BF16)
    ks_d, vs_d, kw_d, vw_d = (_dup_halves(t) for t in (ks, vs, kw, vw))
    vrow = lax.broadcasted_iota(jnp.int32, (LANES, tm), 0)
    for g in range(N_KV_HEADS):
        ksel_out[0, g] = jnp.concatenate([ks_d[g].astype(BF16), onehot], axis=1)
        kw_out[0, g] = kw_d[g].astype(BF16)
        vs_out[0, g, 0] = jnp.where(vrow < HEAD_DIM, vs_d[g].T, 1.0).astype(BF16)
        vwt = jnp.where(vrow < HEAD_DIM, vw_d[g].T, 1.0).astype(BF16)
        for c in range(tm // LANES):
            vw_out[0, g, c] = vwt[:, c * LANES:(c + 1) * LANES]

    gates = jax.nn.sigmoid(_dot(xn, wg_ref[...]))
    for blk_i in range(tm // Q_BLOCK):
        rows = slice(blk_i * Q_BLOCK, (blk_i + 1) * Q_BLOCK)
        for g in range(N_KV_HEADS):
            gate_out[0, blk_i, g] = gates[rows, g * LANES:(g + 1) * LANES].T[:GATE_ROWS]

    cv = _dot(xn, wcv_ref[...])
    cb, cc, cx = cv[:, :D_CONV], cv[:, D_CONV:2 * D_CONV], cv[:, 2 * D_CONV:]
    z = cc * cx

    @pl.when(it == 0)
    def _():
        zbuf[0:SUBLANES, :] = jnp.zeros((SUBLANES, D_CONV), F32)

    zbuf[SUBLANES:SUBLANES + tm, :] = z
    y = (convw_ref[2:3, :] * z + convw_ref[1:2, :] * zbuf[SUBLANES - 1:SUBLANES - 1 + tm, :]
         + convw_ref[0:1, :] * zbuf[SUBLANES - 2:SUBLANES - 2 + tm, :])
    zbuf[0:SUBLANES, :] = zbuf[tm:tm + SUBLANES, :]
    mconv_out[...] = _rms(cb * y, oncg_ref[...]).astype(BF16)


def _inproj(x2, lng, wq, wkv, wg, wcv, qng, kng, bd512, bd128, cos, sa, sb, convw, oncg, *, batch, seq, tm):
    n = batch * seq
    tps = seq // tm
    row = lambda w: pl.BlockSpec((tm, w), lambda i: (i, 0))
    full = lambda a: pl.BlockSpec(a.shape, lambda i: (0,) * a.ndim)
    tab = pl.BlockSpec((tm, LANES), lambda i: (i % tps, 0))
    grp = lambda w: pl.BlockSpec((1, N_KV_HEADS, tm, w), lambda i: (i // tps, 0, i % tps, 0))
    gshape = lambda w: jax.ShapeDtypeStruct((batch, N_KV_HEADS, seq, w), BF16)
    assert tm == SEL_TILE
    vtile = lambda keys: pl.BlockSpec((1, N_KV_HEADS, tm // keys, LANES, keys),
                                      lambda i: (i // tps, 0, i % tps, 0, 0))
    vshape = lambda keys: jax.ShapeDtypeStruct((batch, N_KV_HEADS, seq // keys, LANES, keys), BF16)
    qblk = lambda tile: pl.BlockSpec((1, tm // Q_BLOCK) + tile, lambda i: (i // tps, i % tps) + (0,) * len(tile))
    return pl.pallas_call(
        functools.partial(_inproj_kernel, tiles_per_seq=tps),
        grid=(n // tm,),
        in_specs=[row(D_MODEL), full(lng), full(wq), full(wkv), full(wg), full(wcv), full(qng), full(kng),
                  full(bd512), full(bd128), tab, tab, tab, full(convw), full(oncg)],
        out_specs=[qblk((N_Q_HEADS // 2, LANES, Q_BLOCK)), row(KV_W), row(KV_W), grp(2 * LANES), vtile(SEL_TILE),
                   grp(LANES), vtile(LANES), qblk((N_KV_HEADS, GATE_ROWS, Q_BLOCK)), row(D_CONV)],
        out_shape=[jax.ShapeDtypeStruct((batch, seq // Q_BLOCK, N_Q_HEADS // 2, LANES, Q_BLOCK), BF16),
                   jax.ShapeDtypeStruct((n, KV_W), F32),
                   jax.ShapeDtypeStruct((n, KV_W), F32), gshape(2 * LANES), vshape(SEL_TILE), gshape(LANES),
                   vshape(LANES),
                   jax.ShapeDtypeStruct((batch, seq // Q_BLOCK, N_KV_HEADS, GATE_ROWS, Q_BLOCK), F32),
                   jax.ShapeDtypeStruct((n, D_CONV), BF16)],
        scratch_shapes=[pltpu.VMEM((tm + SUBLANES, D_CONV), F32)],
        compiler_params=pltpu.CompilerParams(dimension_semantics=("arbitrary",),
                                             vmem_limit_bytes=VMEM_LIMIT_BYTES),
        name="inproj",
    )(x2, lng, wq, wkv, wg, wcv, qng, kng, bd512, bd128, cos, sa, sb, convw, oncg)


def _gelu_tanh(x):
    return 0.5 * x * (1.0 + jnp.tanh(np.sqrt(2.0 / np.pi).astype(np.float32) * (x + 0.044715 * (x * x * x))))


def _compress_kernel(kc_ref, vc_ref, w1ak_ref, w1bk_ref, w1av_ref, w1bv_ref, peak_ref, pebk_ref, peav_ref,
                     pebv_ref, w2k_ref, w2v_ref, kng_ref, bd128_ref, cos_ref, sa_ref, sb_ref,
                     kcmp_out, vcmp_out):
    nch = kc_ref.shape[1]

    def compress(x, w1a, w1b, pea, peb, w2):
        a = _dot((x + pea).astype(BF16), w1a)
        b = _dot((x + peb).astype(BF16), w1b)
        hid = _gelu_tanh(a + pltpu.roll(b, nch - 1, 0))
        return _dot(hid.astype(BF16), w2)

    kc = compress(kc_ref[0], w1ak_ref[...], w1bk_ref[...], peak_ref[...], pebk_ref[...], w2k_ref[...])
    vc = compress(vc_ref[0], w1av_ref[...], w1bv_ref[...], peav_ref[...], pebv_ref[...], w2v_ref[...])
    kc = _head_rms_rope(kc, kng_ref[...], bd128_ref[...], cos_ref[...], sa_ref[...], sb_ref[...])
    kd, vd = _dup_halves(kc), _dup_halves(vc)
    for g in range(N_KV_HEADS):
        kcmp_out[0, g] = kd[g].astype(BF16)
        vcmp_out[0, g] = vd[g].T.astype(BF16)


def _compress(kc3, vc3, w1ak, w1bk, w1av, w1bv, peak, pebk, peav, pebv, w2k, w2v, kng0, bd128, cos, sa, sb):
    batch, nch, wide = kc3.shape
    full = lambda a: pl.BlockSpec(a.shape, lambda b: (0,) * a.ndim)
    tok = pl.BlockSpec((1, nch, wide), lambda b: (b, 0, 0))
    kout = pl.BlockSpec((1, N_KV_HEADS, nch, LANES), lambda b: (b, 0, 0, 0))
    vout = pl.BlockSpec((1, N_KV_HEADS, LANES, nch), lambda b: (b, 0, 0, 0))
    consts = (w1ak, w1bk, w1av, w1bv, peak, pebk, peav, pebv, w2k, w2v, kng0, bd128, cos, sa, sb)
    return pl.pallas_call(
        _compress_kernel,
        grid=(batch,),
        in_specs=[tok, tok] + [full(a) for a in consts],
        out_specs=[kout, vout],
        out_shape=[jax.ShapeDtypeStruct((batch, N_KV_HEADS, nch, LANES), BF16),
                   jax.ShapeDtypeStruct((batch, N_KV_HEADS, LANES, nch), BF16)],
        compiler_params=pltpu.CompilerParams(dimension_semantics=("arbitrary",),
                                             vmem_limit_bytes=VMEM_LIMIT_BYTES),
        name="compress",
    )(kc3, vc3, *consts)


SEL_TILE = 256
COLS = Q_PER_KV * Q_BLOCK
PIPE = 2
WIN_MASK_ROWS = 32
WIN_EXP_ROWS = 64
assert (WINDOW + Q_BLOCK) // WIN_MASK_ROWS + (WINDOW + Q_BLOCK) // WIN_EXP_ROWS <= 32


def _split2(x):
    hi = x.astype(BF16)
    return hi, (x - hi.astype(F32)).astype(BF16)


def _zero_after(x):
    return lax.shift_right_logical(lax.shift_right_logical(x, 16), 16)


def _normalize_dup(acc):
    o = acc[:HEAD_DIM] / acc[HEAD_DIM:HEAD_DIM + 1]
    return jnp.concatenate([o, o], axis=0)


def _attn_kernel(q_ref, ksel_ref, vst_ref, kw_ref, vwt_ref, kcmp_ref, vcmpt_ref, c2st_ref, gate_ref, o_ref,
                 s_all, p_all, a_all, m_scr, acc_scr, acct_scr, cw_scr, sw_scr, pw_scr):
    i = pl.program_id(2)
    start = i * Q_BLOCK
    nch = kcmp_ref.shape[2]
    last_tile = ksel_ref.shape[2] // SEL_TILE - 1

    lane = lax.broadcasted_iota(jnp.int32, (Q_BLOCK, LANES), 1)
    lo_half = lane < HEAD_DIM
    top_half = lax.broadcasted_iota(jnp.int32, (LANES, Q_BLOCK), 0) < HEAD_DIM
    zero_bf = jnp.zeros((LANES, Q_BLOCK), BF16)
    qt = jnp.concatenate([jnp.where(top_half if r % 2 == 0 else ~top_half, q_ref[0, 0, r // 2], zero_bf)
                          for r in range(Q_PER_KV)], axis=1)
    tq = start + (lax.broadcasted_iota(jnp.int32, (1, COLS), 1) & (Q_BLOCK - 1))
    gt = gate_ref[0, 0, 0]
    grow = lambda b: jnp.concatenate([gt[r * N_BRANCH + b:r * N_BRANCH + b + 1, :] for r in range(Q_PER_KV)],
                                     axis=1)

    wk = WINDOW + Q_BLOCK
    base = pl.multiple_of(jnp.maximum(start - WINDOW, 0), Q_BLOCK)

    wrow = lax.broadcasted_iota(jnp.int32, (WIN_MASK_ROWS, COLS), 0)

    def window_mask_piece(k, colmax, zero):
        rows = slice(WIN_MASK_ROWS * k, WIN_MASK_ROWS * (k + 1))
        back = (tq + zero - base - WIN_MASK_ROWS * k) - wrow
        in_window = lax.bitcast_convert_type(back, jnp.uint32) < jnp.uint32(WINDOW)
        sm = jnp.where(in_window, sw_scr[rows, :], NEG)
        sw_scr[rows, :] = sm
        for c in range(WIN_MASK_ROWS // SUBLANES):
            colmax = jnp.maximum(colmax, sm[c * SUBLANES:(c + 1) * SUBLANES])
        return colmax

    def window_exp_piece(k, m_w, zero):
        rows = slice(WIN_EXP_ROWS * k, WIN_EXP_ROWS * (k + 1))
        pw_scr[rows, :] = jnp.exp(sw_scr[rows, :] - (m_w + zero.astype(F32))).astype(BF16)

    s_c = _dot(kcmp_ref[0, 0], qt)
    sw_scr[...] = _dot(kw_ref[0, 0, pl.ds(base, wk), :], qt)
    nrow = lax.broadcasted_iota(jnp.int32, (nch, COLS), 0)
    last_valid = lax.shift_right_arithmetic(tq - (CMP_BLOCK - 1), CMP_SHIFT)
    s_c = jnp.where(nrow <= last_valid, s_c, NEG)
    m_c = jnp.max(s_c, axis=0, keepdims=True)
    e_c = jnp.exp(s_c - m_c)
    l_c = jnp.sum(e_c, axis=0, keepdims=True)
    p_c = e_c * jnp.where(m_c > 0.5 * NEG, 1.0 / l_c, 0.0)
    cw_scr[...] = grow(0) * _dot(vcmpt_ref[0, 0], p_c.astype(BF16))

    psum = sum(p_c[:, r * Q_BLOCK:(r + 1) * Q_BLOCK] for r in range(Q_PER_KV))
    c2st = c2st_ref[...]
    impt = sum(_dot(c2st, t) for t in _split2(psum))
    srow = lax.broadcasted_iota(jnp.int32, (N_SLC_LANES, Q_BLOCK), 0)
    tqq = start + lax.broadcasted_iota(jnp.int32, (N_SLC_LANES, Q_BLOCK), 1)
    cur = lax.shift_right_logical(tqq, SLC_SHIFT)
    forced = (srow == 0) | (srow == cur) | (srow == cur - 1)
    future = srow * SLC_BLOCK > tqq
    impt = jnp.where(forced, 1e9, jnp.where(future, -1e9, impt))
    key_to_float = lambda k: lax.bitcast_convert_type(jnp.where(k >= 0, k, k ^ jnp.int32(0x7FFFFFFF)), F32)
    thr = jnp.full((1, Q_BLOCK), INT32_MIN, jnp.int32)
    n_mask, n_exp = wk // WIN_MASK_ROWS, wk // WIN_EXP_ROWS
    colmax = jnp.full((SUBLANES, COLS), NEG, F32)
    wrap32 = lambda v: ((v + 2 ** 31) % 2 ** 32) - 2 ** 31
    piece = 0
    for b in range(30, -1, -2):
        reached = None
        for mult in (1, 2, 3):
            cand = thr + jnp.int32(wrap32(mult << b))
            n_ge = jnp.sum(jnp.where(impt >= key_to_float(cand), 1.0, 0.0), axis=0, keepdims=True)
            reached = jnp.where(n_ge >= N_SELECT, cand, thr if reached is None else reached)
        thr = reached
        zero = jnp.concatenate([_zero_after(thr)] * Q_PER_KV, axis=1)
        for _ in range(2):
            if piece < n_mask:
                colmax = window_mask_piece(piece, colmax, zero)
            elif piece < n_mask + n_exp:
                if piece == n_mask:
                    m_w = jnp.max(colmax, axis=0, keepdims=True)
                window_exp_piece(piece - n_mask, m_w, zero)
            piece += 1
    wtile = base // LANES
    vw = jnp.concatenate([vwt_ref[0, 0, wtile + c] for c in range(wk // LANES)], axis=1)
    cw_scr[...] += grow(2) * _normalize_dup(_dot(vw, pw_scr[...]))
    kth = key_to_float(thr)
    above = impt > kth
    tied = impt == kth
    n_above = jnp.sum(jnp.where(above, 1.0, 0.0), axis=0, keepdims=True)
    scol = lax.broadcasted_iota(jnp.int32, (N_SLC_LANES, N_SLC_LANES), 1)
    earlier = jnp.where(scol < srow, 1.0, 0.0).astype(BF16)
    tied_before = _dot(earlier, jnp.where(tied, 1.0, 0.0).astype(BF16))
    selected = above | (tied & (tied_before < N_SELECT - n_above))
    bias = jnp.where(selected, 0.0, NEG)
    n_main = start // SEL_TILE
    bias_main = jnp.where(srow >= n_main * (SEL_TILE // SLC_BLOCK), NEG, bias)
    widen = lambda b: jnp.concatenate([b.astype(BF16)] * Q_PER_KV, axis=1)
    qt_tail = jnp.concatenate([qt, widen(bias)], axis=0)
    qt_main = jnp.concatenate([qt, widen(bias_main)], axis=0)

    kt = pl.multiple_of(n_main * SEL_TILE, SEL_TILE)
    s_t = _dot(ksel_ref[0, 0, pl.ds(kt, SEL_TILE), :], qt_tail)
    krow = lax.broadcasted_iota(jnp.int32, (SEL_TILE, COLS), 0)
    s_t = jnp.where(kt + krow <= tq, s_t, NEG)
    m_t = jnp.max(s_t, axis=0, keepdims=True)
    acct_scr[...] = _dot(vst_ref[0, 0, n_main], jnp.exp(s_t - m_t).astype(BF16))
    g_sel = grow(1)

    def scores(j, slot):
        k0 = pl.multiple_of(jnp.minimum(j, last_tile) * SEL_TILE, SEL_TILE)
        s_all[slot] = _dot(ksel_ref[0, 0, pl.ds(k0, SEL_TILE), :], qt_main)

    def softmax(slot):
        s = s_all[slot]
        m_prev = m_scr[0:1, :]
        m_new = jnp.maximum(m_prev, jnp.max(s, axis=0, keepdims=True))
        a_all[slot] = jnp.broadcast_to(jnp.exp(m_prev - m_new), (SUBLANES, COLS))
        p_all[slot] = jnp.exp(s - m_new).astype(BF16)
        m_scr[...] = jnp.broadcast_to(m_new, m_scr.shape)

    def values(j, slot):
        acc_scr[...] = (a_all[slot, 0:1, :] * acc_scr[...]
                        + _dot(vst_ref[0, 0, jnp.minimum(j, last_tile)], p_all[slot]))

    m_scr[...] = jnp.full(m_scr.shape, NEG, F32)
    acc_scr[...] = jnp.zeros(acc_scr.shape, F32)
    for k in range(PIPE):
        scores(k, k)
    for k in range(PIPE // 2):
        softmax(k)

    def sweep(t, carry):
        for k in range(PIPE):
            values(PIPE * t + k, k)
            softmax((k + PIPE // 2) % PIPE)
            scores(PIPE * t + k + PIPE, k)
        return carry

    lax.fori_loop(0, (n_main + PIPE - 1) // PIPE, sweep, 0)
    m_p = m_scr[0:1, :]
    m_tot = jnp.maximum(m_p, m_t)
    o_s = _normalize_dup(jnp.exp(m_p - m_tot) * acc_scr[...] + jnp.exp(m_t - m_tot) * acct_scr[...])

    comb = cw_scr[...] + g_sel * o_s
    outs = [comb[:, r * Q_BLOCK:(r + 1) * Q_BLOCK].T for r in range(Q_PER_KV)]
    o_ref[0] = jnp.concatenate([jnp.where(lo_half, outs[0], outs[1]), jnp.where(lo_half, outs[2], outs[3])],
                               axis=1)


def _attn(qpt, ksel, vst, kw, vwt, kcmp, vcmpt, c2st, gatest):
    batch, seq = qpt.shape[0], qpt.shape[1] * Q_BLOCK
    per_group = lambda a: pl.BlockSpec((1, 1) + a.shape[2:], lambda b, g, i: (b, g) + (0,) * (a.ndim - 2))
    return pl.pallas_call(
        _attn_kernel,
        grid=(batch, N_KV_HEADS, seq // Q_BLOCK),
        in_specs=[pl.BlockSpec((1, 1, Q_PER_KV // 2, LANES, Q_BLOCK), lambda b, g, i: (b, i, g, 0, 0)),
                  per_group(ksel), per_group(vst), per_group(kw), per_group(vwt), per_group(kcmp),
                  per_group(vcmpt), pl.BlockSpec(c2st.shape, lambda b, g, i: (0, 0)),
                  pl.BlockSpec((1, 1, 1, GATE_ROWS, Q_BLOCK), lambda b, g, i: (b, i, g, 0, 0))],
        out_specs=pl.BlockSpec((1, Q_BLOCK, Q_PER_KV * HEAD_DIM), lambda b, g, i: (b, i, g)),
        out_shape=jax.ShapeDtypeStruct((batch, seq, D_ATT), F32),
        scratch_shapes=[pltpu.VMEM((PIPE, SEL_TILE, COLS), F32), pltpu.VMEM((PIPE, SEL_TILE, COLS), BF16),
                        pltpu.VMEM((PIPE, SUBLANES, COLS), F32),
                        pltpu.VMEM((SUBLANES, COLS), F32), pltpu.VMEM((LANES, COLS), F32),
                        pltpu.VMEM((LANES, COLS), F32), pltpu.VMEM((LANES, COLS), F32),
                        pltpu.VMEM((WINDOW + Q_BLOCK, COLS), F32), pltpu.VMEM((WINDOW + Q_BLOCK, COLS), BF16)],
        compiler_params=pltpu.CompilerParams(dimension_semantics=("arbitrary", "arbitrary", "arbitrary"),
                                             vmem_limit_bytes=VMEM_LIMIT_BYTES),
        name="attn",
    )(qpt, ksel, vst, kw, vwt, kcmp, vcmpt, c2st, gatest)


FF_CHUNKS = 2
FF_CHUNK = D_FF // FF_CHUNKS


def _post_kernel(x_ref, oatt_ref, mconv_ref, p_ref, onag_ref, wo_ref, lnf_ref, wup_ref, fcw_ref, fcb_ref,
                 wdn_ref, lnp_ref, wpg_ref, wpe_ref, out_ref, gbuf, *, tiles_per_seq):
    tm = x_ref.shape[0]
    it = pl.program_id(0) % tiles_per_seq

    mixed = jnp.concatenate([_rms(oatt_ref[...], onag_ref[...]).astype(BF16), mconv_ref[...]], axis=1)
    h1 = x_ref[...] + _dot(mixed, wo_ref[...])
    xn = _rms(h1, lnf_ref[...]).astype(BF16)

    @pl.when(it == 0)
    def _():
        gbuf[:, 0:SUBLANES, :] = jnp.zeros((FF_CHUNKS, SUBLANES, FF_CHUNK), F32)

    h2 = h1
    for c in range(FF_CHUNKS):
        cols = slice(c * FF_CHUNK, (c + 1) * FF_CHUNK)
        gpre = _dot(xn, wup_ref[:, cols])
        up = _dot(xn, wup_ref[:, D_FF + c * FF_CHUNK:D_FF + (c + 1) * FF_CHUNK])
        gbuf[c, SUBLANES:SUBLANES + tm, :] = gpre
        gate = (fcw_ref[2:3, cols] * gpre + fcw_ref[1:2, cols] * gbuf[c, SUBLANES - 1:SUBLANES - 1 + tm, :]
                + fcw_ref[0:1, cols] * gbuf[c, SUBLANES - 2:SUBLANES - 2 + tm, :]) + fcb_ref[:, cols]
        gbuf[c, 0:SUBLANES, :] = gbuf[c, tm:tm + SUBLANES, :]
        act = (gate * jax.nn.sigmoid(gate) * up).astype(BF16)
        h2 = h2 + _dot(act, wdn_ref[cols, :])

    xn2 = _rms(h2, lnp_ref[...]).astype(BF16)
    out_ref[...] = h2 + jax.nn.sigmoid(_dot(xn2, wpg_ref[...])) * _dot(p_ref[...].astype(BF16), wpe_ref[...])


def _post(x2, oatt, mconv, p2, onag, wo, lnf, wup, fcw, fcb, wdn, lnp, wpg, wpe, *, seq, tm):
    n = x2.shape[0]
    tps = seq // tm
    row = lambda w: pl.BlockSpec((tm, w), lambda i: (i, 0))
    full = lambda a: pl.BlockSpec(a.shape, lambda i: (0,) * a.ndim, pipeline_mode=pl.Buffered(1))
    return pl.pallas_call(
        functools.partial(_post_kernel, tiles_per_seq=tps),
        grid=(n // tm,),
        in_specs=[row(D_MODEL), row(D_ATT), row(D_CONV), row(D_PLE), full(onag), full(wo), full(lnf), full(wup),
                  full(fcw), full(fcb), full(wdn), full(lnp), full(wpg), full(wpe)],
        out_specs=row(D_MODEL),
        out_shape=jax.ShapeDtypeStruct((n, D_MODEL), F32),
        scratch_shapes=[pltpu.VMEM((FF_CHUNKS, tm + SUBLANES, FF_CHUNK), F32)],
        compiler_params=pltpu.CompilerParams(dimension_semantics=("arbitrary",),
                                             vmem_limit_bytes=VMEM_LIMIT_BYTES),
        name="post",
    )(x2, oatt, mconv, p2, onag, wo, lnf, wup, fcw, fcb, wdn, lnp, wpg, wpe)


def _rope_tables(pos):
    half = ROT_DIM // 2
    inv_freq = ROPE_THETA ** (-jnp.arange(half, dtype=F32) * 2.0 / ROT_DIM)
    ang = pos.astype(F32)[:, None] * inv_freq[None, :]
    c, s = jnp.cos(ang), jnp.sin(ang)
    z = jnp.zeros((pos.shape[0], HEAD_DIM - ROT_DIM), F32)
    zh = jnp.zeros_like(c)
    cos = jnp.concatenate([c, c, z + 1.0], axis=1)
    sa = jnp.concatenate([-s, zh, z], axis=1)
    sb = jnp.concatenate([zh, s, z], axis=1)
    return tuple(jnp.concatenate([t, t], axis=1) for t in (cos, sa, sb))


def _block_diag_mean(width):
    idx = np.arange(width) // HEAD_DIM
    return jnp.asarray((idx[:, None] == idx[None, :]).astype(np.float32) / HEAD_DIM, BF16)


def _cmp_to_slc_t(nch):
    cs = CMP_STRIDE * np.arange(nch)[None, :]
    ss = SLC_BLOCK * np.arange(N_SLC_LANES)[:, None]
    ov = np.clip(np.minimum(cs + CMP_BLOCK, ss + SLC_BLOCK) - np.maximum(cs, ss), 0, None)
    return jnp.asarray(ov.astype(np.float32) / CMP_BLOCK, BF16)


def _layer(h, p_l, ln_mix_g, w_in, qn_g, kn_g, pe_k, pe_v, w_ck1, w_ck2, w_cv1, w_cv2, conv_w, on_att_g,
           on_conv_g, w_o, ln_ffn_g, w_up, ffn_conv_w, ffn_conv_b, w_down, ln_ple_g, w_pg, w_pe):
    batch, seq, _ = h.shape
    assert seq % SEL_TILE == 0 and seq // SLC_BLOCK <= N_SLC_LANES and seq // SLC_BLOCK >= N_SELECT
    assert seq >= WINDOW + Q_BLOCK
    n = batch * seq
    nch = seq // CMP_STRIDE
    x2 = h.reshape(n, D_MODEL)
    row = lambda v: v.reshape(1, -1).astype(F32)

    o_q, o_kv, o_g, o_cv = 0, D_ATT, D_ATT + 6 * KV_W, D_ATT + 6 * KV_W + N_BRANCH * N_Q_HEADS
    wq = w_in[:, o_q:o_kv].astype(BF16)
    wkv = w_in[:, o_kv:o_g].astype(BF16)
    wcv = w_in[:, o_cv:].astype(BF16)
    per_g = Q_PER_KV * N_BRANCH
    wg = jnp.concatenate(
        [jnp.pad(w_in[:, o_g + g * per_g:o_g + (g + 1) * per_g], ((0, 0), (0, LANES - per_g)))
         for g in range(N_KV_HEADS)], axis=1).astype(BF16)
    cos, sa, sb = _rope_tables(jnp.arange(seq))
    bd512, bd128 = _block_diag_mean(D_ATT), _block_diag_mean(KV_W)
    tile_heads = lambda v, k: jnp.tile(v.astype(F32), k).reshape(1, -1)

    tm = SEL_TILE
    qpt, kc2, vc2, ksel, vst, kw, vwt, gatest, mconv = _inproj(
        x2, row(ln_mix_g), wq, wkv, wg, wcv, tile_heads(qn_g, N_Q_HEADS),
        jnp.stack([jnp.tile(kn_g[1], N_KV_HEADS), jnp.tile(kn_g[2], N_KV_HEADS)]).astype(F32),
        bd512, bd128, cos, sa, sb, conv_w.astype(F32), row(on_conv_g), batch=batch, seq=seq, tm=tm)

    eye = jnp.eye(N_KV_HEADS, dtype=F32)
    half = CMP_BLOCK // 2

    def w1_parts(w1):
        wfull = jnp.einsum('ldh,gk->lgdkh', w1, eye).reshape(CMP_BLOCK, KV_W, N_KV_HEADS * CMP_HIDDEN)
        return (wfull[:half].reshape(half * KV_W, -1).astype(BF16),
                wfull[half:].reshape(half * KV_W, -1).astype(BF16))

    def pe_parts(pe):
        pf = jnp.broadcast_to(pe[:, None, :], (CMP_BLOCK, N_KV_HEADS, HEAD_DIM)).astype(F32)
        return pf[:half].reshape(1, -1), pf[half:].reshape(1, -1)

    w2bd = lambda w2: jnp.einsum('hd,gk->ghkd', w2, eye).reshape(N_KV_HEADS * CMP_HIDDEN, KV_W).astype(BF16)
    w1ak, w1bk = w1_parts(w_ck1)
    w1av, w1bv = w1_parts(w_cv1)
    peak, pebk = pe_parts(pe_k)
    peav, pebv = pe_parts(pe_v)
    ccos, csa, csb = _rope_tables(CMP_STRIDE * jnp.arange(nch) + CMP_BLOCK - 1)
    kcmp, vcmpt = _compress(kc2.reshape(batch, nch, CMP_STRIDE * KV_W), vc2.reshape(batch, nch, CMP_STRIDE * KV_W),
                            w1ak, w1bk, w1av, w1bv, peak, pebk, peav, pebv, w2bd(w_ck2), w2bd(w_cv2),
                            tile_heads(kn_g[0], N_KV_HEADS), bd128, ccos, csa, csb)

    oatt = _attn(qpt, ksel, vst, kw, vwt, kcmp, vcmpt, _cmp_to_slc_t(nch), gatest)

    out = _post(x2, oatt.reshape(n, D_ATT), mconv, p_l.reshape(n, D_PLE), row(on_att_g), w_o.astype(BF16),
                row(ln_ffn_g), w_up.astype(BF16), ffn_conv_w.astype(F32), row(ffn_conv_b), w_down.astype(BF16),
                row(ln_ple_g), w_pg.astype(BF16), w_pe.astype(BF16), seq=seq, tm=tm)
    return out.reshape(batch, seq, D_MODEL)


def kernel(x, p, ln_mix_g, w_in, qn_g, kn_g, pe_k, pe_v, w_ck1, w_ck2, w_cv1, w_cv2, conv_w, on_att_g,
           on_conv_g, w_o, ln_ffn_g, w_up, ffn_conv_w, ffn_conv_b, w_down, ln_ple_g, w_pg, w_pe):
    h = x
    for i in range(p.shape[0]):
        h = _layer(h, p[i], ln_mix_g[i], w_in[i], qn_g[i], kn_g[i], pe_k[i], pe_v[i], w_ck1[i], w_ck2[i],
                   w_cv1[i], w_cv2[i], conv_w[i], on_att_g[i], on_conv_g[i], w_o[i], ln_ffn_g[i], w_up[i],
                   ffn_conv_w[i], ffn_conv_b[i], w_down[i], ln_ple_g[i], w_pg[i], w_pe[i])
    return h
```

```python
import functools

import jax
import jax.numpy as jnp
import numpy as np
from jax import lax
from jax.experimental import pallas as pl
from jax.experimental.pallas import tpu as pltpu

D_MODEL = 1024
HEAD_DIM = 64
N_Q_HEADS = 8
N_KV_HEADS = 2
Q_PER_KV = N_Q_HEADS // N_KV_HEADS
D_ATT = N_Q_HEADS * HEAD_DIM
D_CONV = D_MODEL - D_ATT
KV_W = N_KV_HEADS * HEAD_DIM
N_BRANCH = 3
CONV_TAPS = 3
ROT_DIM = HEAD_DIM // 4
ROPE_THETA = 500000.0
CMP_BLOCK = 32
CMP_STRIDE = 16
CMP_HIDDEN = 256
SLC_BLOCK = 64
N_SELECT = 16
WINDOW = 512
Q_BLOCK = 128
D_FF = 2816
D_PLE = 256
EPS = 1e-6
NEG = -1e30
INT32_MIN = -2 ** 31
LOG2E = float(np.log2(np.e))
V_ROWS = HEAD_DIM + 16
GATE_ROWS = 16
SLC_SHIFT = SLC_BLOCK.bit_length() - 1
CMP_SHIFT = CMP_STRIDE.bit_length() - 1
assert 1 << SLC_SHIFT == SLC_BLOCK and 1 << CMP_SHIFT == CMP_STRIDE

LANES = 128
SUBLANES = 8
N_SLC_LANES = LANES
VMEM_LIMIT_BYTES = 56 * 1024 * 1024

F32 = jnp.float32
BF16 = jnp.bfloat16


def _dot(a, b):
    return jnp.dot(a, b, preferred_element_type=F32)


def _rms(x, g):
    return x * lax.rsqrt(jnp.mean(x * x, axis=-1, keepdims=True) + EPS) * g


def _head_rms_rope(x, g, bd, cos, sa, sb):
    w = x.shape[-1]
    msq = _dot((x * x).astype(BF16), bd)
    xn = x * lax.rsqrt(msq + EPS) * g
    return xn * cos + pltpu.roll(xn, w - ROT_DIM // 2, 1) * sa + pltpu.roll(xn, ROT_DIM // 2, 1) * sb


def _dup_halves(x):
    r = pltpu.roll(x, HEAD_DIM, 1)
    lane = lax.broadcasted_iota(jnp.int32, x.shape, 1)
    lo = lane < HEAD_DIM
    return jnp.where(lo, x, r), jnp.where(lo, r, x)


def _inproj_kernel(x_ref, lng_ref, wq_ref, wkv_ref, wg_ref, wcv_ref, qng_ref, kng_ref, bd512_ref, bd128_ref,
                   cos_ref, sa_ref, sb_ref, convw_ref, oncg_ref,
                   q_out, kc_out, vc_out, ksel_out, vs_out, kw_out, vw_out, gate_out, mconv_out,
                   zbuf, *, tiles_per_seq):
    tm = x_ref.shape[0]
    it = pl.program_id(0) % tiles_per_seq
    x = x_ref[...]
    xn = _rms(x, lng_ref[...]).astype(BF16)

    cos, sa, sb = cos_ref[...], sa_ref[...], sb_ref[...]
    cos4, sa4, sb4 = (jnp.concatenate([t] * 4, axis=1) for t in (cos, sa, sb))
    q = _dot(xn, wq_ref[...])
    qr = _head_rms_rope(q, qng_ref[...], bd512_ref[...], cos4, sa4, sb4)
    qs = qr * (HEAD_DIM ** -0.5 * LOG2E)
    for blk_i in range(tm // Q_BLOCK):
        rows = slice(blk_i * Q_BLOCK, (blk_i + 1) * Q_BLOCK)
        for pr in range(N_Q_HEADS // 2):
            q_out[0, blk_i, pr] = qs[rows, pr * LANES:(pr + 1) * LANES].T.astype(BF16)

    kv = _dot(xn, wkv_ref[...])
    kc_out[...] = kv[:, 0 * KV_W:1 * KV_W]
    vc_out[...] = kv[:, 1 * KV_W:2 * KV_W]
    ks = _head_rms_rope(kv[:, 2 * KV_W:3 * KV_W], kng_ref[0:1, :], bd128_ref[...], cos, sa, sb)
    kw = _head_rms_rope(kv[:, 4 * KV_W:5 * KV_W], kng_ref[1:2, :], bd128_ref[...], cos, sa, sb)
    vs = kv[:, 3 * KV_W:4 * KV_W]
    vw = kv[:, 5 * KV_W:6 * KV_W]
    tpos = it * tm + lax.broadcasted_iota(jnp.int32, (tm, N_SLC_LANES), 0)
    blk = lax.broadcasted_iota(jnp.int32, (tm, N_SLC_LANES), 1)
    onehot = jnp.where(lax.shift_right_logical(tpos, SLC_SHIFT) == blk, 1.0, 0.0).astype(BF16)
    ks_d, vs_d, kw_d, vw_d = (_dup_halves(t) for t in (ks, vs, kw, vw))
    vrow = lax.broadcasted_iota(jnp.int32, (V_ROWS, tm), 0)
    for g in range(N_KV_HEADS):
        ksel_out[0, g] = jnp.concatenate([ks_d[g].astype(BF16), onehot], axis=1)
        kw_out[0, g] = kw_d[g].astype(BF16)
        vs_out[0, g, 0] = jnp.where(vrow < HEAD_DIM, vs_d[g].T[:V_ROWS], 1.0).astype(BF16)
        vwt = jnp.where(vrow < HEAD_DIM, vw_d[g].T[:V_ROWS], 1.0).astype(BF16)
        for c in range(tm // LANES):
            vw_out[0, g, c] = vwt[:, c * LANES:(c + 1) * LANES]

    gates = jax.nn.sigmoid(_dot(xn, wg_ref[...]))
    for blk_i in range(tm // Q_BLOCK):
        rows = slice(blk_i * Q_BLOCK, (blk_i + 1) * Q_BLOCK)
        for g in range(N_KV_HEADS):
            gate_out[0, blk_i, g] = gates[rows, g * LANES:(g + 1) * LANES].T[:GATE_ROWS]

    cv = _dot(xn, wcv_ref[...])
    cb, cc, cx = cv[:, :D_CONV], cv[:, D_CONV:2 * D_CONV], cv[:, 2 * D_CONV:]
    z = cc * cx

    @pl.when(it == 0)
    def _():
        zbuf[0:SUBLANES, :] = jnp.zeros((SUBLANES, D_CONV), F32)

    zbuf[SUBLANES:SUBLANES + tm, :] = z
    y = (convw_ref[2:3, :] * z + convw_ref[1:2, :] * zbuf[SUBLANES - 1:SUBLANES - 1 + tm, :]
         + convw_ref[0:1, :] * zbuf[SUBLANES - 2:SUBLANES - 2 + tm, :])
    zbuf[0:SUBLANES, :] = zbuf[tm:tm + SUBLANES, :]
    mconv_out[...] = _rms(cb * y, oncg_ref[...]).astype(BF16)


def _inproj(x2, lng, wq, wkv, wg, wcv, qng, kng, bd512, bd128, cos, sa, sb, convw, oncg, *, batch, seq, tm):
    n = batch * seq
    tps = seq // tm
    row = lambda w: pl.BlockSpec((tm, w), lambda i: (i, 0))
    full = lambda a: pl.BlockSpec(a.shape, lambda i: (0,) * a.ndim)
    tab = pl.BlockSpec((tm, LANES), lambda i: (i % tps, 0))
    grp = lambda w: pl.BlockSpec((1, N_KV_HEADS, tm, w), lambda i: (i // tps, 0, i % tps, 0))
    gshape = lambda w: jax.ShapeDtypeStruct((batch, N_KV_HEADS, seq, w), BF16)
    assert tm == SEL_TILE
    vtile = lambda keys: pl.BlockSpec((1, N_KV_HEADS, tm // keys, V_ROWS, keys),
                                      lambda i: (i // tps, 0, i % tps, 0, 0))
    vshape = lambda keys: jax.ShapeDtypeStruct((batch, N_KV_HEADS, seq // keys, V_ROWS, keys), BF16)
    qblk = lambda tile: pl.BlockSpec((1, tm // Q_BLOCK) + tile, lambda i: (i // tps, i % tps) + (0,) * len(tile))
    return pl.pallas_call(
        functools.partial(_inproj_kernel, tiles_per_seq=tps),
        grid=(n // tm,),
        in_specs=[row(D_MODEL), full(lng), full(wq), full(wkv), full(wg), full(wcv), full(qng), full(kng),
                  full(bd512), full(bd128), tab, tab, tab, full(convw), full(oncg)],
        out_specs=[qblk((N_Q_HEADS // 2, LANES, Q_BLOCK)), row(KV_W), row(KV_W), grp(2 * LANES), vtile(SEL_TILE),
                   grp(LANES), vtile(LANES), qblk((N_KV_HEADS, GATE_ROWS, Q_BLOCK)), row(D_CONV)],
        out_shape=[jax.ShapeDtypeStruct((batch, seq // Q_BLOCK, N_Q_HEADS // 2, LANES, Q_BLOCK), BF16),
                   jax.ShapeDtypeStruct((n, KV_W), F32),
                   jax.ShapeDtypeStruct((n, KV_W), F32), gshape(2 * LANES), vshape(SEL_TILE), gshape(LANES),
                   vshape(LANES),
                   jax.ShapeDtypeStruct((batch, seq // Q_BLOCK, N_KV_HEADS, GATE_ROWS, Q_BLOCK), F32),
                   jax.ShapeDtypeStruct((n, D_CONV), BF16)],
        scratch_shapes=[pltpu.VMEM((tm + SUBLANES, D_CONV), F32)],
        compiler_params=pltpu.CompilerParams(dimension_semantics=("arbitrary",),
                                             vmem_limit_bytes=VMEM_LIMIT_BYTES),
        name="inproj",
    )(x2, lng, wq, wkv, wg, wcv, qng, kng, bd512, bd128, cos, sa, sb, convw, oncg)


def _gelu_tanh(x):
    return 0.5 * x * (1.0 + jnp.tanh(np.sqrt(2.0 / np.pi).astype(np.float32) * (x + 0.044715 * (x * x * x))))


def _compress_kernel(kc_ref, vc_ref, w1ak_ref, w1bk_ref, w1av_ref, w1bv_ref, peak_ref, pebk_ref, peav_ref,
                     pebv_ref, w2k_ref, w2v_ref, kng_ref, bd128_ref, cos_ref, sa_ref, sb_ref,
                     kcmp_out, vcmp_out):
    nch = kc_ref.shape[1]

    def compress(x, w1a, w1b, pea, peb, w2):
        a = _dot((x + pea).astype(BF16), w1a)
        b = _dot((x + peb).astype(BF16), w1b)
        hid = _gelu_tanh(a + pltpu.roll(b, nch - 1, 0))
        return _dot(hid.astype(BF16), w2)

    kc = compress(kc_ref[0], w1ak_ref[...], w1bk_ref[...], peak_ref[...], pebk_ref[...], w2k_ref[...])
    vc = compress(vc_ref[0], w1av_ref[...], w1bv_ref[...], peav_ref[...], pebv_ref[...], w2v_ref[...])
    kc = _head_rms_rope(kc, kng_ref[...], bd128_ref[...], cos_ref[...], sa_ref[...], sb_ref[...])
    kd, vd = _dup_halves(kc), _dup_halves(vc)
    for g in range(N_KV_HEADS):
        kcmp_out[0, g] = kd[g].astype(BF16)
        vcmp_out[0, g] = vd[g].T.astype(BF16)


def _compress(kc3, vc3, w1ak, w1bk, w1av, w1bv, peak, pebk, peav, pebv, w2k, w2v, kng0, bd128, cos, sa, sb):
    batch, nch, wide = kc3.shape
    full = lambda a: pl.BlockSpec(a.shape, lambda b: (0,) * a.ndim)
    tok = pl.BlockSpec((1, nch, wide), lambda b: (b, 0, 0))
    kout = pl.BlockSpec((1, N_KV_HEADS, nch, LANES), lambda b: (b, 0, 0, 0))
    vout = pl.BlockSpec((1, N_KV_HEADS, LANES, nch), lambda b: (b, 0, 0, 0))
    consts = (w1ak, w1bk, w1av, w1bv, peak, pebk, peav, pebv, w2k, w2v, kng0, bd128, cos, sa, sb)
    return pl.pallas_call(
        _compress_kernel,
        grid=(batch,),
        in_specs=[tok, tok] + [full(a) for a in consts],
        out_specs=[kout, vout],
        out_shape=[jax.ShapeDtypeStruct((batch, N_KV_HEADS, nch, LANES), BF16),
                   jax.ShapeDtypeStruct((batch, N_KV_HEADS, LANES, nch), BF16)],
        compiler_params=pltpu.CompilerParams(dimension_semantics=("arbitrary",),
                                             vmem_limit_bytes=VMEM_LIMIT_BYTES),
        name="compress",
    )(kc3, vc3, *consts)


SEL_TILE = 256
COLS = Q_PER_KV * Q_BLOCK
PIPE = 2
SWEEP_UNROLL = 2
assert SWEEP_UNROLL % PIPE == 0
WIN_MASK_ROWS = 32
WIN_EXP_ROWS = 64
assert (WINDOW + Q_BLOCK) // WIN_MASK_ROWS + (WINDOW + Q_BLOCK) // WIN_EXP_ROWS <= 32


def _split2(x):
    hi = x.astype(BF16)
    return hi, (x - hi.astype(F32)).astype(BF16)


def _zero_after(x):
    return lax.shift_right_logical(lax.shift_right_logical(x, 16), 16)


def _normalize_dup(acc):
    o = acc[:HEAD_DIM] / acc[HEAD_DIM:HEAD_DIM + 1]
    return jnp.concatenate([o, o], axis=0)


def _attn_kernel(q_ref, ksel_ref, vst_ref, kw_ref, vwt_ref, kcmp_ref, vcmpt_ref, c2st_ref, gate_ref, o_ref,
                 s_all, p_all, a_all, m_scr, acc_scr, acct_scr, cw_scr, sw_scr, pw_scr):
    i = pl.program_id(2)
    start = i * Q_BLOCK
    nch = kcmp_ref.shape[2]
    last_tile = ksel_ref.shape[2] // SEL_TILE - 1

    lane = lax.broadcasted_iota(jnp.int32, (Q_BLOCK, LANES), 1)
    lo_half = lane < HEAD_DIM
    top_half = lax.broadcasted_iota(jnp.int32, (LANES, Q_BLOCK), 0) < HEAD_DIM
    zero_bf = jnp.zeros((LANES, Q_BLOCK), BF16)
    qt = jnp.concatenate([jnp.where(top_half if r % 2 == 0 else ~top_half, q_ref[0, 0, r // 2], zero_bf)
                          for r in range(Q_PER_KV)], axis=1)
    tq = start + (lax.broadcasted_iota(jnp.int32, (1, COLS), 1) & (Q_BLOCK - 1))
    gt = gate_ref[0, 0, 0]
    grow = lambda b: jnp.concatenate([gt[r * N_BRANCH + b:r * N_BRANCH + b + 1, :] for r in range(Q_PER_KV)],
                                     axis=1)

    wk = WINDOW + Q_BLOCK
    base = pl.multiple_of(jnp.maximum(start - WINDOW, 0), Q_BLOCK)

    wrow = lax.broadcasted_iota(jnp.int32, (WIN_MASK_ROWS, COLS), 0)

    def window_mask_piece(k, colmax, zero):
        rows = slice(WIN_MASK_ROWS * k, WIN_MASK_ROWS * (k + 1))
        back = (tq + zero - base - WIN_MASK_ROWS * k) - wrow
        in_window = lax.bitcast_convert_type(back, jnp.uint32) < jnp.uint32(WINDOW)
        sm = jnp.where(in_window, sw_scr[rows, :], NEG)
        sw_scr[rows, :] = sm
        for c in range(WIN_MASK_ROWS // SUBLANES):
            colmax = jnp.maximum(colmax, sm[c * SUBLANES:(c + 1) * SUBLANES])
        return colmax

    def window_exp_piece(k, m_w, zero):
        rows = slice(WIN_EXP_ROWS * k, WIN_EXP_ROWS * (k + 1))
        pw_scr[rows, :] = jnp.exp2(sw_scr[rows, :] - (m_w + zero.astype(F32))).astype(BF16)

    s_c = _dot(kcmp_ref[0, 0], qt)
    sw_scr[...] = _dot(kw_ref[0, 0, pl.ds(base, wk), :], qt)
    nrow = lax.broadcasted_iota(jnp.int32, (nch, COLS), 0)
    last_valid = lax.shift_right_arithmetic(tq - (CMP_BLOCK - 1), CMP_SHIFT)
    s_c = jnp.where(nrow <= last_valid, s_c, NEG)
    m_c = jnp.max(s_c, axis=0, keepdims=True)
    e_c = jnp.exp2(s_c - m_c)
    l_c = jnp.sum(e_c, axis=0, keepdims=True)
    p_c = e_c * jnp.where(m_c > 0.5 * NEG, 1.0 / l_c, 0.0)
    cw_scr[...] = grow(0) * _dot(vcmpt_ref[0, 0], p_c.astype(BF16))

    psum = sum(p_c[:, r * Q_BLOCK:(r + 1) * Q_BLOCK] for r in range(Q_PER_KV))
    c2st = c2st_ref[...]
    impt = sum(_dot(c2st, t) for t in _split2(psum))
    srow = lax.broadcasted_iota(jnp.int32, (N_SLC_LANES, Q_BLOCK), 0)
    tqq = start + lax.broadcasted_iota(jnp.int32, (N_SLC_LANES, Q_BLOCK), 1)
    cur = lax.shift_right_logical(tqq, SLC_SHIFT)
    forced = (srow == 0) | (srow == cur) | (srow == cur - 1)
    future = srow * SLC_BLOCK > tqq
    impt = jnp.where(forced, 1e9, jnp.where(future, -1e9, impt))
    key_to_float = lambda k: lax.bitcast_convert_type(jnp.where(k >= 0, k, k ^ jnp.int32(0x7FFFFFFF)), F32)
    thr = jnp.full((1, Q_BLOCK), INT32_MIN, jnp.int32)
    n_mask, n_exp = wk // WIN_MASK_ROWS, wk // WIN_EXP_ROWS
    colmax = jnp.full((SUBLANES, COLS), NEG, F32)
    wrap32 = lambda v: ((v + 2 ** 31) % 2 ** 32) - 2 ** 31
    piece = 0
    for b in range(30, -1, -2):
        reached = None
        for mult in (1, 2, 3):
            cand = thr + jnp.int32(wrap32(mult << b))
            n_ge = jnp.sum(jnp.where(impt >= key_to_float(cand), 1.0, 0.0), axis=0, keepdims=True)
            reached = jnp.where(n_ge >= N_SELECT, cand, thr if reached is None else reached)
        thr = reached
        zero = jnp.concatenate([_zero_after(thr)] * Q_PER_KV, axis=1)
        for _ in range(2):
            if piece < n_mask:
                colmax = window_mask_piece(piece, colmax, zero)
            elif piece < n_mask + n_exp:
                if piece == n_mask:
                    m_w = jnp.max(colmax, axis=0, keepdims=True)
                window_exp_piece(piece - n_mask, m_w, zero)
            piece += 1
    wtile = base // LANES
    vw = jnp.concatenate([vwt_ref[0, 0, wtile + c] for c in range(wk // LANES)], axis=1)
    cw_scr[...] += grow(2) * _normalize_dup(_dot(vw, pw_scr[...]))
    kth = key_to_float(thr)
    above = impt > kth
    tied = impt == kth
    n_above = jnp.sum(jnp.where(above, 1.0, 0.0), axis=0, keepdims=True)
    scol = lax.broadcasted_iota(jnp.int32, (N_SLC_LANES, N_SLC_LANES), 1)
    earlier = jnp.where(scol < srow, 1.0, 0.0).astype(BF16)
    tied_before = _dot(earlier, jnp.where(tied, 1.0, 0.0).astype(BF16))
    selected = above | (tied & (tied_before < N_SELECT - n_above))
    bias = jnp.where(selected, 0.0, NEG)
    n_main = start // SEL_TILE
    bias_main = jnp.where(srow >= n_main * (SEL_TILE // SLC_BLOCK), NEG, bias)
    widen = lambda b: jnp.concatenate([b.astype(BF16)] * Q_PER_KV, axis=1)
    qt_tail = jnp.concatenate([qt, widen(bias)], axis=0)
    qt_main = jnp.concatenate([qt, widen(bias_main)], axis=0)

    kt = pl.multiple_of(n_main * SEL_TILE, SEL_TILE)
    s_t = _dot(ksel_ref[0, 0, pl.ds(kt, SEL_TILE), :], qt_tail)
    krow = lax.broadcasted_iota(jnp.int32, (SEL_TILE, COLS), 0)
    s_t = jnp.where(kt + krow <= tq, s_t, NEG)
    m_t = jnp.max(s_t, axis=0, keepdims=True)
    acct_scr[...] = _dot(vst_ref[0, 0, n_main], jnp.exp2(s_t - m_t).astype(BF16))
    g_sel = grow(1)

    def scores(j, slot):
        k0 = pl.multiple_of(jnp.minimum(j, last_tile) * SEL_TILE, SEL_TILE)
        s_all[slot] = _dot(ksel_ref[0, 0, pl.ds(k0, SEL_TILE), :], qt_main)

    def softmax(slot):
        s = s_all[slot]
        m_prev = m_scr[0:1, :]
        m_new = jnp.maximum(m_prev, jnp.max(s, axis=0, keepdims=True))
        a_all[slot] = jnp.broadcast_to(jnp.exp2(m_prev - m_new), (SUBLANES, COLS))
        p_all[slot] = jnp.exp2(s - m_new).astype(BF16)
        m_scr[...] = jnp.broadcast_to(m_new, m_scr.shape)

    def values(j, slot):
        acc_scr[...] = (a_all[slot, 0:1, :] * acc_scr[...]
                        + _dot(vst_ref[0, 0, jnp.minimum(j, last_tile)], p_all[slot]))

    m_scr[...] = jnp.full(m_scr.shape, NEG, F32)
    acc_scr[...] = jnp.zeros(acc_scr.shape, F32)
    for k in range(PIPE):
        scores(k, k)
    for k in range(PIPE // 2):
        softmax(k)

    def sweep(t, carry):
        for k in range(SWEEP_UNROLL):
            values(SWEEP_UNROLL * t + k, k % PIPE)
            softmax((k + PIPE // 2) % PIPE)
            scores(SWEEP_UNROLL * t + k + PIPE, k % PIPE)
        return carry

    lax.fori_loop(0, (n_main + SWEEP_UNROLL - 1) // SWEEP_UNROLL, sweep, 0)
    m_p = m_scr[0:1, :]
    m_tot = jnp.maximum(m_p, m_t)
    o_s = _normalize_dup(jnp.exp2(m_p - m_tot) * acc_scr[...] + jnp.exp2(m_t - m_tot) * acct_scr[...])

    comb = cw_scr[...] + g_sel * o_s
    outs = [comb[:, r * Q_BLOCK:(r + 1) * Q_BLOCK].T for r in range(Q_PER_KV)]
    o_ref[0] = jnp.concatenate([jnp.where(lo_half, outs[0], outs[1]), jnp.where(lo_half, outs[2], outs[3])],
                               axis=1)


def _attn(qpt, ksel, vst, kw, vwt, kcmp, vcmpt, c2st, gatest):
    batch, seq = qpt.shape[0], qpt.shape[1] * Q_BLOCK
    per_group = lambda a: pl.BlockSpec((1, 1) + a.shape[2:], lambda b, g, i: (b, g) + (0,) * (a.ndim - 2))
    return pl.pallas_call(
        _attn_kernel,
        grid=(batch, N_KV_HEADS, seq // Q_BLOCK),
        in_specs=[pl.BlockSpec((1, 1, Q_PER_KV // 2, LANES, Q_BLOCK), lambda b, g, i: (b, i, g, 0, 0)),
                  per_group(ksel), per_group(vst), per_group(kw), per_group(vwt), per_group(kcmp),
                  per_group(vcmpt), pl.BlockSpec(c2st.shape, lambda b, g, i: (0, 0)),
                  pl.BlockSpec((1, 1, 1, GATE_ROWS, Q_BLOCK), lambda b, g, i: (b, i, g, 0, 0))],
        out_specs=pl.BlockSpec((1, Q_BLOCK, Q_PER_KV * HEAD_DIM), lambda b, g, i: (b, i, g)),
        out_shape=jax.ShapeDtypeStruct((batch, seq, D_ATT), F32),
        scratch_shapes=[pltpu.VMEM((PIPE, SEL_TILE, COLS), F32), pltpu.VMEM((PIPE, SEL_TILE, COLS), BF16),
                        pltpu.VMEM((PIPE, SUBLANES, COLS), F32),
                        pltpu.VMEM((SUBLANES, COLS), F32), pltpu.VMEM((V_ROWS, COLS), F32),
                        pltpu.VMEM((V_ROWS, COLS), F32), pltpu.VMEM((LANES, COLS), F32),
                        pltpu.VMEM((WINDOW + Q_BLOCK, COLS), F32), pltpu.VMEM((WINDOW + Q_BLOCK, COLS), BF16)],
        compiler_params=pltpu.CompilerParams(dimension_semantics=("arbitrary", "arbitrary", "arbitrary"),
                                             vmem_limit_bytes=VMEM_LIMIT_BYTES),
        name="attn",
    )(qpt, ksel, vst, kw, vwt, kcmp, vcmpt, c2st, gatest)


FF_CHUNKS = 2
FF_CHUNK = D_FF // FF_CHUNKS


def _post_kernel(x_ref, oatt_ref, mconv_ref, p_ref, onag_ref, wo_ref, lnf_ref, wup_ref, fcw_ref, fcb_ref,
                 wdn_ref, lnp_ref, wpg_ref, wpe_ref, out_ref, gbuf, *, tiles_per_seq):
    tm = x_ref.shape[0]
    it = pl.program_id(0) % tiles_per_seq

    mixed = jnp.concatenate([_rms(oatt_ref[...], onag_ref[...]).astype(BF16), mconv_ref[...]], axis=1)
    h1 = x_ref[...] + _dot(mixed, wo_ref[...])
    xn = _rms(h1, lnf_ref[...]).astype(BF16)

    @pl.when(it == 0)
    def _():
        gbuf[:, 0:SUBLANES, :] = jnp.zeros((FF_CHUNKS, SUBLANES, FF_CHUNK), F32)

    h2 = h1
    for c in range(FF_CHUNKS):
        cols = slice(c * FF_CHUNK, (c + 1) * FF_CHUNK)
        gpre = _dot(xn, wup_ref[:, cols])
        up = _dot(xn, wup_ref[:, D_FF + c * FF_CHUNK:D_FF + (c + 1) * FF_CHUNK])
        gbuf[c, SUBLANES:SUBLANES + tm, :] = gpre
        gate = (fcw_ref[2:3, cols] * gpre + fcw_ref[1:2, cols] * gbuf[c, SUBLANES - 1:SUBLANES - 1 + tm, :]
                + fcw_ref[0:1, cols] * gbuf[c, SUBLANES - 2:SUBLANES - 2 + tm, :]) + fcb_ref[:, cols]
        gbuf[c, 0:SUBLANES, :] = gbuf[c, tm:tm + SUBLANES, :]
        act = (gate * jax.nn.sigmoid(gate) * up).astype(BF16)
        h2 = h2 + _dot(act, wdn_ref[cols, :])

    xn2 = _rms(h2, lnp_ref[...]).astype(BF16)
    out_ref[...] = h2 + jax.nn.sigmoid(_dot(xn2, wpg_ref[...])) * _dot(p_ref[...].astype(BF16), wpe_ref[...])


def _post(x2, oatt, mconv, p2, onag, wo, lnf, wup, fcw, fcb, wdn, lnp, wpg, wpe, *, seq, tm):
    n = x2.shape[0]
    tps = seq // tm
    row = lambda w: pl.BlockSpec((tm, w), lambda i: (i, 0))
    full = lambda a: pl.BlockSpec(a.shape, lambda i: (0,) * a.ndim, pipeline_mode=pl.Buffered(1))
    return pl.pallas_call(
        functools.partial(_post_kernel, tiles_per_seq=tps),
        grid=(n // tm,),
        in_specs=[row(D_MODEL), row(D_ATT), row(D_CONV), row(D_PLE), full(onag), full(wo), full(lnf), full(wup),
                  full(fcw), full(fcb), full(wdn), full(lnp), full(wpg), full(wpe)],
        out_specs=row(D_MODEL),
        out_shape=jax.ShapeDtypeStruct((n, D_MODEL), F32),
        scratch_shapes=[pltpu.VMEM((FF_CHUNKS, tm + SUBLANES, FF_CHUNK), F32)],
        compiler_params=pltpu.CompilerParams(dimension_semantics=("arbitrary",),
                                             vmem_limit_bytes=VMEM_LIMIT_BYTES),
        name="post",
    )(x2, oatt, mconv, p2, onag, wo, lnf, wup, fcw, fcb, wdn, lnp, wpg, wpe)


def _rope_tables(pos):
    half = ROT_DIM // 2
    d = np.arange(LANES) % HEAD_DIM
    inv_freq = jnp.tile(ROPE_THETA ** (-jnp.arange(half, dtype=F32) * 2.0 / ROT_DIM), LANES // half)
    ang = pos.astype(F32)[:, None] * inv_freq[None, :]
    c, sn = jnp.cos(ang), jnp.sin(ang)
    first, second = (d < half)[None, :], ((d >= half) & (d < ROT_DIM))[None, :]
    cos = jnp.where(first | second, c, 1.0)
    sa = jnp.where(first, -sn, 0.0)
    sb = jnp.where(second, sn, 0.0)
    return cos, sa, sb


def _block_diag_mean(width):
    idx = np.arange(width) // HEAD_DIM
    return jnp.asarray((idx[:, None] == idx[None, :]).astype(np.float32) / HEAD_DIM, BF16)


def _cmp_to_slc_t(nch):
    cs = CMP_STRIDE * np.arange(nch)[None, :]
    ss = SLC_BLOCK * np.arange(N_SLC_LANES)[:, None]
    ov = np.clip(np.minimum(cs + CMP_BLOCK, ss + SLC_BLOCK) - np.maximum(cs, ss), 0, None)
    return jnp.asarray(ov.astype(np.float32) / CMP_BLOCK, BF16)


def _layer(h, p_l, ln_mix_g, w_in, qn_g, kn_g, pe_k, pe_v, w_ck1, w_ck2, w_cv1, w_cv2, conv_w, on_att_g,
           on_conv_g, w_o, ln_ffn_g, w_up, ffn_conv_w, ffn_conv_b, w_down, ln_ple_g, w_pg, w_pe):
    batch, seq, _ = h.shape
    assert seq % SEL_TILE == 0 and seq // SLC_BLOCK <= N_SLC_LANES and seq // SLC_BLOCK >= N_SELECT
    assert seq >= WINDOW + Q_BLOCK
    n = batch * seq
    nch = seq // CMP_STRIDE
    x2 = h.reshape(n, D_MODEL)
    row = lambda v: v.reshape(1, -1).astype(F32)

    o_q, o_kv, o_g, o_cv = 0, D_ATT, D_ATT + 6 * KV_W, D_ATT + 6 * KV_W + N_BRANCH * N_Q_HEADS
    wq = w_in[:, o_q:o_kv].astype(BF16)
    wkv = w_in[:, o_kv:o_g].astype(BF16)
    wcv = w_in[:, o_cv:].astype(BF16)
    per_g = Q_PER_KV * N_BRANCH
    wg = jnp.concatenate(
        [jnp.pad(w_in[:, o_g + g * per_g:o_g + (g + 1) * per_g], ((0, 0), (0, LANES - per_g)))
         for g in range(N_KV_HEADS)], axis=1).astype(BF16)
    cos, sa, sb = _rope_tables(jnp.arange(seq))
    bd512, bd128 = _block_diag_mean(D_ATT), _block_diag_mean(KV_W)
    tile_heads = lambda v, k: jnp.tile(v.astype(F32), k).reshape(1, -1)

    tm = SEL_TILE
    qpt, kc2, vc2, ksel, vst, kw, vwt, gatest, mconv = _inproj(
        x2, row(ln_mix_g), wq, wkv, wg, wcv, tile_heads(qn_g, N_Q_HEADS),
        jnp.stack([jnp.tile(kn_g[1], N_KV_HEADS), jnp.tile(kn_g[2], N_KV_HEADS)]).astype(F32),
        bd512, bd128, cos, sa, sb, conv_w.astype(F32), row(on_conv_g), batch=batch, seq=seq, tm=tm)

    eye = jnp.eye(N_KV_HEADS, dtype=F32)
    half = CMP_BLOCK // 2

    def w1_parts(w1):
        wfull = jnp.einsum('ldh,gk->lgdkh', w1, eye).reshape(CMP_BLOCK, KV_W, N_KV_HEADS * CMP_HIDDEN)
        return (wfull[:half].reshape(half * KV_W, -1).astype(BF16),
                wfull[half:].reshape(half * KV_W, -1).astype(BF16))

    def pe_parts(pe):
        pf = jnp.broadcast_to(pe[:, None, :], (CMP_BLOCK, N_KV_HEADS, HEAD_DIM)).astype(F32)
        return pf[:half].reshape(1, -1), pf[half:].reshape(1, -1)

    w2bd = lambda w2: jnp.einsum('hd,gk->ghkd', w2, eye).reshape(N_KV_HEADS * CMP_HIDDEN, KV_W).astype(BF16)
    w1ak, w1bk = w1_parts(w_ck1)
    w1av, w1bv = w1_parts(w_cv1)
    peak, pebk = pe_parts(pe_k)
    peav, pebv = pe_parts(pe_v)
    ccos, csa, csb = _rope_tables(CMP_STRIDE * jnp.arange(nch) + CMP_BLOCK - 1)
    kcmp, vcmpt = _compress(kc2.reshape(batch, nch, CMP_STRIDE * KV_W), vc2.reshape(batch, nch, CMP_STRIDE * KV_W),
                            w1ak, w1bk, w1av, w1bv, peak, pebk, peav, pebv, w2bd(w_ck2), w2bd(w_cv2),
                            tile_heads(kn_g[0], N_KV_HEADS), bd128, ccos, csa, csb)

    oatt = _attn(qpt, ksel, vst, kw, vwt, kcmp, vcmpt, _cmp_to_slc_t(nch), gatest)

    out = _post(x2, oatt.reshape(n, D_ATT), mconv, p_l.reshape(n, D_PLE), row(on_att_g), w_o.astype(BF16),
                row(ln_ffn_g), w_up.astype(BF16), ffn_conv_w.astype(F32), row(ffn_conv_b), w_down.astype(BF16),
                row(ln_ple_g), w_pg.astype(BF16), w_pe.astype(BF16), seq=seq, tm=tm)
    return out.reshape(batch, seq, D_MODEL)


def kernel(x, p, ln_mix_g, w_in, qn_g, kn_g, pe_k, pe_v, w_ck1, w_ck2, w_cv1, w_cv2, conv_w, on_att_g,
           on_conv_g, w_o, ln_ffn_g, w_up, ffn_conv_w, ffn_conv_b, w_down, ln_ple_g, w_pg, w_pe):
    h = x
    for i in range(p.shape[0]):
        h = _layer(h, p[i], ln_mix_g[i], w_in[i], qn_g[i], kn_g[i], pe_k[i], pe_v[i], w_ck1[i], w_ck2[i],
                   w_cv1[i], w_cv2[i], conv_w[i], on_att_g[i], on_conv_g[i], w_o[i], ln_ffn_g[i], w_up[i],
                   ffn_conv_w[i], ffn_conv_b[i], w_down[i], ln_ple_g[i], w_pg[i], w_pe[i])
    return h
```

```python
import functools

import jax
import jax.numpy as jnp
import numpy as np
from jax import lax
from jax.experimental import pallas as pl
from jax.experimental.pallas import tpu as pltpu

D_MODEL = 1024
HEAD_DIM = 64
N_Q_HEADS = 8
N_KV_HEADS = 2
Q_PER_KV = N_Q_HEADS // N_KV_HEADS
D_ATT = N_Q_HEADS * HEAD_DIM
D_CONV = D_MODEL - D_ATT
KV_W = N_KV_HEADS * HEAD_DIM
N_BRANCH = 3
CONV_TAPS = 3
ROT_DIM = HEAD_DIM // 4
ROPE_THETA = 500000.0
CMP_BLOCK = 32
CMP_STRIDE = 16
CMP_HIDDEN = 256
SLC_BLOCK = 64
N_SELECT = 16
WINDOW = 512
Q_BLOCK = 128
D_FF = 2816
D_PLE = 256
EPS = 1e-6
NEG = -1e30
INT32_MIN = -2 ** 31
LOG2E = float(np.log2(np.e))
V_ROWS = HEAD_DIM + 16
GATE_ROWS = 16
SLC_SHIFT = SLC_BLOCK.bit_length() - 1
CMP_SHIFT = CMP_STRIDE.bit_length() - 1
assert 1 << SLC_SHIFT == SLC_BLOCK and 1 << CMP_SHIFT == CMP_STRIDE

LANES = 128
SUBLANES = 8
N_SLC_LANES = LANES
VMEM_LIMIT_BYTES = 56 * 1024 * 1024

F32 = jnp.float32
BF16 = jnp.bfloat16


def _dot(a, b):
    return jnp.dot(a, b, preferred_element_type=F32)


def _rms(x, g):
    return x * lax.rsqrt(jnp.mean(x * x, axis=-1, keepdims=True) + EPS) * g


def _head_rms_rope(x, g, bd, cos, sa, sb):
    w = x.shape[-1]
    msq = _dot((x * x).astype(BF16), bd)
    xn = x * lax.rsqrt(msq + EPS) * g
    return xn * cos + pltpu.roll(xn, w - ROT_DIM // 2, 1) * sa + pltpu.roll(xn, ROT_DIM // 2, 1) * sb


def _dup_halves(x):
    r = pltpu.roll(x, HEAD_DIM, 1)
    lane = lax.broadcasted_iota(jnp.int32, x.shape, 1)
    lo = lane < HEAD_DIM
    return jnp.where(lo, x, r), jnp.where(lo, r, x)


def _inproj_kernel(x_ref, lng_ref, wq_ref, wkv_ref, wg_ref, wcv_ref, qng_ref, kng_ref, bd512_ref, bd128_ref,
                   cos_ref, sa_ref, sb_ref, convw_ref, oncg_ref,
                   q_out, kc_out, vc_out, ksel_out, vs_out, kw_out, vw_out, gate_out, mconv_out,
                   zbuf, *, tiles_per_seq):
    tm = x_ref.shape[0]
    it = pl.program_id(0) % tiles_per_seq
    x = x_ref[...]
    xn = _rms(x, lng_ref[...]).astype(BF16)

    cos, sa, sb = cos_ref[...], sa_ref[...], sb_ref[...]
    cos4, sa4, sb4 = (jnp.concatenate([t] * 4, axis=1) for t in (cos, sa, sb))
    q = _dot(xn, wq_ref[...])
    qr = _head_rms_rope(q, qng_ref[...], bd512_ref[...], cos4, sa4, sb4)
    qs = qr * (HEAD_DIM ** -0.5 * LOG2E)
    for blk_i in range(tm // Q_BLOCK):
        rows = slice(blk_i * Q_BLOCK, (blk_i + 1) * Q_BLOCK)
        for pr in range(N_Q_HEADS // 2):
            q_out[0, blk_i, pr] = qs[rows, pr * LANES:(pr + 1) * LANES].T.astype(BF16)

    kv = _dot(xn, wkv_ref[...])
    kc_out[...] = kv[:, 0 * KV_W:1 * KV_W]
    vc_out[...] = kv[:, 1 * KV_W:2 * KV_W]
    ks = _head_rms_rope(kv[:, 2 * KV_W:3 * KV_W], kng_ref[0:1, :], bd128_ref[...], cos, sa, sb)
    kw = _head_rms_rope(kv[:, 4 * KV_W:5 * KV_W], kng_ref[1:2, :], bd128_ref[...], cos, sa, sb)
    vs = kv[:, 3 * KV_W:4 * KV_W]
    vw = kv[:, 5 * KV_W:6 * KV_W]
    tpos = it * tm + lax.broadcasted_iota(jnp.int32, (tm, N_SLC_LANES), 0)
    blk = lax.broadcasted_iota(jnp.int32, (tm, N_SLC_LANES), 1)
    onehot = jnp.where(lax.shift_right_logical(tpos, SLC_SHIFT) == blk, 1.0, 0.0).astype(BF16)
    ks_d, vs_d, kw_d, vw_d = (_dup_halves(t) for t in (ks, vs, kw, vw))
    vrow = lax.broadcasted_iota(jnp.int32, (V_ROWS, tm), 0)
    for g in range(N_KV_HEADS):
        ksel_out[0, g] = jnp.concatenate([ks_d[g].astype(BF16), onehot], axis=1)
        kw_out[0, g] = kw_d[g].astype(BF16)
        vst = jnp.where(vrow < HEAD_DIM, vs_d[g].T[:V_ROWS], 1.0).astype(BF16)
        for c in range(tm // SEL_TILE):
            vs_out[0, g, c] = vst[:, c * SEL_TILE:(c + 1) * SEL_TILE]
        vwt = jnp.where(vrow < HEAD_DIM, vw_d[g].T[:V_ROWS], 1.0).astype(BF16)
        for c in range(tm // LANES):
            vw_out[0, g, c] = vwt[:, c * LANES:(c + 1) * LANES]

    gates = jax.nn.sigmoid(_dot(xn, wg_ref[...]))
    for blk_i in range(tm // Q_BLOCK):
        rows = slice(blk_i * Q_BLOCK, (blk_i + 1) * Q_BLOCK)
        for g in range(N_KV_HEADS):
            gate_out[0, blk_i, g] = gates[rows, g * LANES:(g + 1) * LANES].T[:GATE_ROWS]

    cv = _dot(xn, wcv_ref[...])
    cb, cc, cx = cv[:, :D_CONV], cv[:, D_CONV:2 * D_CONV], cv[:, 2 * D_CONV:]
    z = cc * cx

    @pl.when(it == 0)
    def _():
        zbuf[0:SUBLANES, :] = jnp.zeros((SUBLANES, D_CONV), F32)

    zbuf[SUBLANES:SUBLANES + tm, :] = z
    y = (convw_ref[2:3, :] * z + convw_ref[1:2, :] * zbuf[SUBLANES - 1:SUBLANES - 1 + tm, :]
         + convw_ref[0:1, :] * zbuf[SUBLANES - 2:SUBLANES - 2 + tm, :])
    zbuf[0:SUBLANES, :] = zbuf[tm:tm + SUBLANES, :]
    mconv_out[...] = _rms(cb * y, oncg_ref[...]).astype(BF16)


def _inproj(x2, lng, wq, wkv, wg, wcv, qng, kng, bd512, bd128, cos, sa, sb, convw, oncg, *, batch, seq, tm):
    n = batch * seq
    tps = seq // tm
    row = lambda w: pl.BlockSpec((tm, w), lambda i: (i, 0))
    full = lambda a: pl.BlockSpec(a.shape, lambda i: (0,) * a.ndim)
    tab = pl.BlockSpec((tm, LANES), lambda i: (i % tps, 0))
    grp = lambda w: pl.BlockSpec((1, N_KV_HEADS, tm, w), lambda i: (i // tps, 0, i % tps, 0))
    gshape = lambda w: jax.ShapeDtypeStruct((batch, N_KV_HEADS, seq, w), BF16)
    assert tm % SEL_TILE == 0
    vtile = lambda keys: pl.BlockSpec((1, N_KV_HEADS, tm // keys, V_ROWS, keys),
                                      lambda i: (i // tps, 0, i % tps, 0, 0))
    vshape = lambda keys: jax.ShapeDtypeStruct((batch, N_KV_HEADS, seq // keys, V_ROWS, keys), BF16)
    qblk = lambda tile: pl.BlockSpec((1, tm // Q_BLOCK) + tile, lambda i: (i // tps, i % tps) + (0,) * len(tile))
    return pl.pallas_call(
        functools.partial(_inproj_kernel, tiles_per_seq=tps),
        grid=(n // tm,),
        in_specs=[row(D_MODEL), full(lng), full(wq), full(wkv), full(wg), full(wcv), full(qng), full(kng),
                  full(bd512), full(bd128), tab, tab, tab, full(convw), full(oncg)],
        out_specs=[qblk((N_Q_HEADS // 2, LANES, Q_BLOCK)), row(KV_W), row(KV_W), grp(2 * LANES), vtile(SEL_TILE),
                   grp(LANES), vtile(LANES), qblk((N_KV_HEADS, GATE_ROWS, Q_BLOCK)), row(D_CONV)],
        out_shape=[jax.ShapeDtypeStruct((batch, seq // Q_BLOCK, N_Q_HEADS // 2, LANES, Q_BLOCK), BF16),
                   jax.ShapeDtypeStruct((n, KV_W), F32),
                   jax.ShapeDtypeStruct((n, KV_W), F32), gshape(2 * LANES), vshape(SEL_TILE), gshape(LANES),
                   vshape(LANES),
                   jax.ShapeDtypeStruct((batch, seq // Q_BLOCK, N_KV_HEADS, GATE_ROWS, Q_BLOCK), F32),
                   jax.ShapeDtypeStruct((n, D_CONV), BF16)],
        scratch_shapes=[pltpu.VMEM((tm + SUBLANES, D_CONV), F32)],
        compiler_params=pltpu.CompilerParams(dimension_semantics=("arbitrary",),
                                             vmem_limit_bytes=VMEM_LIMIT_BYTES),
        name="inproj",
    )(x2, lng, wq, wkv, wg, wcv, qng, kng, bd512, bd128, cos, sa, sb, convw, oncg)


def _gelu_tanh(x):
    return 0.5 * x * (1.0 + jnp.tanh(np.sqrt(2.0 / np.pi).astype(np.float32) * (x + 0.044715 * (x * x * x))))


def _compress_kernel(kc_ref, vc_ref, w1ak_ref, w1bk_ref, w1av_ref, w1bv_ref, peak_ref, pebk_ref, peav_ref,
                     pebv_ref, w2k_ref, w2v_ref, kng_ref, bd128_ref, cos_ref, sa_ref, sb_ref,
                     kcmp_out, vcmp_out):
    nch = kc_ref.shape[1]

    def compress(x, w1a, w1b, pea, peb, w2):
        a = _dot((x + pea).astype(BF16), w1a)
        b = _dot((x + peb).astype(BF16), w1b)
        hid = _gelu_tanh(a + pltpu.roll(b, nch - 1, 0))
        return _dot(hid.astype(BF16), w2)

    kc = compress(kc_ref[0], w1ak_ref[...], w1bk_ref[...], peak_ref[...], pebk_ref[...], w2k_ref[...])
    vc = compress(vc_ref[0], w1av_ref[...], w1bv_ref[...], peav_ref[...], pebv_ref[...], w2v_ref[...])
    kc = _head_rms_rope(kc, kng_ref[...], bd128_ref[...], cos_ref[...], sa_ref[...], sb_ref[...])
    kd, vd = _dup_halves(kc), _dup_halves(vc)
    for g in range(N_KV_HEADS):
        kcmp_out[0, g] = kd[g].astype(BF16)
        vcmp_out[0, g] = vd[g].T.astype(BF16)


def _compress(kc3, vc3, w1ak, w1bk, w1av, w1bv, peak, pebk, peav, pebv, w2k, w2v, kng0, bd128, cos, sa, sb):
    batch, nch, wide = kc3.shape
    full = lambda a: pl.BlockSpec(a.shape, lambda b: (0,) * a.ndim)
    tok = pl.BlockSpec((1, nch, wide), lambda b: (b, 0, 0))
    kout = pl.BlockSpec((1, N_KV_HEADS, nch, LANES), lambda b: (b, 0, 0, 0))
    vout = pl.BlockSpec((1, N_KV_HEADS, LANES, nch), lambda b: (b, 0, 0, 0))
    consts = (w1ak, w1bk, w1av, w1bv, peak, pebk, peav, pebv, w2k, w2v, kng0, bd128, cos, sa, sb)
    return pl.pallas_call(
        _compress_kernel,
        grid=(batch,),
        in_specs=[tok, tok] + [full(a) for a in consts],
        out_specs=[kout, vout],
        out_shape=[jax.ShapeDtypeStruct((batch, N_KV_HEADS, nch, LANES), BF16),
                   jax.ShapeDtypeStruct((batch, N_KV_HEADS, LANES, nch), BF16)],
        compiler_params=pltpu.CompilerParams(dimension_semantics=("arbitrary",),
                                             vmem_limit_bytes=VMEM_LIMIT_BYTES),
        name="compress",
    )(kc3, vc3, *consts)


SEL_TILE = 256
COLS = Q_PER_KV * Q_BLOCK
ROW_TILE = 512
POST_TILE = 256
PIPE = 2
SWEEP_UNROLL = 2
assert SWEEP_UNROLL % PIPE == 0
WIN_MASK_ROWS = 32
WIN_EXP_ROWS = 64
assert (WINDOW + Q_BLOCK) // WIN_MASK_ROWS + (WINDOW + Q_BLOCK) // WIN_EXP_ROWS <= 32


def _split2(x):
    hi = x.astype(BF16)
    return hi, (x - hi.astype(F32)).astype(BF16)


def _zero_after(x):
    return lax.shift_right_logical(lax.shift_right_logical(x, 16), 16)


def _normalize_dup(acc):
    o = acc[:HEAD_DIM] / acc[HEAD_DIM:HEAD_DIM + 1]
    return jnp.concatenate([o, o], axis=0)


def _attn_kernel_single(q_ref, ksel_ref, vst_ref, kw_ref, vwt_ref, kcmp_ref, vcmpt_ref, c2st_ref, gate_ref, o_ref,
                        s_all, p_all, a_all, m_scr, acc_scr, acct_scr, cw_scr, sw_scr, pw_scr):
    i = pl.program_id(2)
    start = i * Q_BLOCK
    nch = kcmp_ref.shape[2]
    last_tile = ksel_ref.shape[2] // SEL_TILE - 1

    lane = lax.broadcasted_iota(jnp.int32, (Q_BLOCK, LANES), 1)
    lo_half = lane < HEAD_DIM
    top_half = lax.broadcasted_iota(jnp.int32, (LANES, Q_BLOCK), 0) < HEAD_DIM
    zero_bf = jnp.zeros((LANES, Q_BLOCK), BF16)
    qt = jnp.concatenate([jnp.where(top_half if r % 2 == 0 else ~top_half, q_ref[0, 0, r // 2], zero_bf)
                          for r in range(Q_PER_KV)], axis=1)
    tq = start + (lax.broadcasted_iota(jnp.int32, (1, COLS), 1) & (Q_BLOCK - 1))
    gt = gate_ref[0, 0, 0]
    grow = lambda b: jnp.concatenate([gt[r * N_BRANCH + b:r * N_BRANCH + b + 1, :] for r in range(Q_PER_KV)],
                                     axis=1)

    wk = WINDOW + Q_BLOCK
    base = pl.multiple_of(jnp.maximum(start - WINDOW, 0), Q_BLOCK)

    wrow = lax.broadcasted_iota(jnp.int32, (WIN_MASK_ROWS, COLS), 0)

    def window_mask_piece(k, colmax, zero):
        rows = slice(WIN_MASK_ROWS * k, WIN_MASK_ROWS * (k + 1))
        back = (tq + zero - base - WIN_MASK_ROWS * k) - wrow
        in_window = lax.bitcast_convert_type(back, jnp.uint32) < jnp.uint32(WINDOW)
        sm = jnp.where(in_window, sw_scr[rows, :], NEG)
        sw_scr[rows, :] = sm
        for c in range(WIN_MASK_ROWS // SUBLANES):
            colmax = jnp.maximum(colmax, sm[c * SUBLANES:(c + 1) * SUBLANES])
        return colmax

    def window_exp_piece(k, m_w, zero):
        rows = slice(WIN_EXP_ROWS * k, WIN_EXP_ROWS * (k + 1))
        pw_scr[rows, :] = jnp.exp2(sw_scr[rows, :] - (m_w + zero.astype(F32))).astype(BF16)

    s_c = _dot(kcmp_ref[0, 0], qt)
    sw_scr[...] = _dot(kw_ref[0, 0, pl.ds(base, wk), :], qt)
    nrow = lax.broadcasted_iota(jnp.int32, (nch, COLS), 0)
    last_valid = lax.shift_right_arithmetic(tq - (CMP_BLOCK - 1), CMP_SHIFT)
    s_c = jnp.where(nrow <= last_valid, s_c, NEG)
    m_c = jnp.max(s_c, axis=0, keepdims=True)
    e_c = jnp.exp2(s_c - m_c)
    l_c = jnp.sum(e_c, axis=0, keepdims=True)
    p_c = e_c * jnp.where(m_c > 0.5 * NEG, 1.0 / l_c, 0.0)
    cw_scr[...] = grow(0) * _dot(vcmpt_ref[0, 0], p_c.astype(BF16))

    psum = sum(p_c[:, r * Q_BLOCK:(r + 1) * Q_BLOCK] for r in range(Q_PER_KV))
    c2st = c2st_ref[...]
    impt = sum(_dot(c2st, t) for t in _split2(psum))
    srow = lax.broadcasted_iota(jnp.int32, (N_SLC_LANES, Q_BLOCK), 0)
    tqq = start + lax.broadcasted_iota(jnp.int32, (N_SLC_LANES, Q_BLOCK), 1)
    cur = lax.shift_right_logical(tqq, SLC_SHIFT)
    forced = (srow == 0) | (srow == cur) | (srow == cur - 1)
    future = srow * SLC_BLOCK > tqq
    impt = jnp.where(forced, 1e9, jnp.where(future, -1e9, impt))
    key_to_float = lambda k: lax.bitcast_convert_type(jnp.where(k >= 0, k, k ^ jnp.int32(0x7FFFFFFF)), F32)
    thr = jnp.full((1, Q_BLOCK), INT32_MIN, jnp.int32)
    n_mask, n_exp = wk // WIN_MASK_ROWS, wk // WIN_EXP_ROWS
    colmax = jnp.full((SUBLANES, COLS), NEG, F32)
    wrap32 = lambda v: ((v + 2 ** 31) % 2 ** 32) - 2 ** 31
    piece = 0
    for b in range(30, -1, -2):
        reached = None
        for mult in (1, 2, 3):
            cand = thr + jnp.int32(wrap32(mult << b))
            n_ge = jnp.sum(jnp.where(impt >= key_to_float(cand), 1.0, 0.0), axis=0, keepdims=True)
            reached = jnp.where(n_ge >= N_SELECT, cand, thr if reached is None else reached)
        thr = reached
        zero = jnp.concatenate([_zero_after(thr)] * Q_PER_KV, axis=1)
        for _ in range(2):
            if piece < n_mask:
                colmax = window_mask_piece(piece, colmax, zero)
            elif piece < n_mask + n_exp:
                if piece == n_mask:
                    m_w = jnp.max(colmax, axis=0, keepdims=True)
                window_exp_piece(piece - n_mask, m_w, zero)
            piece += 1
    wtile = base // LANES
    vw = jnp.concatenate([vwt_ref[0, 0, wtile + c] for c in range(wk // LANES)], axis=1)
    cw_scr[...] += grow(2) * _normalize_dup(_dot(vw, pw_scr[...]))
    kth = key_to_float(thr)
    above = impt > kth
    tied = impt == kth
    n_above = jnp.sum(jnp.where(above, 1.0, 0.0), axis=0, keepdims=True)
    scol = lax.broadcasted_iota(jnp.int32, (N_SLC_LANES, N_SLC_LANES), 1)
    earlier = jnp.where(scol < srow, 1.0, 0.0).astype(BF16)
    tied_before = _dot(earlier, jnp.where(tied, 1.0, 0.0).astype(BF16))
    selected = above | (tied & (tied_before < N_SELECT - n_above))
    bias = jnp.where(selected, 0.0, NEG)
    n_main = start // SEL_TILE
    bias_main = jnp.where(srow >= n_main * (SEL_TILE // SLC_BLOCK), NEG, bias)
    widen = lambda b: jnp.concatenate([b.astype(BF16)] * Q_PER_KV, axis=1)
    qt_tail = jnp.concatenate([qt, widen(bias)], axis=0)
    qt_main = jnp.concatenate([qt, widen(bias_main)], axis=0)

    kt = pl.multiple_of(n_main * SEL_TILE, SEL_TILE)
    s_t = _dot(ksel_ref[0, 0, pl.ds(kt, SEL_TILE), :], qt_tail)
    krow = lax.broadcasted_iota(jnp.int32, (SEL_TILE, COLS), 0)
    s_t = jnp.where(kt + krow <= tq, s_t, NEG)
    m_t = jnp.max(s_t, axis=0, keepdims=True)
    acct_scr[...] = _dot(vst_ref[0, 0, n_main], jnp.exp2(s_t - m_t).astype(BF16))
    g_sel = grow(1)

    def scores(j, slot):
        k0 = pl.multiple_of(jnp.minimum(j, last_tile) * SEL_TILE, SEL_TILE)
        s_all[slot] = _dot(ksel_ref[0, 0, pl.ds(k0, SEL_TILE), :], qt_main)

    def softmax(slot):
        s = s_all[slot]
        m_prev = m_scr[0:1, :]
        m_new = jnp.maximum(m_prev, jnp.max(s, axis=0, keepdims=True))
        a_all[slot] = jnp.broadcast_to(jnp.exp2(m_prev - m_new), (SUBLANES, COLS))
        p_all[slot] = jnp.exp2(s - m_new).astype(BF16)
        m_scr[...] = jnp.broadcast_to(m_new, m_scr.shape)

    def values(j, slot):
        acc_scr[...] = (a_all[slot, 0:1, :] * acc_scr[...]
                        + _dot(vst_ref[0, 0, jnp.minimum(j, last_tile)], p_all[slot]))

    m_scr[...] = jnp.full(m_scr.shape, NEG, F32)
    acc_scr[...] = jnp.zeros(acc_scr.shape, F32)
    for k in range(PIPE):
        scores(k, k)
    for k in range(PIPE // 2):
        softmax(k)

    def sweep(t, carry):
        for k in range(SWEEP_UNROLL):
            values(SWEEP_UNROLL * t + k, k % PIPE)
            softmax((k + PIPE // 2) % PIPE)
            scores(SWEEP_UNROLL * t + k + PIPE, k % PIPE)
        return carry

    lax.fori_loop(0, (n_main + SWEEP_UNROLL - 1) // SWEEP_UNROLL, sweep, 0)
    m_p = m_scr[0:1, :]
    m_tot = jnp.maximum(m_p, m_t)
    o_s = _normalize_dup(jnp.exp2(m_p - m_tot) * acc_scr[...] + jnp.exp2(m_t - m_tot) * acct_scr[...])

    comb = cw_scr[...] + g_sel * o_s
    outs = [comb[:, r * Q_BLOCK:(r + 1) * Q_BLOCK].T for r in range(Q_PER_KV)]
    o_ref[0] = jnp.concatenate([jnp.where(lo_half, outs[0], outs[1]), jnp.where(lo_half, outs[2], outs[3])],
                               axis=1)


def _attn_single(qpt, ksel, vst, kw, vwt, kcmp, vcmpt, c2st, gatest):
    batch, seq = qpt.shape[0], qpt.shape[1] * Q_BLOCK
    per_group = lambda a: pl.BlockSpec((1, 1) + a.shape[2:], lambda b, g, i: (b, g) + (0,) * (a.ndim - 2))
    return pl.pallas_call(
        _attn_kernel_single,
        grid=(batch, N_KV_HEADS, seq // Q_BLOCK),
        in_specs=[pl.BlockSpec((1, 1, Q_PER_KV // 2, LANES, Q_BLOCK), lambda b, g, i: (b, i, g, 0, 0)),
                  per_group(ksel), per_group(vst), per_group(kw), per_group(vwt), per_group(kcmp),
                  per_group(vcmpt), pl.BlockSpec(c2st.shape, lambda b, g, i: (0, 0)),
                  pl.BlockSpec((1, 1, 1, GATE_ROWS, Q_BLOCK), lambda b, g, i: (b, i, g, 0, 0))],
        out_specs=pl.BlockSpec((1, Q_BLOCK, Q_PER_KV * HEAD_DIM), lambda b, g, i: (b, i, g)),
        out_shape=jax.ShapeDtypeStruct((batch, seq, D_ATT), F32),
        scratch_shapes=[pltpu.VMEM((PIPE, SEL_TILE, COLS), F32), pltpu.VMEM((PIPE, SEL_TILE, COLS), BF16),
                        pltpu.VMEM((PIPE, SUBLANES, COLS), F32),
                        pltpu.VMEM((SUBLANES, COLS), F32), pltpu.VMEM((V_ROWS, COLS), F32),
                        pltpu.VMEM((V_ROWS, COLS), F32), pltpu.VMEM((LANES, COLS), F32),
                        pltpu.VMEM((WINDOW + Q_BLOCK, COLS), F32), pltpu.VMEM((WINDOW + Q_BLOCK, COLS), BF16)],
        compiler_params=pltpu.CompilerParams(dimension_semantics=("arbitrary", "arbitrary", "arbitrary"),
                                             vmem_limit_bytes=VMEM_LIMIT_BYTES),
        name="attn",
    )(qpt, ksel, vst, kw, vwt, kcmp, vcmpt, c2st, gatest)


def _attn_kernel(q_ref, ksel_ref, vst_ref, kw_ref, vwt_ref, kcmp_ref, vcmpt_ref, c2st_ref, gate_ref, o_ref,
                 s_all, p_all, a_all, m_scr, acc_scr, acct_scr, cw_scr, sw_scr, pw_scr):
    i = pl.program_id(1)
    start = i * Q_BLOCK
    nch = kcmp_ref.shape[2]
    last_tile = ksel_ref.shape[2] // SEL_TILE - 1
    groups = range(N_KV_HEADS)

    lane = lax.broadcasted_iota(jnp.int32, (Q_BLOCK, LANES), 1)
    lo_half = lane < HEAD_DIM
    top_half = lax.broadcasted_iota(jnp.int32, (LANES, Q_BLOCK), 0) < HEAD_DIM
    zero_bf = jnp.zeros((LANES, Q_BLOCK), BF16)
    tq = start + (lax.broadcasted_iota(jnp.int32, (1, COLS), 1) & (Q_BLOCK - 1))
    wk = WINDOW + Q_BLOCK
    base = pl.multiple_of(jnp.maximum(start - WINDOW, 0), Q_BLOCK)
    wrow = lax.broadcasted_iota(jnp.int32, (WIN_MASK_ROWS, COLS), 0)
    nrow = lax.broadcasted_iota(jnp.int32, (nch, COLS), 0)
    srow = lax.broadcasted_iota(jnp.int32, (N_SLC_LANES, Q_BLOCK), 0)
    tqq = start + lax.broadcasted_iota(jnp.int32, (N_SLC_LANES, Q_BLOCK), 1)
    c2st = c2st_ref[...]
    n_main = start // SEL_TILE

    qt = [jnp.concatenate([jnp.where(top_half if r % 2 == 0 else ~top_half,
                                     q_ref[0, 0, g * (Q_PER_KV // 2) + r // 2], zero_bf)
                           for r in range(Q_PER_KV)], axis=1) for g in groups]

    def grow(g, b):
        gt = gate_ref[0, 0, g]
        return jnp.concatenate([gt[r * N_BRANCH + b:r * N_BRANCH + b + 1, :] for r in range(Q_PER_KV)], axis=1)

    s_c = [_dot(kcmp_ref[0, g], qt[g]) for g in groups]
    for g in groups:
        sw_scr[g] = _dot(kw_ref[0, g, pl.ds(base, wk), :], qt[g])

    last_valid = lax.shift_right_arithmetic(tq - (CMP_BLOCK - 1), CMP_SHIFT)
    cur = lax.shift_right_logical(tqq, SLC_SHIFT)
    forced = (srow == 0) | (srow == cur) | (srow == cur - 1)
    future = srow * SLC_BLOCK > tqq
    impt = []
    for g in groups:
        sc = jnp.where(nrow <= last_valid, s_c[g], NEG)
        m_c = jnp.max(sc, axis=0, keepdims=True)
        e_c = jnp.exp2(sc - m_c)
        l_c = jnp.sum(e_c, axis=0, keepdims=True)
        p_c = e_c * jnp.where(m_c > 0.5 * NEG, 1.0 / l_c, 0.0)
        cw_scr[g] = grow(g, 0) * _dot(vcmpt_ref[0, g], p_c.astype(BF16))
        psum = sum(p_c[:, r * Q_BLOCK:(r + 1) * Q_BLOCK] for r in range(Q_PER_KV))
        imp = sum(_dot(c2st, t) for t in _split2(psum))
        impt.append(jnp.where(forced, 1e9, jnp.where(future, -1e9, imp)))

    def window_mask_piece(g, k, colmax, zero):
        rows = slice(WIN_MASK_ROWS * k, WIN_MASK_ROWS * (k + 1))
        back = (tq + zero - base - WIN_MASK_ROWS * k) - wrow
        in_window = lax.bitcast_convert_type(back, jnp.uint32) < jnp.uint32(WINDOW)
        sm = jnp.where(in_window, sw_scr[g, rows, :], NEG)
        sw_scr[g, rows, :] = sm
        for c in range(WIN_MASK_ROWS // SUBLANES):
            colmax = jnp.maximum(colmax, sm[c * SUBLANES:(c + 1) * SUBLANES])
        return colmax

    def window_exp_piece(g, k, m_w, zero):
        rows = slice(WIN_EXP_ROWS * k, WIN_EXP_ROWS * (k + 1))
        pw_scr[g, rows, :] = jnp.exp2(sw_scr[g, rows, :] - (m_w + zero.astype(F32))).astype(BF16)

    key_to_float = lambda k: lax.bitcast_convert_type(jnp.where(k >= 0, k, k ^ jnp.int32(0x7FFFFFFF)), F32)
    wrap32 = lambda v: ((v + 2 ** 31) % 2 ** 32) - 2 ** 31
    n_mask, n_exp = wk // WIN_MASK_ROWS, wk // WIN_EXP_ROWS
    thr = [jnp.full((1, Q_BLOCK), INT32_MIN, jnp.int32) for _ in groups]
    colmax = [jnp.full((SUBLANES, COLS), NEG, F32) for _ in groups]
    m_w = [None for _ in groups]
    piece = 0
    for b in range(30, -1, -2):
        for g in groups:
            reached = thr[g]
            for mult in (1, 2, 3):
                cand = thr[g] + jnp.int32(wrap32(mult << b))
                n_ge = jnp.sum(jnp.where(impt[g] >= key_to_float(cand), 1.0, 0.0), axis=0, keepdims=True)
                reached = jnp.where(n_ge >= N_SELECT, cand, reached)
            thr[g] = reached
        for _ in range(2):
            for g in groups:
                zero = jnp.concatenate([_zero_after(thr[g])] * Q_PER_KV, axis=1)
                if piece < n_mask:
                    colmax[g] = window_mask_piece(g, piece, colmax[g], zero)
                elif piece < n_mask + n_exp:
                    if piece == n_mask:
                        m_w[g] = jnp.max(colmax[g], axis=0, keepdims=True)
                    window_exp_piece(g, piece - n_mask, m_w[g], zero)
            piece += 1

    wtile = base // LANES
    scol = lax.broadcasted_iota(jnp.int32, (N_SLC_LANES, N_SLC_LANES), 1)
    earlier = jnp.where(scol < srow, 1.0, 0.0).astype(BF16)
    widen = lambda x: jnp.concatenate([x.astype(BF16)] * Q_PER_KV, axis=1)
    qt_tail, qt_main = [], []
    for g in groups:
        vw = jnp.concatenate([vwt_ref[0, g, wtile + c] for c in range(wk // LANES)], axis=1)
        cw_scr[g] += grow(g, 2) * _normalize_dup(_dot(vw, pw_scr[g]))
        kth = key_to_float(thr[g])
        above = impt[g] > kth
        tied = impt[g] == kth
        n_above = jnp.sum(jnp.where(above, 1.0, 0.0), axis=0, keepdims=True)
        tied_before = _dot(earlier, jnp.where(tied, 1.0, 0.0).astype(BF16))
        selected = above | (tied & (tied_before < N_SELECT - n_above))
        bias = jnp.where(selected, 0.0, NEG)
        bias_main = jnp.where(srow >= n_main * (SEL_TILE // SLC_BLOCK), NEG, bias)
        qt_tail.append(jnp.concatenate([qt[g], widen(bias)], axis=0))
        qt_main.append(jnp.concatenate([qt[g], widen(bias_main)], axis=0))

    kt = pl.multiple_of(n_main * SEL_TILE, SEL_TILE)
    krow = lax.broadcasted_iota(jnp.int32, (SEL_TILE, COLS), 0)
    m_t = []
    for g in groups:
        s_t = _dot(ksel_ref[0, g, pl.ds(kt, SEL_TILE), :], qt_tail[g])
        s_t = jnp.where(kt + krow <= tq, s_t, NEG)
        m_t.append(jnp.max(s_t, axis=0, keepdims=True))
        acct_scr[g] = _dot(vst_ref[0, g, n_main], jnp.exp2(s_t - m_t[g]).astype(BF16))
    g_sel = [grow(g, 1) for g in groups]

    def scores(g, j, slot):
        k0 = pl.multiple_of(jnp.minimum(j, last_tile) * SEL_TILE, SEL_TILE)
        s_all[g, slot] = _dot(ksel_ref[0, g, pl.ds(k0, SEL_TILE), :], qt_main[g])

    def softmax(g, slot):
        s = s_all[g, slot]
        m_prev = m_scr[g, 0:1, :]
        m_new = jnp.maximum(m_prev, jnp.max(s, axis=0, keepdims=True))
        a_all[g, slot] = jnp.broadcast_to(jnp.exp2(m_prev - m_new), (SUBLANES, COLS))
        p_all[g, slot] = jnp.exp2(s - m_new).astype(BF16)
        m_scr[g] = jnp.broadcast_to(m_new, (SUBLANES, COLS))

    def values(g, j, slot):
        acc_scr[g] = (a_all[g, slot, 0:1, :] * acc_scr[g]
                      + _dot(vst_ref[0, g, jnp.minimum(j, last_tile)], p_all[g, slot]))

    m_scr[...] = jnp.full(m_scr.shape, NEG, F32)
    acc_scr[...] = jnp.zeros(acc_scr.shape, F32)
    for k in range(PIPE):
        for g in groups:
            scores(g, k, k)
    for k in range(PIPE // 2):
        for g in groups:
            softmax(g, k)

    def sweep(t, carry):
        for k in range(SWEEP_UNROLL):
            for g in groups:
                values(g, SWEEP_UNROLL * t + k, k % PIPE)
                softmax(g, (k + PIPE // 2) % PIPE)
                scores(g, SWEEP_UNROLL * t + k + PIPE, k % PIPE)
        return carry

    lax.fori_loop(0, (n_main + SWEEP_UNROLL - 1) // SWEEP_UNROLL, sweep, 0)

    pairs = []
    for g in groups:
        m_p = m_scr[g, 0:1, :]
        m_tot = jnp.maximum(m_p, m_t[g])
        o_s = _normalize_dup(jnp.exp2(m_p - m_tot) * acc_scr[g] + jnp.exp2(m_t[g] - m_tot) * acct_scr[g])
        comb = cw_scr[g] + g_sel[g] * o_s
        outs = [comb[:, r * Q_BLOCK:(r + 1) * Q_BLOCK].T for r in range(Q_PER_KV)]
        pairs += [jnp.where(lo_half, outs[0], outs[1]), jnp.where(lo_half, outs[2], outs[3])]
    o_ref[0] = jnp.concatenate(pairs, axis=1)


def _attn(qpt, ksel, vst, kw, vwt, kcmp, vcmpt, c2st, gatest):
    batch, seq = qpt.shape[0], qpt.shape[1] * Q_BLOCK
    per_batch = lambda a: pl.BlockSpec((1,) + a.shape[1:], lambda b, i: (b,) + (0,) * (a.ndim - 1),
                                       pipeline_mode=pl.Buffered(1))
    per_block = lambda a: pl.BlockSpec((1, 1) + a.shape[2:], lambda b, i: (b, i) + (0,) * (a.ndim - 2))
    grp = lambda *shape: pltpu.VMEM((N_KV_HEADS,) + shape, F32)
    return pl.pallas_call(
        _attn_kernel,
        grid=(batch, seq // Q_BLOCK),
        in_specs=[per_block(qpt), per_batch(ksel), per_batch(vst), per_batch(kw), per_batch(vwt), per_batch(kcmp),
                  per_batch(vcmpt), pl.BlockSpec(c2st.shape, lambda b, i: (0, 0)), per_block(gatest)],
        out_specs=pl.BlockSpec((1, Q_BLOCK, D_ATT), lambda b, i: (b, i, 0)),
        out_shape=jax.ShapeDtypeStruct((batch, seq, D_ATT), F32),
        scratch_shapes=[grp(PIPE, SEL_TILE, COLS), pltpu.VMEM((N_KV_HEADS, PIPE, SEL_TILE, COLS), BF16),
                        grp(PIPE, SUBLANES, COLS), grp(SUBLANES, COLS), grp(V_ROWS, COLS), grp(V_ROWS, COLS),
                        grp(LANES, COLS), grp(WINDOW + Q_BLOCK, COLS),
                        pltpu.VMEM((N_KV_HEADS, WINDOW + Q_BLOCK, COLS), BF16)],
        compiler_params=pltpu.CompilerParams(dimension_semantics=("arbitrary", "arbitrary"),
                                             vmem_limit_bytes=VMEM_LIMIT_BYTES),
        name="attn",
    )(qpt, ksel, vst, kw, vwt, kcmp, vcmpt, c2st, gatest)


FF_CHUNKS = 1
FF_CHUNK = D_FF // FF_CHUNKS


def _post_kernel(x_ref, oatt_ref, mconv_ref, p_ref, onag_ref, wo_ref, lnf_ref, wup_ref, fcw_ref, fcb_ref,
                 wdn_ref, lnp_ref, wpg_ref, wpe_ref, out_ref, gbuf, *, tiles_per_seq):
    tm = x_ref.shape[0]
    it = pl.program_id(0) % tiles_per_seq

    mixed = jnp.concatenate([_rms(oatt_ref[...], onag_ref[...]).astype(BF16), mconv_ref[...]], axis=1)
    h1 = x_ref[...] + _dot(mixed, wo_ref[...])
    xn = _rms(h1, lnf_ref[...]).astype(BF16)

    @pl.when(it == 0)
    def _():
        gbuf[:, 0:SUBLANES, :] = jnp.zeros((FF_CHUNKS, SUBLANES, FF_CHUNK), F32)

    h2 = h1
    for c in range(FF_CHUNKS):
        cols = slice(c * FF_CHUNK, (c + 1) * FF_CHUNK)
        gpre = _dot(xn, wup_ref[:, cols])
        up = _dot(xn, wup_ref[:, D_FF + c * FF_CHUNK:D_FF + (c + 1) * FF_CHUNK])
        gbuf[c, SUBLANES:SUBLANES + tm, :] = gpre
        gate = (fcw_ref[2:3, cols] * gpre + fcw_ref[1:2, cols] * gbuf[c, SUBLANES - 1:SUBLANES - 1 + tm, :]
                + fcw_ref[0:1, cols] * gbuf[c, SUBLANES - 2:SUBLANES - 2 + tm, :]) + fcb_ref[:, cols]
        gbuf[c, 0:SUBLANES, :] = gbuf[c, tm:tm + SUBLANES, :]
        act = (gate * jax.nn.sigmoid(gate) * up).astype(BF16)
        h2 = h2 + _dot(act, wdn_ref[cols, :])

    xn2 = _rms(h2, lnp_ref[...]).astype(BF16)
    out_ref[...] = h2 + jax.nn.sigmoid(_dot(xn2, wpg_ref[...])) * _dot(p_ref[...].astype(BF16), wpe_ref[...])


def _post(x2, oatt, mconv, p2, onag, wo, lnf, wup, fcw, fcb, wdn, lnp, wpg, wpe, *, seq, tm):
    n = x2.shape[0]
    tps = seq // tm
    row = lambda w: pl.BlockSpec((tm, w), lambda i: (i, 0))
    full = lambda a: pl.BlockSpec(a.shape, lambda i: (0,) * a.ndim, pipeline_mode=pl.Buffered(1))
    return pl.pallas_call(
        functools.partial(_post_kernel, tiles_per_seq=tps),
        grid=(n // tm,),
        in_specs=[row(D_MODEL), row(D_ATT), row(D_CONV), row(D_PLE), full(onag), full(wo), full(lnf), full(wup),
                  full(fcw), full(fcb), full(wdn), full(lnp), full(wpg), full(wpe)],
        out_specs=row(D_MODEL),
        out_shape=jax.ShapeDtypeStruct((n, D_MODEL), F32),
        scratch_shapes=[pltpu.VMEM((FF_CHUNKS, tm + SUBLANES, FF_CHUNK), F32)],
        compiler_params=pltpu.CompilerParams(dimension_semantics=("arbitrary",),
                                             vmem_limit_bytes=VMEM_LIMIT_BYTES),
        name="post",
    )(x2, oatt, mconv, p2, onag, wo, lnf, wup, fcw, fcb, wdn, lnp, wpg, wpe)


def _rope_tables(pos):
    half = ROT_DIM // 2
    d = np.arange(LANES) % HEAD_DIM
    inv_freq = jnp.tile(ROPE_THETA ** (-jnp.arange(half, dtype=F32) * 2.0 / ROT_DIM), LANES // half)
    ang = pos.astype(F32)[:, None] * inv_freq[None, :]
    c, sn = jnp.cos(ang), jnp.sin(ang)
    first, second = (d < half)[None, :], ((d >= half) & (d < ROT_DIM))[None, :]
    cos = jnp.where(first | second, c, 1.0)
    sa = jnp.where(first, -sn, 0.0)
    sb = jnp.where(second, sn, 0.0)
    return cos, sa, sb


def _block_diag_mean(width):
    idx = np.arange(width) // HEAD_DIM
    return jnp.asarray((idx[:, None] == idx[None, :]).astype(np.float32) / HEAD_DIM, BF16)


def _cmp_to_slc_t(nch):
    cs = CMP_STRIDE * np.arange(nch)[None, :]
    ss = SLC_BLOCK * np.arange(N_SLC_LANES)[:, None]
    ov = np.clip(np.minimum(cs + CMP_BLOCK, ss + SLC_BLOCK) - np.maximum(cs, ss), 0, None)
    return jnp.asarray(ov.astype(np.float32) / CMP_BLOCK, BF16)


def _layer(h, p_l, ln_mix_g, w_in, qn_g, kn_g, pe_k, pe_v, w_ck1, w_ck2, w_cv1, w_cv2, conv_w, on_att_g,
           on_conv_g, w_o, ln_ffn_g, w_up, ffn_conv_w, ffn_conv_b, w_down, ln_ple_g, w_pg, w_pe):
    batch, seq, _ = h.shape
    assert seq % SEL_TILE == 0 and seq // SLC_BLOCK <= N_SLC_LANES and seq // SLC_BLOCK >= N_SELECT
    assert seq >= WINDOW + Q_BLOCK
    n = batch * seq
    nch = seq // CMP_STRIDE
    x2 = h.reshape(n, D_MODEL)
    row = lambda v: v.reshape(1, -1).astype(F32)

    o_q, o_kv, o_g, o_cv = 0, D_ATT, D_ATT + 6 * KV_W, D_ATT + 6 * KV_W + N_BRANCH * N_Q_HEADS
    wq = w_in[:, o_q:o_kv].astype(BF16)
    wkv = w_in[:, o_kv:o_g].astype(BF16)
    wcv = w_in[:, o_cv:].astype(BF16)
    per_g = Q_PER_KV * N_BRANCH
    wg = jnp.concatenate(
        [jnp.pad(w_in[:, o_g + g * per_g:o_g + (g + 1) * per_g], ((0, 0), (0, LANES - per_g)))
         for g in range(N_KV_HEADS)], axis=1).astype(BF16)
    cos, sa, sb = _rope_tables(jnp.arange(seq))
    bd512, bd128 = _block_diag_mean(D_ATT), _block_diag_mean(KV_W)
    tile_heads = lambda v, k: jnp.tile(v.astype(F32), k).reshape(1, -1)

    tm = ROW_TILE
    assert seq % ROW_TILE == 0 and seq % POST_TILE == 0
    qpt, kc2, vc2, ksel, vst, kw, vwt, gatest, mconv = _inproj(
        x2, row(ln_mix_g), wq, wkv, wg, wcv, tile_heads(qn_g, N_Q_HEADS),
        jnp.stack([jnp.tile(kn_g[1], N_KV_HEADS), jnp.tile(kn_g[2], N_KV_HEADS)]).astype(F32),
        bd512, bd128, cos, sa, sb, conv_w.astype(F32), row(on_conv_g), batch=batch, seq=seq, tm=tm)

    eye = jnp.eye(N_KV_HEADS, dtype=F32)
    half = CMP_BLOCK // 2

    def w1_parts(w1):
        wfull = jnp.einsum('ldh,gk->lgdkh', w1, eye).reshape(CMP_BLOCK, KV_W, N_KV_HEADS * CMP_HIDDEN)
        return (wfull[:half].reshape(half * KV_W, -1).astype(BF16),
                wfull[half:].reshape(half * KV_W, -1).astype(BF16))

    def pe_parts(pe):
        pf = jnp.broadcast_to(pe[:, None, :], (CMP_BLOCK, N_KV_HEADS, HEAD_DIM)).astype(F32)
        return pf[:half].reshape(1, -1), pf[half:].reshape(1, -1)

    w2bd = lambda w2: jnp.einsum('hd,gk->ghkd', w2, eye).reshape(N_KV_HEADS * CMP_HIDDEN, KV_W).astype(BF16)
    w1ak, w1bk = w1_parts(w_ck1)
    w1av, w1bv = w1_parts(w_cv1)
    peak, pebk = pe_parts(pe_k)
    peav, pebv = pe_parts(pe_v)
    ccos, csa, csb = _rope_tables(CMP_STRIDE * jnp.arange(nch) + CMP_BLOCK - 1)
    kcmp, vcmpt = _compress(kc2.reshape(batch, nch, CMP_STRIDE * KV_W), vc2.reshape(batch, nch, CMP_STRIDE * KV_W),
                            w1ak, w1bk, w1av, w1bv, peak, pebk, peav, pebv, w2bd(w_ck2), w2bd(w_cv2),
                            tile_heads(kn_g[0], N_KV_HEADS), bd128, ccos, csa, csb)

    oatt = _attn(qpt, ksel, vst, kw, vwt, kcmp, vcmpt, _cmp_to_slc_t(nch), gatest)

    out = _post(x2, oatt.reshape(n, D_ATT), mconv, p_l.reshape(n, D_PLE), row(on_att_g), w_o.astype(BF16),
                row(ln_ffn_g), w_up.astype(BF16), ffn_conv_w.astype(F32), row(ffn_conv_b), w_down.astype(BF16),
                row(ln_ple_g), w_pg.astype(BF16), w_pe.astype(BF16), seq=seq, tm=POST_TILE)
    return out.reshape(batch, seq, D_MODEL)


def kernel(x, p, ln_mix_g, w_in, qn_g, kn_g, pe_k, pe_v, w_ck1, w_ck2, w_cv1, w_cv2, conv_w, on_att_g,
           on_conv_g, w_o, ln_ffn_g, w_up, ffn_conv_w, ffn_conv_b, w_down, ln_ple_g, w_pg, w_pe):
    h = x
    for i in range(p.shape[0]):
        h = _layer(h, p[i], ln_mix_g[i], w_in[i], qn_g[i], kn_g[i], pe_k[i], pe_v[i], w_ck1[i], w_ck2[i],
                   w_cv1[i], w_cv2[i], conv_w[i], on_att_g[i], on_conv_g[i], w_o[i], ln_ffn_g[i], w_up[i],
                   ffn_conv_w[i], ffn_conv_b[i], w_down[i], ln_ple_g[i], w_pg[i], w_pe[i])
    return h
```

```python
import functools

import jax
import jax.numpy as jnp
import numpy as np
from jax import lax
from jax.experimental import pallas as pl
from jax.experimental.pallas import tpu as pltpu

D_MODEL = 1024
HEAD_DIM = 64
N_Q_HEADS = 8
N_KV_HEADS = 2
Q_PER_KV = N_Q_HEADS // N_KV_HEADS
D_ATT = N_Q_HEADS * HEAD_DIM
D_CONV = D_MODEL - D_ATT
KV_W = N_KV_HEADS * HEAD_DIM
N_BRANCH = 3
CONV_TAPS = 3
ROT_DIM = HEAD_DIM // 4
ROPE_THETA = 500000.0
CMP_BLOCK = 32
CMP_STRIDE = 16
CMP_HIDDEN = 256
SLC_BLOCK = 64
N_SELECT = 16
WINDOW = 512
Q_BLOCK = 128
D_FF = 2816
D_PLE = 256
EPS = 1e-6
NEG = -1e30
INT32_MIN = -2 ** 31
LOG2E = float(np.log2(np.e))
V_ROWS = HEAD_DIM + 16
GATE_ROWS = 16
SLC_SHIFT = SLC_BLOCK.bit_length() - 1
CMP_SHIFT = CMP_STRIDE.bit_length() - 1
assert 1 << SLC_SHIFT == SLC_BLOCK and 1 << CMP_SHIFT == CMP_STRIDE

LANES = 128
SUBLANES = 8
N_SLC_LANES = LANES
VMEM_LIMIT_BYTES = 56 * 1024 * 1024

F32 = jnp.float32
BF16 = jnp.bfloat16


def _dot(a, b):
    return jnp.dot(a, b, preferred_element_type=F32)


def _rms(x, g):
    return x * lax.rsqrt(jnp.mean(x * x, axis=-1, keepdims=True) + EPS) * g


def _head_rms_rope(x, g, bd, cos, sa, sb):
    w = x.shape[-1]
    msq = _dot((x * x).astype(BF16), bd)
    xn = x * lax.rsqrt(msq + EPS) * g
    return xn * cos + pltpu.roll(xn, w - ROT_DIM // 2, 1) * sa + pltpu.roll(xn, ROT_DIM // 2, 1) * sb


def _dup_halves(x):
    r = pltpu.roll(x, HEAD_DIM, 1)
    lane = lax.broadcasted_iota(jnp.int32, x.shape, 1)
    lo = lane < HEAD_DIM
    return jnp.where(lo, x, r), jnp.where(lo, r, x)


def _inproj_kernel(x_ref, lng_ref, wq_ref, wkv_ref, wg_ref, wcv_ref, qng_ref, kng_ref, bd512_ref, bd128_ref,
                   cos_ref, sa_ref, sb_ref, convw_ref, oncg_ref,
                   q_out, kc_out, vc_out, ksel_out, vs_out, kw_out, vw_out, gate_out, mconv_out,
                   zbuf, *, tiles_per_seq):
    tm = x_ref.shape[0]
    it = pl.program_id(0) % tiles_per_seq
    x = x_ref[...]
    xn = _rms(x, lng_ref[...]).astype(BF16)

    cos, sa, sb = cos_ref[...], sa_ref[...], sb_ref[...]
    cos4, sa4, sb4 = (jnp.concatenate([t] * 4, axis=1) for t in (cos, sa, sb))
    q = _dot(xn, wq_ref[...])
    qr = _head_rms_rope(q, qng_ref[...], bd512_ref[...], cos4, sa4, sb4)
    qs = qr * (HEAD_DIM ** -0.5 * LOG2E)
    for blk_i in range(tm // Q_BLOCK):
        rows = slice(blk_i * Q_BLOCK, (blk_i + 1) * Q_BLOCK)
        for pr in range(N_Q_HEADS // 2):
            q_out[0, blk_i, pr] = qs[rows, pr * LANES:(pr + 1) * LANES].T.astype(BF16)

    kv = _dot(xn, wkv_ref[...])
    kc_out[...] = kv[:, 0 * KV_W:1 * KV_W]
    vc_out[...] = kv[:, 1 * KV_W:2 * KV_W]
    ks = _head_rms_rope(kv[:, 2 * KV_W:3 * KV_W], kng_ref[0:1, :], bd128_ref[...], cos, sa, sb)
    kw = _head_rms_rope(kv[:, 4 * KV_W:5 * KV_W], kng_ref[1:2, :], bd128_ref[...], cos, sa, sb)
    vs = kv[:, 3 * KV_W:4 * KV_W]
    vw = kv[:, 5 * KV_W:6 * KV_W]
    tpos = it * tm + lax.broadcasted_iota(jnp.int32, (tm, N_SLC_LANES), 0)
    blk = lax.broadcasted_iota(jnp.int32, (tm, N_SLC_LANES), 1)
    onehot = jnp.where(lax.shift_right_logical(tpos, SLC_SHIFT) == blk, 1.0, 0.0).astype(BF16)
    ks_d, vs_d, kw_d, vw_d = (_dup_halves(t) for t in (ks, vs, kw, vw))
    vrow = lax.broadcasted_iota(jnp.int32, (V_ROWS, tm), 0)
    for g in range(N_KV_HEADS):
        ksel_out[0, g] = jnp.concatenate([ks_d[g].astype(BF16), onehot], axis=1)
        kw_out[0, g] = kw_d[g].astype(BF16)
        vst = jnp.where(vrow < HEAD_DIM, vs_d[g].T[:V_ROWS], 1.0).astype(BF16)
        for c in range(tm // SEL_TILE):
            vs_out[0, g, c] = vst[:, c * SEL_TILE:(c + 1) * SEL_TILE]
        vwt = jnp.where(vrow < HEAD_DIM, vw_d[g].T[:V_ROWS], 1.0).astype(BF16)
        for c in range(tm // LANES):
            vw_out[0, g, c] = vwt[:, c * LANES:(c + 1) * LANES]

    gates = jax.nn.sigmoid(_dot(xn, wg_ref[...]))
    for blk_i in range(tm // Q_BLOCK):
        rows = slice(blk_i * Q_BLOCK, (blk_i + 1) * Q_BLOCK)
        for g in range(N_KV_HEADS):
            gate_out[0, blk_i, g] = gates[rows, g * LANES:(g + 1) * LANES].T[:GATE_ROWS]

    cv = _dot(xn, wcv_ref[...])
    cb, cc, cx = cv[:, :D_CONV], cv[:, D_CONV:2 * D_CONV], cv[:, 2 * D_CONV:]
    z = cc * cx

    @pl.when(it == 0)
    def _():
        zbuf[0:SUBLANES, :] = jnp.zeros((SUBLANES, D_CONV), F32)

    zbuf[SUBLANES:SUBLANES + tm, :] = z
    y = (convw_ref[2:3, :] * z + convw_ref[1:2, :] * zbuf[SUBLANES - 1:SUBLANES - 1 + tm, :]
         + convw_ref[0:1, :] * zbuf[SUBLANES - 2:SUBLANES - 2 + tm, :])
    zbuf[0:SUBLANES, :] = zbuf[tm:tm + SUBLANES, :]
    mconv_out[...] = _rms(cb * y, oncg_ref[...]).astype(BF16)


def _inproj(x2, lng, wq, wkv, wg, wcv, qng, kng, bd512, bd128, cos, sa, sb, convw, oncg, *, batch, seq, tm):
    n = batch * seq
    tps = seq // tm
    row = lambda w: pl.BlockSpec((tm, w), lambda i: (i, 0))
    full = lambda a: pl.BlockSpec(a.shape, lambda i: (0,) * a.ndim)
    tab = pl.BlockSpec((tm, LANES), lambda i: (i % tps, 0))
    grp = lambda w: pl.BlockSpec((1, N_KV_HEADS, tm, w), lambda i: (i // tps, 0, i % tps, 0))
    gshape = lambda w: jax.ShapeDtypeStruct((batch, N_KV_HEADS, seq, w), BF16)
    assert tm % SEL_TILE == 0
    vtile = lambda keys: pl.BlockSpec((1, N_KV_HEADS, tm // keys, V_ROWS, keys),
                                      lambda i: (i // tps, 0, i % tps, 0, 0))
    vshape = lambda keys: jax.ShapeDtypeStruct((batch, N_KV_HEADS, seq // keys, V_ROWS, keys), BF16)
    qblk = lambda tile: pl.BlockSpec((1, tm // Q_BLOCK) + tile, lambda i: (i // tps, i % tps) + (0,) * len(tile))
    return pl.pallas_call(
        functools.partial(_inproj_kernel, tiles_per_seq=tps),
        grid=(n // tm,),
        in_specs=[row(D_MODEL), full(lng), full(wq), full(wkv), full(wg), full(wcv), full(qng), full(kng),
                  full(bd512), full(bd128), tab, tab, tab, full(convw), full(oncg)],
        out_specs=[qblk((N_Q_HEADS // 2, LANES, Q_BLOCK)), row(KV_W), row(KV_W), grp(2 * LANES), vtile(SEL_TILE),
                   grp(LANES), vtile(LANES), qblk((N_KV_HEADS, GATE_ROWS, Q_BLOCK)), row(D_CONV)],
        out_shape=[jax.ShapeDtypeStruct((batch, seq // Q_BLOCK, N_Q_HEADS // 2, LANES, Q_BLOCK), BF16),
                   jax.ShapeDtypeStruct((n, KV_W), F32),
                   jax.ShapeDtypeStruct((n, KV_W), F32), gshape(2 * LANES), vshape(SEL_TILE), gshape(LANES),
                   vshape(LANES),
                   jax.ShapeDtypeStruct((batch, seq // Q_BLOCK, N_KV_HEADS, GATE_ROWS, Q_BLOCK), F32),
                   jax.ShapeDtypeStruct((n, D_CONV), BF16)],
        scratch_shapes=[pltpu.VMEM((tm + SUBLANES, D_CONV), F32)],
        compiler_params=pltpu.CompilerParams(dimension_semantics=("arbitrary",),
                                             vmem_limit_bytes=VMEM_LIMIT_BYTES),
        name="inproj",
    )(x2, lng, wq, wkv, wg, wcv, qng, kng, bd512, bd128, cos, sa, sb, convw, oncg)


def _gelu_tanh(x):
    return 0.5 * x * (1.0 + jnp.tanh(np.sqrt(2.0 / np.pi).astype(np.float32) * (x + 0.044715 * (x * x * x))))


def _compress_kernel(kc_ref, vc_ref, w1ak_ref, w1bk_ref, w1av_ref, w1bv_ref, peak_ref, pebk_ref, peav_ref,
                     pebv_ref, w2k_ref, w2v_ref, kng_ref, bd128_ref, cos_ref, sa_ref, sb_ref,
                     kcmp_out, vcmp_out):
    nch = kc_ref.shape[1]

    def compress(x, w1a, w1b, pea, peb, w2):
        a = _dot((x + pea).astype(BF16), w1a)
        b = _dot((x + peb).astype(BF16), w1b)
        hid = _gelu_tanh(a + pltpu.roll(b, nch - 1, 0))
        return _dot(hid.astype(BF16), w2)

    kc = compress(kc_ref[0], w1ak_ref[...], w1bk_ref[...], peak_ref[...], pebk_ref[...], w2k_ref[...])
    vc = compress(vc_ref[0], w1av_ref[...], w1bv_ref[...], peav_ref[...], pebv_ref[...], w2v_ref[...])
    kc = _head_rms_rope(kc, kng_ref[...], bd128_ref[...], cos_ref[...], sa_ref[...], sb_ref[...])
    kd, vd = _dup_halves(kc), _dup_halves(vc)
    for g in range(N_KV_HEADS):
        kcmp_out[0, g] = kd[g].astype(BF16)
        vcmp_out[0, g] = vd[g].T.astype(BF16)


def _compress(kc3, vc3, w1ak, w1bk, w1av, w1bv, peak, pebk, peav, pebv, w2k, w2v, kng0, bd128, cos, sa, sb):
    batch, nch, wide = kc3.shape
    full = lambda a: pl.BlockSpec(a.shape, lambda b: (0,) * a.ndim)
    tok = pl.BlockSpec((1, nch, wide), lambda b: (b, 0, 0))
    kout = pl.BlockSpec((1, N_KV_HEADS, nch, LANES), lambda b: (b, 0, 0, 0))
    vout = pl.BlockSpec((1, N_KV_HEADS, LANES, nch), lambda b: (b, 0, 0, 0))
    consts = (w1ak, w1bk, w1av, w1bv, peak, pebk, peav, pebv, w2k, w2v, kng0, bd128, cos, sa, sb)
    return pl.pallas_call(
        _compress_kernel,
        grid=(batch,),
        in_specs=[tok, tok] + [full(a) for a in consts],
        out_specs=[kout, vout],
        out_shape=[jax.ShapeDtypeStruct((batch, N_KV_HEADS, nch, LANES), BF16),
                   jax.ShapeDtypeStruct((batch, N_KV_HEADS, LANES, nch), BF16)],
        compiler_params=pltpu.CompilerParams(dimension_semantics=("arbitrary",),
                                             vmem_limit_bytes=VMEM_LIMIT_BYTES),
        name="compress",
    )(kc3, vc3, *consts)


SEL_TILE = 256
COLS = Q_PER_KV * Q_BLOCK
ROW_TILE = 512
POST_TILE = 256
Q_PER_STEP = 2
PIPE = 2
SWEEP_UNROLL = 2
assert SWEEP_UNROLL % PIPE == 0 and SEL_TILE % (Q_PER_STEP * Q_BLOCK) == 0
WIN_MASK_ROWS = 32
WIN_EXP_ROWS = 64
assert (WINDOW + Q_BLOCK) // WIN_MASK_ROWS + (WINDOW + Q_BLOCK) // WIN_EXP_ROWS <= 32


def _split2(x):
    hi = x.astype(BF16)
    return hi, (x - hi.astype(F32)).astype(BF16)


def _zero_after(x):
    return lax.shift_right_logical(lax.shift_right_logical(x, 16), 16)


def _normalize_dup(acc):
    o = acc[:HEAD_DIM] / acc[HEAD_DIM:HEAD_DIM + 1]
    return jnp.concatenate([o, o], axis=0)


def _attn_kernel_single(q_ref, ksel_ref, vst_ref, kw_ref, vwt_ref, kcmp_ref, vcmpt_ref, c2st_ref, gate_ref, o_ref,
                        s_all, p_all, a_all, m_scr, acc_scr, acct_scr, cw_scr, sw_scr, pw_scr):
    i = pl.program_id(2)
    start = i * Q_BLOCK
    nch = kcmp_ref.shape[2]
    last_tile = ksel_ref.shape[2] // SEL_TILE - 1

    lane = lax.broadcasted_iota(jnp.int32, (Q_BLOCK, LANES), 1)
    lo_half = lane < HEAD_DIM
    top_half = lax.broadcasted_iota(jnp.int32, (LANES, Q_BLOCK), 0) < HEAD_DIM
    zero_bf = jnp.zeros((LANES, Q_BLOCK), BF16)
    qt = jnp.concatenate([jnp.where(top_half if r % 2 == 0 else ~top_half, q_ref[0, 0, r // 2], zero_bf)
                          for r in range(Q_PER_KV)], axis=1)
    tq = start + (lax.broadcasted_iota(jnp.int32, (1, COLS), 1) & (Q_BLOCK - 1))
    gt = gate_ref[0, 0, 0]
    grow = lambda b: jnp.concatenate([gt[r * N_BRANCH + b:r * N_BRANCH + b + 1, :] for r in range(Q_PER_KV)],
                                     axis=1)

    wk = WINDOW + Q_BLOCK
    base = pl.multiple_of(jnp.maximum(start - WINDOW, 0), Q_BLOCK)

    wrow = lax.broadcasted_iota(jnp.int32, (WIN_MASK_ROWS, COLS), 0)

    def window_mask_piece(k, colmax, zero):
        rows = slice(WIN_MASK_ROWS * k, WIN_MASK_ROWS * (k + 1))
        back = (tq + zero - base - WIN_MASK_ROWS * k) - wrow
        in_window = lax.bitcast_convert_type(back, jnp.uint32) < jnp.uint32(WINDOW)
        sm = jnp.where(in_window, sw_scr[rows, :], NEG)
        sw_scr[rows, :] = sm
        for c in range(WIN_MASK_ROWS // SUBLANES):
            colmax = jnp.maximum(colmax, sm[c * SUBLANES:(c + 1) * SUBLANES])
        return colmax

    def window_exp_piece(k, m_w, zero):
        rows = slice(WIN_EXP_ROWS * k, WIN_EXP_ROWS * (k + 1))
        pw_scr[rows, :] = jnp.exp2(sw_scr[rows, :] - (m_w + zero.astype(F32))).astype(BF16)

    s_c = _dot(kcmp_ref[0, 0], qt)
    sw_scr[...] = _dot(kw_ref[0, 0, pl.ds(base, wk), :], qt)
    nrow = lax.broadcasted_iota(jnp.int32, (nch, COLS), 0)
    last_valid = lax.shift_right_arithmetic(tq - (CMP_BLOCK - 1), CMP_SHIFT)
    s_c = jnp.where(nrow <= last_valid, s_c, NEG)
    m_c = jnp.max(s_c, axis=0, keepdims=True)
    e_c = jnp.exp2(s_c - m_c)
    l_c = jnp.sum(e_c, axis=0, keepdims=True)
    p_c = e_c * jnp.where(m_c > 0.5 * NEG, 1.0 / l_c, 0.0)
    cw_scr[...] = grow(0) * _dot(vcmpt_ref[0, 0], p_c.astype(BF16))

    psum = sum(p_c[:, r * Q_BLOCK:(r + 1) * Q_BLOCK] for r in range(Q_PER_KV))
    c2st = c2st_ref[...]
    impt = sum(_dot(c2st, t) for t in _split2(psum))
    srow = lax.broadcasted_iota(jnp.int32, (N_SLC_LANES, Q_BLOCK), 0)
    tqq = start + lax.broadcasted_iota(jnp.int32, (N_SLC_LANES, Q_BLOCK), 1)
    cur = lax.shift_right_logical(tqq, SLC_SHIFT)
    forced = (srow == 0) | (srow == cur) | (srow == cur - 1)
    future = srow * SLC_BLOCK > tqq
    impt = jnp.where(forced, 1e9, jnp.where(future, -1e9, impt))
    key_to_float = lambda k: lax.bitcast_convert_type(jnp.where(k >= 0, k, k ^ jnp.int32(0x7FFFFFFF)), F32)
    thr = jnp.full((1, Q_BLOCK), INT32_MIN, jnp.int32)
    n_mask, n_exp = wk // WIN_MASK_ROWS, wk // WIN_EXP_ROWS
    colmax = jnp.full((SUBLANES, COLS), NEG, F32)
    wrap32 = lambda v: ((v + 2 ** 31) % 2 ** 32) - 2 ** 31
    piece = 0
    for b in range(30, -1, -2):
        reached = None
        for mult in (1, 2, 3):
            cand = thr + jnp.int32(wrap32(mult << b))
            n_ge = jnp.sum(jnp.where(impt >= key_to_float(cand), 1.0, 0.0), axis=0, keepdims=True)
            reached = jnp.where(n_ge >= N_SELECT, cand, thr if reached is None else reached)
        thr = reached
        zero = jnp.concatenate([_zero_after(thr)] * Q_PER_KV, axis=1)
        for _ in range(2):
            if piece < n_mask:
                colmax = window_mask_piece(piece, colmax, zero)
            elif piece < n_mask + n_exp:
                if piece == n_mask:
                    m_w = jnp.max(colmax, axis=0, keepdims=True)
                window_exp_piece(piece - n_mask, m_w, zero)
            piece += 1
    wtile = base // LANES
    vw = jnp.concatenate([vwt_ref[0, 0, wtile + c] for c in range(wk // LANES)], axis=1)
    cw_scr[...] += grow(2) * _normalize_dup(_dot(vw, pw_scr[...]))
    kth = key_to_float(thr)
    above = impt > kth
    tied = impt == kth
    n_above = jnp.sum(jnp.where(above, 1.0, 0.0), axis=0, keepdims=True)
    scol = lax.broadcasted_iota(jnp.int32, (N_SLC_LANES, N_SLC_LANES), 1)
    earlier = jnp.where(scol < srow, 1.0, 0.0).astype(BF16)
    tied_before = _dot(earlier, jnp.where(tied, 1.0, 0.0).astype(BF16))
    selected = above | (tied & (tied_before < N_SELECT - n_above))
    bias = jnp.where(selected, 0.0, NEG)
    n_main = start // SEL_TILE
    bias_main = jnp.where(srow >= n_main * (SEL_TILE // SLC_BLOCK), NEG, bias)
    widen = lambda b: jnp.concatenate([b.astype(BF16)] * Q_PER_KV, axis=1)
    qt_tail = jnp.concatenate([qt, widen(bias)], axis=0)
    qt_main = jnp.concatenate([qt, widen(bias_main)], axis=0)

    kt = pl.multiple_of(n_main * SEL_TILE, SEL_TILE)
    s_t = _dot(ksel_ref[0, 0, pl.ds(kt, SEL_TILE), :], qt_tail)
    krow = lax.broadcasted_iota(jnp.int32, (SEL_TILE, COLS), 0)
    s_t = jnp.where(kt + krow <= tq, s_t, NEG)
    m_t = jnp.max(s_t, axis=0, keepdims=True)
    acct_scr[...] = _dot(vst_ref[0, 0, n_main], jnp.exp2(s_t - m_t).astype(BF16))
    g_sel = grow(1)

    def scores(j, slot):
        k0 = pl.multiple_of(jnp.minimum(j, last_tile) * SEL_TILE, SEL_TILE)
        s_all[slot] = _dot(ksel_ref[0, 0, pl.ds(k0, SEL_TILE), :], qt_main)

    def softmax(slot):
        s = s_all[slot]
        m_prev = m_scr[0:1, :]
        m_new = jnp.maximum(m_prev, jnp.max(s, axis=0, keepdims=True))
        a_all[slot] = jnp.broadcast_to(jnp.exp2(m_prev - m_new), (SUBLANES, COLS))
        p_all[slot] = jnp.exp2(s - m_new).astype(BF16)
        m_scr[...] = jnp.broadcast_to(m_new, m_scr.shape)

    def values(j, slot):
        acc_scr[...] = (a_all[slot, 0:1, :] * acc_scr[...]
                        + _dot(vst_ref[0, 0, jnp.minimum(j, last_tile)], p_all[slot]))

    m_scr[...] = jnp.full(m_scr.shape, NEG, F32)
    acc_scr[...] = jnp.zeros(acc_scr.shape, F32)
    for k in range(PIPE):
        scores(k, k)
    for k in range(PIPE // 2):
        softmax(k)

    def sweep(t, carry):
        for k in range(SWEEP_UNROLL):
            values(SWEEP_UNROLL * t + k, k % PIPE)
            softmax((k + PIPE // 2) % PIPE)
            scores(SWEEP_UNROLL * t + k + PIPE, k % PIPE)
        return carry

    lax.fori_loop(0, (n_main + SWEEP_UNROLL - 1) // SWEEP_UNROLL, sweep, 0)
    m_p = m_scr[0:1, :]
    m_tot = jnp.maximum(m_p, m_t)
    o_s = _normalize_dup(jnp.exp2(m_p - m_tot) * acc_scr[...] + jnp.exp2(m_t - m_tot) * acct_scr[...])

    comb = cw_scr[...] + g_sel * o_s
    outs = [comb[:, r * Q_BLOCK:(r + 1) * Q_BLOCK].T for r in range(Q_PER_KV)]
    o_ref[0] = jnp.concatenate([jnp.where(lo_half, outs[0], outs[1]), jnp.where(lo_half, outs[2], outs[3])],
                               axis=1)


def _attn_single(qpt, ksel, vst, kw, vwt, kcmp, vcmpt, c2st, gatest):
    batch, seq = qpt.shape[0], qpt.shape[1] * Q_BLOCK
    per_group = lambda a: pl.BlockSpec((1, 1) + a.shape[2:], lambda b, g, i: (b, g) + (0,) * (a.ndim - 2))
    return pl.pallas_call(
        _attn_kernel_single,
        grid=(batch, N_KV_HEADS, seq // Q_BLOCK),
        in_specs=[pl.BlockSpec((1, 1, Q_PER_KV // 2, LANES, Q_BLOCK), lambda b, g, i: (b, i, g, 0, 0)),
                  per_group(ksel), per_group(vst), per_group(kw), per_group(vwt), per_group(kcmp),
                  per_group(vcmpt), pl.BlockSpec(c2st.shape, lambda b, g, i: (0, 0)),
                  pl.BlockSpec((1, 1, 1, GATE_ROWS, Q_BLOCK), lambda b, g, i: (b, i, g, 0, 0))],
        out_specs=pl.BlockSpec((1, Q_BLOCK, Q_PER_KV * HEAD_DIM), lambda b, g, i: (b, i, g)),
        out_shape=jax.ShapeDtypeStruct((batch, seq, D_ATT), F32),
        scratch_shapes=[pltpu.VMEM((PIPE, SEL_TILE, COLS), F32), pltpu.VMEM((PIPE, SEL_TILE, COLS), BF16),
                        pltpu.VMEM((PIPE, SUBLANES, COLS), F32),
                        pltpu.VMEM((SUBLANES, COLS), F32), pltpu.VMEM((V_ROWS, COLS), F32),
                        pltpu.VMEM((V_ROWS, COLS), F32), pltpu.VMEM((LANES, COLS), F32),
                        pltpu.VMEM((WINDOW + Q_BLOCK, COLS), F32), pltpu.VMEM((WINDOW + Q_BLOCK, COLS), BF16)],
        compiler_params=pltpu.CompilerParams(dimension_semantics=("arbitrary", "arbitrary", "arbitrary"),
                                             vmem_limit_bytes=VMEM_LIMIT_BYTES),
        name="attn",
    )(qpt, ksel, vst, kw, vwt, kcmp, vcmpt, c2st, gatest)


def _attn_kernel(q_ref, ksel_ref, vst_ref, kw_ref, vwt_ref, kcmp_ref, vcmpt_ref, c2st_ref, gate_ref, o_ref,
                 s_all, p_all, a_all, m_scr, acc_scr, acct_scr, cw_scr, sw_scr, pw_scr):
    i = pl.program_id(1)
    starts = [(Q_PER_STEP * i + d) * Q_BLOCK for d in range(Q_PER_STEP)]
    kv = lambda g: g % N_KV_HEADS
    qb = lambda g: g // N_KV_HEADS
    nch = kcmp_ref.shape[2]
    last_tile = ksel_ref.shape[2] // SEL_TILE - 1
    groups = range(Q_PER_STEP * N_KV_HEADS)

    lane = lax.broadcasted_iota(jnp.int32, (Q_BLOCK, LANES), 1)
    lo_half = lane < HEAD_DIM
    top_half = lax.broadcasted_iota(jnp.int32, (LANES, Q_BLOCK), 0) < HEAD_DIM
    zero_bf = jnp.zeros((LANES, Q_BLOCK), BF16)
    tqs = [st + (lax.broadcasted_iota(jnp.int32, (1, COLS), 1) & (Q_BLOCK - 1)) for st in starts]
    wk = WINDOW + Q_BLOCK
    bases = [pl.multiple_of(jnp.maximum(st - WINDOW, 0), Q_BLOCK) for st in starts]
    wrow = lax.broadcasted_iota(jnp.int32, (WIN_MASK_ROWS, COLS), 0)
    nrow = lax.broadcasted_iota(jnp.int32, (nch, COLS), 0)
    srow = lax.broadcasted_iota(jnp.int32, (N_SLC_LANES, Q_BLOCK), 0)
    tqqs = [st + lax.broadcasted_iota(jnp.int32, (N_SLC_LANES, Q_BLOCK), 1) for st in starts]
    c2st = c2st_ref[...]
    n_main = starts[0] // SEL_TILE

    qt = [jnp.concatenate([jnp.where(top_half if r % 2 == 0 else ~top_half,
                                     q_ref[0, qb(g), kv(g) * (Q_PER_KV // 2) + r // 2], zero_bf)
                           for r in range(Q_PER_KV)], axis=1) for g in groups]

    def grow(g, b):
        gt = gate_ref[0, qb(g), kv(g)]
        return jnp.concatenate([gt[r * N_BRANCH + b:r * N_BRANCH + b + 1, :] for r in range(Q_PER_KV)], axis=1)

    s_c = [_dot(kcmp_ref[0, kv(g)], qt[g]) for g in groups]
    for g in groups:
        sw_scr[g] = _dot(kw_ref[0, kv(g), pl.ds(bases[qb(g)], wk), :], qt[g])

    last_valid = [lax.shift_right_arithmetic(t - (CMP_BLOCK - 1), CMP_SHIFT) for t in tqs]
    curs = [lax.shift_right_logical(t, SLC_SHIFT) for t in tqqs]
    forced = [(srow == 0) | (srow == c) | (srow == c - 1) for c in curs]
    future = [srow * SLC_BLOCK > t for t in tqqs]
    impt = []
    for g in groups:
        sc = jnp.where(nrow <= last_valid[qb(g)], s_c[g], NEG)
        m_c = jnp.max(sc, axis=0, keepdims=True)
        e_c = jnp.exp2(sc - m_c)
        l_c = jnp.sum(e_c, axis=0, keepdims=True)
        p_c = e_c * jnp.where(m_c > 0.5 * NEG, 1.0 / l_c, 0.0)
        cw_scr[g] = grow(g, 0) * _dot(vcmpt_ref[0, kv(g)], p_c.astype(BF16))
        psum = sum(p_c[:, r * Q_BLOCK:(r + 1) * Q_BLOCK] for r in range(Q_PER_KV))
        imp = sum(_dot(c2st, t) for t in _split2(psum))
        impt.append(jnp.where(forced[qb(g)], 1e9, jnp.where(future[qb(g)], -1e9, imp)))

    def window_mask_piece(g, k, colmax, zero):
        rows = slice(WIN_MASK_ROWS * k, WIN_MASK_ROWS * (k + 1))
        back = (tqs[qb(g)] + zero - bases[qb(g)] - WIN_MASK_ROWS * k) - wrow
        in_window = lax.bitcast_convert_type(back, jnp.uint32) < jnp.uint32(WINDOW)
        sm = jnp.where(in_window, sw_scr[g, rows, :], NEG)
        sw_scr[g, rows, :] = sm
        for c in range(WIN_MASK_ROWS // SUBLANES):
            colmax = jnp.maximum(colmax, sm[c * SUBLANES:(c + 1) * SUBLANES])
        return colmax

    def window_exp_piece(g, k, m_w, zero):
        rows = slice(WIN_EXP_ROWS * k, WIN_EXP_ROWS * (k + 1))
        pw_scr[g, rows, :] = jnp.exp2(sw_scr[g, rows, :] - (m_w + zero.astype(F32))).astype(BF16)

    key_to_float = lambda k: lax.bitcast_convert_type(jnp.where(k >= 0, k, k ^ jnp.int32(0x7FFFFFFF)), F32)
    wrap32 = lambda v: ((v + 2 ** 31) % 2 ** 32) - 2 ** 31
    n_mask, n_exp = wk // WIN_MASK_ROWS, wk // WIN_EXP_ROWS
    thr = [jnp.full((1, Q_BLOCK), INT32_MIN, jnp.int32) for _ in groups]
    colmax = [jnp.full((SUBLANES, COLS), NEG, F32) for _ in groups]
    m_w = [None for _ in groups]
    piece = 0
    for b in range(30, -1, -2):
        for g in groups:
            reached = thr[g]
            for mult in (1, 2, 3):
                cand = thr[g] + jnp.int32(wrap32(mult << b))
                n_ge = jnp.sum(jnp.where(impt[g] >= key_to_float(cand), 1.0, 0.0), axis=0, keepdims=True)
                reached = jnp.where(n_ge >= N_SELECT, cand, reached)
            thr[g] = reached
        for _ in range(2):
            for g in groups:
                zero = jnp.concatenate([_zero_after(thr[g])] * Q_PER_KV, axis=1)
                if piece < n_mask:
                    colmax[g] = window_mask_piece(g, piece, colmax[g], zero)
                elif piece < n_mask + n_exp:
                    if piece == n_mask:
                        m_w[g] = jnp.max(colmax[g], axis=0, keepdims=True)
                    window_exp_piece(g, piece - n_mask, m_w[g], zero)
            piece += 1

    scol = lax.broadcasted_iota(jnp.int32, (N_SLC_LANES, N_SLC_LANES), 1)
    earlier = jnp.where(scol < srow, 1.0, 0.0).astype(BF16)
    widen = lambda x: jnp.concatenate([x.astype(BF16)] * Q_PER_KV, axis=1)
    qt_tail, qt_main = [], []
    for g in groups:
        vw = jnp.concatenate([vwt_ref[0, kv(g), bases[qb(g)] // LANES + c] for c in range(wk // LANES)],
                             axis=1)
        cw_scr[g] += grow(g, 2) * _normalize_dup(_dot(vw, pw_scr[g]))
        kth = key_to_float(thr[g])
        above = impt[g] > kth
        tied = impt[g] == kth
        n_above = jnp.sum(jnp.where(above, 1.0, 0.0), axis=0, keepdims=True)
        tied_before = _dot(earlier, jnp.where(tied, 1.0, 0.0).astype(BF16))
        selected = above | (tied & (tied_before < N_SELECT - n_above))
        bias = jnp.where(selected, 0.0, NEG)
        bias_main = jnp.where(srow >= n_main * (SEL_TILE // SLC_BLOCK), NEG, bias)
        qt_tail.append(jnp.concatenate([qt[g], widen(bias)], axis=0))
        qt_main.append(jnp.concatenate([qt[g], widen(bias_main)], axis=0))

    kt = pl.multiple_of(n_main * SEL_TILE, SEL_TILE)
    krow = lax.broadcasted_iota(jnp.int32, (SEL_TILE, COLS), 0)
    m_t = []
    for g in groups:
        s_t = _dot(ksel_ref[0, kv(g), pl.ds(kt, SEL_TILE), :], qt_tail[g])
        s_t = jnp.where(kt + krow <= tqs[qb(g)], s_t, NEG)
        m_t.append(jnp.max(s_t, axis=0, keepdims=True))
        acct_scr[g] = _dot(vst_ref[0, kv(g), n_main], jnp.exp2(s_t - m_t[g]).astype(BF16))
    g_sel = [grow(g, 1) for g in groups]

    def scores(g, j, slot):
        k0 = pl.multiple_of(jnp.minimum(j, last_tile) * SEL_TILE, SEL_TILE)
        s_all[g, slot] = _dot(ksel_ref[0, kv(g), pl.ds(k0, SEL_TILE), :], qt_main[g])

    def softmax(g, slot):
        s = s_all[g, slot]
        m_prev = m_scr[g, 0:1, :]
        m_new = jnp.maximum(m_prev, jnp.max(s, axis=0, keepdims=True))
        a_all[g, slot] = jnp.broadcast_to(jnp.exp2(m_prev - m_new), (SUBLANES, COLS))
        p_all[g, slot] = jnp.exp2(s - m_new).astype(BF16)
        m_scr[g] = jnp.broadcast_to(m_new, (SUBLANES, COLS))

    def values(g, j, slot):
        acc_scr[g] = (a_all[g, slot, 0:1, :] * acc_scr[g]
                      + _dot(vst_ref[0, kv(g), jnp.minimum(j, last_tile)], p_all[g, slot]))

    m_scr[...] = jnp.full(m_scr.shape, NEG, F32)
    acc_scr[...] = jnp.zeros(acc_scr.shape, F32)
    for k in range(PIPE):
        for g in groups:
            scores(g, k, k)
    for k in range(PIPE // 2):
        for g in groups:
            softmax(g, k)

    def sweep(t, carry):
        for k in range(SWEEP_UNROLL):
            for g in groups:
                values(g, SWEEP_UNROLL * t + k, k % PIPE)
                softmax(g, (k + PIPE // 2) % PIPE)
                scores(g, SWEEP_UNROLL * t + k + PIPE, k % PIPE)
        return carry

    lax.fori_loop(0, (n_main + SWEEP_UNROLL - 1) // SWEEP_UNROLL, sweep, 0)

    pairs = [[] for _ in range(Q_PER_STEP)]
    for g in groups:
        m_p = m_scr[g, 0:1, :]
        m_tot = jnp.maximum(m_p, m_t[g])
        o_s = _normalize_dup(jnp.exp2(m_p - m_tot) * acc_scr[g] + jnp.exp2(m_t[g] - m_tot) * acct_scr[g])
        comb = cw_scr[g] + g_sel[g] * o_s
        outs = [comb[:, r * Q_BLOCK:(r + 1) * Q_BLOCK].T for r in range(Q_PER_KV)]
        pairs[qb(g)] += [jnp.where(lo_half, outs[0], outs[1]), jnp.where(lo_half, outs[2], outs[3])]
    for d in range(Q_PER_STEP):
        o_ref[0, d * Q_BLOCK:(d + 1) * Q_BLOCK, :] = jnp.concatenate(pairs[d], axis=1)


def _attn(qpt, ksel, vst, kw, vwt, kcmp, vcmpt, c2st, gatest):
    batch, seq = qpt.shape[0], qpt.shape[1] * Q_BLOCK
    per_batch = lambda a: pl.BlockSpec((1,) + a.shape[1:], lambda b, i: (b,) + (0,) * (a.ndim - 1),
                                       pipeline_mode=pl.Buffered(1))
    per_block = lambda a: pl.BlockSpec((1, Q_PER_STEP) + a.shape[2:], lambda b, i: (b, i) + (0,) * (a.ndim - 2))
    chains = Q_PER_STEP * N_KV_HEADS
    grp = lambda *shape: pltpu.VMEM((chains,) + shape, F32)
    return pl.pallas_call(
        _attn_kernel,
        grid=(batch, seq // (Q_PER_STEP * Q_BLOCK)),
        in_specs=[per_block(qpt), per_batch(ksel), per_batch(vst), per_batch(kw), per_batch(vwt), per_batch(kcmp),
                  per_batch(vcmpt), pl.BlockSpec(c2st.shape, lambda b, i: (0, 0)), per_block(gatest)],
        out_specs=pl.BlockSpec((1, Q_PER_STEP * Q_BLOCK, D_ATT), lambda b, i: (b, i, 0)),
        out_shape=jax.ShapeDtypeStruct((batch, seq, D_ATT), F32),
        scratch_shapes=[grp(PIPE, SEL_TILE, COLS), pltpu.VMEM((chains, PIPE, SEL_TILE, COLS), BF16),
                        grp(PIPE, SUBLANES, COLS), grp(SUBLANES, COLS), grp(V_ROWS, COLS), grp(V_ROWS, COLS),
                        grp(LANES, COLS), grp(WINDOW + Q_BLOCK, COLS),
                        pltpu.VMEM((chains, WINDOW + Q_BLOCK, COLS), BF16)],
        compiler_params=pltpu.CompilerParams(dimension_semantics=("arbitrary", "arbitrary"),
                                             vmem_limit_bytes=VMEM_LIMIT_BYTES),
        name="attn",
    )(qpt, ksel, vst, kw, vwt, kcmp, vcmpt, c2st, gatest)


FF_CHUNKS = 1
FF_CHUNK = D_FF // FF_CHUNKS


def _post_kernel(x_ref, oatt_ref, mconv_ref, p_ref, onag_ref, wo_ref, lnf_ref, wup_ref, fcw_ref, fcb_ref,
                 wdn_ref, lnp_ref, wpg_ref, wpe_ref, out_ref, gbuf, *, tiles_per_seq):
    tm = x_ref.shape[0]
    it = pl.program_id(0) % tiles_per_seq

    mixed = jnp.concatenate([_rms(oatt_ref[...], onag_ref[...]).astype(BF16), mconv_ref[...]], axis=1)
    h1 = x_ref[...] + _dot(mixed, wo_ref[...])
    xn = _rms(h1, lnf_ref[...]).astype(BF16)

    @pl.when(it == 0)
    def _():
        gbuf[:, 0:SUBLANES, :] = jnp.zeros((FF_CHUNKS, SUBLANES, FF_CHUNK), F32)

    h2 = h1
    for c in range(FF_CHUNKS):
        cols = slice(c * FF_CHUNK, (c + 1) * FF_CHUNK)
        gpre = _dot(xn, wup_ref[:, cols])
        up = _dot(xn, wup_ref[:, D_FF + c * FF_CHUNK:D_FF + (c + 1) * FF_CHUNK])
        gbuf[c, SUBLANES:SUBLANES + tm, :] = gpre
        gate = (fcw_ref[2:3, cols] * gpre + fcw_ref[1:2, cols] * gbuf[c, SUBLANES - 1:SUBLANES - 1 + tm, :]
                + fcw_ref[0:1, cols] * gbuf[c, SUBLANES - 2:SUBLANES - 2 + tm, :]) + fcb_ref[:, cols]
        gbuf[c, 0:SUBLANES, :] = gbuf[c, tm:tm + SUBLANES, :]
        act = (gate * jax.nn.sigmoid(gate) * up).astype(BF16)
        h2 = h2 + _dot(act, wdn_ref[cols, :])

    xn2 = _rms(h2, lnp_ref[...]).astype(BF16)
    out_ref[...] = h2 + jax.nn.sigmoid(_dot(xn2, wpg_ref[...])) * _dot(p_ref[...].astype(BF16), wpe_ref[...])


def _post(x2, oatt, mconv, p2, onag, wo, lnf, wup, fcw, fcb, wdn, lnp, wpg, wpe, *, seq, tm):
    n = x2.shape[0]
    tps = seq // tm
    row = lambda w: pl.BlockSpec((tm, w), lambda i: (i, 0))
    full = lambda a: pl.BlockSpec(a.shape, lambda i: (0,) * a.ndim, pipeline_mode=pl.Buffered(1))
    return pl.pallas_call(
        functools.partial(_post_kernel, tiles_per_seq=tps),
        grid=(n // tm,),
        in_specs=[row(D_MODEL), row(D_ATT), row(D_CONV), row(D_PLE), full(onag), full(wo), full(lnf), full(wup),
                  full(fcw), full(fcb), full(wdn), full(lnp), full(wpg), full(wpe)],
        out_specs=row(D_MODEL),
        out_shape=jax.ShapeDtypeStruct((n, D_MODEL), F32),
        scratch_shapes=[pltpu.VMEM((FF_CHUNKS, tm + SUBLANES, FF_CHUNK), F32)],
        compiler_params=pltpu.CompilerParams(dimension_semantics=("arbitrary",),
                                             vmem_limit_bytes=VMEM_LIMIT_BYTES),
        name="post",
    )(x2, oatt, mconv, p2, onag, wo, lnf, wup, fcw, fcb, wdn, lnp, wpg, wpe)


def _rope_tables(pos):
    half = ROT_DIM // 2
    d = np.arange(LANES) % HEAD_DIM
    inv_freq = jnp.tile(ROPE_THETA ** (-jnp.arange(half, dtype=F32) * 2.0 / ROT_DIM), LANES // half)
    ang = pos.astype(F32)[:, None] * inv_freq[None, :]
    c, sn = jnp.cos(ang), jnp.sin(ang)
    first, second = (d < half)[None, :], ((d >= half) & (d < ROT_DIM))[None, :]
    cos = jnp.where(first | second, c, 1.0)
    sa = jnp.where(first, -sn, 0.0)
    sb = jnp.where(second, sn, 0.0)
    return cos, sa, sb


def _block_diag_mean(width):
    idx = np.arange(width) // HEAD_DIM
    return jnp.asarray((idx[:, None] == idx[None, :]).astype(np.float32) / HEAD_DIM, BF16)


def _cmp_to_slc_t(nch):
    cs = CMP_STRIDE * np.arange(nch)[None, :]
    ss = SLC_BLOCK * np.arange(N_SLC_LANES)[:, None]
    ov = np.clip(np.minimum(cs + CMP_BLOCK, ss + SLC_BLOCK) - np.maximum(cs, ss), 0, None)
    return jnp.asarray(ov.astype(np.float32) / CMP_BLOCK, BF16)


def _layer(h, p_l, ln_mix_g, w_in, qn_g, kn_g, pe_k, pe_v, w_ck1, w_ck2, w_cv1, w_cv2, conv_w, on_att_g,
           on_conv_g, w_o, ln_ffn_g, w_up, ffn_conv_w, ffn_conv_b, w_down, ln_ple_g, w_pg, w_pe):
    batch, seq, _ = h.shape
    assert seq % SEL_TILE == 0 and seq // SLC_BLOCK <= N_SLC_LANES and seq // SLC_BLOCK >= N_SELECT
    assert seq >= WINDOW + Q_BLOCK
    n = batch * seq
    nch = seq // CMP_STRIDE
    x2 = h.reshape(n, D_MODEL)
    row = lambda v: v.reshape(1, -1).astype(F32)

    o_q, o_kv, o_g, o_cv = 0, D_ATT, D_ATT + 6 * KV_W, D_ATT + 6 * KV_W + N_BRANCH * N_Q_HEADS
    wq = w_in[:, o_q:o_kv].astype(BF16)
    wkv = w_in[:, o_kv:o_g].astype(BF16)
    wcv = w_in[:, o_cv:].astype(BF16)
    per_g = Q_PER_KV * N_BRANCH
    wg = jnp.concatenate(
        [jnp.pad(w_in[:, o_g + g * per_g:o_g + (g + 1) * per_g], ((0, 0), (0, LANES - per_g)))
         for g in range(N_KV_HEADS)], axis=1).astype(BF16)
    cos, sa, sb = _rope_tables(jnp.arange(seq))
    bd512, bd128 = _block_diag_mean(D_ATT), _block_diag_mean(KV_W)
    tile_heads = lambda v, k: jnp.tile(v.astype(F32), k).reshape(1, -1)

    tm = ROW_TILE
    assert seq % ROW_TILE == 0 and seq % POST_TILE == 0
    qpt, kc2, vc2, ksel, vst, kw, vwt, gatest, mconv = _inproj(
        x2, row(ln_mix_g), wq, wkv, wg, wcv, tile_heads(qn_g, N_Q_HEADS),
        jnp.stack([jnp.tile(kn_g[1], N_KV_HEADS), jnp.tile(kn_g[2], N_KV_HEADS)]).astype(F32),
        bd512, bd128, cos, sa, sb, conv_w.astype(F32), row(on_conv_g), batch=batch, seq=seq, tm=tm)

    eye = jnp.eye(N_KV_HEADS, dtype=F32)
    half = CMP_BLOCK // 2

    def w1_parts(w1):
        wfull = jnp.einsum('ldh,gk->lgdkh', w1, eye).reshape(CMP_BLOCK, KV_W, N_KV_HEADS * CMP_HIDDEN)
        return (wfull[:half].reshape(half * KV_W, -1).astype(BF16),
                wfull[half:].reshape(half * KV_W, -1).astype(BF16))

    def pe_parts(pe):
        pf = jnp.broadcast_to(pe[:, None, :], (CMP_BLOCK, N_KV_HEADS, HEAD_DIM)).astype(F32)
        return pf[:half].reshape(1, -1), pf[half:].reshape(1, -1)

    w2bd = lambda w2: jnp.einsum('hd,gk->ghkd', w2, eye).reshape(N_KV_HEADS * CMP_HIDDEN, KV_W).astype(BF16)
    w1ak, w1bk = w1_parts(w_ck1)
    w1av, w1bv = w1_parts(w_cv1)
    peak, pebk = pe_parts(pe_k)
    peav, pebv = pe_parts(pe_v)
    ccos, csa, csb = _rope_tables(CMP_STRIDE * jnp.arange(nch) + CMP_BLOCK - 1)
    kcmp, vcmpt = _compress(kc2.reshape(batch, nch, CMP_STRIDE * KV_W), vc2.reshape(batch, nch, CMP_STRIDE * KV_W),
                            w1ak, w1bk, w1av, w1bv, peak, pebk, peav, pebv, w2bd(w_ck2), w2bd(w_cv2),
                            tile_heads(kn_g[0], N_KV_HEADS), bd128, ccos, csa, csb)

    oatt = _attn(qpt, ksel, vst, kw, vwt, kcmp, vcmpt, _cmp_to_slc_t(nch), gatest)

    out = _post(x2, oatt.reshape(n, D_ATT), mconv, p_l.reshape(n, D_PLE), row(on_att_g), w_o.astype(BF16),
                row(ln_ffn_g), w_up.astype(BF16), ffn_conv_w.astype(F32), row(ffn_conv_b), w_down.astype(BF16),
                row(ln_ple_g), w_pg.astype(BF16), w_pe.astype(BF16), seq=seq, tm=POST_TILE)
    return out.reshape(batch, seq, D_MODEL)


def kernel(x, p, ln_mix_g, w_in, qn_g, kn_g, pe_k, pe_v, w_ck1, w_ck2, w_cv1, w_cv2, conv_w, on_att_g,
           on_conv_g, w_o, ln_ffn_g, w_up, ffn_conv_w, ffn_conv_b, w_down, ln_ple_g, w_pg, w_pe):
    h = x
    for i in range(p.shape[0]):
        h = _layer(h, p[i], ln_mix_g[i], w_in[i], qn_g[i], kn_g[i], pe_k[i], pe_v[i], w_ck1[i], w_ck2[i],
                   w_cv1[i], w_cv2[i], conv_w[i], on_att_g[i], on_conv_g[i], w_o[i], ln_ffn_g[i], w_up[i],
                   ffn_conv_w[i], ffn_conv_b[i], w_down[i], ln_ple_g[i], w_pg[i], w_pe[i])
    return h
```

```python
import functools

import jax
import jax.numpy as jnp
import numpy as np
from jax import lax
from jax.experimental import pallas as pl
from jax.experimental.pallas import tpu as pltpu

D_MODEL = 1024
HEAD_DIM = 64
N_Q_HEADS = 8
N_KV_HEADS = 2
Q_PER_KV = N_Q_HEADS // N_KV_HEADS
D_ATT = N_Q_HEADS * HEAD_DIM
D_CONV = D_MODEL - D_ATT
KV_W = N_KV_HEADS * HEAD_DIM
N_BRANCH = 3
CONV_TAPS = 3
ROT_DIM = HEAD_DIM // 4
ROPE_THETA = 500000.0
CMP_BLOCK = 32
CMP_STRIDE = 16
CMP_HIDDEN = 256
SLC_BLOCK = 64
N_SELECT = 16
WINDOW = 512
Q_BLOCK = 128
D_FF = 2816
D_PLE = 256
EPS = 1e-6
NEG = -1e30
INT32_MIN = -2 ** 31
LOG2E = float(np.log2(np.e))
V_ROWS = HEAD_DIM + 16
GATE_ROWS = 16
SLC_SHIFT = SLC_BLOCK.bit_length() - 1
CMP_SHIFT = CMP_STRIDE.bit_length() - 1
assert 1 << SLC_SHIFT == SLC_BLOCK and 1 << CMP_SHIFT == CMP_STRIDE

LANES = 128
SUBLANES = 8
N_SLC_LANES = LANES
VMEM_LIMIT_BYTES = 56 * 1024 * 1024

F32 = jnp.float32
BF16 = jnp.bfloat16


def _dot(a, b):
    return jnp.dot(a, b, preferred_element_type=F32)


def _rms(x, g):
    return x * lax.rsqrt(jnp.mean(x * x, axis=-1, keepdims=True) + EPS) * g


def _head_rms_rope(x, g, bd, cos, sa, sb):
    w = x.shape[-1]
    msq = _dot((x * x).astype(BF16), bd)
    xn = x * lax.rsqrt(msq + EPS) * g
    return xn * cos + pltpu.roll(xn, w - ROT_DIM // 2, 1) * sa + pltpu.roll(xn, ROT_DIM // 2, 1) * sb


def _dup_halves(x):
    r = pltpu.roll(x, HEAD_DIM, 1)
    lane = lax.broadcasted_iota(jnp.int32, x.shape, 1)
    lo = lane < HEAD_DIM
    return jnp.where(lo, x, r), jnp.where(lo, r, x)


def _inproj_kernel(x_ref, lng_ref, wq_ref, wkv_ref, wg_ref, wcv_ref, qng_ref, kng_ref, bd512_ref, bd128_ref,
                   cos_ref, sa_ref, sb_ref, convw_ref, oncg_ref,
                   q_out, kc_out, vc_out, ksel_out, vs_out, kw_out, vw_out, gate_out, mconv_out,
                   zbuf, *, tiles_per_seq):
    tm = x_ref.shape[0]
    it = pl.program_id(0) % tiles_per_seq
    x = x_ref[...]
    xn = _rms(x, lng_ref[...]).astype(BF16)

    cos, sa, sb = cos_ref[...], sa_ref[...], sb_ref[...]
    cos4, sa4, sb4 = (jnp.concatenate([t] * 4, axis=1) for t in (cos, sa, sb))
    q = _dot(xn, wq_ref[...])
    qr = _head_rms_rope(q, qng_ref[...], bd512_ref[...], cos4, sa4, sb4)
    qs = qr * (HEAD_DIM ** -0.5 * LOG2E)
    for blk_i in range(tm // Q_BLOCK):
        rows = slice(blk_i * Q_BLOCK, (blk_i + 1) * Q_BLOCK)
        for pr in range(N_Q_HEADS // 2):
            q_out[0, blk_i, pr] = qs[rows, pr * LANES:(pr + 1) * LANES].T.astype(BF16)

    kv = _dot(xn, wkv_ref[...])
    kc_out[...] = kv[:, 0 * KV_W:1 * KV_W]
    vc_out[...] = kv[:, 1 * KV_W:2 * KV_W]
    ks = _head_rms_rope(kv[:, 2 * KV_W:3 * KV_W], kng_ref[0:1, :], bd128_ref[...], cos, sa, sb)
    kw = _head_rms_rope(kv[:, 4 * KV_W:5 * KV_W], kng_ref[1:2, :], bd128_ref[...], cos, sa, sb)
    vs = kv[:, 3 * KV_W:4 * KV_W]
    vw = kv[:, 5 * KV_W:6 * KV_W]
    tpos = it * tm + lax.broadcasted_iota(jnp.int32, (tm, N_SLC_LANES), 0)
    blk = lax.broadcasted_iota(jnp.int32, (tm, N_SLC_LANES), 1)
    onehot = jnp.where(lax.shift_right_logical(tpos, SLC_SHIFT) == blk, 1.0, 0.0).astype(BF16)
    ks_d, vs_d, kw_d, vw_d = (_dup_halves(t) for t in (ks, vs, kw, vw))
    vrow = lax.broadcasted_iota(jnp.int32, (V_ROWS, tm), 0)
    for g in range(N_KV_HEADS):
        ksel_out[0, g] = jnp.concatenate([ks_d[g].astype(BF16), onehot], axis=1)
        kw_out[0, g] = kw_d[g].astype(BF16)
        vst = jnp.where(vrow < HEAD_DIM, vs_d[g].T[:V_ROWS], 1.0).astype(BF16)
        for c in range(tm // SEL_TILE):
            vs_out[0, g, c] = vst[:, c * SEL_TILE:(c + 1) * SEL_TILE]
        vwt = jnp.where(vrow < HEAD_DIM, vw_d[g].T[:V_ROWS], 1.0).astype(BF16)
        for c in range(tm // LANES):
            vw_out[0, g, c] = vwt[:, c * LANES:(c + 1) * LANES]

    gates = jax.nn.sigmoid(_dot(xn, wg_ref[...]))
    for blk_i in range(tm // Q_BLOCK):
        rows = slice(blk_i * Q_BLOCK, (blk_i + 1) * Q_BLOCK)
        for g in range(N_KV_HEADS):
            gate_out[0, blk_i, g] = gates[rows, g * LANES:(g + 1) * LANES].T[:GATE_ROWS]

    cv = _dot(xn, wcv_ref[...])
    cb, cc, cx = cv[:, :D_CONV], cv[:, D_CONV:2 * D_CONV], cv[:, 2 * D_CONV:]
    z = cc * cx

    @pl.when(it == 0)
    def _():
        zbuf[0:SUBLANES, :] = jnp.zeros((SUBLANES, D_CONV), F32)

    zbuf[SUBLANES:SUBLANES + tm, :] = z
    y = (convw_ref[2:3, :] * z + convw_ref[1:2, :] * zbuf[SUBLANES - 1:SUBLANES - 1 + tm, :]
         + convw_ref[0:1, :] * zbuf[SUBLANES - 2:SUBLANES - 2 + tm, :])
    zbuf[0:SUBLANES, :] = zbuf[tm:tm + SUBLANES, :]
    mconv_out[...] = _rms(cb * y, oncg_ref[...]).astype(BF16)


def _inproj(x2, lng, wq, wkv, wg, wcv, qng, kng, bd512, bd128, cos, sa, sb, convw, oncg, *, batch, seq, tm):
    n = batch * seq
    tps = seq // tm
    row = lambda w: pl.BlockSpec((tm, w), lambda i: (i, 0))
    full = lambda a: pl.BlockSpec(a.shape, lambda i: (0,) * a.ndim)
    tab = pl.BlockSpec((tm, LANES), lambda i: (i % tps, 0))
    grp = lambda w: pl.BlockSpec((1, N_KV_HEADS, tm, w), lambda i: (i // tps, 0, i % tps, 0))
    gshape = lambda w: jax.ShapeDtypeStruct((batch, N_KV_HEADS, seq, w), BF16)
    assert tm % SEL_TILE == 0
    vtile = lambda keys: pl.BlockSpec((1, N_KV_HEADS, tm // keys, V_ROWS, keys),
                                      lambda i: (i // tps, 0, i % tps, 0, 0))
    vshape = lambda keys: jax.ShapeDtypeStruct((batch, N_KV_HEADS, seq // keys, V_ROWS, keys), BF16)
    qblk = lambda tile: pl.BlockSpec((1, tm // Q_BLOCK) + tile, lambda i: (i // tps, i % tps) + (0,) * len(tile))
    return pl.pallas_call(
        functools.partial(_inproj_kernel, tiles_per_seq=tps),
        grid=(n // tm,),
        in_specs=[row(D_MODEL), full(lng), full(wq), full(wkv), full(wg), full(wcv), full(qng), full(kng),
                  full(bd512), full(bd128), tab, tab, tab, full(convw), full(oncg)],
        out_specs=[qblk((N_Q_HEADS // 2, LANES, Q_BLOCK)), row(KV_W), row(KV_W), grp(2 * LANES), vtile(SEL_TILE),
                   grp(LANES), vtile(LANES), qblk((N_KV_HEADS, GATE_ROWS, Q_BLOCK)), row(D_CONV)],
        out_shape=[jax.ShapeDtypeStruct((batch, seq // Q_BLOCK, N_Q_HEADS // 2, LANES, Q_BLOCK), BF16),
                   jax.ShapeDtypeStruct((n, KV_W), F32),
                   jax.ShapeDtypeStruct((n, KV_W), F32), gshape(2 * LANES), vshape(SEL_TILE), gshape(LANES),
                   vshape(LANES),
                   jax.ShapeDtypeStruct((batch, seq // Q_BLOCK, N_KV_HEADS, GATE_ROWS, Q_BLOCK), F32),
                   jax.ShapeDtypeStruct((n, D_CONV), BF16)],
        scratch_shapes=[pltpu.VMEM((tm + SUBLANES, D_CONV), F32)],
        compiler_params=pltpu.CompilerParams(dimension_semantics=("arbitrary",),
                                             vmem_limit_bytes=VMEM_LIMIT_BYTES),
        name="inproj",
    )(x2, lng, wq, wkv, wg, wcv, qng, kng, bd512, bd128, cos, sa, sb, convw, oncg)


def _gelu_tanh(x):
    return 0.5 * x * (1.0 + jnp.tanh(np.sqrt(2.0 / np.pi).astype(np.float32) * (x + 0.044715 * (x * x * x))))


def _compress_kernel(kc_ref, vc_ref, w1ak_ref, w1bk_ref, w1av_ref, w1bv_ref, peak_ref, pebk_ref, peav_ref,
                     pebv_ref, w2k_ref, w2v_ref, kng_ref, bd128_ref, cos_ref, sa_ref, sb_ref,
                     kcmp_out, vcmp_out):
    nch = kc_ref.shape[1] // CMP_STRIDE

    def chunks(ref):
        return jnp.concatenate([ref[0, pl.ds(l, nch, stride=CMP_STRIDE), :] for l in range(CMP_STRIDE)], axis=1)

    def compress(x, w1a, w1b, pea, peb, w2):
        a = _dot((x + pea).astype(BF16), w1a)
        b = _dot((x + peb).astype(BF16), w1b)
        hid = _gelu_tanh(a + pltpu.roll(b, nch - 1, 0))
        return _dot(hid.astype(BF16), w2)

    kc = compress(chunks(kc_ref), w1ak_ref[...], w1bk_ref[...], peak_ref[...], pebk_ref[...], w2k_ref[...])
    vc = compress(chunks(vc_ref), w1av_ref[...], w1bv_ref[...], peav_ref[...], pebv_ref[...], w2v_ref[...])
    kc = _head_rms_rope(kc, kng_ref[...], bd128_ref[...], cos_ref[...], sa_ref[...], sb_ref[...])
    kd, vd = _dup_halves(kc), _dup_halves(vc)
    for g in range(N_KV_HEADS):
        kcmp_out[0, g] = kd[g].astype(BF16)
        vcmp_out[0, g] = vd[g].T.astype(BF16)


def _compress(kc3, vc3, w1ak, w1bk, w1av, w1bv, peak, pebk, peav, pebv, w2k, w2v, kng0, bd128, cos, sa, sb):
    batch, seq, wide = kc3.shape
    nch = seq // CMP_STRIDE
    full = lambda a: pl.BlockSpec(a.shape, lambda b: (0,) * a.ndim)
    tok = pl.BlockSpec((1, seq, wide), lambda b: (b, 0, 0))
    kout = pl.BlockSpec((1, N_KV_HEADS, nch, LANES), lambda b: (b, 0, 0, 0))
    vout = pl.BlockSpec((1, N_KV_HEADS, LANES, nch), lambda b: (b, 0, 0, 0))
    consts = (w1ak, w1bk, w1av, w1bv, peak, pebk, peav, pebv, w2k, w2v, kng0, bd128, cos, sa, sb)
    return pl.pallas_call(
        _compress_kernel,
        grid=(batch,),
        in_specs=[tok, tok] + [full(a) for a in consts],
        out_specs=[kout, vout],
        out_shape=[jax.ShapeDtypeStruct((batch, N_KV_HEADS, nch, LANES), BF16),
                   jax.ShapeDtypeStruct((batch, N_KV_HEADS, LANES, nch), BF16)],
        compiler_params=pltpu.CompilerParams(dimension_semantics=("arbitrary",),
                                             vmem_limit_bytes=VMEM_LIMIT_BYTES),
        name="compress",
    )(kc3, vc3, *consts)


SEL_TILE = 256
COLS = Q_PER_KV * Q_BLOCK
ROW_TILE = 512
POST_TILE = 512
Q_PER_STEP = 2
PIPE = 2
SWEEP_UNROLL = 2
assert SWEEP_UNROLL % PIPE == 0 and SEL_TILE % (Q_PER_STEP * Q_BLOCK) == 0
WIN_MASK_ROWS = 32
WIN_EXP_ROWS = 64
assert (WINDOW + Q_BLOCK) // WIN_MASK_ROWS + (WINDOW + Q_BLOCK) // WIN_EXP_ROWS <= 32


def _split2(x):
    hi = x.astype(BF16)
    return hi, (x - hi.astype(F32)).astype(BF16)


def _zero_after(x):
    return lax.shift_right_logical(lax.shift_right_logical(x, 16), 16)


def _normalize_dup(acc):
    o = acc[:HEAD_DIM] / acc[HEAD_DIM:HEAD_DIM + 1]
    return jnp.concatenate([o, o], axis=0)


def _attn_kernel_single(q_ref, ksel_ref, vst_ref, kw_ref, vwt_ref, kcmp_ref, vcmpt_ref, c2st_ref, gate_ref, o_ref,
                        s_all, p_all, a_all, m_scr, acc_scr, acct_scr, cw_scr, sw_scr, pw_scr):
    i = pl.program_id(2)
    start = i * Q_BLOCK
    nch = kcmp_ref.shape[2]
    last_tile = ksel_ref.shape[2] // SEL_TILE - 1

    lane = lax.broadcasted_iota(jnp.int32, (Q_BLOCK, LANES), 1)
    lo_half = lane < HEAD_DIM
    top_half = lax.broadcasted_iota(jnp.int32, (LANES, Q_BLOCK), 0) < HEAD_DIM
    zero_bf = jnp.zeros((LANES, Q_BLOCK), BF16)
    qt = jnp.concatenate([jnp.where(top_half if r % 2 == 0 else ~top_half, q_ref[0, 0, r // 2], zero_bf)
                          for r in range(Q_PER_KV)], axis=1)
    tq = start + (lax.broadcasted_iota(jnp.int32, (1, COLS), 1) & (Q_BLOCK - 1))
    gt = gate_ref[0, 0, 0]
    grow = lambda b: jnp.concatenate([gt[r * N_BRANCH + b:r * N_BRANCH + b + 1, :] for r in range(Q_PER_KV)],
                                     axis=1)

    wk = WINDOW + Q_BLOCK
    base = pl.multiple_of(jnp.maximum(start - WINDOW, 0), Q_BLOCK)

    wrow = lax.broadcasted_iota(jnp.int32, (WIN_MASK_ROWS, COLS), 0)

    def window_mask_piece(k, colmax, zero):
        rows = slice(WIN_MASK_ROWS * k, WIN_MASK_ROWS * (k + 1))
        back = (tq + zero - base - WIN_MASK_ROWS * k) - wrow
        in_window = lax.bitcast_convert_type(back, jnp.uint32) < jnp.uint32(WINDOW)
        sm = jnp.where(in_window, sw_scr[rows, :], NEG)
        sw_scr[rows, :] = sm
        for c in range(WIN_MASK_ROWS // SUBLANES):
            colmax = jnp.maximum(colmax, sm[c * SUBLANES:(c + 1) * SUBLANES])
        return colmax

    def window_exp_piece(k, m_w, zero):
        rows = slice(WIN_EXP_ROWS * k, WIN_EXP_ROWS * (k + 1))
        pw_scr[rows, :] = jnp.exp2(sw_scr[rows, :] - (m_w + zero.astype(F32))).astype(BF16)

    s_c = _dot(kcmp_ref[0, 0], qt)
    sw_scr[...] = _dot(kw_ref[0, 0, pl.ds(base, wk), :], qt)
    nrow = lax.broadcasted_iota(jnp.int32, (nch, COLS), 0)
    last_valid = lax.shift_right_arithmetic(tq - (CMP_BLOCK - 1), CMP_SHIFT)
    s_c = jnp.where(nrow <= last_valid, s_c, NEG)
    m_c = jnp.max(s_c, axis=0, keepdims=True)
    e_c = jnp.exp2(s_c - m_c)
    l_c = jnp.sum(e_c, axis=0, keepdims=True)
    p_c = e_c * jnp.where(m_c > 0.5 * NEG, 1.0 / l_c, 0.0)
    cw_scr[...] = grow(0) * _dot(vcmpt_ref[0, 0], p_c.astype(BF16))

    psum = sum(p_c[:, r * Q_BLOCK:(r + 1) * Q_BLOCK] for r in range(Q_PER_KV))
    c2st = c2st_ref[...]
    impt = sum(_dot(c2st, t) for t in _split2(psum))
    srow = lax.broadcasted_iota(jnp.int32, (N_SLC_LANES, Q_BLOCK), 0)
    tqq = start + lax.broadcasted_iota(jnp.int32, (N_SLC_LANES, Q_BLOCK), 1)
    cur = lax.shift_right_logical(tqq, SLC_SHIFT)
    forced = (srow == 0) | (srow == cur) | (srow == cur - 1)
    future = srow * SLC_BLOCK > tqq
    impt = jnp.where(forced, 1e9, jnp.where(future, -1e9, impt))
    key_to_float = lambda k: lax.bitcast_convert_type(jnp.where(k >= 0, k, k ^ jnp.int32(0x7FFFFFFF)), F32)
    thr = jnp.full((1, Q_BLOCK), INT32_MIN, jnp.int32)
    n_mask, n_exp = wk // WIN_MASK_ROWS, wk // WIN_EXP_ROWS
    colmax = jnp.full((SUBLANES, COLS), NEG, F32)
    wrap32 = lambda v: ((v + 2 ** 31) % 2 ** 32) - 2 ** 31
    piece = 0
    for b in range(30, -1, -2):
        reached = None
        for mult in (1, 2, 3):
            cand = thr + jnp.int32(wrap32(mult << b))
            n_ge = jnp.sum(jnp.where(impt >= key_to_float(cand), 1.0, 0.0), axis=0, keepdims=True)
            reached = jnp.where(n_ge >= N_SELECT, cand, thr if reached is None else reached)
        thr = reached
        zero = jnp.concatenate([_zero_after(thr)] * Q_PER_KV, axis=1)
        for _ in range(2):
            if piece < n_mask:
                colmax = window_mask_piece(piece, colmax, zero)
            elif piece < n_mask + n_exp:
                if piece == n_mask:
                    m_w = jnp.max(colmax, axis=0, keepdims=True)
                window_exp_piece(piece - n_mask, m_w, zero)
            piece += 1
    wtile = base // LANES
    vw = jnp.concatenate([vwt_ref[0, 0, wtile + c] for c in range(wk // LANES)], axis=1)
    cw_scr[...] += grow(2) * _normalize_dup(_dot(vw, pw_scr[...]))
    kth = key_to_float(thr)
    above = impt > kth
    tied = impt == kth
    n_above = jnp.sum(jnp.where(above, 1.0, 0.0), axis=0, keepdims=True)
    scol = lax.broadcasted_iota(jnp.int32, (N_SLC_LANES, N_SLC_LANES), 1)
    earlier = jnp.where(scol < srow, 1.0, 0.0).astype(BF16)
    tied_before = _dot(earlier, jnp.where(tied, 1.0, 0.0).astype(BF16))
    selected = above | (tied & (tied_before < N_SELECT - n_above))
    bias = jnp.where(selected, 0.0, NEG)
    n_main = start // SEL_TILE
    bias_main = jnp.where(srow >= n_main * (SEL_TILE // SLC_BLOCK), NEG, bias)
    widen = lambda b: jnp.concatenate([b.astype(BF16)] * Q_PER_KV, axis=1)
    qt_tail = jnp.concatenate([qt, widen(bias)], axis=0)
    qt_main = jnp.concatenate([qt, widen(bias_main)], axis=0)

    kt = pl.multiple_of(n_main * SEL_TILE, SEL_TILE)
    s_t = _dot(ksel_ref[0, 0, pl.ds(kt, SEL_TILE), :], qt_tail)
    krow = lax.broadcasted_iota(jnp.int32, (SEL_TILE, COLS), 0)
    s_t = jnp.where(kt + krow <= tq, s_t, NEG)
    m_t = jnp.max(s_t, axis=0, keepdims=True)
    acct_scr[...] = _dot(vst_ref[0, 0, n_main], jnp.exp2(s_t - m_t).astype(BF16))
    g_sel = grow(1)

    def scores(j, slot):
        k0 = pl.multiple_of(jnp.minimum(j, last_tile) * SEL_TILE, SEL_TILE)
        s_all[slot] = _dot(ksel_ref[0, 0, pl.ds(k0, SEL_TILE), :], qt_main)

    def softmax(slot):
        s = s_all[slot]
        m_prev = m_scr[0:1, :]
        m_new = jnp.maximum(m_prev, jnp.max(s, axis=0, keepdims=True))
        a_all[slot] = jnp.broadcast_to(jnp.exp2(m_prev - m_new), (SUBLANES, COLS))
        p_all[slot] = jnp.exp2(s - m_new).astype(BF16)
        m_scr[...] = jnp.broadcast_to(m_new, m_scr.shape)

    def values(j, slot):
        acc_scr[...] = (a_all[slot, 0:1, :] * acc_scr[...]
                        + _dot(vst_ref[0, 0, jnp.minimum(j, last_tile)], p_all[slot]))

    m_scr[...] = jnp.full(m_scr.shape, NEG, F32)
    acc_scr[...] = jnp.zeros(acc_scr.shape, F32)
    for k in range(PIPE):
        scores(k, k)
    for k in range(PIPE // 2):
        softmax(k)

    def sweep(t, carry):
        for k in range(SWEEP_UNROLL):
            values(SWEEP_UNROLL * t + k, k % PIPE)
            softmax((k + PIPE // 2) % PIPE)
            scores(SWEEP_UNROLL * t + k + PIPE, k % PIPE)
        return carry

    lax.fori_loop(0, (n_main + SWEEP_UNROLL - 1) // SWEEP_UNROLL, sweep, 0)
    m_p = m_scr[0:1, :]
    m_tot = jnp.maximum(m_p, m_t)
    o_s = _normalize_dup(jnp.exp2(m_p - m_tot) * acc_scr[...] + jnp.exp2(m_t - m_tot) * acct_scr[...])

    comb = cw_scr[...] + g_sel * o_s
    outs = [comb[:, r * Q_BLOCK:(r + 1) * Q_BLOCK].T for r in range(Q_PER_KV)]
    o_ref[0] = jnp.concatenate([jnp.where(lo_half, outs[0], outs[1]), jnp.where(lo_half, outs[2], outs[3])],
                               axis=1)


def _attn_single(qpt, ksel, vst, kw, vwt, kcmp, vcmpt, c2st, gatest):
    batch, seq = qpt.shape[0], qpt.shape[1] * Q_BLOCK
    per_group = lambda a: pl.BlockSpec((1, 1) + a.shape[2:], lambda b, g, i: (b, g) + (0,) * (a.ndim - 2))
    return pl.pallas_call(
        _attn_kernel_single,
        grid=(batch, N_KV_HEADS, seq // Q_BLOCK),
        in_specs=[pl.BlockSpec((1, 1, Q_PER_KV // 2, LANES, Q_BLOCK), lambda b, g, i: (b, i, g, 0, 0)),
                  per_group(ksel), per_group(vst), per_group(kw), per_group(vwt), per_group(kcmp),
                  per_group(vcmpt), pl.BlockSpec(c2st.shape, lambda b, g, i: (0, 0)),
                  pl.BlockSpec((1, 1, 1, GATE_ROWS, Q_BLOCK), lambda b, g, i: (b, i, g, 0, 0))],
        out_specs=pl.BlockSpec((1, Q_BLOCK, Q_PER_KV * HEAD_DIM), lambda b, g, i: (b, i, g)),
        out_shape=jax.ShapeDtypeStruct((batch, seq, D_ATT), F32),
        scratch_shapes=[pltpu.VMEM((PIPE, SEL_TILE, COLS), F32), pltpu.VMEM((PIPE, SEL_TILE, COLS), BF16),
                        pltpu.VMEM((PIPE, SUBLANES, COLS), F32),
                        pltpu.VMEM((SUBLANES, COLS), F32), pltpu.VMEM((V_ROWS, COLS), F32),
                        pltpu.VMEM((V_ROWS, COLS), F32), pltpu.VMEM((LANES, COLS), F32),
                        pltpu.VMEM((WINDOW + Q_BLOCK, COLS), F32), pltpu.VMEM((WINDOW + Q_BLOCK, COLS), BF16)],
        compiler_params=pltpu.CompilerParams(dimension_semantics=("arbitrary", "arbitrary", "arbitrary"),
                                             vmem_limit_bytes=VMEM_LIMIT_BYTES),
        name="attn",
    )(qpt, ksel, vst, kw, vwt, kcmp, vcmpt, c2st, gatest)


def _attn_kernel(q_ref, ksel_ref, vst_ref, kw_ref, vwt_ref, kcmp_ref, vcmpt_ref, c2st_ref, gate_ref, o_ref,
                 s_all, p_all, a_all, m_scr, acc_scr, acct_scr, cw_scr, sw_scr, pw_scr):
    i = pl.program_id(1)
    starts = [(Q_PER_STEP * i + d) * Q_BLOCK for d in range(Q_PER_STEP)]
    kv = lambda g: g % N_KV_HEADS
    qb = lambda g: g // N_KV_HEADS
    nch = kcmp_ref.shape[2]
    last_tile = ksel_ref.shape[2] // SEL_TILE - 1
    groups = range(Q_PER_STEP * N_KV_HEADS)

    lane = lax.broadcasted_iota(jnp.int32, (Q_BLOCK, LANES), 1)
    lo_half = lane < HEAD_DIM
    top_half = lax.broadcasted_iota(jnp.int32, (LANES, Q_BLOCK), 0) < HEAD_DIM
    zero_bf = jnp.zeros((LANES, Q_BLOCK), BF16)
    tqs = [st + (lax.broadcasted_iota(jnp.int32, (1, COLS), 1) & (Q_BLOCK - 1)) for st in starts]
    wk = WINDOW + Q_BLOCK
    bases = [pl.multiple_of(jnp.maximum(st - WINDOW, 0), Q_BLOCK) for st in starts]
    wrow = lax.broadcasted_iota(jnp.int32, (WIN_MASK_ROWS, COLS), 0)
    nrow = lax.broadcasted_iota(jnp.int32, (nch, COLS), 0)
    srow = lax.broadcasted_iota(jnp.int32, (N_SLC_LANES, Q_BLOCK), 0)
    tqqs = [st + lax.broadcasted_iota(jnp.int32, (N_SLC_LANES, Q_BLOCK), 1) for st in starts]
    c2st = c2st_ref[...]
    n_main = starts[0] // SEL_TILE

    qt = [jnp.concatenate([jnp.where(top_half if r % 2 == 0 else ~top_half,
                                     q_ref[0, qb(g), kv(g) * (Q_PER_KV // 2) + r // 2], zero_bf)
                           for r in range(Q_PER_KV)], axis=1) for g in groups]

    def grow(g, b):
        gt = gate_ref[0, qb(g), kv(g)]
        return jnp.concatenate([gt[r * N_BRANCH + b:r * N_BRANCH + b + 1, :] for r in range(Q_PER_KV)], axis=1)

    s_c = [_dot(kcmp_ref[0, kv(g)], qt[g]) for g in groups]
    for g in groups:
        sw_scr[g] = _dot(kw_ref[0, kv(g), pl.ds(bases[qb(g)], wk), :], qt[g])

    last_valid = [lax.shift_right_arithmetic(t - (CMP_BLOCK - 1), CMP_SHIFT) for t in tqs]
    curs = [lax.shift_right_logical(t, SLC_SHIFT) for t in tqqs]
    forced = [(srow == 0) | (srow == c) | (srow == c - 1) for c in curs]
    future = [srow * SLC_BLOCK > t for t in tqqs]
    impt = []
    for g in groups:
        sc = jnp.where(nrow <= last_valid[qb(g)], s_c[g], NEG)
        m_c = jnp.max(sc, axis=0, keepdims=True)
        e_c = jnp.exp2(sc - m_c)
        l_c = jnp.sum(e_c, axis=0, keepdims=True)
        p_c = e_c * jnp.where(m_c > 0.5 * NEG, 1.0 / l_c, 0.0)
        cw_scr[g] = grow(g, 0) * _dot(vcmpt_ref[0, kv(g)], p_c.astype(BF16))
        psum = sum(p_c[:, r * Q_BLOCK:(r + 1) * Q_BLOCK] for r in range(Q_PER_KV))
        imp = sum(_dot(c2st, t) for t in _split2(psum))
        impt.append(jnp.where(forced[qb(g)], 1e9, jnp.where(future[qb(g)], -1e9, imp)))

    def window_mask_piece(g, k, colmax, zero):
        rows = slice(WIN_MASK_ROWS * k, WIN_MASK_ROWS * (k + 1))
        back = (tqs[qb(g)] + zero - bases[qb(g)] - WIN_MASK_ROWS * k) - wrow
        in_window = lax.bitcast_convert_type(back, jnp.uint32) < jnp.uint32(WINDOW)
        sm = jnp.where(in_window, sw_scr[g, rows, :], NEG)
        sw_scr[g, rows, :] = sm
        for c in range(WIN_MASK_ROWS // SUBLANES):
            colmax = jnp.maximum(colmax, sm[c * SUBLANES:(c + 1) * SUBLANES])
        return colmax

    def window_exp_piece(g, k, m_w, zero):
        rows = slice(WIN_EXP_ROWS * k, WIN_EXP_ROWS * (k + 1))
        pw_scr[g, rows, :] = jnp.exp2(sw_scr[g, rows, :] - (m_w + zero.astype(F32))).astype(BF16)

    key_to_float = lambda k: lax.bitcast_convert_type(jnp.where(k >= 0, k, k ^ jnp.int32(0x7FFFFFFF)), F32)
    wrap32 = lambda v: ((v + 2 ** 31) % 2 ** 32) - 2 ** 31
    n_mask, n_exp = wk // WIN_MASK_ROWS, wk // WIN_EXP_ROWS
    thr = [jnp.full((1, Q_BLOCK), INT32_MIN, jnp.int32) for _ in groups]
    colmax = [jnp.full((SUBLANES, COLS), NEG, F32) for _ in groups]
    m_w = [None for _ in groups]
    piece = 0
    for b in range(30, -1, -2):
        for g in groups:
            reached = thr[g]
            for mult in (1, 2, 3):
                cand = thr[g] + jnp.int32(wrap32(mult << b))
                n_ge = jnp.sum(jnp.where(impt[g] >= key_to_float(cand), 1.0, 0.0), axis=0, keepdims=True)
                reached = jnp.where(n_ge >= N_SELECT, cand, reached)
            thr[g] = reached
        for _ in range(2):
            for g in groups:
                zero = jnp.concatenate([_zero_after(thr[g])] * Q_PER_KV, axis=1)
                if piece < n_mask:
                    colmax[g] = window_mask_piece(g, piece, colmax[g], zero)
                elif piece < n_mask + n_exp:
                    if piece == n_mask:
                        m_w[g] = jnp.max(colmax[g], axis=0, keepdims=True)
                    window_exp_piece(g, piece - n_mask, m_w[g], zero)
            piece += 1

    scol = lax.broadcasted_iota(jnp.int32, (N_SLC_LANES, N_SLC_LANES), 1)
    earlier = jnp.where(scol < srow, 1.0, 0.0).astype(BF16)
    widen = lambda x: jnp.concatenate([x.astype(BF16)] * Q_PER_KV, axis=1)
    qt_tail, qt_main = [], []
    for g in groups:
        vw = jnp.concatenate([vwt_ref[0, kv(g), bases[qb(g)] // LANES + c] for c in range(wk // LANES)],
                             axis=1)
        cw_scr[g] += grow(g, 2) * _normalize_dup(_dot(vw, pw_scr[g]))
        kth = key_to_float(thr[g])
        above = impt[g] > kth
        tied = impt[g] == kth
        n_above = jnp.sum(jnp.where(above, 1.0, 0.0), axis=0, keepdims=True)
        tied_before = _dot(earlier, jnp.where(tied, 1.0, 0.0).astype(BF16))
        selected = above | (tied & (tied_before < N_SELECT - n_above))
        bias = jnp.where(selected, 0.0, NEG)
        bias_main = jnp.where(srow >= n_main * (SEL_TILE // SLC_BLOCK), NEG, bias)
        qt_tail.append(jnp.concatenate([qt[g], widen(bias)], axis=0))
        qt_main.append(jnp.concatenate([qt[g], widen(bias_main)], axis=0))

    kt = pl.multiple_of(n_main * SEL_TILE, SEL_TILE)
    krow = lax.broadcasted_iota(jnp.int32, (SEL_TILE, COLS), 0)
    m_t = []
    for g in groups:
        s_t = _dot(ksel_ref[0, kv(g), pl.ds(kt, SEL_TILE), :], qt_tail[g])
        s_t = jnp.where(kt + krow <= tqs[qb(g)], s_t, NEG)
        m_t.append(jnp.max(s_t, axis=0, keepdims=True))
        acct_scr[g] = _dot(vst_ref[0, kv(g), n_main], jnp.exp2(s_t - m_t[g]).astype(BF16))
    g_sel = [grow(g, 1) for g in groups]

    def scores(g, j, slot):
        k0 = pl.multiple_of(jnp.minimum(j, last_tile) * SEL_TILE, SEL_TILE)
        s_all[g, slot] = _dot(ksel_ref[0, kv(g), pl.ds(k0, SEL_TILE), :], qt_main[g])

    def softmax(g, slot):
        s = s_all[g, slot]
        m_prev = m_scr[g, 0:1, :]
        m_new = jnp.maximum(m_prev, jnp.max(s, axis=0, keepdims=True))
        a_all[g, slot] = jnp.broadcast_to(jnp.exp2(m_prev - m_new), (SUBLANES, COLS))
        p_all[g, slot] = jnp.exp2(s - m_new).astype(BF16)
        m_scr[g] = jnp.broadcast_to(m_new, (SUBLANES, COLS))

    def values(g, j, slot):
        acc_scr[g] = (a_all[g, slot, 0:1, :] * acc_scr[g]
                      + _dot(vst_ref[0, kv(g), jnp.minimum(j, last_tile)], p_all[g, slot]))

    m_scr[...] = jnp.full(m_scr.shape, NEG, F32)
    acc_scr[...] = jnp.zeros(acc_scr.shape, F32)
    for k in range(PIPE):
        for g in groups:
            scores(g, k, k)
    for k in range(PIPE // 2):
        for g in groups:
            softmax(g, k)

    def sweep(t, carry):
        for k in range(SWEEP_UNROLL):
            for g in groups:
                values(g, SWEEP_UNROLL * t + k, k % PIPE)
                softmax(g, (k + PIPE // 2) % PIPE)
                scores(g, SWEEP_UNROLL * t + k + PIPE, k % PIPE)
        return carry

    lax.fori_loop(0, (n_main + SWEEP_UNROLL - 1) // SWEEP_UNROLL, sweep, 0)

    pairs = [[] for _ in range(Q_PER_STEP)]
    for g in groups:
        m_p = m_scr[g, 0:1, :]
        m_tot = jnp.maximum(m_p, m_t[g])
        o_s = _normalize_dup(jnp.exp2(m_p - m_tot) * acc_scr[g] + jnp.exp2(m_t[g] - m_tot) * acct_scr[g])
        comb = cw_scr[g] + g_sel[g] * o_s
        outs = [comb[:, r * Q_BLOCK:(r + 1) * Q_BLOCK].T for r in range(Q_PER_KV)]
        pairs[qb(g)] += [jnp.where(lo_half, outs[0], outs[1]), jnp.where(lo_half, outs[2], outs[3])]
    for d in range(Q_PER_STEP):
        o_ref[0, d * Q_BLOCK:(d + 1) * Q_BLOCK, :] = jnp.concatenate(pairs[d], axis=1)


def _attn(qpt, ksel, vst, kw, vwt, kcmp, vcmpt, c2st, gatest):
    batch, seq = qpt.shape[0], qpt.shape[1] * Q_BLOCK
    per_batch = lambda a: pl.BlockSpec((1,) + a.shape[1:], lambda b, i: (b,) + (0,) * (a.ndim - 1),
                                       pipeline_mode=pl.Buffered(1))
    per_block = lambda a: pl.BlockSpec((1, Q_PER_STEP) + a.shape[2:], lambda b, i: (b, i) + (0,) * (a.ndim - 2))
    chains = Q_PER_STEP * N_KV_HEADS
    grp = lambda *shape: pltpu.VMEM((chains,) + shape, F32)
    return pl.pallas_call(
        _attn_kernel,
        grid=(batch, seq // (Q_PER_STEP * Q_BLOCK)),
        in_specs=[per_block(qpt), per_batch(ksel), per_batch(vst), per_batch(kw), per_batch(vwt), per_batch(kcmp),
                  per_batch(vcmpt), pl.BlockSpec(c2st.shape, lambda b, i: (0, 0)), per_block(gatest)],
        out_specs=pl.BlockSpec((1, Q_PER_STEP * Q_BLOCK, D_ATT), lambda b, i: (b, i, 0)),
        out_shape=jax.ShapeDtypeStruct((batch, seq, D_ATT), F32),
        scratch_shapes=[grp(PIPE, SEL_TILE, COLS), pltpu.VMEM((chains, PIPE, SEL_TILE, COLS), BF16),
                        grp(PIPE, SUBLANES, COLS), grp(SUBLANES, COLS), grp(V_ROWS, COLS), grp(V_ROWS, COLS),
                        grp(LANES, COLS), grp(WINDOW + Q_BLOCK, COLS),
                        pltpu.VMEM((chains, WINDOW + Q_BLOCK, COLS), BF16)],
        compiler_params=pltpu.CompilerParams(dimension_semantics=("arbitrary", "arbitrary"),
                                             vmem_limit_bytes=VMEM_LIMIT_BYTES),
        name="attn",
    )(qpt, ksel, vst, kw, vwt, kcmp, vcmpt, c2st, gatest)


POST_SUB = 2


def _post_kernel(x_ref, oatt_ref, mconv_ref, p_ref, onag_ref, wo_ref, lnf_ref, wup_ref, fcw_ref, fcb_ref,
                 wdn_ref, lnp_ref, wpg_ref, wpe_ref, out_ref, gbuf, *, tiles_per_seq):
    tm = x_ref.shape[0]
    sub = tm // POST_SUB
    it = pl.program_id(0) % tiles_per_seq
    subs = [slice(k * sub, (k + 1) * sub) for k in range(POST_SUB)]

    @pl.when(it == 0)
    def _():
        gbuf[0:SUBLANES, :] = jnp.zeros((SUBLANES, D_FF), F32)

    h1 = [x_ref[r, :] + _dot(jnp.concatenate([_rms(oatt_ref[r, :], onag_ref[...]).astype(BF16), mconv_ref[r, :]],
                                             axis=1), wo_ref[...]) for r in subs]
    xn = [_rms(h, lnf_ref[...]).astype(BF16) for h in h1]
    h2 = []
    for k, r in enumerate(subs):
        gpre = _dot(xn[k], wup_ref[:, :D_FF])
        up = _dot(xn[k], wup_ref[:, D_FF:])
        lo = SUBLANES + k * sub
        gbuf[lo:lo + sub, :] = gpre
        gate = (fcw_ref[2:3, :] * gpre + fcw_ref[1:2, :] * gbuf[lo - 1:lo - 1 + sub, :]
                + fcw_ref[0:1, :] * gbuf[lo - 2:lo - 2 + sub, :]) + fcb_ref[...]
        act = (gate * jax.nn.sigmoid(gate) * up).astype(BF16)
        h2.append(h1[k] + _dot(act, wdn_ref[...]))
    gbuf[0:SUBLANES, :] = gbuf[tm:tm + SUBLANES, :]
    for k, r in enumerate(subs):
        xn2 = _rms(h2[k], lnp_ref[...]).astype(BF16)
        out_ref[r, :] = (h2[k] + jax.nn.sigmoid(_dot(xn2, wpg_ref[...]))
                         * _dot(p_ref[r, :].astype(BF16), wpe_ref[...]))


def _post(x2, oatt, mconv, p2, onag, wo, lnf, wup, fcw, fcb, wdn, lnp, wpg, wpe, *, seq, tm):
    n = x2.shape[0]
    tps = seq // tm
    row = lambda w: pl.BlockSpec((tm, w), lambda i: (i, 0))
    full = lambda a: pl.BlockSpec(a.shape, lambda i: (0,) * a.ndim, pipeline_mode=pl.Buffered(1))
    return pl.pallas_call(
        functools.partial(_post_kernel, tiles_per_seq=tps),
        grid=(n // tm,),
        in_specs=[row(D_MODEL), row(D_ATT), row(D_CONV), row(D_PLE), full(onag), full(wo), full(lnf), full(wup),
                  full(fcw), full(fcb), full(wdn), full(lnp), full(wpg), full(wpe)],
        out_specs=row(D_MODEL),
        out_shape=jax.ShapeDtypeStruct((n, D_MODEL), F32),
        scratch_shapes=[pltpu.VMEM((tm + SUBLANES, D_FF), F32)],
        compiler_params=pltpu.CompilerParams(dimension_semantics=("arbitrary",),
                                             vmem_limit_bytes=VMEM_LIMIT_BYTES),
        name="post",
    )(x2, oatt, mconv, p2, onag, wo, lnf, wup, fcw, fcb, wdn, lnp, wpg, wpe)


def _rope_tables(pos):
    half = ROT_DIM // 2
    d = np.arange(LANES) % HEAD_DIM
    inv_freq = np.float64(ROPE_THETA) ** (-np.arange(half, dtype=np.float64) * 2.0 / ROT_DIM)
    ang = np.asarray(pos, np.float64)[:, None] * np.tile(inv_freq, LANES // half)[None, :]
    c, sn = np.cos(ang), np.sin(ang)
    first, second = (d < half)[None, :], ((d >= half) & (d < ROT_DIM))[None, :]
    cos = np.where(first | second, c, 1.0)
    sa = np.where(first, -sn, 0.0)
    sb = np.where(second, sn, 0.0)
    return tuple(jnp.asarray(t, F32) for t in (cos, sa, sb))


def _block_diag_mean(width):
    idx = np.arange(width) // HEAD_DIM
    return jnp.asarray((idx[:, None] == idx[None, :]).astype(np.float32) / HEAD_DIM, BF16)


def _cmp_to_slc_t(nch):
    cs = CMP_STRIDE * np.arange(nch)[None, :]
    ss = SLC_BLOCK * np.arange(N_SLC_LANES)[:, None]
    ov = np.clip(np.minimum(cs + CMP_BLOCK, ss + SLC_BLOCK) - np.maximum(cs, ss), 0, None)
    return jnp.asarray(ov.astype(np.float32) / CMP_BLOCK, BF16)


def _layer(h, p_l, ln_mix_g, w_in, qn_g, kn_g, pe_k, pe_v, w_ck1, w_ck2, w_cv1, w_cv2, conv_w, on_att_g,
           on_conv_g, w_o, ln_ffn_g, w_up, ffn_conv_w, ffn_conv_b, w_down, ln_ple_g, w_pg, w_pe):
    batch, seq, _ = h.shape
    assert seq % SEL_TILE == 0 and seq // SLC_BLOCK <= N_SLC_LANES and seq // SLC_BLOCK >= N_SELECT
    assert seq >= WINDOW + Q_BLOCK
    n = batch * seq
    nch = seq // CMP_STRIDE
    x2 = h.reshape(n, D_MODEL)
    row = lambda v: v.reshape(1, -1).astype(F32)

    o_q, o_kv, o_g, o_cv = 0, D_ATT, D_ATT + 6 * KV_W, D_ATT + 6 * KV_W + N_BRANCH * N_Q_HEADS
    wq = w_in[:, o_q:o_kv].astype(BF16)
    wkv = w_in[:, o_kv:o_g].astype(BF16)
    wcv = w_in[:, o_cv:].astype(BF16)
    per_g = Q_PER_KV * N_BRANCH
    wg = jnp.concatenate(
        [jnp.pad(w_in[:, o_g + g * per_g:o_g + (g + 1) * per_g], ((0, 0), (0, LANES - per_g)))
         for g in range(N_KV_HEADS)], axis=1).astype(BF16)
    cos, sa, sb = _rope_tables(np.arange(seq))
    bd512, bd128 = _block_diag_mean(D_ATT), _block_diag_mean(KV_W)
    tile_heads = lambda v, k: jnp.tile(v.astype(F32), k).reshape(1, -1)

    tm = ROW_TILE
    assert seq % ROW_TILE == 0 and seq % POST_TILE == 0
    qpt, kc2, vc2, ksel, vst, kw, vwt, gatest, mconv = _inproj(
        x2, row(ln_mix_g), wq, wkv, wg, wcv, tile_heads(qn_g, N_Q_HEADS),
        jnp.stack([jnp.tile(kn_g[1], N_KV_HEADS), jnp.tile(kn_g[2], N_KV_HEADS)]).astype(F32),
        bd512, bd128, cos, sa, sb, conv_w.astype(F32), row(on_conv_g), batch=batch, seq=seq, tm=tm)

    assert N_KV_HEADS == 2
    half = CMP_BLOCK // 2

    def w1_parts(w1):
        w = w1.astype(BF16)
        z = jnp.zeros_like(w)
        wfull = jnp.concatenate([jnp.concatenate([w, z], axis=2), jnp.concatenate([z, w], axis=2)], axis=1)
        return wfull[:half].reshape(half * KV_W, -1), wfull[half:].reshape(half * KV_W, -1)

    def pe_parts(pe):
        pf = jnp.broadcast_to(pe[:, None, :], (CMP_BLOCK, N_KV_HEADS, HEAD_DIM)).astype(F32)
        return pf[:half].reshape(1, -1), pf[half:].reshape(1, -1)

    def w2bd(w2):
        w = w2.astype(BF16)
        z = jnp.zeros_like(w)
        return jnp.concatenate([jnp.concatenate([w, z], axis=1), jnp.concatenate([z, w], axis=1)], axis=0)
    w1ak, w1bk = w1_parts(w_ck1)
    w1av, w1bv = w1_parts(w_cv1)
    peak, pebk = pe_parts(pe_k)
    peav, pebv = pe_parts(pe_v)
    ccos, csa, csb = _rope_tables(CMP_STRIDE * np.arange(nch) + CMP_BLOCK - 1)
    kcmp, vcmpt = _compress(kc2.reshape(batch, seq, KV_W), vc2.reshape(batch, seq, KV_W),
                            w1ak, w1bk, w1av, w1bv, peak, pebk, peav, pebv, w2bd(w_ck2), w2bd(w_cv2),
                            tile_heads(kn_g[0], N_KV_HEADS), bd128, ccos, csa, csb)

    oatt = _attn(qpt, ksel, vst, kw, vwt, kcmp, vcmpt, _cmp_to_slc_t(nch), gatest)

    out = _post(x2, oatt.reshape(n, D_ATT), mconv, p_l.reshape(n, D_PLE), row(on_att_g), w_o.astype(BF16),
                row(ln_ffn_g), w_up.astype(BF16), ffn_conv_w.astype(F32), row(ffn_conv_b), w_down.astype(BF16),
                row(ln_ple_g), w_pg.astype(BF16), w_pe.astype(BF16), seq=seq, tm=POST_TILE)
    return out.reshape(batch, seq, D_MODEL)


def kernel(x, p, ln_mix_g, w_in, qn_g, kn_g, pe_k, pe_v, w_ck1, w_ck2, w_cv1, w_cv2, conv_w, on_att_g,
           on_conv_g, w_o, ln_ffn_g, w_up, ffn_conv_w, ffn_conv_b, w_down, ln_ple_g, w_pg, w_pe):
    h = x
    for i in range(p.shape[0]):
        h = _layer(h, p[i], ln_mix_g[i], w_in[i], qn_g[i], kn_g[i], pe_k[i], pe_v[i], w_ck1[i], w_ck2[i],
                   w_cv1[i], w_cv2[i], conv_w[i], on_att_g[i], on_conv_g[i], w_o[i], ln_ffn_g[i], w_up[i],
                   ffn_conv_w[i], ffn_conv_b[i], w_down[i], ln_ple_g[i], w_pg[i], w_pe[i])
    return h
```

```python
import functools

import jax
import jax.numpy as jnp
import numpy as np
from jax import lax
from jax.experimental import pallas as pl
from jax.experimental.pallas import tpu as pltpu

D_MODEL = 1024
HEAD_DIM = 64
N_Q_HEADS = 8
N_KV_HEADS = 2
Q_PER_KV = N_Q_HEADS // N_KV_HEADS
D_ATT = N_Q_HEADS * HEAD_DIM
D_CONV = D_MODEL - D_ATT
KV_W = N_KV_HEADS * HEAD_DIM
N_BRANCH = 3
CONV_TAPS = 3
ROT_DIM = HEAD_DIM // 4
ROPE_THETA = 500000.0
CMP_BLOCK = 32
CMP_STRIDE = 16
CMP_HIDDEN = 256
SLC_BLOCK = 64
N_SELECT = 16
WINDOW = 512
Q_BLOCK = 128
D_FF = 2816
D_PLE = 256
EPS = 1e-6
NEG = -1e30
INT32_MIN = -2 ** 31
LOG2E = float(np.log2(np.e))
V_ROWS = HEAD_DIM + 16
GATE_ROWS = 16
SLC_SHIFT = SLC_BLOCK.bit_length() - 1
CMP_SHIFT = CMP_STRIDE.bit_length() - 1
assert 1 << SLC_SHIFT == SLC_BLOCK and 1 << CMP_SHIFT == CMP_STRIDE

LANES = 128
SUBLANES = 8
N_SLC_LANES = LANES
VMEM_LIMIT_BYTES = 56 * 1024 * 1024

F32 = jnp.float32
BF16 = jnp.bfloat16


def _dot(a, b):
    return jnp.dot(a, b, preferred_element_type=F32)


def _rms(x, g):
    return x * lax.rsqrt(jnp.mean(x * x, axis=-1, keepdims=True) + EPS) * g


def _head_rms_rope(x, g, bd, cos, sa, sb):
    w = x.shape[-1]
    msq = _dot((x * x).astype(BF16), bd)
    xn = x * lax.rsqrt(msq + EPS) * g
    return xn * cos + pltpu.roll(xn, w - ROT_DIM // 2, 1) * sa + pltpu.roll(xn, ROT_DIM // 2, 1) * sb


def _dup_halves(x):
    r = pltpu.roll(x, HEAD_DIM, 1)
    lane = lax.broadcasted_iota(jnp.int32, x.shape, 1)
    lo = lane < HEAD_DIM
    return jnp.where(lo, x, r), jnp.where(lo, r, x)


def _inproj_kernel(x_ref, lng_ref, wq_ref, wkv_ref, wg_ref, wcv_ref, qng_ref, kng_ref, bd512_ref, bd128_ref,
                   cos_ref, sa_ref, sb_ref, convw_ref, oncg_ref,
                   q_out, kc_out, vc_out, ksel_out, vs_out, kw_out, vw_out, gate_out, mconv_out,
                   zbuf, *, tiles_per_seq):
    tm = x_ref.shape[0]
    it = pl.program_id(0) % tiles_per_seq
    x = x_ref[...]
    xn = _rms(x, lng_ref[...]).astype(BF16)

    cos, sa, sb = cos_ref[...], sa_ref[...], sb_ref[...]
    cos4, sa4, sb4 = (jnp.concatenate([t] * 4, axis=1) for t in (cos, sa, sb))
    q = _dot(xn, wq_ref[...])
    qr = _head_rms_rope(q, qng_ref[...], bd512_ref[...], cos4, sa4, sb4)
    qs = qr * (HEAD_DIM ** -0.5 * LOG2E)
    for blk_i in range(tm // Q_BLOCK):
        rows = slice(blk_i * Q_BLOCK, (blk_i + 1) * Q_BLOCK)
        for pr in range(N_Q_HEADS // 2):
            q_out[0, blk_i, pr] = qs[rows, pr * LANES:(pr + 1) * LANES].T.astype(BF16)

    kv = _dot(xn, wkv_ref[...])
    kc_out[...] = kv[:, 0 * KV_W:1 * KV_W]
    vc_out[...] = kv[:, 1 * KV_W:2 * KV_W]
    ks = _head_rms_rope(kv[:, 2 * KV_W:3 * KV_W], kng_ref[0:1, :], bd128_ref[...], cos, sa, sb)
    kw = _head_rms_rope(kv[:, 4 * KV_W:5 * KV_W], kng_ref[1:2, :], bd128_ref[...], cos, sa, sb)
    vs = kv[:, 3 * KV_W:4 * KV_W]
    vw = kv[:, 5 * KV_W:6 * KV_W]
    tpos = it * tm + lax.broadcasted_iota(jnp.int32, (tm, N_SLC_LANES), 0)
    blk = lax.broadcasted_iota(jnp.int32, (tm, N_SLC_LANES), 1)
    onehot = jnp.where(lax.shift_right_logical(tpos, SLC_SHIFT) == blk, 1.0, 0.0).astype(BF16)
    ks_d, vs_d, kw_d, vw_d = (_dup_halves(t) for t in (ks, vs, kw, vw))
    vrow = lax.broadcasted_iota(jnp.int32, (V_ROWS, tm), 0)
    for g in range(N_KV_HEADS):
        ksel_out[0, g] = jnp.concatenate([ks_d[g].astype(BF16), onehot], axis=1)
        kw_out[0, g] = kw_d[g].astype(BF16)
        vst = jnp.where(vrow < HEAD_DIM, vs_d[g].T[:V_ROWS], 1.0).astype(BF16)
        for c in range(tm // SEL_TILE):
            vs_out[0, g, c] = vst[:, c * SEL_TILE:(c + 1) * SEL_TILE]
        vwt = jnp.where(vrow < HEAD_DIM, vw_d[g].T[:V_ROWS], 1.0).astype(BF16)
        for c in range(tm // LANES):
            vw_out[0, g, c] = vwt[:, c * LANES:(c + 1) * LANES]

    gates = jax.nn.sigmoid(_dot(xn, wg_ref[...]))
    for blk_i in range(tm // Q_BLOCK):
        rows = slice(blk_i * Q_BLOCK, (blk_i + 1) * Q_BLOCK)
        for g in range(N_KV_HEADS):
            gate_out[0, blk_i, g] = gates[rows, g * LANES:(g + 1) * LANES].T[:GATE_ROWS]

    cv = _dot(xn, wcv_ref[...])
    cb, cc, cx = cv[:, :D_CONV], cv[:, D_CONV:2 * D_CONV], cv[:, 2 * D_CONV:]
    z = cc * cx

    @pl.when(it == 0)
    def _():
        zbuf[0:SUBLANES, :] = jnp.zeros((SUBLANES, D_CONV), F32)

    zbuf[SUBLANES:SUBLANES + tm, :] = z
    y = (convw_ref[2:3, :] * z + convw_ref[1:2, :] * zbuf[SUBLANES - 1:SUBLANES - 1 + tm, :]
         + convw_ref[0:1, :] * zbuf[SUBLANES - 2:SUBLANES - 2 + tm, :])
    zbuf[0:SUBLANES, :] = zbuf[tm:tm + SUBLANES, :]
    mconv_out[...] = _rms(cb * y, oncg_ref[...]).astype(BF16)


def _inproj(x2, lng, wq, wkv, wg, wcv, qng, kng, bd512, bd128, cos, sa, sb, convw, oncg, *, batch, seq, tm):
    n = batch * seq
    tps = seq // tm
    row = lambda w: pl.BlockSpec((tm, w), lambda i: (i, 0))
    full = lambda a: pl.BlockSpec(a.shape, lambda i: (0,) * a.ndim)
    tab = pl.BlockSpec((tm, LANES), lambda i: (i % tps, 0))
    grp = lambda w: pl.BlockSpec((1, N_KV_HEADS, tm, w), lambda i: (i // tps, 0, i % tps, 0))
    gshape = lambda w: jax.ShapeDtypeStruct((batch, N_KV_HEADS, seq, w), BF16)
    assert tm % SEL_TILE == 0
    vtile = lambda keys: pl.BlockSpec((1, N_KV_HEADS, tm // keys, V_ROWS, keys),
                                      lambda i: (i // tps, 0, i % tps, 0, 0))
    vshape = lambda keys: jax.ShapeDtypeStruct((batch, N_KV_HEADS, seq // keys, V_ROWS, keys), BF16)
    qblk = lambda tile: pl.BlockSpec((1, tm // Q_BLOCK) + tile, lambda i: (i // tps, i % tps) + (0,) * len(tile))
    return pl.pallas_call(
        functools.partial(_inproj_kernel, tiles_per_seq=tps),
        grid=(n // tm,),
        in_specs=[row(D_MODEL), full(lng), full(wq), full(wkv), full(wg), full(wcv), full(qng), full(kng),
                  full(bd512), full(bd128), tab, tab, tab, full(convw), full(oncg)],
        out_specs=[qblk((N_Q_HEADS // 2, LANES, Q_BLOCK)), row(KV_W), row(KV_W), grp(2 * LANES), vtile(SEL_TILE),
                   grp(LANES), vtile(LANES), qblk((N_KV_HEADS, GATE_ROWS, Q_BLOCK)), row(D_CONV)],
        out_shape=[jax.ShapeDtypeStruct((batch, seq // Q_BLOCK, N_Q_HEADS // 2, LANES, Q_BLOCK), BF16),
                   jax.ShapeDtypeStruct((n, KV_W), F32),
                   jax.ShapeDtypeStruct((n, KV_W), F32), gshape(2 * LANES), vshape(SEL_TILE), gshape(LANES),
                   vshape(LANES),
                   jax.ShapeDtypeStruct((batch, seq // Q_BLOCK, N_KV_HEADS, GATE_ROWS, Q_BLOCK), F32),
                   jax.ShapeDtypeStruct((n, D_CONV), BF16)],
        scratch_shapes=[pltpu.VMEM((tm + SUBLANES, D_CONV), F32)],
        compiler_params=pltpu.CompilerParams(dimension_semantics=("arbitrary",),
                                             vmem_limit_bytes=VMEM_LIMIT_BYTES),
        name="inproj",
    )(x2, lng, wq, wkv, wg, wcv, qng, kng, bd512, bd128, cos, sa, sb, convw, oncg)


def _gelu_tanh(x):
    return 0.5 * x * (1.0 + jnp.tanh(np.sqrt(2.0 / np.pi).astype(np.float32) * (x + 0.044715 * (x * x * x))))


def _compress_kernel(kc_ref, vc_ref, w1ak_ref, w1bk_ref, w1av_ref, w1bv_ref, peak_ref, pebk_ref, peav_ref,
                     pebv_ref, w2k_ref, w2v_ref, kng_ref, bd128_ref, cos_ref, sa_ref, sb_ref,
                     kcmp_out, vcmp_out):
    nch = kc_ref.shape[1] // CMP_STRIDE

    def chunks(ref):
        return jnp.concatenate([ref[0, pl.ds(l, nch, stride=CMP_STRIDE), :] for l in range(CMP_STRIDE)], axis=1)

    def compress(x, w1a, w1b, pea, peb, w2):
        a = _dot((x + pea).astype(BF16), w1a)
        b = _dot((x + peb).astype(BF16), w1b)
        hid = _gelu_tanh(a + pltpu.roll(b, nch - 1, 0))
        return _dot(hid.astype(BF16), w2)

    kc = compress(chunks(kc_ref), w1ak_ref[...], w1bk_ref[...], peak_ref[...], pebk_ref[...], w2k_ref[...])
    vc = compress(chunks(vc_ref), w1av_ref[...], w1bv_ref[...], peav_ref[...], pebv_ref[...], w2v_ref[...])
    kc = _head_rms_rope(kc, kng_ref[...], bd128_ref[...], cos_ref[...], sa_ref[...], sb_ref[...])
    kd, vd = _dup_halves(kc), _dup_halves(vc)
    for g in range(N_KV_HEADS):
        kcmp_out[0, g] = kd[g].astype(BF16)
        vcmp_out[0, g] = vd[g].T.astype(BF16)


def _compress(kc3, vc3, w1ak, w1bk, w1av, w1bv, peak, pebk, peav, pebv, w2k, w2v, kng0, bd128, cos, sa, sb):
    batch, seq, wide = kc3.shape
    nch = seq // CMP_STRIDE
    full = lambda a: pl.BlockSpec(a.shape, lambda b: (0,) * a.ndim)
    tok = pl.BlockSpec((1, seq, wide), lambda b: (b, 0, 0))
    kout = pl.BlockSpec((1, N_KV_HEADS, nch, LANES), lambda b: (b, 0, 0, 0))
    vout = pl.BlockSpec((1, N_KV_HEADS, LANES, nch), lambda b: (b, 0, 0, 0))
    consts = (w1ak, w1bk, w1av, w1bv, peak, pebk, peav, pebv, w2k, w2v, kng0, bd128, cos, sa, sb)
    return pl.pallas_call(
        _compress_kernel,
        grid=(batch,),
        in_specs=[tok, tok] + [full(a) for a in consts],
        out_specs=[kout, vout],
        out_shape=[jax.ShapeDtypeStruct((batch, N_KV_HEADS, nch, LANES), BF16),
                   jax.ShapeDtypeStruct((batch, N_KV_HEADS, LANES, nch), BF16)],
        compiler_params=pltpu.CompilerParams(dimension_semantics=("arbitrary",),
                                             vmem_limit_bytes=VMEM_LIMIT_BYTES),
        name="compress",
    )(kc3, vc3, *consts)


SEL_TILE = 256
COLS = Q_PER_KV * Q_BLOCK
ROW_TILE = 512
POST_TILE = 512
RADIX_BITS = 1
Q_PER_STEP = 2
PIPE = 2
SWEEP_UNROLL = 2
assert SWEEP_UNROLL % PIPE == 0 and SEL_TILE % (Q_PER_STEP * Q_BLOCK) == 0
WIN_MASK_ROWS = 32
WIN_EXP_ROWS = 64
assert (WINDOW + Q_BLOCK) // WIN_MASK_ROWS + (WINDOW + Q_BLOCK) // WIN_EXP_ROWS <= 32


def _split2(x):
    hi = x.astype(BF16)
    return hi, (x - hi.astype(F32)).astype(BF16)


def _zero_after(x):
    return lax.shift_right_logical(lax.shift_right_logical(x, 16), 16)


def _normalize_dup(acc):
    o = acc[:HEAD_DIM] / acc[HEAD_DIM:HEAD_DIM + 1]
    return jnp.concatenate([o, o], axis=0)


def _attn_kernel_single(q_ref, ksel_ref, vst_ref, kw_ref, vwt_ref, kcmp_ref, vcmpt_ref, c2st_ref, gate_ref, o_ref,
                        s_all, p_all, a_all, m_scr, acc_scr, acct_scr, cw_scr, sw_scr, pw_scr):
    i = pl.program_id(2)
    start = i * Q_BLOCK
    nch = kcmp_ref.shape[2]
    last_tile = ksel_ref.shape[2] // SEL_TILE - 1

    lane = lax.broadcasted_iota(jnp.int32, (Q_BLOCK, LANES), 1)
    lo_half = lane < HEAD_DIM
    top_half = lax.broadcasted_iota(jnp.int32, (LANES, Q_BLOCK), 0) < HEAD_DIM
    zero_bf = jnp.zeros((LANES, Q_BLOCK), BF16)
    qt = jnp.concatenate([jnp.where(top_half if r % 2 == 0 else ~top_half, q_ref[0, 0, r // 2], zero_bf)
                          for r in range(Q_PER_KV)], axis=1)
    tq = start + (lax.broadcasted_iota(jnp.int32, (1, COLS), 1) & (Q_BLOCK - 1))
    gt = gate_ref[0, 0, 0]
    grow = lambda b: jnp.concatenate([gt[r * N_BRANCH + b:r * N_BRANCH + b + 1, :] for r in range(Q_PER_KV)],
                                     axis=1)

    wk = WINDOW + Q_BLOCK
    base = pl.multiple_of(jnp.maximum(start - WINDOW, 0), Q_BLOCK)

    wrow = lax.broadcasted_iota(jnp.int32, (WIN_MASK_ROWS, COLS), 0)

    def window_mask_piece(k, colmax, zero):
        rows = slice(WIN_MASK_ROWS * k, WIN_MASK_ROWS * (k + 1))
        back = (tq + zero - base - WIN_MASK_ROWS * k) - wrow
        in_window = lax.bitcast_convert_type(back, jnp.uint32) < jnp.uint32(WINDOW)
        sm = jnp.where(in_window, sw_scr[rows, :], NEG)
        sw_scr[rows, :] = sm
        for c in range(WIN_MASK_ROWS // SUBLANES):
            colmax = jnp.maximum(colmax, sm[c * SUBLANES:(c + 1) * SUBLANES])
        return colmax

    def window_exp_piece(k, m_w, zero):
        rows = slice(WIN_EXP_ROWS * k, WIN_EXP_ROWS * (k + 1))
        pw_scr[rows, :] = jnp.exp2(sw_scr[rows, :] - (m_w + zero.astype(F32))).astype(BF16)

    s_c = _dot(kcmp_ref[0, 0], qt)
    sw_scr[...] = _dot(kw_ref[0, 0, pl.ds(base, wk), :], qt)
    nrow = lax.broadcasted_iota(jnp.int32, (nch, COLS), 0)
    last_valid = lax.shift_right_arithmetic(tq - (CMP_BLOCK - 1), CMP_SHIFT)
    s_c = jnp.where(nrow <= last_valid, s_c, NEG)
    m_c = jnp.max(s_c, axis=0, keepdims=True)
    e_c = jnp.exp2(s_c - m_c)
    l_c = jnp.sum(e_c, axis=0, keepdims=True)
    p_c = e_c * jnp.where(m_c > 0.5 * NEG, 1.0 / l_c, 0.0)
    cw_scr[...] = grow(0) * _dot(vcmpt_ref[0, 0], p_c.astype(BF16))

    psum = sum(p_c[:, r * Q_BLOCK:(r + 1) * Q_BLOCK] for r in range(Q_PER_KV))
    c2st = c2st_ref[...]
    impt = sum(_dot(c2st, t) for t in _split2(psum))
    srow = lax.broadcasted_iota(jnp.int32, (N_SLC_LANES, Q_BLOCK), 0)
    tqq = start + lax.broadcasted_iota(jnp.int32, (N_SLC_LANES, Q_BLOCK), 1)
    cur = lax.shift_right_logical(tqq, SLC_SHIFT)
    forced = (srow == 0) | (srow == cur) | (srow == cur - 1)
    future = srow * SLC_BLOCK > tqq
    impt = jnp.where(forced, 1e9, jnp.where(future, -1e9, impt))
    key_to_float = lambda k: lax.bitcast_convert_type(jnp.where(k >= 0, k, k ^ jnp.int32(0x7FFFFFFF)), F32)
    thr = jnp.full((1, Q_BLOCK), INT32_MIN, jnp.int32)
    n_mask, n_exp = wk // WIN_MASK_ROWS, wk // WIN_EXP_ROWS
    colmax = jnp.full((SUBLANES, COLS), NEG, F32)
    wrap32 = lambda v: ((v + 2 ** 31) % 2 ** 32) - 2 ** 31
    piece = 0
    for b in range(30, -1, -2):
        reached = None
        for mult in (1, 2, 3):
            cand = thr + jnp.int32(wrap32(mult << b))
            n_ge = jnp.sum(jnp.where(impt >= key_to_float(cand), 1.0, 0.0), axis=0, keepdims=True)
            reached = jnp.where(n_ge >= N_SELECT, cand, thr if reached is None else reached)
        thr = reached
        zero = jnp.concatenate([_zero_after(thr)] * Q_PER_KV, axis=1)
        for _ in range(2):
            if piece < n_mask:
                colmax = window_mask_piece(piece, colmax, zero)
            elif piece < n_mask + n_exp:
                if piece == n_mask:
                    m_w = jnp.max(colmax, axis=0, keepdims=True)
                window_exp_piece(piece - n_mask, m_w, zero)
            piece += 1
    wtile = base // LANES
    vw = jnp.concatenate([vwt_ref[0, 0, wtile + c] for c in range(wk // LANES)], axis=1)
    cw_scr[...] += grow(2) * _normalize_dup(_dot(vw, pw_scr[...]))
    kth = key_to_float(thr)
    above = impt > kth
    tied = impt == kth
    n_above = jnp.sum(jnp.where(above, 1.0, 0.0), axis=0, keepdims=True)
    scol = lax.broadcasted_iota(jnp.int32, (N_SLC_LANES, N_SLC_LANES), 1)
    earlier = jnp.where(scol < srow, 1.0, 0.0).astype(BF16)
    tied_before = _dot(earlier, jnp.where(tied, 1.0, 0.0).astype(BF16))
    selected = above | (tied & (tied_before < N_SELECT - n_above))
    bias = jnp.where(selected, 0.0, NEG)
    n_main = start // SEL_TILE
    bias_main = jnp.where(srow >= n_main * (SEL_TILE // SLC_BLOCK), NEG, bias)
    widen = lambda b: jnp.concatenate([b.astype(BF16)] * Q_PER_KV, axis=1)
    qt_tail = jnp.concatenate([qt, widen(bias)], axis=0)
    qt_main = jnp.concatenate([qt, widen(bias_main)], axis=0)

    kt = pl.multiple_of(n_main * SEL_TILE, SEL_TILE)
    s_t = _dot(ksel_ref[0, 0, pl.ds(kt, SEL_TILE), :], qt_tail)
    krow = lax.broadcasted_iota(jnp.int32, (SEL_TILE, COLS), 0)
    s_t = jnp.where(kt + krow <= tq, s_t, NEG)
    m_t = jnp.max(s_t, axis=0, keepdims=True)
    acct_scr[...] = _dot(vst_ref[0, 0, n_main], jnp.exp2(s_t - m_t).astype(BF16))
    g_sel = grow(1)

    def scores(j, slot):
        k0 = pl.multiple_of(jnp.minimum(j, last_tile) * SEL_TILE, SEL_TILE)
        s_all[slot] = _dot(ksel_ref[0, 0, pl.ds(k0, SEL_TILE), :], qt_main)

    def softmax(slot):
        s = s_all[slot]
        m_prev = m_scr[0:1, :]
        m_new = jnp.maximum(m_prev, jnp.max(s, axis=0, keepdims=True))
        a_all[slot] = jnp.broadcast_to(jnp.exp2(m_prev - m_new), (SUBLANES, COLS))
        p_all[slot] = jnp.exp2(s - m_new).astype(BF16)
        m_scr[...] = jnp.broadcast_to(m_new, m_scr.shape)

    def values(j, slot):
        acc_scr[...] = (a_all[slot, 0:1, :] * acc_scr[...]
                        + _dot(vst_ref[0, 0, jnp.minimum(j, last_tile)], p_all[slot]))

    m_scr[...] = jnp.full(m_scr.shape, NEG, F32)
    acc_scr[...] = jnp.zeros(acc_scr.shape, F32)
    for k in range(PIPE):
        scores(k, k)
    for k in range(PIPE // 2):
        softmax(k)

    def sweep(t, carry):
        for k in range(SWEEP_UNROLL):
            values(SWEEP_UNROLL * t + k, k % PIPE)
            softmax((k + PIPE // 2) % PIPE)
            scores(SWEEP_UNROLL * t + k + PIPE, k % PIPE)
        return carry

    lax.fori_loop(0, (n_main + SWEEP_UNROLL - 1) // SWEEP_UNROLL, sweep, 0)
    m_p = m_scr[0:1, :]
    m_tot = jnp.maximum(m_p, m_t)
    o_s = _normalize_dup(jnp.exp2(m_p - m_tot) * acc_scr[...] + jnp.exp2(m_t - m_tot) * acct_scr[...])

    comb = cw_scr[...] + g_sel * o_s
    outs = [comb[:, r * Q_BLOCK:(r + 1) * Q_BLOCK].T for r in range(Q_PER_KV)]
    o_ref[0] = jnp.concatenate([jnp.where(lo_half, outs[0], outs[1]), jnp.where(lo_half, outs[2], outs[3])],
                               axis=1)


def _attn_single(qpt, ksel, vst, kw, vwt, kcmp, vcmpt, c2st, gatest):
    batch, seq = qpt.shape[0], qpt.shape[1] * Q_BLOCK
    per_group = lambda a: pl.BlockSpec((1, 1) + a.shape[2:], lambda b, g, i: (b, g) + (0,) * (a.ndim - 2))
    return pl.pallas_call(
        _attn_kernel_single,
        grid=(batch, N_KV_HEADS, seq // Q_BLOCK),
        in_specs=[pl.BlockSpec((1, 1, Q_PER_KV // 2, LANES, Q_BLOCK), lambda b, g, i: (b, i, g, 0, 0)),
                  per_group(ksel), per_group(vst), per_group(kw), per_group(vwt), per_group(kcmp),
                  per_group(vcmpt), pl.BlockSpec(c2st.shape, lambda b, g, i: (0, 0)),
                  pl.BlockSpec((1, 1, 1, GATE_ROWS, Q_BLOCK), lambda b, g, i: (b, i, g, 0, 0))],
        out_specs=pl.BlockSpec((1, Q_BLOCK, Q_PER_KV * HEAD_DIM), lambda b, g, i: (b, i, g)),
        out_shape=jax.ShapeDtypeStruct((batch, seq, D_ATT), F32),
        scratch_shapes=[pltpu.VMEM((PIPE, SEL_TILE, COLS), F32), pltpu.VMEM((PIPE, SEL_TILE, COLS), BF16),
                        pltpu.VMEM((PIPE, SUBLANES, COLS), F32),
                        pltpu.VMEM((SUBLANES, COLS), F32), pltpu.VMEM((V_ROWS, COLS), F32),
                        pltpu.VMEM((V_ROWS, COLS), F32), pltpu.VMEM((LANES, COLS), F32),
                        pltpu.VMEM((WINDOW + Q_BLOCK, COLS), F32), pltpu.VMEM((WINDOW + Q_BLOCK, COLS), BF16)],
        compiler_params=pltpu.CompilerParams(dimension_semantics=("arbitrary", "arbitrary", "arbitrary"),
                                             vmem_limit_bytes=VMEM_LIMIT_BYTES),
        name="attn",
    )(qpt, ksel, vst, kw, vwt, kcmp, vcmpt, c2st, gatest)


def _attn_kernel(q_ref, ksel_ref, vst_ref, kw_ref, vwt_ref, kcmp_ref, vcmpt_ref, c2st_ref, gate_ref, o_ref,
                 s_all, p_all, a_all, cm_all, m_scr, acc_scr, acct_scr, cw_scr, sw_scr, pw_scr):
    i = pl.program_id(1)
    starts = [(Q_PER_STEP * i + d) * Q_BLOCK for d in range(Q_PER_STEP)]
    kv = lambda g: g % N_KV_HEADS
    qb = lambda g: g // N_KV_HEADS
    nch = kcmp_ref.shape[2]
    last_tile = ksel_ref.shape[2] // SEL_TILE - 1
    groups = range(Q_PER_STEP * N_KV_HEADS)

    lane = lax.broadcasted_iota(jnp.int32, (Q_BLOCK, LANES), 1)
    lo_half = lane < HEAD_DIM
    top_half = lax.broadcasted_iota(jnp.int32, (LANES, Q_BLOCK), 0) < HEAD_DIM
    zero_bf = jnp.zeros((LANES, Q_BLOCK), BF16)
    tqs = [st + (lax.broadcasted_iota(jnp.int32, (1, COLS), 1) & (Q_BLOCK - 1)) for st in starts]
    wk = WINDOW + Q_BLOCK
    bases = [pl.multiple_of(jnp.maximum(st - WINDOW, 0), Q_BLOCK) for st in starts]
    wrow = lax.broadcasted_iota(jnp.int32, (WIN_MASK_ROWS, COLS), 0)
    nrow = lax.broadcasted_iota(jnp.int32, (nch, COLS), 0)
    srow = lax.broadcasted_iota(jnp.int32, (N_SLC_LANES, Q_BLOCK), 0)
    tqqs = [st + lax.broadcasted_iota(jnp.int32, (N_SLC_LANES, Q_BLOCK), 1) for st in starts]
    c2st = c2st_ref[...]
    n_main = starts[0] // SEL_TILE

    qt = [jnp.concatenate([jnp.where(top_half if r % 2 == 0 else ~top_half,
                                     q_ref[0, qb(g), kv(g) * (Q_PER_KV // 2) + r // 2], zero_bf)
                           for r in range(Q_PER_KV)], axis=1) for g in groups]

    def grow(g, b):
        gt = gate_ref[0, qb(g), kv(g)]
        return jnp.concatenate([gt[r * N_BRANCH + b:r * N_BRANCH + b + 1, :] for r in range(Q_PER_KV)], axis=1)

    s_c = [_dot(kcmp_ref[0, kv(g)], qt[g]) for g in groups]
    for g in groups:
        sw_scr[g] = _dot(kw_ref[0, kv(g), pl.ds(bases[qb(g)], wk), :], qt[g])

    last_valid = [lax.shift_right_arithmetic(t - (CMP_BLOCK - 1), CMP_SHIFT) for t in tqs]
    curs = [lax.shift_right_logical(t, SLC_SHIFT) for t in tqqs]
    forced = [(srow == 0) | (srow == c) | (srow == c - 1) for c in curs]
    future = [srow * SLC_BLOCK > t for t in tqqs]
    impt = []
    for g in groups:
        sc = jnp.where(nrow <= last_valid[qb(g)], s_c[g], NEG)
        m_c = jnp.max(sc, axis=0, keepdims=True)
        e_c = jnp.exp2(sc - m_c)
        l_c = jnp.sum(e_c, axis=0, keepdims=True)
        p_c = e_c * jnp.where(m_c > 0.5 * NEG, 1.0 / l_c, 0.0)
        cw_scr[g] = grow(g, 0) * _dot(vcmpt_ref[0, kv(g)], p_c.astype(BF16))
        psum = sum(p_c[:, r * Q_BLOCK:(r + 1) * Q_BLOCK] for r in range(Q_PER_KV))
        imp = sum(_dot(c2st, t) for t in _split2(psum))
        impt.append(jnp.where(forced[qb(g)], 1e9, jnp.where(future[qb(g)], -1e9, imp)))

    def window_mask_piece(g, k, colmax, zero):
        rows = slice(WIN_MASK_ROWS * k, WIN_MASK_ROWS * (k + 1))
        back = (tqs[qb(g)] + zero - bases[qb(g)] - WIN_MASK_ROWS * k) - wrow
        in_window = lax.bitcast_convert_type(back, jnp.uint32) < jnp.uint32(WINDOW)
        sm = jnp.where(in_window, sw_scr[g, rows, :], NEG)
        sw_scr[g, rows, :] = sm
        for c in range(WIN_MASK_ROWS // SUBLANES):
            colmax = jnp.maximum(colmax, sm[c * SUBLANES:(c + 1) * SUBLANES])
        return colmax

    def window_exp_piece(g, k, m_w, zero):
        rows = slice(WIN_EXP_ROWS * k, WIN_EXP_ROWS * (k + 1))
        pw_scr[g, rows, :] = jnp.exp2(sw_scr[g, rows, :] - (m_w + zero.astype(F32))).astype(BF16)

    key_to_float = lambda k: lax.bitcast_convert_type(jnp.where(k >= 0, k, k ^ jnp.int32(0x7FFFFFFF)), F32)
    wrap32 = lambda v: ((v + 2 ** 31) % 2 ** 32) - 2 ** 31
    n_mask, n_exp = wk // WIN_MASK_ROWS, wk // WIN_EXP_ROWS
    thr = [jnp.full((1, Q_BLOCK), INT32_MIN, jnp.int32) for _ in groups]
    colmax = [jnp.full((SUBLANES, COLS), NEG, F32) for _ in groups]
    m_w = [None for _ in groups]
    piece = 0
    for b in range(32 - RADIX_BITS, -1, -RADIX_BITS):
        for g in groups:
            reached = thr[g]
            for mult in range(1, 2 ** RADIX_BITS):
                cand = thr[g] + jnp.int32(wrap32(mult << b))
                n_ge = jnp.sum(jnp.where(impt[g] >= key_to_float(cand), 1.0, 0.0), axis=0, keepdims=True)
                reached = jnp.where(n_ge >= N_SELECT, cand, reached)
            thr[g] = reached
        for _ in range(RADIX_BITS):
            for g in groups:
                zero = jnp.concatenate([_zero_after(thr[g])] * Q_PER_KV, axis=1)
                if piece < n_mask:
                    colmax[g] = window_mask_piece(g, piece, colmax[g], zero)
                elif piece < n_mask + n_exp:
                    if piece == n_mask:
                        m_w[g] = jnp.max(colmax[g], axis=0, keepdims=True)
                    window_exp_piece(g, piece - n_mask, m_w[g], zero)
            piece += 1

    scol = lax.broadcasted_iota(jnp.int32, (N_SLC_LANES, N_SLC_LANES), 1)
    earlier = jnp.where(scol < srow, 1.0, 0.0).astype(BF16)
    widen = lambda x: jnp.concatenate([x.astype(BF16)] * Q_PER_KV, axis=1)
    qt_tail, qt_main = [], []
    for g in groups:
        vw = jnp.concatenate([vwt_ref[0, kv(g), bases[qb(g)] // LANES + c] for c in range(wk // LANES)],
                             axis=1)
        cw_scr[g] += grow(g, 2) * _normalize_dup(_dot(vw, pw_scr[g]))
        kth = key_to_float(thr[g])
        above = impt[g] > kth
        tied = impt[g] == kth
        n_above = jnp.sum(jnp.where(above, 1.0, 0.0), axis=0, keepdims=True)
        tied_before = _dot(earlier, jnp.where(tied, 1.0, 0.0).astype(BF16))
        selected = above | (tied & (tied_before < N_SELECT - n_above))
        bias = jnp.where(selected, 0.0, NEG)
        bias_main = jnp.where(srow >= n_main * (SEL_TILE // SLC_BLOCK), NEG, bias)
        qt_tail.append(jnp.concatenate([qt[g], widen(bias)], axis=0))
        qt_main.append(jnp.concatenate([qt[g], widen(bias_main)], axis=0))

    kt = pl.multiple_of(n_main * SEL_TILE, SEL_TILE)
    krow = lax.broadcasted_iota(jnp.int32, (SEL_TILE, COLS), 0)
    for g in groups:
        s_all[g, 0] = _dot(ksel_ref[0, kv(g), pl.ds(kt, SEL_TILE), :], qt_tail[g])
    m_t = []
    for g in groups:
        s_t = jnp.where(kt + krow <= tqs[qb(g)], s_all[g, 0], NEG)
        m_t.append(jnp.max(s_t, axis=0, keepdims=True))
        p_all[g, 0] = jnp.exp2(s_t - m_t[g]).astype(BF16)
    for g in groups:
        acct_scr[g] = _dot(vst_ref[0, kv(g), n_main], p_all[g, 0])
    g_sel = [grow(g, 1) for g in groups]

    def scores(g, j, slot):
        k0 = pl.multiple_of(jnp.minimum(j, last_tile) * SEL_TILE, SEL_TILE)
        sv = _dot(ksel_ref[0, kv(g), pl.ds(k0, SEL_TILE), :], qt_main[g])
        s_all[g, slot] = sv
        cm = sv[0:SUBLANES]
        for c in range(1, SEL_TILE // SUBLANES):
            cm = jnp.maximum(cm, sv[c * SUBLANES:(c + 1) * SUBLANES])
        cm_all[g, slot] = cm

    def softmax(g, slot):
        s = s_all[g, slot]
        m_prev = m_scr[g, 0:1, :]
        m_new = jnp.maximum(m_prev, jnp.max(cm_all[g, slot], axis=0, keepdims=True))
        a_all[g, slot] = jnp.broadcast_to(jnp.exp2(m_prev - m_new), (SUBLANES, COLS))
        p_all[g, slot] = jnp.exp2(s - m_new).astype(BF16)
        m_scr[g] = jnp.broadcast_to(m_new, (SUBLANES, COLS))

    def values(g, j, slot):
        acc_scr[g] = (a_all[g, slot, 0:1, :] * acc_scr[g]
                      + _dot(vst_ref[0, kv(g), jnp.minimum(j, last_tile)], p_all[g, slot]))

    m_scr[...] = jnp.full(m_scr.shape, NEG, F32)
    acc_scr[...] = jnp.zeros(acc_scr.shape, F32)
    for k in range(PIPE):
        for g in groups:
            scores(g, k, k)
    for k in range(PIPE // 2):
        for g in groups:
            softmax(g, k)

    def sweep(t, carry):
        for k in range(SWEEP_UNROLL):
            for g in groups:
                values(g, SWEEP_UNROLL * t + k, k % PIPE)
                softmax(g, (k + PIPE // 2) % PIPE)
                scores(g, SWEEP_UNROLL * t + k + PIPE, k % PIPE)
        return carry

    lax.fori_loop(0, (n_main + SWEEP_UNROLL - 1) // SWEEP_UNROLL, sweep, 0)

    pairs = [[] for _ in range(Q_PER_STEP)]
    for g in groups:
        m_p = m_scr[g, 0:1, :]
        m_tot = jnp.maximum(m_p, m_t[g])
        o_s = _normalize_dup(jnp.exp2(m_p - m_tot) * acc_scr[g] + jnp.exp2(m_t[g] - m_tot) * acct_scr[g])
        comb = cw_scr[g] + g_sel[g] * o_s
        outs = [comb[:, r * Q_BLOCK:(r + 1) * Q_BLOCK].T for r in range(Q_PER_KV)]
        pairs[qb(g)] += [jnp.where(lo_half, outs[0], outs[1]), jnp.where(lo_half, outs[2], outs[3])]
    for d in range(Q_PER_STEP):
        o_ref[0, d * Q_BLOCK:(d + 1) * Q_BLOCK, :] = jnp.concatenate(pairs[d], axis=1)


def _attn(qpt, ksel, vst, kw, vwt, kcmp, vcmpt, c2st, gatest):
    batch, seq = qpt.shape[0], qpt.shape[1] * Q_BLOCK
    per_batch = lambda a: pl.BlockSpec((1,) + a.shape[1:], lambda b, i: (b,) + (0,) * (a.ndim - 1),
                                       pipeline_mode=pl.Buffered(1))
    per_block = lambda a: pl.BlockSpec((1, Q_PER_STEP) + a.shape[2:], lambda b, i: (b, i) + (0,) * (a.ndim - 2))
    chains = Q_PER_STEP * N_KV_HEADS
    grp = lambda *shape: pltpu.VMEM((chains,) + shape, F32)
    return pl.pallas_call(
        _attn_kernel,
        grid=(batch, seq // (Q_PER_STEP * Q_BLOCK)),
        in_specs=[per_block(qpt), per_batch(ksel), per_batch(vst), per_batch(kw), per_batch(vwt), per_batch(kcmp),
                  per_batch(vcmpt), pl.BlockSpec(c2st.shape, lambda b, i: (0, 0)), per_block(gatest)],
        out_specs=pl.BlockSpec((1, Q_PER_STEP * Q_BLOCK, D_ATT), lambda b, i: (b, i, 0)),
        out_shape=jax.ShapeDtypeStruct((batch, seq, D_ATT), F32),
        scratch_shapes=[grp(PIPE, SEL_TILE, COLS), pltpu.VMEM((chains, PIPE, SEL_TILE, COLS), BF16),
                        grp(PIPE, SUBLANES, COLS), grp(PIPE, SUBLANES, COLS), grp(SUBLANES, COLS),
                        grp(V_ROWS, COLS), grp(V_ROWS, COLS),
                        grp(LANES, COLS), grp(WINDOW + Q_BLOCK, COLS),
                        pltpu.VMEM((chains, WINDOW + Q_BLOCK, COLS), BF16)],
        compiler_params=pltpu.CompilerParams(dimension_semantics=("arbitrary", "arbitrary"),
                                             vmem_limit_bytes=VMEM_LIMIT_BYTES),
        name="attn",
    )(qpt, ksel, vst, kw, vwt, kcmp, vcmpt, c2st, gatest)


POST_SUB = 2


def _post_kernel(x_ref, oatt_ref, mconv_ref, p_ref, onag_ref, wo_ref, lnf_ref, wup_ref, fcw_ref, fcb_ref,
                 wdn_ref, lnp_ref, wpg_ref, wpe_ref, out_ref, gbuf, *, tiles_per_seq):
    tm = x_ref.shape[0]
    sub = tm // POST_SUB
    it = pl.program_id(0) % tiles_per_seq
    subs = [slice(k * sub, (k + 1) * sub) for k in range(POST_SUB)]

    @pl.when(it == 0)
    def _():
        gbuf[0:SUBLANES, :] = jnp.zeros((SUBLANES, D_FF), F32)

    h1 = [x_ref[r, :] + _dot(jnp.concatenate([_rms(oatt_ref[r, :], onag_ref[...]).astype(BF16), mconv_ref[r, :]],
                                             axis=1), wo_ref[...]) for r in subs]
    xn = [_rms(h, lnf_ref[...]).astype(BF16) for h in h1]
    h2 = []
    for k, r in enumerate(subs):
        gpre = _dot(xn[k], wup_ref[:, :D_FF])
        up = _dot(xn[k], wup_ref[:, D_FF:])
        lo = SUBLANES + k * sub
        gbuf[lo:lo + sub, :] = gpre
        gate = (fcw_ref[2:3, :] * gpre + fcw_ref[1:2, :] * gbuf[lo - 1:lo - 1 + sub, :]
                + fcw_ref[0:1, :] * gbuf[lo - 2:lo - 2 + sub, :]) + fcb_ref[...]
        act = (gate * jax.nn.sigmoid(gate) * up).astype(BF16)
        h2.append(h1[k] + _dot(act, wdn_ref[...]))
    gbuf[0:SUBLANES, :] = gbuf[tm:tm + SUBLANES, :]
    for k, r in enumerate(subs):
        xn2 = _rms(h2[k], lnp_ref[...]).astype(BF16)
        out_ref[r, :] = (h2[k] + jax.nn.sigmoid(_dot(xn2, wpg_ref[...]))
                         * _dot(p_ref[r, :].astype(BF16), wpe_ref[...]))


def _post(x2, oatt, mconv, p2, onag, wo, lnf, wup, fcw, fcb, wdn, lnp, wpg, wpe, *, seq, tm):
    n = x2.shape[0]
    tps = seq // tm
    row = lambda w: pl.BlockSpec((tm, w), lambda i: (i, 0))
    full = lambda a: pl.BlockSpec(a.shape, lambda i: (0,) * a.ndim, pipeline_mode=pl.Buffered(1))
    return pl.pallas_call(
        functools.partial(_post_kernel, tiles_per_seq=tps),
        grid=(n // tm,),
        in_specs=[row(D_MODEL), row(D_ATT), row(D_CONV), row(D_PLE), full(onag), full(wo), full(lnf), full(wup),
                  full(fcw), full(fcb), full(wdn), full(lnp), full(wpg), full(wpe)],
        out_specs=row(D_MODEL),
        out_shape=jax.ShapeDtypeStruct((n, D_MODEL), F32),
        scratch_shapes=[pltpu.VMEM((tm + SUBLANES, D_FF), F32)],
        compiler_params=pltpu.CompilerParams(dimension_semantics=("arbitrary",),
                                             vmem_limit_bytes=VMEM_LIMIT_BYTES),
        name="post",
    )(x2, oatt, mconv, p2, onag, wo, lnf, wup, fcw, fcb, wdn, lnp, wpg, wpe)


def _rope_tables(pos):
    half = ROT_DIM // 2
    d = np.arange(LANES) % HEAD_DIM
    inv_freq = np.float64(ROPE_THETA) ** (-np.arange(half, dtype=np.float64) * 2.0 / ROT_DIM)
    ang = np.asarray(pos, np.float64)[:, None] * np.tile(inv_freq, LANES // half)[None, :]
    c, sn = np.cos(ang), np.sin(ang)
    first, second = (d < half)[None, :], ((d >= half) & (d < ROT_DIM))[None, :]
    cos = np.where(first | second, c, 1.0)
    sa = np.where(first, -sn, 0.0)
    sb = np.where(second, sn, 0.0)
    return tuple(jnp.asarray(t, F32) for t in (cos, sa, sb))


def _block_diag_mean(width):
    idx = np.arange(width) // HEAD_DIM
    return jnp.asarray((idx[:, None] == idx[None, :]).astype(np.float32) / HEAD_DIM, BF16)


def _cmp_to_slc_t(nch):
    cs = CMP_STRIDE * np.arange(nch)[None, :]
    ss = SLC_BLOCK * np.arange(N_SLC_LANES)[:, None]
    ov = np.clip(np.minimum(cs + CMP_BLOCK, ss + SLC_BLOCK) - np.maximum(cs, ss), 0, None)
    return jnp.asarray(ov.astype(np.float32) / CMP_BLOCK, BF16)


def _layer(h, p_l, ln_mix_g, w_in, qn_g, kn_g, pe_k, pe_v, w_ck1, w_ck2, w_cv1, w_cv2, conv_w, on_att_g,
           on_conv_g, w_o, ln_ffn_g, w_up, ffn_conv_w, ffn_conv_b, w_down, ln_ple_g, w_pg, w_pe):
    batch, seq, _ = h.shape
    assert seq % SEL_TILE == 0 and seq // SLC_BLOCK <= N_SLC_LANES and seq // SLC_BLOCK >= N_SELECT
    assert seq >= WINDOW + Q_BLOCK
    n = batch * seq
    nch = seq // CMP_STRIDE
    x2 = h.reshape(n, D_MODEL)
    row = lambda v: v.reshape(1, -1).astype(F32)

    o_q, o_kv, o_g, o_cv = 0, D_ATT, D_ATT + 6 * KV_W, D_ATT + 6 * KV_W + N_BRANCH * N_Q_HEADS
    wq = w_in[:, o_q:o_kv].astype(BF16)
    wkv = w_in[:, o_kv:o_g].astype(BF16)
    wcv = w_in[:, o_cv:].astype(BF16)
    per_g = Q_PER_KV * N_BRANCH
    wg = jnp.concatenate(
        [jnp.pad(w_in[:, o_g + g * per_g:o_g + (g + 1) * per_g], ((0, 0), (0, LANES - per_g)))
         for g in range(N_KV_HEADS)], axis=1).astype(BF16)
    cos, sa, sb = _rope_tables(np.arange(seq))
    bd512, bd128 = _block_diag_mean(D_ATT), _block_diag_mean(KV_W)
    tile_heads = lambda v, k: jnp.tile(v.astype(F32), k).reshape(1, -1)

    tm = ROW_TILE
    assert seq % ROW_TILE == 0 and seq % POST_TILE == 0
    qpt, kc2, vc2, ksel, vst, kw, vwt, gatest, mconv = _inproj(
        x2, row(ln_mix_g), wq, wkv, wg, wcv, tile_heads(qn_g, N_Q_HEADS),
        jnp.stack([jnp.tile(kn_g[1], N_KV_HEADS), jnp.tile(kn_g[2], N_KV_HEADS)]).astype(F32),
        bd512, bd128, cos, sa, sb, conv_w.astype(F32), row(on_conv_g), batch=batch, seq=seq, tm=tm)

    assert N_KV_HEADS == 2
    half = CMP_BLOCK // 2

    def w1_parts(w1):
        w = w1.astype(BF16)
        z = jnp.zeros_like(w)
        wfull = jnp.concatenate([jnp.concatenate([w, z], axis=2), jnp.concatenate([z, w], axis=2)], axis=1)
        return wfull[:half].reshape(half * KV_W, -1), wfull[half:].reshape(half * KV_W, -1)

    def pe_parts(pe):
        pf = jnp.broadcast_to(pe[:, None, :], (CMP_BLOCK, N_KV_HEADS, HEAD_DIM)).astype(F32)
        return pf[:half].reshape(1, -1), pf[half:].reshape(1, -1)

    def w2bd(w2):
        w = w2.astype(BF16)
        z = jnp.zeros_like(w)
        return jnp.concatenate([jnp.concatenate([w, z], axis=1), jnp.concatenate([z, w], axis=1)], axis=0)
    w1ak, w1bk = w1_parts(w_ck1)
    w1av, w1bv = w1_parts(w_cv1)
    peak, pebk = pe_parts(pe_k)
    peav, pebv = pe_parts(pe_v)
    ccos, csa, csb = _rope_tables(CMP_STRIDE * np.arange(nch) + CMP_BLOCK - 1)
    kcmp, vcmpt = _compress(kc2.reshape(batch, seq, KV_W), vc2.reshape(batch, seq, KV_W),
                            w1ak, w1bk, w1av, w1bv, peak, pebk, peav, pebv, w2bd(w_ck2), w2bd(w_cv2),
                            tile_heads(kn_g[0], N_KV_HEADS), bd128, ccos, csa, csb)

    oatt = _attn(qpt, ksel, vst, kw, vwt, kcmp, vcmpt, _cmp_to_slc_t(nch), gatest)

    out = _post(x2, oatt.reshape(n, D_ATT), mconv, p_l.reshape(n, D_PLE), row(on_att_g), w_o.astype(BF16),
                row(ln_ffn_g), w_up.astype(BF16), ffn_conv_w.astype(F32), row(ffn_conv_b), w_down.astype(BF16),
                row(ln_ple_g), w_pg.astype(BF16), w_pe.astype(BF16), seq=seq, tm=POST_TILE)
    return out.reshape(batch, seq, D_MODEL)


def kernel(x, p, ln_mix_g, w_in, qn_g, kn_g, pe_k, pe_v, w_ck1, w_ck2, w_cv1, w_cv2, conv_w, on_att_g,
           on_conv_g, w_o, ln_ffn_g, w_up, ffn_conv_w, ffn_conv_b, w_down, ln_ple_g, w_pg, w_pe):
    h = x
    for i in range(p.shape[0]):
        h = _layer(h, p[i], ln_mix_g[i], w_in[i], qn_g[i], kn_g[i], pe_k[i], pe_v[i], w_ck1[i], w_ck2[i],
                   w_cv1[i], w_cv2[i], conv_w[i], on_att_g[i], on_conv_g[i], w_o[i], ln_ffn_g[i], w_up[i],
                   ffn_conv_w[i], ffn_conv_b[i], w_down[i], ln_ple_g[i], w_pg[i], w_pe[i])
    return h
```

```python
import functools

import jax
import jax.numpy as jnp
import numpy as np
from jax import lax
from jax.experimental import pallas as pl
from jax.experimental.pallas import tpu as pltpu

D_MODEL = 1024
HEAD_DIM = 64
N_Q_HEADS = 8
N_KV_HEADS = 2
Q_PER_KV = N_Q_HEADS // N_KV_HEADS
D_ATT = N_Q_HEADS * HEAD_DIM
D_CONV = D_MODEL - D_ATT
KV_W = N_KV_HEADS * HEAD_DIM
N_BRANCH = 3
CONV_TAPS = 3
ROT_DIM = HEAD_DIM // 4
ROPE_THETA = 500000.0
CMP_BLOCK = 32
CMP_STRIDE = 16
CMP_HIDDEN = 256
SLC_BLOCK = 64
N_SELECT = 16
WINDOW = 512
Q_BLOCK = 128
D_FF = 2816
D_PLE = 256
EPS = 1e-6
NEG = -1e30
INT32_MIN = -2 ** 31
LOG2E = float(np.log2(np.e))
V_ROWS = HEAD_DIM + 16
GATE_ROWS = 16
SLC_SHIFT = SLC_BLOCK.bit_length() - 1
CMP_SHIFT = CMP_STRIDE.bit_length() - 1
assert 1 << SLC_SHIFT == SLC_BLOCK and 1 << CMP_SHIFT == CMP_STRIDE

LANES = 128
SUBLANES = 8
N_SLC_LANES = LANES
VMEM_LIMIT_BYTES = 56 * 1024 * 1024

F32 = jnp.float32
BF16 = jnp.bfloat16


def _dot(a, b):
    return jnp.dot(a, b, preferred_element_type=F32)


def _rms(x, g):
    return x * lax.rsqrt(jnp.mean(x * x, axis=-1, keepdims=True) + EPS) * g


def _head_rms_rope(x, g, bd, cos, sa, sb):
    w = x.shape[-1]
    msq = _dot((x * x).astype(BF16), bd)
    xn = x * lax.rsqrt(msq + EPS) * g
    return xn * cos + pltpu.roll(xn, w - ROT_DIM // 2, 1) * sa + pltpu.roll(xn, ROT_DIM // 2, 1) * sb


def _dup_halves(x):
    r = pltpu.roll(x, HEAD_DIM, 1)
    lane = lax.broadcasted_iota(jnp.int32, x.shape, 1)
    lo = lane < HEAD_DIM
    return jnp.where(lo, x, r), jnp.where(lo, r, x)


def _inproj_kernel(x_ref, lng_ref, wq_ref, wkv_ref, wg_ref, wcv_ref, qng_ref, kng_ref, bd512_ref, bd128_ref,
                   cos_ref, sa_ref, sb_ref, convw_ref, oncg_ref,
                   q_out, kc_out, vc_out, ksel_out, vs_out, kw_out, vw_out, gate_out, mconv_out,
                   zbuf, *, tiles_per_seq):
    tm = x_ref.shape[0]
    sub = tm // IN_SUB
    it = pl.program_id(0) % tiles_per_seq

    @pl.when(it == 0)
    def _():
        zbuf[0:SUBLANES, :] = jnp.zeros((SUBLANES, D_CONV), F32)

    for k in range(IN_SUB):
        r = slice(k * sub, (k + 1) * sub)
        xn = _rms(x_ref[r, :], lng_ref[...]).astype(BF16)

        cos, sa, sb = cos_ref[r, :], sa_ref[r, :], sb_ref[r, :]
        cos4, sa4, sb4 = (jnp.concatenate([t] * 4, axis=1) for t in (cos, sa, sb))
        q = _dot(xn, wq_ref[...])
        qr = _head_rms_rope(q, qng_ref[...], bd512_ref[...], cos4, sa4, sb4)
        qs = qr * (HEAD_DIM ** -0.5 * LOG2E)
        for blk_i in range(sub // Q_BLOCK):
            rows = slice(blk_i * Q_BLOCK, (blk_i + 1) * Q_BLOCK)
            for pr in range(N_Q_HEADS // 2):
                q_out[0, k * (sub // Q_BLOCK) + blk_i, pr] = qs[rows, pr * LANES:(pr + 1) * LANES].T.astype(BF16)

        kv = _dot(xn, wkv_ref[...])
        kc_out[r, :] = kv[:, 0 * KV_W:1 * KV_W]
        vc_out[r, :] = kv[:, 1 * KV_W:2 * KV_W]
        ks = _head_rms_rope(kv[:, 2 * KV_W:3 * KV_W], kng_ref[0:1, :], bd128_ref[...], cos, sa, sb)
        kw = _head_rms_rope(kv[:, 4 * KV_W:5 * KV_W], kng_ref[1:2, :], bd128_ref[...], cos, sa, sb)
        vs = kv[:, 3 * KV_W:4 * KV_W]
        vw = kv[:, 5 * KV_W:6 * KV_W]
        tpos = it * tm + k * sub + lax.broadcasted_iota(jnp.int32, (sub, N_SLC_LANES), 0)
        blk = lax.broadcasted_iota(jnp.int32, (sub, N_SLC_LANES), 1)
        onehot = jnp.where(lax.shift_right_logical(tpos, SLC_SHIFT) == blk, 1.0, 0.0).astype(BF16)
        ks_d, vs_d, kw_d, vw_d = (_dup_halves(t) for t in (ks, vs, kw, vw))
        vrow = lax.broadcasted_iota(jnp.int32, (V_ROWS, sub), 0)
        for g in range(N_KV_HEADS):
            ksel_out[0, g, r, :] = jnp.concatenate([ks_d[g].astype(BF16), onehot], axis=1)
            kw_out[0, g, r, :] = kw_d[g].astype(BF16)
            vst = jnp.where(vrow < HEAD_DIM, vs_d[g].T[:V_ROWS], 1.0).astype(BF16)
            for c in range(sub // SEL_TILE):
                vs_out[0, g, k * (sub // SEL_TILE) + c] = vst[:, c * SEL_TILE:(c + 1) * SEL_TILE]
            vwt = jnp.where(vrow < HEAD_DIM, vw_d[g].T[:V_ROWS], 1.0).astype(BF16)
            for c in range(sub // LANES):
                vw_out[0, g, k * (sub // LANES) + c] = vwt[:, c * LANES:(c + 1) * LANES]

        gates = jax.nn.sigmoid(_dot(xn, wg_ref[...]))
        for blk_i in range(sub // Q_BLOCK):
            rows = slice(blk_i * Q_BLOCK, (blk_i + 1) * Q_BLOCK)
            for g in range(N_KV_HEADS):
                gate_out[0, k * (sub // Q_BLOCK) + blk_i, g] = gates[rows, g * LANES:(g + 1) * LANES].T[:GATE_ROWS]

        cv = _dot(xn, wcv_ref[...])
        cb, cc, cx = cv[:, :D_CONV], cv[:, D_CONV:2 * D_CONV], cv[:, 2 * D_CONV:]
        z = cc * cx
        lo = SUBLANES + k * sub
        zbuf[lo:lo + sub, :] = z
        y = (convw_ref[2:3, :] * z + convw_ref[1:2, :] * zbuf[lo - 1:lo - 1 + sub, :]
             + convw_ref[0:1, :] * zbuf[lo - 2:lo - 2 + sub, :])
        mconv_out[r, :] = _rms(cb * y, oncg_ref[...]).astype(BF16)
    zbuf[0:SUBLANES, :] = zbuf[tm:tm + SUBLANES, :]


def _inproj(x2, lng, wq, wkv, wg, wcv, qng, kng, bd512, bd128, cos, sa, sb, convw, oncg, *, batch, seq, tm):
    n = batch * seq
    tps = seq // tm
    row = lambda w: pl.BlockSpec((tm, w), lambda i: (i, 0))
    full = lambda a: pl.BlockSpec(a.shape, lambda i: (0,) * a.ndim)
    tab = pl.BlockSpec((tm, LANES), lambda i: (i % tps, 0))
    grp = lambda w: pl.BlockSpec((1, N_KV_HEADS, tm, w), lambda i: (i // tps, 0, i % tps, 0))
    gshape = lambda w: jax.ShapeDtypeStruct((batch, N_KV_HEADS, seq, w), BF16)
    assert tm % (IN_SUB * SEL_TILE) == 0
    vtile = lambda keys: pl.BlockSpec((1, N_KV_HEADS, tm // keys, V_ROWS, keys),
                                      lambda i: (i // tps, 0, i % tps, 0, 0))
    vshape = lambda keys: jax.ShapeDtypeStruct((batch, N_KV_HEADS, seq // keys, V_ROWS, keys), BF16)
    qblk = lambda tile: pl.BlockSpec((1, tm // Q_BLOCK) + tile, lambda i: (i // tps, i % tps) + (0,) * len(tile))
    return pl.pallas_call(
        functools.partial(_inproj_kernel, tiles_per_seq=tps),
        grid=(n // tm,),
        in_specs=[row(D_MODEL), full(lng), full(wq), full(wkv), full(wg), full(wcv), full(qng), full(kng),
                  full(bd512), full(bd128), tab, tab, tab, full(convw), full(oncg)],
        out_specs=[qblk((N_Q_HEADS // 2, LANES, Q_BLOCK)), row(KV_W), row(KV_W), grp(2 * LANES), vtile(SEL_TILE),
                   grp(LANES), vtile(LANES), qblk((N_KV_HEADS, GATE_ROWS, Q_BLOCK)), row(D_CONV)],
        out_shape=[jax.ShapeDtypeStruct((batch, seq // Q_BLOCK, N_Q_HEADS // 2, LANES, Q_BLOCK), BF16),
                   jax.ShapeDtypeStruct((n, KV_W), F32),
                   jax.ShapeDtypeStruct((n, KV_W), F32), gshape(2 * LANES), vshape(SEL_TILE), gshape(LANES),
                   vshape(LANES),
                   jax.ShapeDtypeStruct((batch, seq // Q_BLOCK, N_KV_HEADS, GATE_ROWS, Q_BLOCK), F32),
                   jax.ShapeDtypeStruct((n, D_CONV), BF16)],
        scratch_shapes=[pltpu.VMEM((tm + SUBLANES, D_CONV), F32)],
        compiler_params=pltpu.CompilerParams(dimension_semantics=("arbitrary",),
                                             vmem_limit_bytes=VMEM_LIMIT_BYTES),
        name="inproj",
    )(x2, lng, wq, wkv, wg, wcv, qng, kng, bd512, bd128, cos, sa, sb, convw, oncg)


def _gelu_tanh(x):
    return 0.5 * x * (1.0 + jnp.tanh(np.sqrt(2.0 / np.pi).astype(np.float32) * (x + 0.044715 * (x * x * x))))


def _compress_kernel(kc_ref, vc_ref, w1ak_ref, w1bk_ref, w1av_ref, w1bv_ref, peak_ref, pebk_ref, peav_ref,
                     pebv_ref, w2k_ref, w2v_ref, kng_ref, bd128_ref, cos_ref, sa_ref, sb_ref,
                     kcmp_out, vcmp_out):
    nch = kc_ref.shape[1] // CMP_STRIDE

    def chunks(ref):
        return jnp.concatenate([ref[0, pl.ds(l, nch, stride=CMP_STRIDE), :] for l in range(CMP_STRIDE)], axis=1)

    def compress(x, w1a, w1b, pea, peb, w2):
        a = _dot((x + pea).astype(BF16), w1a)
        b = _dot((x + peb).astype(BF16), w1b)
        hid = _gelu_tanh(a + pltpu.roll(b, nch - 1, 0))
        return _dot(hid.astype(BF16), w2)

    kc = compress(chunks(kc_ref), w1ak_ref[...], w1bk_ref[...], peak_ref[...], pebk_ref[...], w2k_ref[...])
    vc = compress(chunks(vc_ref), w1av_ref[...], w1bv_ref[...], peav_ref[...], pebv_ref[...], w2v_ref[...])
    kc = _head_rms_rope(kc, kng_ref[...], bd128_ref[...], cos_ref[...], sa_ref[...], sb_ref[...])
    kd, vd = _dup_halves(kc), _dup_halves(vc)
    for g in range(N_KV_HEADS):
        kcmp_out[0, g] = kd[g].astype(BF16)
        vcmp_out[0, g] = vd[g].T.astype(BF16)


def _compress(kc3, vc3, w1ak, w1bk, w1av, w1bv, peak, pebk, peav, pebv, w2k, w2v, kng0, bd128, cos, sa, sb):
    batch, seq, wide = kc3.shape
    nch = seq // CMP_STRIDE
    full = lambda a: pl.BlockSpec(a.shape, lambda b: (0,) * a.ndim)
    tok = pl.BlockSpec((1, seq, wide), lambda b: (b, 0, 0))
    kout = pl.BlockSpec((1, N_KV_HEADS, nch, LANES), lambda b: (b, 0, 0, 0))
    vout = pl.BlockSpec((1, N_KV_HEADS, LANES, nch), lambda b: (b, 0, 0, 0))
    consts = (w1ak, w1bk, w1av, w1bv, peak, pebk, peav, pebv, w2k, w2v, kng0, bd128, cos, sa, sb)
    return pl.pallas_call(
        _compress_kernel,
        grid=(batch,),
        in_specs=[tok, tok] + [full(a) for a in consts],
        out_specs=[kout, vout],
        out_shape=[jax.ShapeDtypeStruct((batch, N_KV_HEADS, nch, LANES), BF16),
                   jax.ShapeDtypeStruct((batch, N_KV_HEADS, LANES, nch), BF16)],
        compiler_params=pltpu.CompilerParams(dimension_semantics=("arbitrary",),
                                             vmem_limit_bytes=VMEM_LIMIT_BYTES),
        name="compress",
    )(kc3, vc3, *consts)


SEL_TILE = 256
COLS = Q_PER_KV * Q_BLOCK
ROW_TILE = 512
IN_SUB = 2
POST_TILE = 512
RADIX_BITS = 1
Q_PER_STEP = 2
PIPE = 2
SWEEP_UNROLL = 2
assert SWEEP_UNROLL % PIPE == 0 and SEL_TILE % (Q_PER_STEP * Q_BLOCK) == 0
WIN_MASK_ROWS = 32
WIN_EXP_ROWS = 64
assert (WINDOW + Q_BLOCK) // WIN_MASK_ROWS + (WINDOW + Q_BLOCK) // WIN_EXP_ROWS <= 32


def _split2(x):
    hi = x.astype(BF16)
    return hi, (x - hi.astype(F32)).astype(BF16)


def _zero_after(x):
    return lax.shift_right_logical(lax.shift_right_logical(x, 16), 16)


def _normalize_dup(acc):
    o = acc[:HEAD_DIM] / acc[HEAD_DIM:HEAD_DIM + 1]
    return jnp.concatenate([o, o], axis=0)


def _attn_kernel_single(q_ref, ksel_ref, vst_ref, kw_ref, vwt_ref, kcmp_ref, vcmpt_ref, c2st_ref, gate_ref, o_ref,
                        s_all, p_all, a_all, m_scr, acc_scr, acct_scr, cw_scr, sw_scr, pw_scr):
    i = pl.program_id(2)
    start = i * Q_BLOCK
    nch = kcmp_ref.shape[2]
    last_tile = ksel_ref.shape[2] // SEL_TILE - 1

    lane = lax.broadcasted_iota(jnp.int32, (Q_BLOCK, LANES), 1)
    lo_half = lane < HEAD_DIM
    top_half = lax.broadcasted_iota(jnp.int32, (LANES, Q_BLOCK), 0) < HEAD_DIM
    zero_bf = jnp.zeros((LANES, Q_BLOCK), BF16)
    qt = jnp.concatenate([jnp.where(top_half if r % 2 == 0 else ~top_half, q_ref[0, 0, r // 2], zero_bf)
                          for r in range(Q_PER_KV)], axis=1)
    tq = start + (lax.broadcasted_iota(jnp.int32, (1, COLS), 1) & (Q_BLOCK - 1))
    gt = gate_ref[0, 0, 0]
    grow = lambda b: jnp.concatenate([gt[r * N_BRANCH + b:r * N_BRANCH + b + 1, :] for r in range(Q_PER_KV)],
                                     axis=1)

    wk = WINDOW + Q_BLOCK
    base = pl.multiple_of(jnp.maximum(start - WINDOW, 0), Q_BLOCK)

    wrow = lax.broadcasted_iota(jnp.int32, (WIN_MASK_ROWS, COLS), 0)

    def window_mask_piece(k, colmax, zero):
        rows = slice(WIN_MASK_ROWS * k, WIN_MASK_ROWS * (k + 1))
        back = (tq + zero - base - WIN_MASK_ROWS * k) - wrow
        in_window = lax.bitcast_convert_type(back, jnp.uint32) < jnp.uint32(WINDOW)
        sm = jnp.where(in_window, sw_scr[rows, :], NEG)
        sw_scr[rows, :] = sm
        for c in range(WIN_MASK_ROWS // SUBLANES):
            colmax = jnp.maximum(colmax, sm[c * SUBLANES:(c + 1) * SUBLANES])
        return colmax

    def window_exp_piece(k, m_w, zero):
        rows = slice(WIN_EXP_ROWS * k, WIN_EXP_ROWS * (k + 1))
        pw_scr[rows, :] = jnp.exp2(sw_scr[rows, :] - (m_w + zero.astype(F32))).astype(BF16)

    s_c = _dot(kcmp_ref[0, 0], qt)
    sw_scr[...] = _dot(kw_ref[0, 0, pl.ds(base, wk), :], qt)
    nrow = lax.broadcasted_iota(jnp.int32, (nch, COLS), 0)
    last_valid = lax.shift_right_arithmetic(tq - (CMP_BLOCK - 1), CMP_SHIFT)
    s_c = jnp.where(nrow <= last_valid, s_c, NEG)
    m_c = jnp.max(s_c, axis=0, keepdims=True)
    e_c = jnp.exp2(s_c - m_c)
    l_c = jnp.sum(e_c, axis=0, keepdims=True)
    p_c = e_c * jnp.where(m_c > 0.5 * NEG, 1.0 / l_c, 0.0)
    cw_scr[...] = grow(0) * _dot(vcmpt_ref[0, 0], p_c.astype(BF16))

    psum = sum(p_c[:, r * Q_BLOCK:(r + 1) * Q_BLOCK] for r in range(Q_PER_KV))
    c2st = c2st_ref[...]
    impt = sum(_dot(c2st, t) for t in _split2(psum))
    srow = lax.broadcasted_iota(jnp.int32, (N_SLC_LANES, Q_BLOCK), 0)
    tqq = start + lax.broadcasted_iota(jnp.int32, (N_SLC_LANES, Q_BLOCK), 1)
    cur = lax.shift_right_logical(tqq, SLC_SHIFT)
    forced = (srow == 0) | (srow == cur) | (srow == cur - 1)
    future = srow * SLC_BLOCK > tqq
    impt = jnp.where(forced, 1e9, jnp.where(future, -1e9, impt))
    key_to_float = lambda k: lax.bitcast_convert_type(jnp.where(k >= 0, k, k ^ jnp.int32(0x7FFFFFFF)), F32)
    thr = jnp.full((1, Q_BLOCK), INT32_MIN, jnp.int32)
    n_mask, n_exp = wk // WIN_MASK_ROWS, wk // WIN_EXP_ROWS
    colmax = jnp.full((SUBLANES, COLS), NEG, F32)
    wrap32 = lambda v: ((v + 2 ** 31) % 2 ** 32) - 2 ** 31
    piece = 0
    for b in range(30, -1, -2):
        reached = None
        for mult in (1, 2, 3):
            cand = thr + jnp.int32(wrap32(mult << b))
            n_ge = jnp.sum(jnp.where(impt >= key_to_float(cand), 1.0, 0.0), axis=0, keepdims=True)
            reached = jnp.where(n_ge >= N_SELECT, cand, thr if reached is None else reached)
        thr = reached
        zero = jnp.concatenate([_zero_after(thr)] * Q_PER_KV, axis=1)
        for _ in range(2):
            if piece < n_mask:
                colmax = window_mask_piece(piece, colmax, zero)
            elif piece < n_mask + n_exp:
                if piece == n_mask:
                    m_w = jnp.max(colmax, axis=0, keepdims=True)
                window_exp_piece(piece - n_mask, m_w, zero)
            piece += 1
    wtile = base // LANES
    vw = jnp.concatenate([vwt_ref[0, 0, wtile + c] for c in range(wk // LANES)], axis=1)
    cw_scr[...] += grow(2) * _normalize_dup(_dot(vw, pw_scr[...]))
    kth = key_to_float(thr)
    above = impt > kth
    tied = impt == kth
    n_above = jnp.sum(jnp.where(above, 1.0, 0.0), axis=0, keepdims=True)
    scol = lax.broadcasted_iota(jnp.int32, (N_SLC_LANES, N_SLC_LANES), 1)
    earlier = jnp.where(scol < srow, 1.0, 0.0).astype(BF16)
    tied_before = _dot(earlier, jnp.where(tied, 1.0, 0.0).astype(BF16))
    selected = above | (tied & (tied_before < N_SELECT - n_above))
    bias = jnp.where(selected, 0.0, NEG)
    n_main = start // SEL_TILE
    bias_main = jnp.where(srow >= n_main * (SEL_TILE // SLC_BLOCK), NEG, bias)
    widen = lambda b: jnp.concatenate([b.astype(BF16)] * Q_PER_KV, axis=1)
    qt_tail = jnp.concatenate([qt, widen(bias)], axis=0)
    qt_main = jnp.concatenate([qt, widen(bias_main)], axis=0)

    kt = pl.multiple_of(n_main * SEL_TILE, SEL_TILE)
    s_t = _dot(ksel_ref[0, 0, pl.ds(kt, SEL_TILE), :], qt_tail)
    krow = lax.broadcasted_iota(jnp.int32, (SEL_TILE, COLS), 0)
    s_t = jnp.where(kt + krow <= tq, s_t, NEG)
    m_t = jnp.max(s_t, axis=0, keepdims=True)
    acct_scr[...] = _dot(vst_ref[0, 0, n_main], jnp.exp2(s_t - m_t).astype(BF16))
    g_sel = grow(1)

    def scores(j, slot):
        k0 = pl.multiple_of(jnp.minimum(j, last_tile) * SEL_TILE, SEL_TILE)
        s_all[slot] = _dot(ksel_ref[0, 0, pl.ds(k0, SEL_TILE), :], qt_main)

    def softmax(slot):
        s = s_all[slot]
        m_prev = m_scr[0:1, :]
        m_new = jnp.maximum(m_prev, jnp.max(s, axis=0, keepdims=True))
        a_all[slot] = jnp.broadcast_to(jnp.exp2(m_prev - m_new), (SUBLANES, COLS))
        p_all[slot] = jnp.exp2(s - m_new).astype(BF16)
        m_scr[...] = jnp.broadcast_to(m_new, m_scr.shape)

    def values(j, slot):
        acc_scr[...] = (a_all[slot, 0:1, :] * acc_scr[...]
                        + _dot(vst_ref[0, 0, jnp.minimum(j, last_tile)], p_all[slot]))

    m_scr[...] = jnp.full(m_scr.shape, NEG, F32)
    acc_scr[...] = jnp.zeros(acc_scr.shape, F32)
    for k in range(PIPE):
        scores(k, k)
    for k in range(PIPE // 2):
        softmax(k)

    def sweep(t, carry):
        for k in range(SWEEP_UNROLL):
            values(SWEEP_UNROLL * t + k, k % PIPE)
            softmax((k + PIPE // 2) % PIPE)
            scores(SWEEP_UNROLL * t + k + PIPE, k % PIPE)
        return carry

    lax.fori_loop(0, (n_main + SWEEP_UNROLL - 1) // SWEEP_UNROLL, sweep, 0)
    m_p = m_scr[0:1, :]
    m_tot = jnp.maximum(m_p, m_t)
    o_s = _normalize_dup(jnp.exp2(m_p - m_tot) * acc_scr[...] + jnp.exp2(m_t - m_tot) * acct_scr[...])

    comb = cw_scr[...] + g_sel * o_s
    outs = [comb[:, r * Q_BLOCK:(r + 1) * Q_BLOCK].T for r in range(Q_PER_KV)]
    o_ref[0] = jnp.concatenate([jnp.where(lo_half, outs[0], outs[1]), jnp.where(lo_half, outs[2], outs[3])],
                               axis=1)


def _attn_single(qpt, ksel, vst, kw, vwt, kcmp, vcmpt, c2st, gatest):
    batch, seq = qpt.shape[0], qpt.shape[1] * Q_BLOCK
    per_group = lambda a: pl.BlockSpec((1, 1) + a.shape[2:], lambda b, g, i: (b, g) + (0,) * (a.ndim - 2))
    return pl.pallas_call(
        _attn_kernel_single,
        grid=(batch, N_KV_HEADS, seq // Q_BLOCK),
        in_specs=[pl.BlockSpec((1, 1, Q_PER_KV // 2, LANES, Q_BLOCK), lambda b, g, i: (b, i, g, 0, 0)),
                  per_group(ksel), per_group(vst), per_group(kw), per_group(vwt), per_group(kcmp),
                  per_group(vcmpt), pl.BlockSpec(c2st.shape, lambda b, g, i: (0, 0)),
                  pl.BlockSpec((1, 1, 1, GATE_ROWS, Q_BLOCK), lambda b, g, i: (b, i, g, 0, 0))],
        out_specs=pl.BlockSpec((1, Q_BLOCK, Q_PER_KV * HEAD_DIM), lambda b, g, i: (b, i, g)),
        out_shape=jax.ShapeDtypeStruct((batch, seq, D_ATT), F32),
        scratch_shapes=[pltpu.VMEM((PIPE, SEL_TILE, COLS), F32), pltpu.VMEM((PIPE, SEL_TILE, COLS), BF16),
                        pltpu.VMEM((PIPE, SUBLANES, COLS), F32),
                        pltpu.VMEM((SUBLANES, COLS), F32), pltpu.VMEM((V_ROWS, COLS), F32),
                        pltpu.VMEM((V_ROWS, COLS), F32), pltpu.VMEM((LANES, COLS), F32),
                        pltpu.VMEM((WINDOW + Q_BLOCK, COLS), F32), pltpu.VMEM((WINDOW + Q_BLOCK, COLS), BF16)],
        compiler_params=pltpu.CompilerParams(dimension_semantics=("arbitrary", "arbitrary", "arbitrary"),
                                             vmem_limit_bytes=VMEM_LIMIT_BYTES),
        name="attn",
    )(qpt, ksel, vst, kw, vwt, kcmp, vcmpt, c2st, gatest)


def _attn_kernel(q_ref, ksel_ref, vst_ref, kw_ref, vwt_ref, kcmp_ref, vcmpt_ref, c2st_ref, gate_ref, o_ref,
                 s_all, p_all, a_all, cm_all, m_scr, acc_scr, acct_scr, cw_scr, sw_scr, pw_scr):
    i = pl.program_id(1)
    starts = [(Q_PER_STEP * i + d) * Q_BLOCK for d in range(Q_PER_STEP)]
    kv = lambda g: g % N_KV_HEADS
    qb = lambda g: g // N_KV_HEADS
    nch = kcmp_ref.shape[2]
    last_tile = ksel_ref.shape[2] // SEL_TILE - 1
    groups = range(Q_PER_STEP * N_KV_HEADS)

    lane = lax.broadcasted_iota(jnp.int32, (Q_BLOCK, LANES), 1)
    lo_half = lane < HEAD_DIM
    top_half = lax.broadcasted_iota(jnp.int32, (LANES, Q_BLOCK), 0) < HEAD_DIM
    zero_bf = jnp.zeros((LANES, Q_BLOCK), BF16)
    tqs = [st + (lax.broadcasted_iota(jnp.int32, (1, COLS), 1) & (Q_BLOCK - 1)) for st in starts]
    wk = WINDOW + Q_BLOCK
    bases = [pl.multiple_of(jnp.maximum(st - WINDOW, 0), Q_BLOCK) for st in starts]
    wrow = lax.broadcasted_iota(jnp.int32, (WIN_MASK_ROWS, COLS), 0)
    nrow = lax.broadcasted_iota(jnp.int32, (nch, COLS), 0)
    srow = lax.broadcasted_iota(jnp.int32, (N_SLC_LANES, Q_BLOCK), 0)
    tqqs = [st + lax.broadcasted_iota(jnp.int32, (N_SLC_LANES, Q_BLOCK), 1) for st in starts]
    c2st = c2st_ref[...]
    n_main = starts[0] // SEL_TILE

    qt = [jnp.concatenate([jnp.where(top_half if r % 2 == 0 else ~top_half,
                                     q_ref[0, qb(g), kv(g) * (Q_PER_KV // 2) + r // 2], zero_bf)
                           for r in range(Q_PER_KV)], axis=1) for g in groups]

    def grow(g, b):
        gt = gate_ref[0, qb(g), kv(g)]
        return jnp.concatenate([gt[r * N_BRANCH + b:r * N_BRANCH + b + 1, :] for r in range(Q_PER_KV)], axis=1)

    s_c = [_dot(kcmp_ref[0, kv(g)], qt[g]) for g in groups]
    for g in groups:
        sw_scr[g] = _dot(kw_ref[0, kv(g), pl.ds(bases[qb(g)], wk), :], qt[g])

    last_valid = [lax.shift_right_arithmetic(t - (CMP_BLOCK - 1), CMP_SHIFT) for t in tqs]
    curs = [lax.shift_right_logical(t, SLC_SHIFT) for t in tqqs]
    forced = [(srow == 0) | (srow == c) | (srow == c - 1) for c in curs]
    future = [srow * SLC_BLOCK > t for t in tqqs]
    impt = []
    for g in groups:
        sc = jnp.where(nrow <= last_valid[qb(g)], s_c[g], NEG)
        m_c = jnp.max(sc, axis=0, keepdims=True)
        e_c = jnp.exp2(sc - m_c)
        l_c = jnp.sum(e_c, axis=0, keepdims=True)
        p_c = e_c * jnp.where(m_c > 0.5 * NEG, 1.0 / l_c, 0.0)
        cw_scr[g] = grow(g, 0) * _dot(vcmpt_ref[0, kv(g)], p_c.astype(BF16))
        psum = sum(p_c[:, r * Q_BLOCK:(r + 1) * Q_BLOCK] for r in range(Q_PER_KV))
        imp = sum(_dot(c2st, t) for t in _split2(psum))
        impt.append(jnp.where(forced[qb(g)], 1e9, jnp.where(future[qb(g)], -1e9, imp)))

    def window_mask_piece(g, k, colmax, zero):
        rows = slice(WIN_MASK_ROWS * k, WIN_MASK_ROWS * (k + 1))
        back = (tqs[qb(g)] + zero - bases[qb(g)] - WIN_MASK_ROWS * k) - wrow
        in_window = lax.bitcast_convert_type(back, jnp.uint32) < jnp.uint32(WINDOW)
        sm = jnp.where(in_window, sw_scr[g, rows, :], NEG)
        sw_scr[g, rows, :] = sm
        for c in range(WIN_MASK_ROWS // SUBLANES):
            colmax = jnp.maximum(colmax, sm[c * SUBLANES:(c + 1) * SUBLANES])
        return colmax

    def window_exp_piece(g, k, m_w, zero):
        rows = slice(WIN_EXP_ROWS * k, WIN_EXP_ROWS * (k + 1))
        pw_scr[g, rows, :] = jnp.exp2(sw_scr[g, rows, :] - (m_w + zero.astype(F32))).astype(BF16)

    key_to_float = lambda k: lax.bitcast_convert_type(jnp.where(k >= 0, k, k ^ jnp.int32(0x7FFFFFFF)), F32)
    wrap32 = lambda v: ((v + 2 ** 31) % 2 ** 32) - 2 ** 31
    n_mask, n_exp = wk // WIN_MASK_ROWS, wk // WIN_EXP_ROWS
    thr = [jnp.full((1, Q_BLOCK), INT32_MIN, jnp.int32) for _ in groups]
    colmax = [jnp.full((SUBLANES, COLS), NEG, F32) for _ in groups]
    m_w = [None for _ in groups]
    piece = 0
    for b in range(32 - RADIX_BITS, -1, -RADIX_BITS):
        for g in groups:
            reached = thr[g]
            for mult in range(1, 2 ** RADIX_BITS):
                cand = thr[g] + jnp.int32(wrap32(mult << b))
                n_ge = jnp.sum(jnp.where(impt[g] >= key_to_float(cand), 1.0, 0.0), axis=0, keepdims=True)
                reached = jnp.where(n_ge >= N_SELECT, cand, reached)
            thr[g] = reached
        for _ in range(RADIX_BITS):
            for g in groups:
                zero = jnp.concatenate([_zero_after(thr[g])] * Q_PER_KV, axis=1)
                if piece < n_mask:
                    colmax[g] = window_mask_piece(g, piece, colmax[g], zero)
                elif piece < n_mask + n_exp:
                    if piece == n_mask:
                        m_w[g] = jnp.max(colmax[g], axis=0, keepdims=True)
                    window_exp_piece(g, piece - n_mask, m_w[g], zero)
            piece += 1

    scol = lax.broadcasted_iota(jnp.int32, (N_SLC_LANES, N_SLC_LANES), 1)
    earlier = jnp.where(scol < srow, 1.0, 0.0).astype(BF16)
    widen = lambda x: jnp.concatenate([x.astype(BF16)] * Q_PER_KV, axis=1)
    qt_tail, qt_main = [], []
    for g in groups:
        vw = jnp.concatenate([vwt_ref[0, kv(g), bases[qb(g)] // LANES + c] for c in range(wk // LANES)],
                             axis=1)
        cw_scr[g] += grow(g, 2) * _normalize_dup(_dot(vw, pw_scr[g]))
        kth = key_to_float(thr[g])
        above = impt[g] > kth
        tied = impt[g] == kth
        n_above = jnp.sum(jnp.where(above, 1.0, 0.0), axis=0, keepdims=True)
        tied_before = _dot(earlier, jnp.where(tied, 1.0, 0.0).astype(BF16))
        selected = above | (tied & (tied_before < N_SELECT - n_above))
        bias = jnp.where(selected, 0.0, NEG)
        bias_main = jnp.where(srow >= n_main * (SEL_TILE // SLC_BLOCK), NEG, bias)
        qt_tail.append(jnp.concatenate([qt[g], widen(bias)], axis=0))
        qt_main.append(jnp.concatenate([qt[g], widen(bias_main)], axis=0))

    kt = pl.multiple_of(n_main * SEL_TILE, SEL_TILE)
    krow = lax.broadcasted_iota(jnp.int32, (SEL_TILE, COLS), 0)
    for g in groups:
        s_all[g, 0] = _dot(ksel_ref[0, kv(g), pl.ds(kt, SEL_TILE), :], qt_tail[g])
    m_t = []
    for g in groups:
        s_t = jnp.where(kt + krow <= tqs[qb(g)], s_all[g, 0], NEG)
        m_t.append(jnp.max(s_t, axis=0, keepdims=True))
        p_all[g, 0] = jnp.exp2(s_t - m_t[g]).astype(BF16)
    for g in groups:
        acct_scr[g] = _dot(vst_ref[0, kv(g), n_main], p_all[g, 0])
    g_sel = [grow(g, 1) for g in groups]

    def scores(g, j, slot):
        k0 = pl.multiple_of(jnp.minimum(j, last_tile) * SEL_TILE, SEL_TILE)
        sv = _dot(ksel_ref[0, kv(g), pl.ds(k0, SEL_TILE), :], qt_main[g])
        s_all[g, slot] = sv
        cm = sv[0:SUBLANES]
        for c in range(1, SEL_TILE // SUBLANES):
            cm = jnp.maximum(cm, sv[c * SUBLANES:(c + 1) * SUBLANES])
        cm_all[g, slot] = cm

    def softmax(g, slot):
        s = s_all[g, slot]
        m_prev = m_scr[g, 0:1, :]
        m_new = jnp.maximum(m_prev, jnp.max(cm_all[g, slot], axis=0, keepdims=True))
        a_all[g, slot] = jnp.broadcast_to(jnp.exp2(m_prev - m_new), (SUBLANES, COLS))
        p_all[g, slot] = jnp.exp2(s - m_new).astype(BF16)
        m_scr[g] = jnp.broadcast_to(m_new, (SUBLANES, COLS))

    def values(g, j, slot):
        acc_scr[g] = (a_all[g, slot, 0:1, :] * acc_scr[g]
                      + _dot(vst_ref[0, kv(g), jnp.minimum(j, last_tile)], p_all[g, slot]))

    m_scr[...] = jnp.full(m_scr.shape, NEG, F32)
    acc_scr[...] = jnp.zeros(acc_scr.shape, F32)
    for k in range(PIPE):
        for g in groups:
            scores(g, k, k)
    for k in range(PIPE // 2):
        for g in groups:
            softmax(g, k)

    def sweep(t, carry):
        for k in range(SWEEP_UNROLL):
            for g in groups:
                values(g, SWEEP_UNROLL * t + k, k % PIPE)
                softmax(g, (k + PIPE // 2) % PIPE)
                scores(g, SWEEP_UNROLL * t + k + PIPE, k % PIPE)
        return carry

    lax.fori_loop(0, (n_main + SWEEP_UNROLL - 1) // SWEEP_UNROLL, sweep, 0)

    pairs = [[] for _ in range(Q_PER_STEP)]
    for g in groups:
        m_p = m_scr[g, 0:1, :]
        m_tot = jnp.maximum(m_p, m_t[g])
        o_s = _normalize_dup(jnp.exp2(m_p - m_tot) * acc_scr[g] + jnp.exp2(m_t[g] - m_tot) * acct_scr[g])
        comb = cw_scr[g] + g_sel[g] * o_s
        outs = [comb[:, r * Q_BLOCK:(r + 1) * Q_BLOCK].T for r in range(Q_PER_KV)]
        pairs[qb(g)] += [jnp.where(lo_half, outs[0], outs[1]), jnp.where(lo_half, outs[2], outs[3])]
    for d in range(Q_PER_STEP):
        o_ref[0, d * Q_BLOCK:(d + 1) * Q_BLOCK, :] = jnp.concatenate(pairs[d], axis=1)


def _attn(qpt, ksel, vst, kw, vwt, kcmp, vcmpt, c2st, gatest):
    batch, seq = qpt.shape[0], qpt.shape[1] * Q_BLOCK
    per_batch = lambda a: pl.BlockSpec((1,) + a.shape[1:], lambda b, i: (b,) + (0,) * (a.ndim - 1),
                                       pipeline_mode=pl.Buffered(1))
    per_block = lambda a: pl.BlockSpec((1, Q_PER_STEP) + a.shape[2:], lambda b, i: (b, i) + (0,) * (a.ndim - 2))
    chains = Q_PER_STEP * N_KV_HEADS
    grp = lambda *shape: pltpu.VMEM((chains,) + shape, F32)
    return pl.pallas_call(
        _attn_kernel,
        grid=(batch, seq // (Q_PER_STEP * Q_BLOCK)),
        in_specs=[per_block(qpt), per_batch(ksel), per_batch(vst), per_batch(kw), per_batch(vwt), per_batch(kcmp),
                  per_batch(vcmpt), pl.BlockSpec(c2st.shape, lambda b, i: (0, 0)), per_block(gatest)],
        out_specs=pl.BlockSpec((1, Q_PER_STEP * Q_BLOCK, D_ATT), lambda b, i: (b, i, 0)),
        out_shape=jax.ShapeDtypeStruct((batch, seq, D_ATT), F32),
        scratch_shapes=[grp(PIPE, SEL_TILE, COLS), pltpu.VMEM((chains, PIPE, SEL_TILE, COLS), BF16),
                        grp(PIPE, SUBLANES, COLS), grp(PIPE, SUBLANES, COLS), grp(SUBLANES, COLS),
                        grp(V_ROWS, COLS), grp(V_ROWS, COLS),
                        grp(LANES, COLS), grp(WINDOW + Q_BLOCK, COLS),
                        pltpu.VMEM((chains, WINDOW + Q_BLOCK, COLS), BF16)],
        compiler_params=pltpu.CompilerParams(dimension_semantics=("arbitrary", "arbitrary"),
                                             vmem_limit_bytes=VMEM_LIMIT_BYTES),
        name="attn",
    )(qpt, ksel, vst, kw, vwt, kcmp, vcmpt, c2st, gatest)


POST_SUB = 2


def _post_kernel(x_ref, oatt_ref, mconv_ref, p_ref, onag_ref, wo_ref, lnf_ref, wup_ref, fcw_ref, fcb_ref,
                 wdn_ref, lnp_ref, wpg_ref, wpe_ref, out_ref, gbuf, *, tiles_per_seq):
    tm = x_ref.shape[0]
    sub = tm // POST_SUB
    it = pl.program_id(0) % tiles_per_seq
    subs = [slice(k * sub, (k + 1) * sub) for k in range(POST_SUB)]

    @pl.when(it == 0)
    def _():
        gbuf[0:SUBLANES, :] = jnp.zeros((SUBLANES, D_FF), F32)

    h1 = [x_ref[r, :] + _dot(jnp.concatenate([_rms(oatt_ref[r, :], onag_ref[...]).astype(BF16), mconv_ref[r, :]],
                                             axis=1), wo_ref[...]) for r in subs]
    xn = [_rms(h, lnf_ref[...]).astype(BF16) for h in h1]
    h2 = []
    for k, r in enumerate(subs):
        gpre = _dot(xn[k], wup_ref[:, :D_FF])
        up = _dot(xn[k], wup_ref[:, D_FF:])
        lo = SUBLANES + k * sub
        gbuf[lo:lo + sub, :] = gpre
        gate = (fcw_ref[2:3, :] * gpre + fcw_ref[1:2, :] * gbuf[lo - 1:lo - 1 + sub, :]
                + fcw_ref[0:1, :] * gbuf[lo - 2:lo - 2 + sub, :]) + fcb_ref[...]
        act = (gate * jax.nn.sigmoid(gate) * up).astype(BF16)
        h2.append(h1[k] + _dot(act, wdn_ref[...]))
    gbuf[0:SUBLANES, :] = gbuf[tm:tm + SUBLANES, :]
    for k, r in enumerate(subs):
        xn2 = _rms(h2[k], lnp_ref[...]).astype(BF16)
        out_ref[r, :] = (h2[k] + jax.nn.sigmoid(_dot(xn2, wpg_ref[...]))
                         * _dot(p_ref[r, :].astype(BF16), wpe_ref[...]))


def _post(x2, oatt, mconv, p2, onag, wo, lnf, wup, fcw, fcb, wdn, lnp, wpg, wpe, *, seq, tm):
    n = x2.shape[0]
    tps = seq // tm
    row = lambda w: pl.BlockSpec((tm, w), lambda i: (i, 0))
    full = lambda a: pl.BlockSpec(a.shape, lambda i: (0,) * a.ndim, pipeline_mode=pl.Buffered(1))
    return pl.pallas_call(
        functools.partial(_post_kernel, tiles_per_seq=tps),
        grid=(n // tm,),
        in_specs=[row(D_MODEL), row(D_ATT), row(D_CONV), row(D_PLE), full(onag), full(wo), full(lnf), full(wup),
                  full(fcw), full(fcb), full(wdn), full(lnp), full(wpg), full(wpe)],
        out_specs=row(D_MODEL),
        out_shape=jax.ShapeDtypeStruct((n, D_MODEL), F32),
        scratch_shapes=[pltpu.VMEM((tm + SUBLANES, D_FF), F32)],
        compiler_params=pltpu.CompilerParams(dimension_semantics=("arbitrary",),
                                             vmem_limit_bytes=VMEM_LIMIT_BYTES),
        name="post",
    )(x2, oatt, mconv, p2, onag, wo, lnf, wup, fcw, fcb, wdn, lnp, wpg, wpe)


def _rope_tables(pos):
    half = ROT_DIM // 2
    d = np.arange(LANES) % HEAD_DIM
    inv_freq = np.float64(ROPE_THETA) ** (-np.arange(half, dtype=np.float64) * 2.0 / ROT_DIM)
    ang = np.asarray(pos, np.float64)[:, None] * np.tile(inv_freq, LANES // half)[None, :]
    c, sn = np.cos(ang), np.sin(ang)
    first, second = (d < half)[None, :], ((d >= half) & (d < ROT_DIM))[None, :]
    cos = np.where(first | second, c, 1.0)
    sa = np.where(first, -sn, 0.0)
    sb = np.where(second, sn, 0.0)
    return tuple(jnp.asarray(t, F32) for t in (cos, sa, sb))


def _block_diag_mean(width):
    idx = np.arange(width) // HEAD_DIM
    return jnp.asarray((idx[:, None] == idx[None, :]).astype(np.float32) / HEAD_DIM, BF16)


def _cmp_to_slc_t(nch):
    cs = CMP_STRIDE * np.arange(nch)[None, :]
    ss = SLC_BLOCK * np.arange(N_SLC_LANES)[:, None]
    ov = np.clip(np.minimum(cs + CMP_BLOCK, ss + SLC_BLOCK) - np.maximum(cs, ss), 0, None)
    return jnp.asarray(ov.astype(np.float32) / CMP_BLOCK, BF16)


def _layer(h, p_l, ln_mix_g, w_in, qn_g, kn_g, pe_k, pe_v, w_ck1, w_ck2, w_cv1, w_cv2, conv_w, on_att_g,
           on_conv_g, w_o, ln_ffn_g, w_up, ffn_conv_w, ffn_conv_b, w_down, ln_ple_g, w_pg, w_pe):
    batch, seq, _ = h.shape
    assert seq % SEL_TILE == 0 and seq // SLC_BLOCK <= N_SLC_LANES and seq // SLC_BLOCK >= N_SELECT
    assert seq >= WINDOW + Q_BLOCK
    n = batch * seq
    nch = seq // CMP_STRIDE
    x2 = h.reshape(n, D_MODEL)
    row = lambda v: v.reshape(1, -1).astype(F32)

    o_q, o_kv, o_g, o_cv = 0, D_ATT, D_ATT + 6 * KV_W, D_ATT + 6 * KV_W + N_BRANCH * N_Q_HEADS
    wq = w_in[:, o_q:o_kv].astype(BF16)
    wkv = w_in[:, o_kv:o_g].astype(BF16)
    wcv = w_in[:, o_cv:].astype(BF16)
    per_g = Q_PER_KV * N_BRANCH
    wg = jnp.concatenate(
        [jnp.pad(w_in[:, o_g + g * per_g:o_g + (g + 1) * per_g], ((0, 0), (0, LANES - per_g)))
         for g in range(N_KV_HEADS)], axis=1).astype(BF16)
    cos, sa, sb = _rope_tables(np.arange(seq))
    bd512, bd128 = _block_diag_mean(D_ATT), _block_diag_mean(KV_W)
    tile_heads = lambda v, k: jnp.tile(v.astype(F32), k).reshape(1, -1)

    tm = ROW_TILE
    assert seq % ROW_TILE == 0 and seq % POST_TILE == 0
    qpt, kc2, vc2, ksel, vst, kw, vwt, gatest, mconv = _inproj(
        x2, row(ln_mix_g), wq, wkv, wg, wcv, tile_heads(qn_g, N_Q_HEADS),
        jnp.stack([jnp.tile(kn_g[1], N_KV_HEADS), jnp.tile(kn_g[2], N_KV_HEADS)]).astype(F32),
        bd512, bd128, cos, sa, sb, conv_w.astype(F32), row(on_conv_g), batch=batch, seq=seq, tm=tm)

    assert N_KV_HEADS == 2
    half = CMP_BLOCK // 2

    def w1_parts(w1):
        w = w1.astype(BF16)
        z = jnp.zeros_like(w)
        wfull = jnp.concatenate([jnp.concatenate([w, z], axis=2), jnp.concatenate([z, w], axis=2)], axis=1)
        return wfull[:half].reshape(half * KV_W, -1), wfull[half:].reshape(half * KV_W, -1)

    def pe_parts(pe):
        pf = jnp.broadcast_to(pe[:, None, :], (CMP_BLOCK, N_KV_HEADS, HEAD_DIM)).astype(F32)
        return pf[:half].reshape(1, -1), pf[half:].reshape(1, -1)

    def w2bd(w2):
        w = w2.astype(BF16)
        z = jnp.zeros_like(w)
        return jnp.concatenate([jnp.concatenate([w, z], axis=1), jnp.concatenate([z, w], axis=1)], axis=0)
    w1ak, w1bk = w1_parts(w_ck1)
    w1av, w1bv = w1_parts(w_cv1)
    peak, pebk = pe_parts(pe_k)
    peav, pebv = pe_parts(pe_v)
    ccos, csa, csb = _rope_tables(CMP_STRIDE * np.arange(nch) + CMP_BLOCK - 1)
    kcmp, vcmpt = _compress(kc2.reshape(batch, seq, KV_W), vc2.reshape(batch, seq, KV_W),
                            w1ak, w1bk, w1av, w1bv, peak, pebk, peav, pebv, w2bd(w_ck2), w2bd(w_cv2),
                            tile_heads(kn_g[0], N_KV_HEADS), bd128, ccos, csa, csb)

    oatt = _attn(qpt, ksel, vst, kw, vwt, kcmp, vcmpt, _cmp_to_slc_t(nch), gatest)

    out = _post(x2, oatt.reshape(n, D_ATT), mconv, p_l.reshape(n, D_PLE), row(on_att_g), w_o.astype(BF16),
                row(ln_ffn_g), w_up.astype(BF16), ffn_conv_w.astype(F32), row(ffn_conv_b), w_down.astype(BF16),
                row(ln_ple_g), w_pg.astype(BF16), w_pe.astype(BF16), seq=seq, tm=POST_TILE)
    return out.reshape(batch, seq, D_MODEL)


def kernel(x, p, ln_mix_g, w_in, qn_g, kn_g, pe_k, pe_v, w_ck1, w_ck2, w_cv1, w_cv2, conv_w, on_att_g,
           on_conv_g, w_o, ln_ffn_g, w_up, ffn_conv_w, ffn_conv_b, w_down, ln_ple_g, w_pg, w_pe):
    h = x
    for i in range(p.shape[0]):
        h = _layer(h, p[i], ln_mix_g[i], w_in[i], qn_g[i], kn_g[i], pe_k[i], pe_v[i], w_ck1[i], w_ck2[i],
                   w_cv1[i], w_cv2[i], conv_w[i], on_att_g[i], on_conv_g[i], w_o[i], ln_ffn_g[i], w_up[i],
                   ffn_conv_w[i], ffn_conv_b[i], w_down[i], ln_ple_g[i], w_pg[i], w_pe[i])
    return h
```

```python
import functools

import jax
import jax.numpy as jnp
import numpy as np
from jax import lax
from jax.experimental import pallas as pl
from jax.experimental.pallas import tpu as pltpu

D_MODEL = 1024
HEAD_DIM = 64
N_Q_HEADS = 8
N_KV_HEADS = 2
Q_PER_KV = N_Q_HEADS // N_KV_HEADS
D_ATT = N_Q_HEADS * HEAD_DIM
D_CONV = D_MODEL - D_ATT
KV_W = N_KV_HEADS * HEAD_DIM
N_BRANCH = 3
CONV_TAPS = 3
ROT_DIM = HEAD_DIM // 4
ROPE_THETA = 500000.0
CMP_BLOCK = 32
CMP_STRIDE = 16
CMP_HIDDEN = 256
SLC_BLOCK = 64
N_SELECT = 16
WINDOW = 512
Q_BLOCK = 128
D_FF = 2816
D_PLE = 256
EPS = 1e-6
NEG = -1e30
INT32_MIN = -2 ** 31
LOG2E = float(np.log2(np.e))
V_ROWS = HEAD_DIM + 16
GATE_ROWS = 16
SLC_SHIFT = SLC_BLOCK.bit_length() - 1
CMP_SHIFT = CMP_STRIDE.bit_length() - 1
assert 1 << SLC_SHIFT == SLC_BLOCK and 1 << CMP_SHIFT == CMP_STRIDE

LANES = 128
SUBLANES = 8
N_SLC_LANES = LANES
VMEM_LIMIT_BYTES = 56 * 1024 * 1024

F32 = jnp.float32
BF16 = jnp.bfloat16


def _dot(a, b):
    return jnp.dot(a, b, preferred_element_type=F32)


def _rms(x, g):
    return x * lax.rsqrt(jnp.mean(x * x, axis=-1, keepdims=True) + EPS) * g


def _head_rms_rope(x, g, bd, cos, sa, sb):
    w = x.shape[-1]
    msq = _dot((x * x).astype(BF16), bd)
    xn = x * lax.rsqrt(msq + EPS) * g
    return xn * cos + pltpu.roll(xn, w - ROT_DIM // 2, 1) * sa + pltpu.roll(xn, ROT_DIM // 2, 1) * sb


def _dup_halves(x):
    r = pltpu.roll(x, HEAD_DIM, 1)
    lane = lax.broadcasted_iota(jnp.int32, x.shape, 1)
    lo = lane < HEAD_DIM
    return jnp.where(lo, x, r), jnp.where(lo, r, x)


def _inproj_kernel(x_ref, lng_ref, wq_ref, wkv_ref, wg_ref, wcv_ref, qng_ref, kng_ref, bd512_ref, bd128_ref,
                   cos_ref, sa_ref, sb_ref, convw_ref, oncg_ref,
                   q_out, kc_out, vc_out, ksel_out, vs_out, kw_out, vw_out, gate_out, mconv_out,
                   zbuf, *, tiles_per_seq):
    tm = x_ref.shape[0]
    sub = tm // IN_SUB
    it = pl.program_id(0) % tiles_per_seq

    @pl.when(it == 0)
    def _():
        zbuf[0:SUBLANES, :] = jnp.zeros((SUBLANES, D_CONV), F32)

    for k in range(IN_SUB):
        r = slice(k * sub, (k + 1) * sub)
        xn = _rms(x_ref[r, :], lng_ref[...]).astype(BF16)

        cos, sa, sb = cos_ref[r, :], sa_ref[r, :], sb_ref[r, :]
        cos4, sa4, sb4 = (jnp.concatenate([t] * 4, axis=1) for t in (cos, sa, sb))
        q = _dot(xn, wq_ref[...])
        qr = _head_rms_rope(q, qng_ref[...], bd512_ref[...], cos4, sa4, sb4)
        qs = qr * (HEAD_DIM ** -0.5 * LOG2E)
        for blk_i in range(sub // Q_BLOCK):
            rows = slice(blk_i * Q_BLOCK, (blk_i + 1) * Q_BLOCK)
            for pr in range(N_Q_HEADS // 2):
                q_out[0, k * (sub // Q_BLOCK) + blk_i, pr] = qs[rows, pr * LANES:(pr + 1) * LANES].T.astype(BF16)

        kv = _dot(xn, wkv_ref[...])
        kc_out[r, :] = kv[:, 0 * KV_W:1 * KV_W]
        vc_out[r, :] = kv[:, 1 * KV_W:2 * KV_W]
        ks = _head_rms_rope(kv[:, 2 * KV_W:3 * KV_W], kng_ref[0:1, :], bd128_ref[...], cos, sa, sb)
        kw = _head_rms_rope(kv[:, 4 * KV_W:5 * KV_W], kng_ref[1:2, :], bd128_ref[...], cos, sa, sb)
        vs = kv[:, 3 * KV_W:4 * KV_W]
        vw = kv[:, 5 * KV_W:6 * KV_W]
        tpos = it * tm + k * sub + lax.broadcasted_iota(jnp.int32, (sub, N_SLC_LANES), 0)
        blk = lax.broadcasted_iota(jnp.int32, (sub, N_SLC_LANES), 1)
        onehot = jnp.where(lax.shift_right_logical(tpos, SLC_SHIFT) == blk, 1.0, 0.0).astype(BF16)
        ks_d, vs_d, kw_d, vw_d = (_dup_halves(t) for t in (ks, vs, kw, vw))
        vrow = lax.broadcasted_iota(jnp.int32, (V_ROWS, sub), 0)
        for g in range(N_KV_HEADS):
            ksel_out[0, g, r, :] = jnp.concatenate([ks_d[g].astype(BF16), onehot], axis=1)
            kw_out[0, g, r, :] = kw_d[g].astype(BF16)
            vst = jnp.where(vrow < HEAD_DIM, vs_d[g].T[:V_ROWS], 1.0).astype(BF16)
            for c in range(sub // SEL_TILE):
                vs_out[0, g, k * (sub // SEL_TILE) + c] = vst[:, c * SEL_TILE:(c + 1) * SEL_TILE]
            vwt = jnp.where(vrow < HEAD_DIM, vw_d[g].T[:V_ROWS], 1.0).astype(BF16)
            for c in range(sub // LANES):
                vw_out[0, g, k * (sub // LANES) + c] = vwt[:, c * LANES:(c + 1) * LANES]

        gates = jax.nn.sigmoid(_dot(xn, wg_ref[...]))
        for blk_i in range(sub // Q_BLOCK):
            rows = slice(blk_i * Q_BLOCK, (blk_i + 1) * Q_BLOCK)
            for g in range(N_KV_HEADS):
                gate_out[0, k * (sub // Q_BLOCK) + blk_i, g] = gates[rows, g * LANES:(g + 1) * LANES].T[:GATE_ROWS]

        cv = _dot(xn, wcv_ref[...])
        cb, cc, cx = cv[:, :D_CONV], cv[:, D_CONV:2 * D_CONV], cv[:, 2 * D_CONV:]
        z = cc * cx
        lo = SUBLANES + k * sub
        zbuf[lo:lo + sub, :] = z
        y = (convw_ref[2:3, :] * z + convw_ref[1:2, :] * zbuf[lo - 1:lo - 1 + sub, :]
             + convw_ref[0:1, :] * zbuf[lo - 2:lo - 2 + sub, :])
        mconv_out[r, :] = _rms(cb * y, oncg_ref[...]).astype(BF16)
    zbuf[0:SUBLANES, :] = zbuf[tm:tm + SUBLANES, :]


def _inproj(x2, lng, wq, wkv, wg, wcv, qng, kng, bd512, bd128, cos, sa, sb, convw, oncg, *, batch, seq, tm):
    n = batch * seq
    tps = seq // tm
    row = lambda w: pl.BlockSpec((tm, w), lambda i: (i, 0))
    full = lambda a: pl.BlockSpec(a.shape, lambda i: (0,) * a.ndim)
    tab = pl.BlockSpec((tm, LANES), lambda i: (i % tps, 0))
    grp = lambda w: pl.BlockSpec((1, N_KV_HEADS, tm, w), lambda i: (i // tps, 0, i % tps, 0))
    gshape = lambda w: jax.ShapeDtypeStruct((batch, N_KV_HEADS, seq, w), BF16)
    assert tm % (IN_SUB * SEL_TILE) == 0
    vtile = lambda keys: pl.BlockSpec((1, N_KV_HEADS, tm // keys, V_ROWS, keys),
                                      lambda i: (i // tps, 0, i % tps, 0, 0))
    vshape = lambda keys: jax.ShapeDtypeStruct((batch, N_KV_HEADS, seq // keys, V_ROWS, keys), BF16)
    qblk = lambda tile: pl.BlockSpec((1, tm // Q_BLOCK) + tile, lambda i: (i // tps, i % tps) + (0,) * len(tile))
    return pl.pallas_call(
        functools.partial(_inproj_kernel, tiles_per_seq=tps),
        grid=(n // tm,),
        in_specs=[row(D_MODEL), full(lng), full(wq), full(wkv), full(wg), full(wcv), full(qng), full(kng),
                  full(bd512), full(bd128), tab, tab, tab, full(convw), full(oncg)],
        out_specs=[qblk((N_Q_HEADS // 2, LANES, Q_BLOCK)), row(KV_W), row(KV_W), grp(2 * LANES), vtile(SEL_TILE),
                   grp(LANES), vtile(LANES), qblk((N_KV_HEADS, GATE_ROWS, Q_BLOCK)), row(D_CONV)],
        out_shape=[jax.ShapeDtypeStruct((batch, seq // Q_BLOCK, N_Q_HEADS // 2, LANES, Q_BLOCK), BF16),
                   jax.ShapeDtypeStruct((n, KV_W), F32),
                   jax.ShapeDtypeStruct((n, KV_W), F32), gshape(2 * LANES), vshape(SEL_TILE), gshape(LANES),
                   vshape(LANES),
                   jax.ShapeDtypeStruct((batch, seq // Q_BLOCK, N_KV_HEADS, GATE_ROWS, Q_BLOCK), F32),
                   jax.ShapeDtypeStruct((n, D_CONV), BF16)],
        scratch_shapes=[pltpu.VMEM((tm + SUBLANES, D_CONV), F32)],
        compiler_params=pltpu.CompilerParams(dimension_semantics=("arbitrary",),
                                             vmem_limit_bytes=VMEM_LIMIT_BYTES),
        name="inproj",
    )(x2, lng, wq, wkv, wg, wcv, qng, kng, bd512, bd128, cos, sa, sb, convw, oncg)


def _gelu_tanh(x):
    return 0.5 * x * (1.0 + jnp.tanh(np.sqrt(2.0 / np.pi).astype(np.float32) * (x + 0.044715 * (x * x * x))))


def _compress_kernel(kc_ref, vc_ref, w1ak_ref, w1bk_ref, w1av_ref, w1bv_ref, peak_ref, pebk_ref, peav_ref,
                     pebv_ref, w2k_ref, w2v_ref, kng_ref, bd128_ref, cos_ref, sa_ref, sb_ref,
                     kcmp_out, vcmp_out):
    nch = kc_ref.shape[1] // CMP_STRIDE

    def chunks(ref):
        return jnp.concatenate([ref[0, pl.ds(l, nch, stride=CMP_STRIDE), :] for l in range(CMP_STRIDE)], axis=1)

    def compress(x, w1a, w1b, pea, peb, w2):
        a = _dot((x + pea).astype(BF16), w1a)
        b = _dot((x + peb).astype(BF16), w1b)
        hid = _gelu_tanh(a + pltpu.roll(b, nch - 1, 0))
        return _dot(hid.astype(BF16), w2)

    kc = compress(chunks(kc_ref), w1ak_ref[...], w1bk_ref[...], peak_ref[...], pebk_ref[...], w2k_ref[...])
    vc = compress(chunks(vc_ref), w1av_ref[...], w1bv_ref[...], peav_ref[...], pebv_ref[...], w2v_ref[...])
    kc = _head_rms_rope(kc, kng_ref[...], bd128_ref[...], cos_ref[...], sa_ref[...], sb_ref[...])
    kd, vd = _dup_halves(kc), _dup_halves(vc)
    for g in range(N_KV_HEADS):
        kcmp_out[0, g] = kd[g].astype(BF16)
        vcmp_out[0, g] = vd[g].T.astype(BF16)


def _compress(kc3, vc3, w1ak, w1bk, w1av, w1bv, peak, pebk, peav, pebv, w2k, w2v, kng0, bd128, cos, sa, sb):
    batch, seq, wide = kc3.shape
    nch = seq // CMP_STRIDE
    full = lambda a: pl.BlockSpec(a.shape, lambda b: (0,) * a.ndim)
    tok = pl.BlockSpec((1, seq, wide), lambda b: (b, 0, 0))
    kout = pl.BlockSpec((1, N_KV_HEADS, nch, LANES), lambda b: (b, 0, 0, 0))
    vout = pl.BlockSpec((1, N_KV_HEADS, LANES, nch), lambda b: (b, 0, 0, 0))
    consts = (w1ak, w1bk, w1av, w1bv, peak, pebk, peav, pebv, w2k, w2v, kng0, bd128, cos, sa, sb)
    return pl.pallas_call(
        _compress_kernel,
        grid=(batch,),
        in_specs=[tok, tok] + [full(a) for a in consts],
        out_specs=[kout, vout],
        out_shape=[jax.ShapeDtypeStruct((batch, N_KV_HEADS, nch, LANES), BF16),
                   jax.ShapeDtypeStruct((batch, N_KV_HEADS, LANES, nch), BF16)],
        compiler_params=pltpu.CompilerParams(dimension_semantics=("arbitrary",),
                                             vmem_limit_bytes=VMEM_LIMIT_BYTES),
        name="compress",
    )(kc3, vc3, *consts)


SEL_TILE = 256
COLS = Q_PER_KV * Q_BLOCK
ROW_TILE = 512
IN_SUB = 2
POST_TILE = 512
RADIX_BITS = 1
Q_PER_STEP = 2
PIPE = 2
SWEEP_UNROLL = 2
assert SWEEP_UNROLL % PIPE == 0 and SEL_TILE % (Q_PER_STEP * Q_BLOCK) == 0
WIN_MASK_ROWS = 128
WIN_EXP_ROWS = 128
assert (WINDOW + Q_BLOCK) // WIN_MASK_ROWS + (WINDOW + Q_BLOCK) // WIN_EXP_ROWS <= 32


def _split2(x):
    hi = x.astype(BF16)
    return hi, (x - hi.astype(F32)).astype(BF16)


def _zero_after(x):
    return lax.shift_right_logical(lax.shift_right_logical(x, 16), 16)


def _normalize_dup(acc):
    o = acc[:HEAD_DIM] / acc[HEAD_DIM:HEAD_DIM + 1]
    return jnp.concatenate([o, o], axis=0)


def _attn_kernel(q_ref, ksel_ref, vst_ref, kw_ref, vwt_ref, kcmp_ref, vcmpt_ref, c2st_ref, gate_ref, o_ref,
                 s_all, p_all, a_all, cm_all, m_scr, acc_scr, acct_scr, cw_scr, sw_scr, pw_scr):
    i = pl.program_id(1)
    starts = [(Q_PER_STEP * i + d) * Q_BLOCK for d in range(Q_PER_STEP)]
    kv = lambda g: g % N_KV_HEADS
    qb = lambda g: g // N_KV_HEADS
    nch = kcmp_ref.shape[2]
    last_tile = ksel_ref.shape[2] // SEL_TILE - 1
    groups = range(Q_PER_STEP * N_KV_HEADS)

    lane = lax.broadcasted_iota(jnp.int32, (Q_BLOCK, LANES), 1)
    lo_half = lane < HEAD_DIM
    top_half = lax.broadcasted_iota(jnp.int32, (LANES, Q_BLOCK), 0) < HEAD_DIM
    zero_bf = jnp.zeros((LANES, Q_BLOCK), BF16)
    tqs = [st + (lax.broadcasted_iota(jnp.int32, (1, COLS), 1) & (Q_BLOCK - 1)) for st in starts]
    wk = WINDOW + Q_BLOCK
    bases = [pl.multiple_of(jnp.maximum(st - WINDOW, 0), Q_BLOCK) for st in starts]
    wrow = lax.broadcasted_iota(jnp.int32, (WIN_MASK_ROWS, COLS), 0)
    nrow = lax.broadcasted_iota(jnp.int32, (nch, COLS), 0)
    srow = lax.broadcasted_iota(jnp.int32, (N_SLC_LANES, Q_BLOCK), 0)
    tqqs = [st + lax.broadcasted_iota(jnp.int32, (N_SLC_LANES, Q_BLOCK), 1) for st in starts]
    c2st = c2st_ref[...]
    n_main = starts[0] // SEL_TILE

    qt = [jnp.concatenate([jnp.where(top_half if r % 2 == 0 else ~top_half,
                                     q_ref[0, qb(g), kv(g) * (Q_PER_KV // 2) + r // 2], zero_bf)
                           for r in range(Q_PER_KV)], axis=1) for g in groups]

    def grow(g, b):
        gt = gate_ref[0, qb(g), kv(g)]
        return jnp.concatenate([gt[r * N_BRANCH + b:r * N_BRANCH + b + 1, :] for r in range(Q_PER_KV)], axis=1)

    s_c = [_dot(kcmp_ref[0, kv(g)], qt[g]) for g in groups]
    for g in groups:
        sw_scr[g] = _dot(kw_ref[0, kv(g), pl.ds(bases[qb(g)], wk), :], qt[g])

    last_valid = [lax.shift_right_arithmetic(t - (CMP_BLOCK - 1), CMP_SHIFT) for t in tqs]
    curs = [lax.shift_right_logical(t, SLC_SHIFT) for t in tqqs]
    forced = [(srow == 0) | (srow == c) | (srow == c - 1) for c in curs]
    future = [srow * SLC_BLOCK > t for t in tqqs]
    impt = []
    for g in groups:
        sc = jnp.where(nrow <= last_valid[qb(g)], s_c[g], NEG)
        m_c = jnp.max(sc, axis=0, keepdims=True)
        e_c = jnp.exp2(sc - m_c)
        l_c = jnp.sum(e_c, axis=0, keepdims=True)
        p_c = e_c * jnp.where(m_c > 0.5 * NEG, 1.0 / l_c, 0.0)
        cw_scr[g] = grow(g, 0) * _dot(vcmpt_ref[0, kv(g)], p_c.astype(BF16))
        psum = sum(p_c[:, r * Q_BLOCK:(r + 1) * Q_BLOCK] for r in range(Q_PER_KV))
        imp = sum(_dot(c2st, t) for t in _split2(psum))
        impt.append(jnp.where(forced[qb(g)], 1e9, jnp.where(future[qb(g)], -1e9, imp)))

    def window_mask_piece(g, k, colmax, zero):
        rows = slice(WIN_MASK_ROWS * k, WIN_MASK_ROWS * (k + 1))
        back = (tqs[qb(g)] + zero - bases[qb(g)] - WIN_MASK_ROWS * k) - wrow
        in_window = lax.bitcast_convert_type(back, jnp.uint32) < jnp.uint32(WINDOW)
        sm = jnp.where(in_window, sw_scr[g, rows, :], NEG)
        sw_scr[g, rows, :] = sm
        for c in range(WIN_MASK_ROWS // SUBLANES):
            colmax = jnp.maximum(colmax, sm[c * SUBLANES:(c + 1) * SUBLANES])
        return colmax

    def window_exp_piece(g, k, m_w, zero):
        rows = slice(WIN_EXP_ROWS * k, WIN_EXP_ROWS * (k + 1))
        pw_scr[g, rows, :] = jnp.exp2(sw_scr[g, rows, :] - (m_w + zero.astype(F32))).astype(BF16)

    key_to_float = lambda k: lax.bitcast_convert_type(jnp.where(k >= 0, k, k ^ jnp.int32(0x7FFFFFFF)), F32)
    wrap32 = lambda v: ((v + 2 ** 31) % 2 ** 32) - 2 ** 31
    n_mask, n_exp = wk // WIN_MASK_ROWS, wk // WIN_EXP_ROWS
    thr = [jnp.full((1, Q_BLOCK), INT32_MIN, jnp.int32) for _ in groups]
    colmax = [jnp.full((SUBLANES, COLS), NEG, F32) for _ in groups]
    m_w = [None for _ in groups]
    piece = 0
    for b in range(32 - RADIX_BITS, -1, -RADIX_BITS):
        for g in groups:
            reached = thr[g]
            for mult in range(1, 2 ** RADIX_BITS):
                cand = thr[g] + jnp.int32(wrap32(mult << b))
                n_ge = jnp.sum(jnp.where(impt[g] >= key_to_float(cand), 1.0, 0.0), axis=0, keepdims=True)
                reached = jnp.where(n_ge >= N_SELECT, cand, reached)
            thr[g] = reached
        for _ in range(RADIX_BITS):
            for g in groups:
                zero = jnp.concatenate([_zero_after(thr[g])] * Q_PER_KV, axis=1)
                if piece < n_mask:
                    colmax[g] = window_mask_piece(g, piece, colmax[g], zero)
                elif piece < n_mask + n_exp:
                    if piece == n_mask:
                        m_w[g] = jnp.max(colmax[g], axis=0, keepdims=True)
                    window_exp_piece(g, piece - n_mask, m_w[g], zero)
            piece += 1

    scol = lax.broadcasted_iota(jnp.int32, (N_SLC_LANES, N_SLC_LANES), 1)
    earlier = jnp.where(scol < srow, 1.0, 0.0).astype(BF16)
    widen = lambda x: jnp.concatenate([x.astype(BF16)] * Q_PER_KV, axis=1)
    qt_tail, qt_main = [], []
    for g in groups:
        vw = jnp.concatenate([vwt_ref[0, kv(g), bases[qb(g)] // LANES + c] for c in range(wk // LANES)],
                             axis=1)
        cw_scr[g] += grow(g, 2) * _normalize_dup(_dot(vw, pw_scr[g]))
        kth = key_to_float(thr[g])
        above = impt[g] > kth
        tied = impt[g] == kth
        n_above = jnp.sum(jnp.where(above, 1.0, 0.0), axis=0, keepdims=True)
        tied_before = _dot(earlier, jnp.where(tied, 1.0, 0.0).astype(BF16))
        selected = above | (tied & (tied_before < N_SELECT - n_above))
        bias = jnp.where(selected, 0.0, NEG)
        bias_main = jnp.where(srow >= n_main * (SEL_TILE // SLC_BLOCK), NEG, bias)
        qt_tail.append(jnp.concatenate([qt[g], widen(bias)], axis=0))
        qt_main.append(jnp.concatenate([qt[g], widen(bias_main)], axis=0))

    kt = pl.multiple_of(n_main * SEL_TILE, SEL_TILE)
    krow = lax.broadcasted_iota(jnp.int32, (SEL_TILE, COLS), 0)
    for g in groups:
        s_all[g, 0] = _dot(ksel_ref[0, kv(g), pl.ds(kt, SEL_TILE), :], qt_tail[g])
    m_t = []
    for g in groups:
        s_t = jnp.where(kt + krow <= tqs[qb(g)], s_all[g, 0], NEG)
        m_t.append(jnp.max(s_t, axis=0, keepdims=True))
        p_all[g, 0] = jnp.exp2(s_t - m_t[g]).astype(BF16)
    for g in groups:
        acct_scr[g] = _dot(vst_ref[0, kv(g), n_main], p_all[g, 0])
    g_sel = [grow(g, 1) for g in groups]

    def scores(g, j, slot):
        k0 = pl.multiple_of(jnp.minimum(j, last_tile) * SEL_TILE, SEL_TILE)
        sv = _dot(ksel_ref[0, kv(g), pl.ds(k0, SEL_TILE), :], qt_main[g])
        s_all[g, slot] = sv
        cm = sv[0:SUBLANES]
        for c in range(1, SEL_TILE // SUBLANES):
            cm = jnp.maximum(cm, sv[c * SUBLANES:(c + 1) * SUBLANES])
        cm_all[g, slot] = cm

    def softmax(g, slot):
        s = s_all[g, slot]
        m_prev = m_scr[g, 0:1, :]
        m_new = jnp.maximum(m_prev, jnp.max(cm_all[g, slot], axis=0, keepdims=True))
        a_all[g, slot] = jnp.broadcast_to(jnp.exp2(m_prev - m_new), (SUBLANES, COLS))
        p_all[g, slot] = jnp.exp2(s - m_new).astype(BF16)
        m_scr[g] = jnp.broadcast_to(m_new, (SUBLANES, COLS))

    def values(g, j, slot):
        acc_scr[g] = (a_all[g, slot, 0:1, :] * acc_scr[g]
                      + _dot(vst_ref[0, kv(g), jnp.minimum(j, last_tile)], p_all[g, slot]))

    m_scr[...] = jnp.full(m_scr.shape, NEG, F32)
    acc_scr[...] = jnp.zeros(acc_scr.shape, F32)
    for k in range(PIPE):
        for g in groups:
            scores(g, k, k)
    for k in range(PIPE // 2):
        for g in groups:
            softmax(g, k)

    def sweep(t, carry):
        for k in range(SWEEP_UNROLL):
            for g in groups:
                values(g, SWEEP_UNROLL * t + k, k % PIPE)
                softmax(g, (k + PIPE // 2) % PIPE)
                scores(g, SWEEP_UNROLL * t + k + PIPE, k % PIPE)
        return carry

    lax.fori_loop(0, (n_main + SWEEP_UNROLL - 1) // SWEEP_UNROLL, sweep, 0)

    pairs = [[] for _ in range(Q_PER_STEP)]
    for g in groups:
        m_p = m_scr[g, 0:1, :]
        m_tot = jnp.maximum(m_p, m_t[g])
        o_s = _normalize_dup(jnp.exp2(m_p - m_tot) * acc_scr[g] + jnp.exp2(m_t[g] - m_tot) * acct_scr[g])
        comb = cw_scr[g] + g_sel[g] * o_s
        outs = [comb[:, r * Q_BLOCK:(r + 1) * Q_BLOCK].T for r in range(Q_PER_KV)]
        pairs[qb(g)] += [jnp.where(lo_half, outs[0], outs[1]), jnp.where(lo_half, outs[2], outs[3])]
    for d in range(Q_PER_STEP):
        o_ref[0, d * Q_BLOCK:(d + 1) * Q_BLOCK, :] = jnp.concatenate(pairs[d], axis=1)


def _attn(qpt, ksel, vst, kw, vwt, kcmp, vcmpt, c2st, gatest):
    batch, seq = qpt.shape[0], qpt.shape[1] * Q_BLOCK
    per_batch = lambda a: pl.BlockSpec((1,) + a.shape[1:], lambda b, i: (b,) + (0,) * (a.ndim - 1),
                                       pipeline_mode=pl.Buffered(1))
    per_block = lambda a: pl.BlockSpec((1, Q_PER_STEP) + a.shape[2:], lambda b, i: (b, i) + (0,) * (a.ndim - 2))
    chains = Q_PER_STEP * N_KV_HEADS
    grp = lambda *shape: pltpu.VMEM((chains,) + shape, F32)
    return pl.pallas_call(
        _attn_kernel,
        grid=(batch, seq // (Q_PER_STEP * Q_BLOCK)),
        in_specs=[per_block(qpt), per_batch(ksel), per_batch(vst), per_batch(kw), per_batch(vwt), per_batch(kcmp),
                  per_batch(vcmpt), pl.BlockSpec(c2st.shape, lambda b, i: (0, 0)), per_block(gatest)],
        out_specs=pl.BlockSpec((1, Q_PER_STEP * Q_BLOCK, D_ATT), lambda b, i: (b, i, 0)),
        out_shape=jax.ShapeDtypeStruct((batch, seq, D_ATT), F32),
        scratch_shapes=[grp(PIPE, SEL_TILE, COLS), pltpu.VMEM((chains, PIPE, SEL_TILE, COLS), BF16),
                        grp(PIPE, SUBLANES, COLS), grp(PIPE, SUBLANES, COLS), grp(SUBLANES, COLS),
                        grp(V_ROWS, COLS), grp(V_ROWS, COLS),
                        grp(LANES, COLS), grp(WINDOW + Q_BLOCK, COLS),
                        pltpu.VMEM((chains, WINDOW + Q_BLOCK, COLS), BF16)],
        compiler_params=pltpu.CompilerParams(dimension_semantics=("arbitrary", "arbitrary"),
                                             vmem_limit_bytes=VMEM_LIMIT_BYTES),
        name="attn",
    )(qpt, ksel, vst, kw, vwt, kcmp, vcmpt, c2st, gatest)


POST_SUB = 2


def _post_kernel(x_ref, oatt_ref, mconv_ref, p_ref, onag_ref, wo_ref, lnf_ref, wup_ref, fcw_ref, fcb_ref,
                 wdn_ref, lnp_ref, wpg_ref, wpe_ref, out_ref, gbuf, *, tiles_per_seq):
    tm = x_ref.shape[0]
    sub = tm // POST_SUB
    it = pl.program_id(0) % tiles_per_seq
    subs = [slice(k * sub, (k + 1) * sub) for k in range(POST_SUB)]

    @pl.when(it == 0)
    def _():
        gbuf[0:SUBLANES, :] = jnp.zeros((SUBLANES, D_FF), F32)

    h1 = [x_ref[r, :] + _dot(jnp.concatenate([_rms(oatt_ref[r, :], onag_ref[...]).astype(BF16), mconv_ref[r, :]],
                                             axis=1), wo_ref[...]) for r in subs]
    xn = [_rms(h, lnf_ref[...]).astype(BF16) for h in h1]
    h2 = []
    for k, r in enumerate(subs):
        gpre = _dot(xn[k], wup_ref[:, :D_FF])
        up = _dot(xn[k], wup_ref[:, D_FF:])
        lo = SUBLANES + k * sub
        gbuf[lo:lo + sub, :] = gpre
        gate = (fcw_ref[2:3, :] * gpre + fcw_ref[1:2, :] * gbuf[lo - 1:lo - 1 + sub, :]
                + fcw_ref[0:1, :] * gbuf[lo - 2:lo - 2 + sub, :]) + fcb_ref[...]
        act = (gate * jax.nn.sigmoid(gate) * up).astype(BF16)
        h2.append(h1[k] + _dot(act, wdn_ref[...]))
    gbuf[0:SUBLANES, :] = gbuf[tm:tm + SUBLANES, :]
    for k, r in enumerate(subs):
        xn2 = _rms(h2[k], lnp_ref[...]).astype(BF16)
        out_ref[r, :] = (h2[k] + jax.nn.sigmoid(_dot(xn2, wpg_ref[...]))
                         * _dot(p_ref[r, :].astype(BF16), wpe_ref[...]))


def _post(x2, oatt, mconv, p2, onag, wo, lnf, wup, fcw, fcb, wdn, lnp, wpg, wpe, *, seq, tm):
    n = x2.shape[0]
    tps = seq // tm
    row = lambda w: pl.BlockSpec((tm, w), lambda i: (i, 0))
    full = lambda a: pl.BlockSpec(a.shape, lambda i: (0,) * a.ndim, pipeline_mode=pl.Buffered(1))
    return pl.pallas_call(
        functools.partial(_post_kernel, tiles_per_seq=tps),
        grid=(n // tm,),
        in_specs=[row(D_MODEL), row(D_ATT), row(D_CONV), row(D_PLE), full(onag), full(wo), full(lnf), full(wup),
                  full(fcw), full(fcb), full(wdn), full(lnp), full(wpg), full(wpe)],
        out_specs=row(D_MODEL),
        out_shape=jax.ShapeDtypeStruct((n, D_MODEL), F32),
        scratch_shapes=[pltpu.VMEM((tm + SUBLANES, D_FF), F32)],
        compiler_params=pltpu.CompilerParams(dimension_semantics=("arbitrary",),
                                             vmem_limit_bytes=VMEM_LIMIT_BYTES),
        name="post",
    )(x2, oatt, mconv, p2, onag, wo, lnf, wup, fcw, fcb, wdn, lnp, wpg, wpe)


def _rope_tables(pos):
    half = ROT_DIM // 2
    d = np.arange(LANES) % HEAD_DIM
    inv_freq = np.float64(ROPE_THETA) ** (-np.arange(half, dtype=np.float64) * 2.0 / ROT_DIM)
    ang = np.asarray(pos, np.float64)[:, None] * np.tile(inv_freq, LANES // half)[None, :]
    c, sn = np.cos(ang), np.sin(ang)
    first, second = (d < half)[None, :], ((d >= half) & (d < ROT_DIM))[None, :]
    cos = np.where(first | second, c, 1.0)
    sa = np.where(first, -sn, 0.0)
    sb = np.where(second, sn, 0.0)
    return tuple(jnp.asarray(t, F32) for t in (cos, sa, sb))


def _block_diag_mean(width):
    idx = np.arange(width) // HEAD_DIM
    return jnp.asarray((idx[:, None] == idx[None, :]).astype(np.float32) / HEAD_DIM, BF16)


def _cmp_to_slc_t(nch):
    cs = CMP_STRIDE * np.arange(nch)[None, :]
    ss = SLC_BLOCK * np.arange(N_SLC_LANES)[:, None]
    ov = np.clip(np.minimum(cs + CMP_BLOCK, ss + SLC_BLOCK) - np.maximum(cs, ss), 0, None)
    return jnp.asarray(ov.astype(np.float32) / CMP_BLOCK, BF16)


def _layer(h, p_l, ln_mix_g, w_in, qn_g, kn_g, pe_k, pe_v, w_ck1, w_ck2, w_cv1, w_cv2, conv_w, on_att_g,
           on_conv_g, w_o, ln_ffn_g, w_up, ffn_conv_w, ffn_conv_b, w_down, ln_ple_g, w_pg, w_pe):
    batch, seq, _ = h.shape
    assert seq % SEL_TILE == 0 and seq // SLC_BLOCK <= N_SLC_LANES and seq // SLC_BLOCK >= N_SELECT
    assert seq >= WINDOW + Q_BLOCK
    n = batch * seq
    nch = seq // CMP_STRIDE
    x2 = h.reshape(n, D_MODEL)
    row = lambda v: v.reshape(1, -1).astype(F32)

    o_q, o_kv, o_g, o_cv = 0, D_ATT, D_ATT + 6 * KV_W, D_ATT + 6 * KV_W + N_BRANCH * N_Q_HEADS
    wq = w_in[:, o_q:o_kv].astype(BF16)
    wkv = w_in[:, o_kv:o_g].astype(BF16)
    wcv = w_in[:, o_cv:].astype(BF16)
    per_g = Q_PER_KV * N_BRANCH
    wg = jnp.concatenate(
        [jnp.pad(w_in[:, o_g + g * per_g:o_g + (g + 1) * per_g], ((0, 0), (0, LANES - per_g)))
         for g in range(N_KV_HEADS)], axis=1).astype(BF16)
    cos, sa, sb = _rope_tables(np.arange(seq))
    bd512, bd128 = _block_diag_mean(D_ATT), _block_diag_mean(KV_W)
    tile_heads = lambda v, k: jnp.tile(v.astype(F32), k).reshape(1, -1)

    tm = ROW_TILE
    assert seq % ROW_TILE == 0 and seq % POST_TILE == 0
    qpt, kc2, vc2, ksel, vst, kw, vwt, gatest, mconv = _inproj(
        x2, row(ln_mix_g), wq, wkv, wg, wcv, tile_heads(qn_g, N_Q_HEADS),
        jnp.stack([jnp.tile(kn_g[1], N_KV_HEADS), jnp.tile(kn_g[2], N_KV_HEADS)]).astype(F32),
        bd512, bd128, cos, sa, sb, conv_w.astype(F32), row(on_conv_g), batch=batch, seq=seq, tm=tm)

    assert N_KV_HEADS == 2
    half = CMP_BLOCK // 2

    def w1_parts(w1):
        w = w1.astype(BF16)
        z = jnp.zeros_like(w)
        wfull = jnp.concatenate([jnp.concatenate([w, z], axis=2), jnp.concatenate([z, w], axis=2)], axis=1)
        return wfull[:half].reshape(half * KV_W, -1), wfull[half:].reshape(half * KV_W, -1)

    def pe_parts(pe):
        pf = jnp.broadcast_to(pe[:, None, :], (CMP_BLOCK, N_KV_HEADS, HEAD_DIM)).astype(F32)
        return pf[:half].reshape(1, -1), pf[half:].reshape(1, -1)

    def w2bd(w2):
        w = w2.astype(BF16)
        z = jnp.zeros_like(w)
        return jnp.concatenate([jnp.concatenate([w, z], axis=1), jnp.concatenate([z, w], axis=1)], axis=0)
    w1ak, w1bk = w1_parts(w_ck1)
    w1av, w1bv = w1_parts(w_cv1)
    peak, pebk = pe_parts(pe_k)
    peav, pebv = pe_parts(pe_v)
    ccos, csa, csb = _rope_tables(CMP_STRIDE * np.arange(nch) + CMP_BLOCK - 1)
    kcmp, vcmpt = _compress(kc2.reshape(batch, seq, KV_W), vc2.reshape(batch, seq, KV_W),
                            w1ak, w1bk, w1av, w1bv, peak, pebk, peav, pebv, w2bd(w_ck2), w2bd(w_cv2),
                            tile_heads(kn_g[0], N_KV_HEADS), bd128, ccos, csa, csb)

    oatt = _attn(qpt, ksel, vst, kw, vwt, kcmp, vcmpt, _cmp_to_slc_t(nch), gatest)

    out = _post(x2, oatt.reshape(n, D_ATT), mconv, p_l.reshape(n, D_PLE), row(on_att_g), w_o.astype(BF16),
                row(ln_ffn_g), w_up.astype(BF16), ffn_conv_w.astype(F32), row(ffn_conv_b), w_down.astype(BF16),
                row(ln_ple_g), w_pg.astype(BF16), w_pe.astype(BF16), seq=seq, tm=POST_TILE)
    return out.reshape(batch, seq, D_MODEL)


def kernel(x, p, ln_mix_g, w_in, qn_g, kn_g, pe_k, pe_v, w_ck1, w_ck2, w_cv1, w_cv2, conv_w, on_att_g,
           on_conv_g, w_o, ln_ffn_g, w_up, ffn_conv_w, ffn_conv_b, w_down, ln_ple_g, w_pg, w_pe):
    h = x
    for i in range(p.shape[0]):
        h = _layer(h, p[i], ln_mix_g[i], w_in[i], qn_g[i], kn_g[i], pe_k[i], pe_v[i], w_ck1[i], w_ck2[i],
                   w_cv1[i], w_cv2[i], conv_w[i], on_att_g[i], on_conv_g[i], w_o[i], ln_ffn_g[i], w_up[i],
                   ffn_conv_w[i], ffn_conv_b[i], w_down[i], ln_ple_g[i], w_pg[i], w_pe[i])
    return h
```

```python
import functools

import jax
import jax.numpy as jnp
import numpy as np
from jax import lax
from jax.experimental import pallas as pl
from jax.experimental.pallas import tpu as pltpu

D_MODEL = 1024
HEAD_DIM = 64
N_Q_HEADS = 8
N_KV_HEADS = 2
Q_PER_KV = N_Q_HEADS // N_KV_HEADS
D_ATT = N_Q_HEADS * HEAD_DIM
D_CONV = D_MODEL - D_ATT
KV_W = N_KV_HEADS * HEAD_DIM
N_BRANCH = 3
CONV_TAPS = 3
ROT_DIM = HEAD_DIM // 4
ROPE_THETA = 500000.0
CMP_BLOCK = 32
CMP_STRIDE = 16
CMP_HIDDEN = 256
SLC_BLOCK = 64
N_SELECT = 16
WINDOW = 512
Q_BLOCK = 128
D_FF = 2816
D_PLE = 256
EPS = 1e-6
NEG = -1e30
INT32_MIN = -2 ** 31
LOG2E = float(np.log2(np.e))
V_ROWS = HEAD_DIM + 16
GATE_ROWS = 16
SLC_SHIFT = SLC_BLOCK.bit_length() - 1
CMP_SHIFT = CMP_STRIDE.bit_length() - 1
assert 1 << SLC_SHIFT == SLC_BLOCK and 1 << CMP_SHIFT == CMP_STRIDE

LANES = 128
SUBLANES = 8
N_SLC_LANES = LANES
VMEM_LIMIT_BYTES = 56 * 1024 * 1024

F32 = jnp.float32
BF16 = jnp.bfloat16


def _dot(a, b):
    return jnp.dot(a, b, preferred_element_type=F32)


def _rms(x, g):
    return x * lax.rsqrt(jnp.mean(x * x, axis=-1, keepdims=True) + EPS) * g


def _head_rms_rope(x, g, bd, cos, sa, sb):
    w = x.shape[-1]
    msq = _dot((x * x).astype(BF16), bd)
    xn = x * lax.rsqrt(msq + EPS) * g
    return xn * cos + pltpu.roll(xn, w - ROT_DIM // 2, 1) * sa + pltpu.roll(xn, ROT_DIM // 2, 1) * sb


def _dup_halves(x):
    r = pltpu.roll(x, HEAD_DIM, 1)
    lane = lax.broadcasted_iota(jnp.int32, x.shape, 1)
    lo = lane < HEAD_DIM
    return jnp.where(lo, x, r), jnp.where(lo, r, x)


def _inproj_kernel(x_ref, lng_ref, wq_ref, wkv_ref, wg_ref, wcv_ref, qng_ref, kng_ref, bd512_ref, bd128_ref,
                   cos_ref, sa_ref, sb_ref, convw_ref, oncg_ref,
                   q_out, kc_out, vc_out, ksel_out, vs_out, kw_out, vw_out, gate_out, mconv_out,
                   zbuf, *, tiles_per_seq):
    tm = x_ref.shape[0]
    sub = tm // IN_SUB
    it = pl.program_id(0) % tiles_per_seq

    @pl.when(it == 0)
    def _():
        zbuf[0:SUBLANES, :] = jnp.zeros((SUBLANES, D_CONV), F32)

    for k in range(IN_SUB):
        r = slice(k * sub, (k + 1) * sub)
        xn = _rms(x_ref[r, :], lng_ref[...]).astype(BF16)

        cos, sa, sb = cos_ref[r, :], sa_ref[r, :], sb_ref[r, :]
        cos4, sa4, sb4 = (jnp.concatenate([t] * 4, axis=1) for t in (cos, sa, sb))
        q = _dot(xn, wq_ref[...])
        qr = _head_rms_rope(q, qng_ref[...], bd512_ref[...], cos4, sa4, sb4)
        qs = qr * (HEAD_DIM ** -0.5 * LOG2E)
        for blk_i in range(sub // Q_BLOCK):
            rows = slice(blk_i * Q_BLOCK, (blk_i + 1) * Q_BLOCK)
            for pr in range(N_Q_HEADS // 2):
                q_out[0, k * (sub // Q_BLOCK) + blk_i, pr] = qs[rows, pr * LANES:(pr + 1) * LANES].T.astype(BF16)

        kv = _dot(xn, wkv_ref[...])
        kc_out[r, :] = kv[:, 0 * KV_W:1 * KV_W]
        vc_out[r, :] = kv[:, 1 * KV_W:2 * KV_W]
        ks = _head_rms_rope(kv[:, 2 * KV_W:3 * KV_W], kng_ref[0:1, :], bd128_ref[...], cos, sa, sb)
        kw = _head_rms_rope(kv[:, 4 * KV_W:5 * KV_W], kng_ref[1:2, :], bd128_ref[...], cos, sa, sb)
        vs = kv[:, 3 * KV_W:4 * KV_W]
        vw = kv[:, 5 * KV_W:6 * KV_W]
        tpos = it * tm + k * sub + lax.broadcasted_iota(jnp.int32, (sub, N_SLC_LANES), 0)
        blk = lax.broadcasted_iota(jnp.int32, (sub, N_SLC_LANES), 1)
        onehot = jnp.where(lax.shift_right_logical(tpos, SLC_SHIFT) == blk, 1.0, 0.0).astype(BF16)
        ks_d, vs_d, kw_d, vw_d = (_dup_halves(t) for t in (ks, vs, kw, vw))
        vrow = lax.broadcasted_iota(jnp.int32, (V_ROWS, sub), 0)
        for g in range(N_KV_HEADS):
            ksel_out[0, g, r, :] = jnp.concatenate([ks_d[g].astype(BF16), onehot], axis=1)
            kw_out[0, g, r, :] = kw_d[g].astype(BF16)
            vst = jnp.where(vrow < HEAD_DIM, vs_d[g].T[:V_ROWS], 1.0).astype(BF16)
            for c in range(sub // SEL_TILE):
                vs_out[0, g, k * (sub // SEL_TILE) + c] = vst[:, c * SEL_TILE:(c + 1) * SEL_TILE]
            vwt = jnp.where(vrow < HEAD_DIM, vw_d[g].T[:V_ROWS], 1.0).astype(BF16)
            for c in range(sub // LANES):
                vw_out[0, g, k * (sub // LANES) + c] = vwt[:, c * LANES:(c + 1) * LANES]

        gates = jax.nn.sigmoid(_dot(xn, wg_ref[...]))
        for blk_i in range(sub // Q_BLOCK):
            rows = slice(blk_i * Q_BLOCK, (blk_i + 1) * Q_BLOCK)
            for g in range(N_KV_HEADS):
                gate_out[0, k * (sub // Q_BLOCK) + blk_i, g] = gates[rows, g * LANES:(g + 1) * LANES].T[:GATE_ROWS]

        cv = _dot(xn, wcv_ref[...])
        cb, cc, cx = cv[:, :D_CONV], cv[:, D_CONV:2 * D_CONV], cv[:, 2 * D_CONV:]
        z = cc * cx
        lo = SUBLANES + k * sub
        zbuf[lo:lo + sub, :] = z
        y = (convw_ref[2:3, :] * z + convw_ref[1:2, :] * zbuf[lo - 1:lo - 1 + sub, :]
             + convw_ref[0:1, :] * zbuf[lo - 2:lo - 2 + sub, :])
        mconv_out[r, :] = _rms(cb * y, oncg_ref[...]).astype(BF16)
    zbuf[0:SUBLANES, :] = zbuf[tm:tm + SUBLANES, :]


def _inproj(x2, lng, wq, wkv, wg, wcv, qng, kng, bd512, bd128, cos, sa, sb, convw, oncg, *, batch, seq, tm):
    n = batch * seq
    tps = seq // tm
    row = lambda w: pl.BlockSpec((tm, w), lambda i: (i, 0))
    full = lambda a: pl.BlockSpec(a.shape, lambda i: (0,) * a.ndim)
    tab = pl.BlockSpec((tm, LANES), lambda i: (i % tps, 0))
    grp = lambda w: pl.BlockSpec((1, N_KV_HEADS, tm, w), lambda i: (i // tps, 0, i % tps, 0))
    gshape = lambda w: jax.ShapeDtypeStruct((batch, N_KV_HEADS, seq, w), BF16)
    assert tm % (IN_SUB * SEL_TILE) == 0
    vtile = lambda keys: pl.BlockSpec((1, N_KV_HEADS, tm // keys, V_ROWS, keys),
                                      lambda i: (i // tps, 0, i % tps, 0, 0))
    vshape = lambda keys: jax.ShapeDtypeStruct((batch, N_KV_HEADS, seq // keys, V_ROWS, keys), BF16)
    qblk = lambda tile: pl.BlockSpec((1, tm // Q_BLOCK) + tile, lambda i: (i // tps, i % tps) + (0,) * len(tile))
    return pl.pallas_call(
        functools.partial(_inproj_kernel, tiles_per_seq=tps),
        grid=(n // tm,),
        in_specs=[row(D_MODEL), full(lng), full(wq), full(wkv), full(wg), full(wcv), full(qng), full(kng),
                  full(bd512), full(bd128), tab, tab, tab, full(convw), full(oncg)],
        out_specs=[qblk((N_Q_HEADS // 2, LANES, Q_BLOCK)), row(KV_W), row(KV_W), grp(2 * LANES), vtile(SEL_TILE),
                   grp(LANES), vtile(LANES), qblk((N_KV_HEADS, GATE_ROWS, Q_BLOCK)), row(D_CONV)],
        out_shape=[jax.ShapeDtypeStruct((batch, seq // Q_BLOCK, N_Q_HEADS // 2, LANES, Q_BLOCK), BF16),
                   jax.ShapeDtypeStruct((n, KV_W), F32),
                   jax.ShapeDtypeStruct((n, KV_W), F32), gshape(2 * LANES), vshape(SEL_TILE), gshape(LANES),
                   vshape(LANES),
                   jax.ShapeDtypeStruct((batch, seq // Q_BLOCK, N_KV_HEADS, GATE_ROWS, Q_BLOCK), F32),
                   jax.ShapeDtypeStruct((n, D_CONV), BF16)],
        scratch_shapes=[pltpu.VMEM((tm + SUBLANES, D_CONV), F32)],
        compiler_params=pltpu.CompilerParams(dimension_semantics=("arbitrary",),
                                             vmem_limit_bytes=VMEM_LIMIT_BYTES),
        name="inproj",
    )(x2, lng, wq, wkv, wg, wcv, qng, kng, bd512, bd128, cos, sa, sb, convw, oncg)


def _gelu_tanh(x):
    return 0.5 * x * (1.0 + jnp.tanh(np.sqrt(2.0 / np.pi).astype(np.float32) * (x + 0.044715 * (x * x * x))))


def _compress_kernel(kc_ref, vc_ref, w1ak_ref, w1bk_ref, w1av_ref, w1bv_ref, peak_ref, pebk_ref, peav_ref,
                     pebv_ref, w2k_ref, w2v_ref, kng_ref, bd128_ref, cos_ref, sa_ref, sb_ref,
                     kcmp_out, vcmp_out):
    nch = kc_ref.shape[1] // CMP_STRIDE

    def chunks(ref):
        return jnp.concatenate([ref[0, pl.ds(l, nch, stride=CMP_STRIDE), :] for l in range(CMP_STRIDE)], axis=1)

    def compress(x, w1a, w1b, pea, peb, w2):
        a = _dot((x + pea).astype(BF16), w1a)
        b = _dot((x + peb).astype(BF16), w1b)
        hid = _gelu_tanh(a + pltpu.roll(b, nch - 1, 0))
        return _dot(hid.astype(BF16), w2)

    kc = compress(chunks(kc_ref), w1ak_ref[...], w1bk_ref[...], peak_ref[...], pebk_ref[...], w2k_ref[...])
    vc = compress(chunks(vc_ref), w1av_ref[...], w1bv_ref[...], peav_ref[...], pebv_ref[...], w2v_ref[...])
    kc = _head_rms_rope(kc, kng_ref[...], bd128_ref[...], cos_ref[...], sa_ref[...], sb_ref[...])
    kd, vd = _dup_halves(kc), _dup_halves(vc)
    for g in range(N_KV_HEADS):
        kcmp_out[0, g] = kd[g].astype(BF16)
        vcmp_out[0, g] = vd[g].T.astype(BF16)


def _compress(kc3, vc3, w1ak, w1bk, w1av, w1bv, peak, pebk, peav, pebv, w2k, w2v, kng0, bd128, cos, sa, sb):
    batch, seq, wide = kc3.shape
    nch = seq // CMP_STRIDE
    full = lambda a: pl.BlockSpec(a.shape, lambda b: (0,) * a.ndim)
    tok = pl.BlockSpec((1, seq, wide), lambda b: (b, 0, 0))
    kout = pl.BlockSpec((1, N_KV_HEADS, nch, LANES), lambda b: (b, 0, 0, 0))
    vout = pl.BlockSpec((1, N_KV_HEADS, LANES, nch), lambda b: (b, 0, 0, 0))
    consts = (w1ak, w1bk, w1av, w1bv, peak, pebk, peav, pebv, w2k, w2v, kng0, bd128, cos, sa, sb)
    return pl.pallas_call(
        _compress_kernel,
        grid=(batch,),
        in_specs=[tok, tok] + [full(a) for a in consts],
        out_specs=[kout, vout],
        out_shape=[jax.ShapeDtypeStruct((batch, N_KV_HEADS, nch, LANES), BF16),
                   jax.ShapeDtypeStruct((batch, N_KV_HEADS, LANES, nch), BF16)],
        compiler_params=pltpu.CompilerParams(dimension_semantics=("arbitrary",),
                                             vmem_limit_bytes=VMEM_LIMIT_BYTES),
        name="compress",
    )(kc3, vc3, *consts)


SEL_TILE = 256
COLS = Q_PER_KV * Q_BLOCK
ROW_TILE = 512
IN_SUB = 2
POST_TILE = 512
RADIX_BITS = 1
Q_PER_STEP = 2
PIPE = 2
SWEEP_UNROLL = 2
assert SWEEP_UNROLL % PIPE == 0 and SEL_TILE % (Q_PER_STEP * Q_BLOCK) == 0
WIN_MASK_ROWS = 128
WIN_EXP_ROWS = 128
assert (WINDOW + Q_BLOCK) // WIN_MASK_ROWS + (WINDOW + Q_BLOCK) // WIN_EXP_ROWS <= 32


def _split2(x):
    hi = x.astype(BF16)
    return hi, (x - hi.astype(F32)).astype(BF16)


def _zero_after(x):
    return lax.shift_right_logical(lax.shift_right_logical(x, 16), 16)


def _normalize_dup(acc):
    o = acc[:HEAD_DIM] / acc[HEAD_DIM:HEAD_DIM + 1]
    return jnp.concatenate([o, o], axis=0)


def _attn_kernel(q_ref, ksel_ref, vst_ref, kw_ref, vwt_ref, kcmp_ref, vcmpt_ref, c2st_ref, gate_ref, o_ref,
                 s_all, p_all, a_all, cm_all, m_scr, acc_scr, acct_scr, cw_scr, sw_scr, pw_scr):
    i = pl.program_id(1)
    starts = [(Q_PER_STEP * i + d) * Q_BLOCK for d in range(Q_PER_STEP)]
    kv = lambda g: g % N_KV_HEADS
    qb = lambda g: g // N_KV_HEADS
    nch = kcmp_ref.shape[2]
    last_tile = ksel_ref.shape[2] // SEL_TILE - 1
    groups = range(Q_PER_STEP * N_KV_HEADS)

    lane = lax.broadcasted_iota(jnp.int32, (Q_BLOCK, LANES), 1)
    lo_half = lane < HEAD_DIM
    top_half = lax.broadcasted_iota(jnp.int32, (LANES, Q_BLOCK), 0) < HEAD_DIM
    zero_bf = jnp.zeros((LANES, Q_BLOCK), BF16)
    tqs = [st + (lax.broadcasted_iota(jnp.int32, (1, COLS), 1) & (Q_BLOCK - 1)) for st in starts]
    wk = WINDOW + Q_BLOCK
    bases = [pl.multiple_of(jnp.maximum(st - WINDOW, 0), Q_BLOCK) for st in starts]
    wrow = lax.broadcasted_iota(jnp.int32, (WIN_MASK_ROWS, COLS), 0)
    nrow = lax.broadcasted_iota(jnp.int32, (nch, COLS), 0)
    srow = lax.broadcasted_iota(jnp.int32, (N_SLC_LANES, Q_BLOCK), 0)
    tqqs = [st + lax.broadcasted_iota(jnp.int32, (N_SLC_LANES, Q_BLOCK), 1) for st in starts]
    c2st = c2st_ref[...]
    n_main = starts[0] // SEL_TILE

    qt = [jnp.concatenate([jnp.where(top_half if r % 2 == 0 else ~top_half,
                                     q_ref[0, qb(g), kv(g) * (Q_PER_KV // 2) + r // 2], zero_bf)
                           for r in range(Q_PER_KV)], axis=1) for g in groups]

    def grow(g, b):
        gt = gate_ref[0, qb(g), kv(g)]
        return jnp.concatenate([gt[r * N_BRANCH + b:r * N_BRANCH + b + 1, :] for r in range(Q_PER_KV)], axis=1)

    s_c = [_dot(kcmp_ref[0, kv(g)], qt[g]) for g in groups]
    for g in groups:
        sw_scr[g] = _dot(kw_ref[0, kv(g), pl.ds(bases[qb(g)], wk), :], qt[g])

    last_valid = [lax.shift_right_arithmetic(t - (CMP_BLOCK - 1), CMP_SHIFT) for t in tqs]
    curs = [lax.shift_right_logical(t, SLC_SHIFT) for t in tqqs]
    forced = [(srow == 0) | (srow == c) | (srow == c - 1) for c in curs]
    future = [srow * SLC_BLOCK > t for t in tqqs]
    impt = []
    for g in groups:
        sc = jnp.where(nrow <= last_valid[qb(g)], s_c[g], NEG)
        m_c = jnp.max(sc, axis=0, keepdims=True)
        e_c = jnp.exp2(sc - m_c)
        l_c = jnp.sum(e_c, axis=0, keepdims=True)
        p_c = e_c * jnp.where(m_c > 0.5 * NEG, 1.0 / l_c, 0.0)
        cw_scr[g] = grow(g, 0) * _dot(vcmpt_ref[0, kv(g)], p_c.astype(BF16))
        psum = sum(p_c[:, r * Q_BLOCK:(r + 1) * Q_BLOCK] for r in range(Q_PER_KV))
        imp = sum(_dot(c2st, t) for t in _split2(psum))
        impt.append(jnp.where(forced[qb(g)], 1e9, jnp.where(future[qb(g)], -1e9, imp)))

    def window_mask_piece(g, k, colmax, zero):
        rows = slice(WIN_MASK_ROWS * k, WIN_MASK_ROWS * (k + 1))
        back = (tqs[qb(g)] + zero - bases[qb(g)] - WIN_MASK_ROWS * k) - wrow
        in_window = lax.bitcast_convert_type(back, jnp.uint32) < jnp.uint32(WINDOW)
        sm = jnp.where(in_window, sw_scr[g, rows, :], NEG)
        sw_scr[g, rows, :] = sm
        for c in range(WIN_MASK_ROWS // SUBLANES):
            colmax = jnp.maximum(colmax, sm[c * SUBLANES:(c + 1) * SUBLANES])
        return colmax

    def window_exp_piece(g, k, m_w, zero):
        rows = slice(WIN_EXP_ROWS * k, WIN_EXP_ROWS * (k + 1))
        pw_scr[g, rows, :] = jnp.exp2(sw_scr[g, rows, :] - (m_w + zero.astype(F32))).astype(BF16)

    key_to_float = lambda k: lax.bitcast_convert_type(jnp.where(k >= 0, k, k ^ jnp.int32(0x7FFFFFFF)), F32)
    wrap32 = lambda v: ((v + 2 ** 31) % 2 ** 32) - 2 ** 31
    n_mask, n_exp = wk // WIN_MASK_ROWS, wk // WIN_EXP_ROWS
    thr = [jnp.full((1, Q_BLOCK), INT32_MIN, jnp.int32) for _ in groups]
    colmax = [jnp.full((SUBLANES, COLS), NEG, F32) for _ in groups]
    m_w = [None for _ in groups]
    piece = 0
    for b in range(32 - RADIX_BITS, -1, -RADIX_BITS):
        for g in groups:
            reached = thr[g]
            for mult in range(1, 2 ** RADIX_BITS):
                cand = thr[g] + jnp.int32(wrap32(mult << b))
                n_ge = jnp.sum(jnp.where(impt[g] >= key_to_float(cand), 1.0, 0.0), axis=0, keepdims=True)
                reached = jnp.where(n_ge >= N_SELECT, cand, reached)
            thr[g] = reached
        for _ in range(RADIX_BITS):
            for g in groups:
                zero = jnp.concatenate([_zero_after(thr[g])] * Q_PER_KV, axis=1)
                if piece < n_mask:
                    colmax[g] = window_mask_piece(g, piece, colmax[g], zero)
                elif piece < n_mask + n_exp:
                    if piece == n_mask:
                        m_w[g] = jnp.max(colmax[g], axis=0, keepdims=True)
                    window_exp_piece(g, piece - n_mask, m_w[g], zero)
            piece += 1

    scol = lax.broadcasted_iota(jnp.int32, (N_SLC_LANES, N_SLC_LANES), 1)
    earlier = jnp.where(scol < srow, 1.0, 0.0).astype(BF16)
    widen = lambda x: jnp.concatenate([x.astype(BF16)] * Q_PER_KV, axis=1)
    qt_tail, qt_main = [], []
    for g in groups:
        vw = jnp.concatenate([vwt_ref[0, kv(g), bases[qb(g)] // LANES + c] for c in range(wk // LANES)],
                             axis=1)
        cw_scr[g] += grow(g, 2) * _normalize_dup(_dot(vw, pw_scr[g]))
        kth = key_to_float(thr[g])
        above = impt[g] > kth
        tied = impt[g] == kth
        n_above = jnp.sum(jnp.where(above, 1.0, 0.0), axis=0, keepdims=True)
        tied_before = _dot(earlier, jnp.where(tied, 1.0, 0.0).astype(BF16))
        selected = above | (tied & (tied_before < N_SELECT - n_above))
        bias = jnp.where(selected, 0.0, NEG)
        bias_main = jnp.where(srow >= n_main * (SEL_TILE // SLC_BLOCK), NEG, bias)
        qt_tail.append(jnp.concatenate([qt[g], widen(bias)], axis=0))
        qt_main.append(jnp.concatenate([qt[g], widen(bias_main)], axis=0))

    kt = pl.multiple_of(n_main * SEL_TILE, SEL_TILE)
    krow = lax.broadcasted_iota(jnp.int32, (SEL_TILE, COLS), 0)
    for g in groups:
        s_all[g, 0] = _dot(ksel_ref[0, kv(g), pl.ds(kt, SEL_TILE), :], qt_tail[g])
    m_t = []
    for g in groups:
        s_t = jnp.where(kt + krow <= tqs[qb(g)], s_all[g, 0], NEG)
        m_t.append(jnp.max(s_t, axis=0, keepdims=True))
        p_all[g, 0] = jnp.exp2(s_t - m_t[g]).astype(BF16)
    for g in groups:
        acct_scr[g] = _dot(vst_ref[0, kv(g), n_main], p_all[g, 0])
    g_sel = [grow(g, 1) for g in groups]

    def scores(g, j, slot):
        k0 = pl.multiple_of(jnp.minimum(j, last_tile) * SEL_TILE, SEL_TILE)
        sv = _dot(ksel_ref[0, kv(g), pl.ds(k0, SEL_TILE), :], qt_main[g])
        s_all[g, slot] = sv
        cm = sv[0:SUBLANES]
        for c in range(1, SEL_TILE // SUBLANES):
            cm = jnp.maximum(cm, sv[c * SUBLANES:(c + 1) * SUBLANES])
        cm_all[g, slot] = cm

    def softmax(g, slot):
        s = s_all[g, slot]
        m_prev = m_scr[g, 0:1, :]
        m_new = jnp.maximum(m_prev, jnp.max(cm_all[g, slot], axis=0, keepdims=True))
        a_all[g, slot] = jnp.broadcast_to(jnp.exp2(m_prev - m_new), (SUBLANES, COLS))
        p_all[g, slot] = jnp.exp2(s - m_new).astype(BF16)
        m_scr[g] = jnp.broadcast_to(m_new, (SUBLANES, COLS))

    def values(g, j, slot):
        acc_scr[g] = (a_all[g, slot, 0:1, :] * acc_scr[g]
                      + _dot(vst_ref[0, kv(g), jnp.minimum(j, last_tile)], p_all[g, slot]))

    m_scr[...] = jnp.full(m_scr.shape, NEG, F32)
    acc_scr[...] = jnp.zeros(acc_scr.shape, F32)
    for k in range(PIPE):
        for g in groups:
            scores(g, k, k)
    for k in range(PIPE // 2):
        for g in groups:
            softmax(g, k)

    def sweep(t, carry):
        for k in range(SWEEP_UNROLL):
            for g in groups:
                scores(g, SWEEP_UNROLL * t + k + PIPE, k % PIPE)
            for g in groups:
                values(g, SWEEP_UNROLL * t + k, k % PIPE)
            for g in groups:
                softmax(g, (k + PIPE // 2) % PIPE)
        return carry

    lax.fori_loop(0, (n_main + SWEEP_UNROLL - 1) // SWEEP_UNROLL, sweep, 0)

    pairs = [[] for _ in range(Q_PER_STEP)]
    for g in groups:
        m_p = m_scr[g, 0:1, :]
        m_tot = jnp.maximum(m_p, m_t[g])
        o_s = _normalize_dup(jnp.exp2(m_p - m_tot) * acc_scr[g] + jnp.exp2(m_t[g] - m_tot) * acct_scr[g])
        comb = cw_scr[g] + g_sel[g] * o_s
        outs = [comb[:, r * Q_BLOCK:(r + 1) * Q_BLOCK].T for r in range(Q_PER_KV)]
        pairs[qb(g)] += [jnp.where(lo_half, outs[0], outs[1]), jnp.where(lo_half, outs[2], outs[3])]
    for d in range(Q_PER_STEP):
        o_ref[0, d * Q_BLOCK:(d + 1) * Q_BLOCK, :] = jnp.concatenate(pairs[d], axis=1)


def _attn(qpt, ksel, vst, kw, vwt, kcmp, vcmpt, c2st, gatest):
    batch, seq = qpt.shape[0], qpt.shape[1] * Q_BLOCK
    per_batch = lambda a: pl.BlockSpec((1,) + a.shape[1:], lambda b, i: (b,) + (0,) * (a.ndim - 1),
                                       pipeline_mode=pl.Buffered(1))
    per_block = lambda a: pl.BlockSpec((1, Q_PER_STEP) + a.shape[2:], lambda b, i: (b, i) + (0,) * (a.ndim - 2))
    chains = Q_PER_STEP * N_KV_HEADS
    grp = lambda *shape: pltpu.VMEM((chains,) + shape, F32)
    return pl.pallas_call(
        _attn_kernel,
        grid=(batch, seq // (Q_PER_STEP * Q_BLOCK)),
        in_specs=[per_block(qpt), per_batch(ksel), per_batch(vst), per_batch(kw), per_batch(vwt), per_batch(kcmp),
                  per_batch(vcmpt), pl.BlockSpec(c2st.shape, lambda b, i: (0, 0)), per_block(gatest)],
        out_specs=pl.BlockSpec((1, Q_PER_STEP * Q_BLOCK, D_ATT), lambda b, i: (b, i, 0)),
        out_shape=jax.ShapeDtypeStruct((batch, seq, D_ATT), F32),
        scratch_shapes=[grp(PIPE, SEL_TILE, COLS), pltpu.VMEM((chains, PIPE, SEL_TILE, COLS), BF16),
                        grp(PIPE, SUBLANES, COLS), grp(PIPE, SUBLANES, COLS), grp(SUBLANES, COLS),
                        grp(V_ROWS, COLS), grp(V_ROWS, COLS),
                        grp(LANES, COLS), grp(WINDOW + Q_BLOCK, COLS),
                        pltpu.VMEM((chains, WINDOW + Q_BLOCK, COLS), BF16)],
        compiler_params=pltpu.CompilerParams(dimension_semantics=("arbitrary", "arbitrary"),
                                             vmem_limit_bytes=VMEM_LIMIT_BYTES),
        name="attn",
    )(qpt, ksel, vst, kw, vwt, kcmp, vcmpt, c2st, gatest)


POST_SUB = 2


def _post_kernel(x_ref, oatt_ref, mconv_ref, p_ref, onag_ref, wo_ref, lnf_ref, wup_ref, fcw_ref, fcb_ref,
                 wdn_ref, lnp_ref, wpg_ref, wpe_ref, out_ref, gbuf, *, tiles_per_seq):
    tm = x_ref.shape[0]
    sub = tm // POST_SUB
    it = pl.program_id(0) % tiles_per_seq
    subs = [slice(k * sub, (k + 1) * sub) for k in range(POST_SUB)]

    @pl.when(it == 0)
    def _():
        gbuf[0:SUBLANES, :] = jnp.zeros((SUBLANES, D_FF), F32)

    h1 = [x_ref[r, :] + _dot(jnp.concatenate([_rms(oatt_ref[r, :], onag_ref[...]).astype(BF16), mconv_ref[r, :]],
                                             axis=1), wo_ref[...]) for r in subs]
    xn = [_rms(h, lnf_ref[...]).astype(BF16) for h in h1]
    h2 = []
    for k, r in enumerate(subs):
        gpre = _dot(xn[k], wup_ref[:, :D_FF])
        up = _dot(xn[k], wup_ref[:, D_FF:])
        lo = SUBLANES + k * sub
        gbuf[lo:lo + sub, :] = gpre
        gate = (fcw_ref[2:3, :] * gpre + fcw_ref[1:2, :] * gbuf[lo - 1:lo - 1 + sub, :]
                + fcw_ref[0:1, :] * gbuf[lo - 2:lo - 2 + sub, :]) + fcb_ref[...]
        act = (gate * jax.nn.sigmoid(gate) * up).astype(BF16)
        h2.append(h1[k] + _dot(act, wdn_ref[...]))
    gbuf[0:SUBLANES, :] = gbuf[tm:tm + SUBLANES, :]
    for k, r in enumerate(subs):
        xn2 = _rms(h2[k], lnp_ref[...]).astype(BF16)
        out_ref[r, :] = (h2[k] + jax.nn.sigmoid(_dot(xn2, wpg_ref[...]))
                         * _dot(p_ref[r, :].astype(BF16), wpe_ref[...]))


def _post(x2, oatt, mconv, p2, onag, wo, lnf, wup, fcw, fcb, wdn, lnp, wpg, wpe, *, seq, tm):
    n = x2.shape[0]
    tps = seq // tm
    row = lambda w: pl.BlockSpec((tm, w), lambda i: (i, 0))
    full = lambda a: pl.BlockSpec(a.shape, lambda i: (0,) * a.ndim, pipeline_mode=pl.Buffered(1))
    return pl.pallas_call(
        functools.partial(_post_kernel, tiles_per_seq=tps),
        grid=(n // tm,),
        in_specs=[row(D_MODEL), row(D_ATT), row(D_CONV), row(D_PLE), full(onag), full(wo), full(lnf), full(wup),
                  full(fcw), full(fcb), full(wdn), full(lnp), full(wpg), full(wpe)],
        out_specs=row(D_MODEL),
        out_shape=jax.ShapeDtypeStruct((n, D_MODEL), F32),
        scratch_shapes=[pltpu.VMEM((tm + SUBLANES, D_FF), F32)],
        compiler_params=pltpu.CompilerParams(dimension_semantics=("arbitrary",),
                                             vmem_limit_bytes=VMEM_LIMIT_BYTES),
        name="post",
    )(x2, oatt, mconv, p2, onag, wo, lnf, wup, fcw, fcb, wdn, lnp, wpg, wpe)


def _rope_tables(pos):
    half = ROT_DIM // 2
    d = np.arange(LANES) % HEAD_DIM
    inv_freq = np.float64(ROPE_THETA) ** (-np.arange(half, dtype=np.float64) * 2.0 / ROT_DIM)
    ang = np.asarray(pos, np.float64)[:, None] * np.tile(inv_freq, LANES // half)[None, :]
    c, sn = np.cos(ang), np.sin(ang)
    first, second = (d < half)[None, :], ((d >= half) & (d < ROT_DIM))[None, :]
    cos = np.where(first | second, c, 1.0)
    sa = np.where(first, -sn, 0.0)
    sb = np.where(second, sn, 0.0)
    return tuple(jnp.asarray(t, F32) for t in (cos, sa, sb))


def _block_diag_mean(width):
    idx = np.arange(width) // HEAD_DIM
    return jnp.asarray((idx[:, None] == idx[None, :]).astype(np.float32) / HEAD_DIM, BF16)


def _cmp_to_slc_t(nch):
    cs = CMP_STRIDE * np.arange(nch)[None, :]
    ss = SLC_BLOCK * np.arange(N_SLC_LANES)[:, None]
    ov = np.clip(np.minimum(cs + CMP_BLOCK, ss + SLC_BLOCK) - np.maximum(cs, ss), 0, None)
    return jnp.asarray(ov.astype(np.float32) / CMP_BLOCK, BF16)


def _layer(h, p_l, ln_mix_g, w_in, qn_g, kn_g, pe_k, pe_v, w_ck1, w_ck2, w_cv1, w_cv2, conv_w, on_att_g,
           on_conv_g, w_o, ln_ffn_g, w_up, ffn_conv_w, ffn_conv_b, w_down, ln_ple_g, w_pg, w_pe):
    batch, seq, _ = h.shape
    assert seq % SEL_TILE == 0 and seq // SLC_BLOCK <= N_SLC_LANES and seq // SLC_BLOCK >= N_SELECT
    assert seq >= WINDOW + Q_BLOCK
    n = batch * seq
    nch = seq // CMP_STRIDE
    x2 = h.reshape(n, D_MODEL)
    row = lambda v: v.reshape(1, -1).astype(F32)

    o_q, o_kv, o_g, o_cv = 0, D_ATT, D_ATT + 6 * KV_W, D_ATT + 6 * KV_W + N_BRANCH * N_Q_HEADS
    wq = w_in[:, o_q:o_kv].astype(BF16)
    wkv = w_in[:, o_kv:o_g].astype(BF16)
    wcv = w_in[:, o_cv:].astype(BF16)
    per_g = Q_PER_KV * N_BRANCH
    wg = jnp.concatenate(
        [jnp.pad(w_in[:, o_g + g * per_g:o_g + (g + 1) * per_g], ((0, 0), (0, LANES - per_g)))
         for g in range(N_KV_HEADS)], axis=1).astype(BF16)
    cos, sa, sb = _rope_tables(np.arange(seq))
    bd512, bd128 = _block_diag_mean(D_ATT), _block_diag_mean(KV_W)
    tile_heads = lambda v, k: jnp.tile(v.astype(F32), k).reshape(1, -1)

    tm = ROW_TILE
    assert seq % ROW_TILE == 0 and seq % POST_TILE == 0
    qpt, kc2, vc2, ksel, vst, kw, vwt, gatest, mconv = _inproj(
        x2, row(ln_mix_g), wq, wkv, wg, wcv, tile_heads(qn_g, N_Q_HEADS),
        jnp.stack([jnp.tile(kn_g[1], N_KV_HEADS), jnp.tile(kn_g[2], N_KV_HEADS)]).astype(F32),
        bd512, bd128, cos, sa, sb, conv_w.astype(F32), row(on_conv_g), batch=batch, seq=seq, tm=tm)

    assert N_KV_HEADS == 2
    half = CMP_BLOCK // 2

    def w1_parts(w1):
        w = w1.astype(BF16)
        z = jnp.zeros_like(w)
        wfull = jnp.concatenate([jnp.concatenate([w, z], axis=2), jnp.concatenate([z, w], axis=2)], axis=1)
        return wfull[:half].reshape(half * KV_W, -1), wfull[half:].reshape(half * KV_W, -1)

    def pe_parts(pe):
        pf = jnp.broadcast_to(pe[:, None, :], (CMP_BLOCK, N_KV_HEADS, HEAD_DIM)).astype(F32)
        return pf[:half].reshape(1, -1), pf[half:].reshape(1, -1)

    def w2bd(w2):
        w = w2.astype(BF16)
        z = jnp.zeros_like(w)
        return jnp.concatenate([jnp.concatenate([w, z], axis=1), jnp.concatenate([z, w], axis=1)], axis=0)
    w1ak, w1bk = w1_parts(w_ck1)
    w1av, w1bv = w1_parts(w_cv1)
    peak, pebk = pe_parts(pe_k)
    peav, pebv = pe_parts(pe_v)
    ccos, csa, csb = _rope_tables(CMP_STRIDE * np.arange(nch) + CMP_BLOCK - 1)
    kcmp, vcmpt = _compress(kc2.reshape(batch, seq, KV_W), vc2.reshape(batch, seq, KV_W),
                            w1ak, w1bk, w1av, w1bv, peak, pebk, peav, pebv, w2bd(w_ck2), w2bd(w_cv2),
                            tile_heads(kn_g[0], N_KV_HEADS), bd128, ccos, csa, csb)

    oatt = _attn(qpt, ksel, vst, kw, vwt, kcmp, vcmpt, _cmp_to_slc_t(nch), gatest)

    out = _post(x2, oatt.reshape(n, D_ATT), mconv, p_l.reshape(n, D_PLE), row(on_att_g), w_o.astype(BF16),
                row(ln_ffn_g), w_up.astype(BF16), ffn_conv_w.astype(F32), row(ffn_conv_b), w_down.astype(BF16),
                row(ln_ple_g), w_pg.astype(BF16), w_pe.astype(BF16), seq=seq, tm=POST_TILE)
    return out.reshape(batch, seq, D_MODEL)


def kernel(x, p, ln_mix_g, w_in, qn_g, kn_g, pe_k, pe_v, w_ck1, w_ck2, w_cv1, w_cv2, conv_w, on_att_g,
           on_conv_g, w_o, ln_ffn_g, w_up, ffn_conv_w, ffn_conv_b, w_down, ln_ple_g, w_pg, w_pe):
    h = x
    for i in range(p.shape[0]):
        h = _layer(h, p[i], ln_mix_g[i], w_in[i], qn_g[i], kn_g[i], pe_k[i], pe_v[i], w_ck1[i], w_ck2[i],
                   w_cv1[i], w_cv2[i], conv_w[i], on_att_g[i], on_conv_g[i], w_o[i], ln_ffn_g[i], w_up[i],
                   ffn_conv_w[i], ffn_conv_b[i], w_down[i], ln_ple_g[i], w_pg[i], w_pe[i])
    return h
```

```python
import functools

import jax
import jax.numpy as jnp
import numpy as np
from jax import lax
from jax.experimental import pallas as pl
from jax.experimental.pallas import tpu as pltpu

D_MODEL = 1024
HEAD_DIM = 64
N_Q_HEADS = 8
N_KV_HEADS = 2
Q_PER_KV = N_Q_HEADS // N_KV_HEADS
D_ATT = N_Q_HEADS * HEAD_DIM
D_CONV = D_MODEL - D_ATT
KV_W = N_KV_HEADS * HEAD_DIM
N_BRANCH = 3
CONV_TAPS = 3
ROT_DIM = HEAD_DIM // 4
ROPE_THETA = 500000.0
CMP_BLOCK = 32
CMP_STRIDE = 16
CMP_HIDDEN = 256
SLC_BLOCK = 64
N_SELECT = 16
WINDOW = 512
Q_BLOCK = 128
D_FF = 2816
D_PLE = 256
EPS = 1e-6
NEG = -1e30
INT32_MIN = -2 ** 31
LOG2E = float(np.log2(np.e))
V_ROWS = HEAD_DIM + 16
GATE_ROWS = 16
SLC_SHIFT = SLC_BLOCK.bit_length() - 1
CMP_SHIFT = CMP_STRIDE.bit_length() - 1
assert 1 << SLC_SHIFT == SLC_BLOCK and 1 << CMP_SHIFT == CMP_STRIDE

LANES = 128
SUBLANES = 8
N_SLC_LANES = LANES
VMEM_LIMIT_BYTES = 56 * 1024 * 1024

F32 = jnp.float32
BF16 = jnp.bfloat16


def _dot(a, b):
    return jnp.dot(a, b, preferred_element_type=F32)


def _rms(x, g):
    return x * lax.rsqrt(jnp.mean(x * x, axis=-1, keepdims=True) + EPS) * g


def _head_rms_rope(x, g, bd, cos, sa, sb):
    w = x.shape[-1]
    msq = _dot((x * x).astype(BF16), bd)
    xn = x * lax.rsqrt(msq + EPS) * g
    return xn * cos + pltpu.roll(xn, w - ROT_DIM // 2, 1) * sa + pltpu.roll(xn, ROT_DIM // 2, 1) * sb


def _dup_halves(x):
    r = pltpu.roll(x, HEAD_DIM, 1)
    lane = lax.broadcasted_iota(jnp.int32, x.shape, 1)
    lo = lane < HEAD_DIM
    return jnp.where(lo, x, r), jnp.where(lo, r, x)


def _inproj_kernel(x_ref, lng_ref, wq_ref, wkv_ref, wg_ref, wcv_ref, qng_ref, kng_ref, bd512_ref, bd128_ref,
                   cos_ref, sa_ref, sb_ref, convw_ref, oncg_ref,
                   q_out, kc_out, vc_out, ksel_out, vs_out, kw_out, vw_out, gate_out, mconv_out,
                   zbuf, *, tiles_per_seq):
    tm = x_ref.shape[0]
    sub = tm // IN_SUB
    it = pl.program_id(0) % tiles_per_seq

    @pl.when(it == 0)
    def _():
        zbuf[0:SUBLANES, :] = jnp.zeros((SUBLANES, D_CONV), F32)

    for k in range(IN_SUB):
        r = slice(k * sub, (k + 1) * sub)
        xn = _rms(x_ref[r, :], lng_ref[...]).astype(BF16)

        cos, sa, sb = cos_ref[r, :], sa_ref[r, :], sb_ref[r, :]
        cos4, sa4, sb4 = (jnp.concatenate([t] * 4, axis=1) for t in (cos, sa, sb))
        q = _dot(xn, wq_ref[...])
        qr = _head_rms_rope(q, qng_ref[...], bd512_ref[...], cos4, sa4, sb4)
        qs = qr * (HEAD_DIM ** -0.5 * LOG2E)
        for blk_i in range(sub // Q_BLOCK):
            rows = slice(blk_i * Q_BLOCK, (blk_i + 1) * Q_BLOCK)
            for pr in range(N_Q_HEADS // 2):
                q_out[0, k * (sub // Q_BLOCK) + blk_i, pr] = qs[rows, pr * LANES:(pr + 1) * LANES].T.astype(BF16)

        kv = _dot(xn, wkv_ref[...])
        kc_out[r, :] = kv[:, 0 * KV_W:1 * KV_W]
        vc_out[r, :] = kv[:, 1 * KV_W:2 * KV_W]
        ks = _head_rms_rope(kv[:, 2 * KV_W:3 * KV_W], kng_ref[0:1, :], bd128_ref[...], cos, sa, sb)
        kw = _head_rms_rope(kv[:, 4 * KV_W:5 * KV_W], kng_ref[1:2, :], bd128_ref[...], cos, sa, sb)
        vs = kv[:, 3 * KV_W:4 * KV_W]
        vw = kv[:, 5 * KV_W:6 * KV_W]
        tpos = it * tm + k * sub + lax.broadcasted_iota(jnp.int32, (sub, N_SLC_LANES), 0)
        blk = lax.broadcasted_iota(jnp.int32, (sub, N_SLC_LANES), 1)
        onehot = jnp.where(lax.shift_right_logical(tpos, SLC_SHIFT) == blk, 1.0, 0.0).astype(BF16)
        ks_d, vs_d, kw_d, vw_d = (_dup_halves(t) for t in (ks, vs, kw, vw))
        vrow = lax.broadcasted_iota(jnp.int32, (V_ROWS, sub), 0)
        for g in range(N_KV_HEADS):
            ksel_out[0, g, r, :] = jnp.concatenate([ks_d[g].astype(BF16), onehot], axis=1)
            kw_out[0, g, r, :] = kw_d[g].astype(BF16)
            vst = jnp.where(vrow < HEAD_DIM, vs_d[g].T[:V_ROWS], 1.0).astype(BF16)
            for c in range(sub // SEL_TILE):
                vs_out[0, g, k * (sub // SEL_TILE) + c] = vst[:, c * SEL_TILE:(c + 1) * SEL_TILE]
            vwt = jnp.where(vrow < HEAD_DIM, vw_d[g].T[:V_ROWS], 1.0).astype(BF16)
            for c in range(sub // LANES):
                vw_out[0, g, k * (sub // LANES) + c] = vwt[:, c * LANES:(c + 1) * LANES]

        gates = jax.nn.sigmoid(_dot(xn, wg_ref[...]))
        for blk_i in range(sub // Q_BLOCK):
            rows = slice(blk_i * Q_BLOCK, (blk_i + 1) * Q_BLOCK)
            for g in range(N_KV_HEADS):
                gate_out[0, k * (sub // Q_BLOCK) + blk_i, g] = gates[rows, g * LANES:(g + 1) * LANES].T[:GATE_ROWS]

        cv = _dot(xn, wcv_ref[...])
        cb, cc, cx = cv[:, :D_CONV], cv[:, D_CONV:2 * D_CONV], cv[:, 2 * D_CONV:]
        z = cc * cx
        lo = SUBLANES + k * sub
        zbuf[lo:lo + sub, :] = z
        y = (convw_ref[2:3, :] * z + convw_ref[1:2, :] * zbuf[lo - 1:lo - 1 + sub, :]
             + convw_ref[0:1, :] * zbuf[lo - 2:lo - 2 + sub, :])
        mconv_out[r, :] = _rms(cb * y, oncg_ref[...]).astype(BF16)
    zbuf[0:SUBLANES, :] = zbuf[tm:tm + SUBLANES, :]


def _inproj(x2, lng, wq, wkv, wg, wcv, qng, kng, bd512, bd128, cos, sa, sb, convw, oncg, *, batch, seq, tm):
    n = batch * seq
    tps = seq // tm
    row = lambda w: pl.BlockSpec((tm, w), lambda i: (i, 0))
    full = lambda a: pl.BlockSpec(a.shape, lambda i: (0,) * a.ndim)
    tab = pl.BlockSpec((tm, LANES), lambda i: (i % tps, 0))
    grp = lambda w: pl.BlockSpec((1, N_KV_HEADS, tm, w), lambda i: (i // tps, 0, i % tps, 0))
    gshape = lambda w: jax.ShapeDtypeStruct((batch, N_KV_HEADS, seq, w), BF16)
    assert tm % (IN_SUB * SEL_TILE) == 0
    vtile = lambda keys: pl.BlockSpec((1, N_KV_HEADS, tm // keys, V_ROWS, keys),
                                      lambda i: (i // tps, 0, i % tps, 0, 0))
    vshape = lambda keys: jax.ShapeDtypeStruct((batch, N_KV_HEADS, seq // keys, V_ROWS, keys), BF16)
    qblk = lambda tile: pl.BlockSpec((1, tm // Q_BLOCK) + tile, lambda i: (i // tps, i % tps) + (0,) * len(tile))
    return pl.pallas_call(
        functools.partial(_inproj_kernel, tiles_per_seq=tps),
        grid=(n // tm,),
        in_specs=[row(D_MODEL), full(lng), full(wq), full(wkv), full(wg), full(wcv), full(qng), full(kng),
                  full(bd512), full(bd128), tab, tab, tab, full(convw), full(oncg)],
        out_specs=[qblk((N_Q_HEADS // 2, LANES, Q_BLOCK)), row(KV_W), row(KV_W), grp(2 * LANES), vtile(SEL_TILE),
                   grp(LANES), vtile(LANES), qblk((N_KV_HEADS, GATE_ROWS, Q_BLOCK)), row(D_CONV)],
        out_shape=[jax.ShapeDtypeStruct((batch, seq // Q_BLOCK, N_Q_HEADS // 2, LANES, Q_BLOCK), BF16),
                   jax.ShapeDtypeStruct((n, KV_W), F32),
                   jax.ShapeDtypeStruct((n, KV_W), F32), gshape(2 * LANES), vshape(SEL_TILE), gshape(LANES),
                   vshape(LANES),
                   jax.ShapeDtypeStruct((batch, seq // Q_BLOCK, N_KV_HEADS, GATE_ROWS, Q_BLOCK), F32),
                   jax.ShapeDtypeStruct((n, D_CONV), BF16)],
        scratch_shapes=[pltpu.VMEM((tm + SUBLANES, D_CONV), F32)],
        compiler_params=pltpu.CompilerParams(dimension_semantics=("arbitrary",),
                                             vmem_limit_bytes=VMEM_LIMIT_BYTES),
        name="inproj",
    )(x2, lng, wq, wkv, wg, wcv, qng, kng, bd512, bd128, cos, sa, sb, convw, oncg)


def _gelu_tanh(x):
    return 0.5 * x * (1.0 + jnp.tanh(np.sqrt(2.0 / np.pi).astype(np.float32) * (x + 0.044715 * (x * x * x))))


def _compress_kernel(kc_ref, vc_ref, w1ak_ref, w1bk_ref, w1av_ref, w1bv_ref, peak_ref, pebk_ref, peav_ref,
                     pebv_ref, w2k_ref, w2v_ref, kng_ref, bd128_ref, cos_ref, sa_ref, sb_ref,
                     kcmp_out, vcmp_out):
    nch = kc_ref.shape[1] // CMP_STRIDE

    def chunks(ref):
        return jnp.concatenate([ref[0, pl.ds(l, nch, stride=CMP_STRIDE), :] for l in range(CMP_STRIDE)], axis=1)

    def compress(x, w1a, w1b, pea, peb, w2):
        a = _dot((x + pea).astype(BF16), w1a)
        b = _dot((x + peb).astype(BF16), w1b)
        hid = _gelu_tanh(a + pltpu.roll(b, nch - 1, 0))
        return _dot(hid.astype(BF16), w2)

    kc = compress(chunks(kc_ref), w1ak_ref[...], w1bk_ref[...], peak_ref[...], pebk_ref[...], w2k_ref[...])
    vc = compress(chunks(vc_ref), w1av_ref[...], w1bv_ref[...], peav_ref[...], pebv_ref[...], w2v_ref[...])
    kc = _head_rms_rope(kc, kng_ref[...], bd128_ref[...], cos_ref[...], sa_ref[...], sb_ref[...])
    kd, vd = _dup_halves(kc), _dup_halves(vc)
    for g in range(N_KV_HEADS):
        kcmp_out[0, g] = kd[g].astype(BF16)
        vcmp_out[0, g] = vd[g].T.astype(BF16)


def _compress(kc3, vc3, w1ak, w1bk, w1av, w1bv, peak, pebk, peav, pebv, w2k, w2v, kng0, bd128, cos, sa, sb):
    batch, seq, wide = kc3.shape
    nch = seq // CMP_STRIDE
    full = lambda a: pl.BlockSpec(a.shape, lambda b: (0,) * a.ndim)
    tok = pl.BlockSpec((1, seq, wide), lambda b: (b, 0, 0))
    kout = pl.BlockSpec((1, N_KV_HEADS, nch, LANES), lambda b: (b, 0, 0, 0))
    vout = pl.BlockSpec((1, N_KV_HEADS, LANES, nch), lambda b: (b, 0, 0, 0))
    consts = (w1ak, w1bk, w1av, w1bv, peak, pebk, peav, pebv, w2k, w2v, kng0, bd128, cos, sa, sb)
    return pl.pallas_call(
        _compress_kernel,
        grid=(batch,),
        in_specs=[tok, tok] + [full(a) for a in consts],
        out_specs=[kout, vout],
        out_shape=[jax.ShapeDtypeStruct((batch, N_KV_HEADS, nch, LANES), BF16),
                   jax.ShapeDtypeStruct((batch, N_KV_HEADS, LANES, nch), BF16)],
        compiler_params=pltpu.CompilerParams(dimension_semantics=("arbitrary",),
                                             vmem_limit_bytes=VMEM_LIMIT_BYTES),
        name="compress",
    )(kc3, vc3, *consts)


SEL_TILE = 256
COLS = Q_PER_KV * Q_BLOCK
ROW_TILE = 512
IN_SUB = 2
POST_TILE = 512
RADIX_BITS = 1
Q_PER_STEP = 2
PIPE = 2
SWEEP_UNROLL = 2
assert SWEEP_UNROLL % PIPE == 0 and SEL_TILE % (Q_PER_STEP * Q_BLOCK) == 0
WIN_MASK_ROWS = 128
WIN_EXP_ROWS = 128
assert (WINDOW + Q_BLOCK) // WIN_MASK_ROWS + (WINDOW + Q_BLOCK) // WIN_EXP_ROWS <= 32


def _split2(x):
    hi = x.astype(BF16)
    return hi, (x - hi.astype(F32)).astype(BF16)


def _zero_after(x):
    return lax.shift_right_logical(lax.shift_right_logical(x, 16), 16)


def _normalize_dup(acc):
    o = acc[:HEAD_DIM] / acc[HEAD_DIM:HEAD_DIM + 1]
    return jnp.concatenate([o, o], axis=0)


def _attn_kernel(q_ref, ksel_ref, vst_ref, kw_ref, vwt_ref, kcmp_ref, vcmpt_ref, c2st_ref, gate_ref, o_ref,
                 s_all, p_all, a_all, cm_all, m_scr, acc_scr, acct_scr, cw_scr, sw_scr, pw_scr):
    i = pl.program_id(1)
    starts = [(Q_PER_STEP * i + d) * Q_BLOCK for d in range(Q_PER_STEP)]
    kv = lambda g: g % N_KV_HEADS
    qb = lambda g: g // N_KV_HEADS
    nch = kcmp_ref.shape[2]
    last_tile = ksel_ref.shape[2] // SEL_TILE - 1
    groups = range(Q_PER_STEP * N_KV_HEADS)

    lane = lax.broadcasted_iota(jnp.int32, (Q_BLOCK, LANES), 1)
    lo_half = lane < HEAD_DIM
    top_half = lax.broadcasted_iota(jnp.int32, (LANES, Q_BLOCK), 0) < HEAD_DIM
    zero_bf = jnp.zeros((LANES, Q_BLOCK), BF16)
    tqs = [st + (lax.broadcasted_iota(jnp.int32, (1, COLS), 1) & (Q_BLOCK - 1)) for st in starts]
    wk = WINDOW + Q_BLOCK
    bases = [pl.multiple_of(jnp.maximum(st - WINDOW, 0), Q_BLOCK) for st in starts]
    wrow = lax.broadcasted_iota(jnp.int32, (WIN_MASK_ROWS, COLS), 0)
    nrow = lax.broadcasted_iota(jnp.int32, (nch, COLS), 0)
    srow = lax.broadcasted_iota(jnp.int32, (N_SLC_LANES, Q_BLOCK), 0)
    tqqs = [st + lax.broadcasted_iota(jnp.int32, (N_SLC_LANES, Q_BLOCK), 1) for st in starts]
    c2st = c2st_ref[...]
    n_main = starts[0] // SEL_TILE

    qt = [jnp.concatenate([jnp.where(top_half if r % 2 == 0 else ~top_half,
                                     q_ref[0, qb(g), kv(g) * (Q_PER_KV // 2) + r // 2], zero_bf)
                           for r in range(Q_PER_KV)], axis=1) for g in groups]

    def grow(g, b):
        gt = gate_ref[0, qb(g), kv(g)]
        return jnp.concatenate([gt[r * N_BRANCH + b:r * N_BRANCH + b + 1, :] for r in range(Q_PER_KV)], axis=1)

    s_c = [_dot(kcmp_ref[0, kv(g)], qt[g]) for g in groups]
    for g in groups:
        sw_scr[g] = _dot(kw_ref[0, kv(g), pl.ds(bases[qb(g)], wk), :], qt[g])

    last_valid = [lax.shift_right_arithmetic(t - (CMP_BLOCK - 1), CMP_SHIFT) for t in tqs]
    curs = [lax.shift_right_logical(t, SLC_SHIFT) for t in tqqs]
    forced = [(srow == 0) | (srow == c) | (srow == c - 1) for c in curs]
    future = [srow * SLC_BLOCK > t for t in tqqs]
    impt = []
    for g in groups:
        sc = jnp.where(nrow <= last_valid[qb(g)], s_c[g], NEG)
        m_c = jnp.max(sc, axis=0, keepdims=True)
        e_c = jnp.exp2(sc - m_c)
        l_c = jnp.sum(e_c, axis=0, keepdims=True)
        p_c = e_c * jnp.where(m_c > 0.5 * NEG, 1.0 / l_c, 0.0)
        cw_scr[g] = grow(g, 0) * _dot(vcmpt_ref[0, kv(g)], p_c.astype(BF16))
        psum = sum(p_c[:, r * Q_BLOCK:(r + 1) * Q_BLOCK] for r in range(Q_PER_KV))
        imp = sum(_dot(c2st, t) for t in _split2(psum))
        impt.append(jnp.where(forced[qb(g)], 1e9, jnp.where(future[qb(g)], -1e9, imp)))

    def window_mask_piece(g, k, colmax, zero):
        rows = slice(WIN_MASK_ROWS * k, WIN_MASK_ROWS * (k + 1))
        back = (tqs[qb(g)] + zero - bases[qb(g)] - WIN_MASK_ROWS * k) - wrow
        in_window = lax.bitcast_convert_type(back, jnp.uint32) < jnp.uint32(WINDOW)
        sm = jnp.where(in_window, sw_scr[g, rows, :], NEG)
        sw_scr[g, rows, :] = sm
        for c in range(WIN_MASK_ROWS // SUBLANES):
            colmax = jnp.maximum(colmax, sm[c * SUBLANES:(c + 1) * SUBLANES])
        return colmax

    def window_exp_piece(g, k, m_w, zero):
        rows = slice(WIN_EXP_ROWS * k, WIN_EXP_ROWS * (k + 1))
        pw_scr[g, rows, :] = jnp.exp2(sw_scr[g, rows, :] - (m_w + zero.astype(F32))).astype(BF16)

    key_to_float = lambda k: lax.bitcast_convert_type(jnp.where(k >= 0, k, k ^ jnp.int32(0x7FFFFFFF)), F32)
    wrap32 = lambda v: ((v + 2 ** 31) % 2 ** 32) - 2 ** 31
    n_mask, n_exp = wk // WIN_MASK_ROWS, wk // WIN_EXP_ROWS
    thr = [jnp.full((1, Q_BLOCK), INT32_MIN, jnp.int32) for _ in groups]
    colmax = [jnp.full((SUBLANES, COLS), NEG, F32) for _ in groups]
    m_w = [None for _ in groups]
    piece = 0
    for b in range(32 - RADIX_BITS, -1, -RADIX_BITS):
        for g in groups:
            reached = thr[g]
            for mult in range(1, 2 ** RADIX_BITS):
                cand = thr[g] + jnp.int32(wrap32(mult << b))
                n_ge = jnp.sum(jnp.where(impt[g] >= key_to_float(cand), 1.0, 0.0), axis=0, keepdims=True)
                reached = jnp.where(n_ge >= N_SELECT, cand, reached)
            thr[g] = reached
        for _ in range(RADIX_BITS):
            for g in groups:
                zero = jnp.concatenate([_zero_after(thr[g])] * Q_PER_KV, axis=1)
                if piece < n_mask:
                    colmax[g] = window_mask_piece(g, piece, colmax[g], zero)
                elif piece < n_mask + n_exp:
                    if piece == n_mask:
                        m_w[g] = jnp.max(colmax[g], axis=0, keepdims=True)
                    window_exp_piece(g, piece - n_mask, m_w[g], zero)
            piece += 1

    scol = lax.broadcasted_iota(jnp.int32, (N_SLC_LANES, N_SLC_LANES), 1)
    earlier = jnp.where(scol < srow, 1.0, 0.0).astype(BF16)
    widen = lambda x: jnp.concatenate([x.astype(BF16)] * Q_PER_KV, axis=1)
    qt_tail, qt_main = [], []
    for g in groups:
        vw = jnp.concatenate([vwt_ref[0, kv(g), bases[qb(g)] // LANES + c] for c in range(wk // LANES)],
                             axis=1)
        cw_scr[g] += grow(g, 2) * _normalize_dup(_dot(vw, pw_scr[g]))
        kth = key_to_float(thr[g])
        above = impt[g] > kth
        tied = impt[g] == kth
        n_above = jnp.sum(jnp.where(above, 1.0, 0.0), axis=0, keepdims=True)
        tied_before = _dot(earlier, jnp.where(tied, 1.0, 0.0).astype(BF16))
        selected = above | (tied & (tied_before < N_SELECT - n_above))
        bias = jnp.where(selected, 0.0, NEG)
        bias_main = jnp.where(srow >= n_main * (SEL_TILE // SLC_BLOCK), NEG, bias)
        qt_tail.append(jnp.concatenate([qt[g], widen(bias)], axis=0))
        qt_main.append(jnp.concatenate([qt[g], widen(bias_main)], axis=0))

    kt = pl.multiple_of(n_main * SEL_TILE, SEL_TILE)
    krow = lax.broadcasted_iota(jnp.int32, (SEL_TILE, COLS), 0)
    tile0 = slice(0, SEL_TILE)
    for g in groups:
        sw_scr[g, tile0, :] = _dot(ksel_ref[0, kv(g), pl.ds(kt, SEL_TILE), :], qt_tail[g])
    g_sel = [grow(g, 1) for g in groups]

    def scores(g, j, slot):
        k0 = pl.multiple_of(jnp.minimum(j, last_tile) * SEL_TILE, SEL_TILE)
        sv = _dot(ksel_ref[0, kv(g), pl.ds(k0, SEL_TILE), :], qt_main[g])
        s_all[g, slot] = sv
        cm = sv[0:SUBLANES]
        for c in range(1, SEL_TILE // SUBLANES):
            cm = jnp.maximum(cm, sv[c * SUBLANES:(c + 1) * SUBLANES])
        cm_all[g, slot] = cm

    def softmax(g, slot):
        s = s_all[g, slot]
        m_prev = m_scr[g, 0:1, :]
        m_new = jnp.maximum(m_prev, jnp.max(cm_all[g, slot], axis=0, keepdims=True))
        a_all[g, slot] = jnp.broadcast_to(jnp.exp2(m_prev - m_new), (SUBLANES, COLS))
        p_all[g, slot] = jnp.exp2(s - m_new).astype(BF16)
        m_scr[g] = jnp.broadcast_to(m_new, (SUBLANES, COLS))

    def values(g, j, slot):
        acc_scr[g] = (a_all[g, slot, 0:1, :] * acc_scr[g]
                      + _dot(vst_ref[0, kv(g), jnp.minimum(j, last_tile)], p_all[g, slot]))

    m_scr[...] = jnp.full(m_scr.shape, NEG, F32)
    acc_scr[...] = jnp.zeros(acc_scr.shape, F32)
    for k in range(PIPE):
        for g in groups:
            scores(g, k, k)
    m_t = []
    for g in groups:
        s_t = jnp.where(kt + krow <= tqs[qb(g)], sw_scr[g, tile0, :], NEG)
        m_t.append(jnp.max(s_t, axis=0, keepdims=True))
        pw_scr[g, tile0, :] = jnp.exp2(s_t - m_t[g]).astype(BF16)
    for g in groups:
        acct_scr[g] = _dot(vst_ref[0, kv(g), n_main], pw_scr[g, tile0, :])
    for k in range(PIPE // 2):
        for g in groups:
            softmax(g, k)

    def sweep(t, carry):
        for k in range(SWEEP_UNROLL):
            for g in groups:
                scores(g, SWEEP_UNROLL * t + k + PIPE, k % PIPE)
            for g in groups:
                values(g, SWEEP_UNROLL * t + k, k % PIPE)
            for g in groups:
                softmax(g, (k + PIPE // 2) % PIPE)
        return carry

    lax.fori_loop(0, (n_main + SWEEP_UNROLL - 1) // SWEEP_UNROLL, sweep, 0)

    pairs = [[] for _ in range(Q_PER_STEP)]
    for g in groups:
        m_p = m_scr[g, 0:1, :]
        m_tot = jnp.maximum(m_p, m_t[g])
        o_s = _normalize_dup(jnp.exp2(m_p - m_tot) * acc_scr[g] + jnp.exp2(m_t[g] - m_tot) * acct_scr[g])
        comb = cw_scr[g] + g_sel[g] * o_s
        outs = [comb[:, r * Q_BLOCK:(r + 1) * Q_BLOCK].T for r in range(Q_PER_KV)]
        pairs[qb(g)] += [jnp.where(lo_half, outs[0], outs[1]), jnp.where(lo_half, outs[2], outs[3])]
    for d in range(Q_PER_STEP):
        o_ref[0, d * Q_BLOCK:(d + 1) * Q_BLOCK, :] = jnp.concatenate(pairs[d], axis=1)


def _attn(qpt, ksel, vst, kw, vwt, kcmp, vcmpt, c2st, gatest):
    batch, seq = qpt.shape[0], qpt.shape[1] * Q_BLOCK
    per_batch = lambda a: pl.BlockSpec((1,) + a.shape[1:], lambda b, i: (b,) + (0,) * (a.ndim - 1),
                                       pipeline_mode=pl.Buffered(1))
    per_block = lambda a: pl.BlockSpec((1, Q_PER_STEP) + a.shape[2:], lambda b, i: (b, i) + (0,) * (a.ndim - 2))
    chains = Q_PER_STEP * N_KV_HEADS
    grp = lambda *shape: pltpu.VMEM((chains,) + shape, F32)
    return pl.pallas_call(
        _attn_kernel,
        grid=(batch, seq // (Q_PER_STEP * Q_BLOCK)),
        in_specs=[per_block(qpt), per_batch(ksel), per_batch(vst), per_batch(kw), per_batch(vwt), per_batch(kcmp),
                  per_batch(vcmpt), pl.BlockSpec(c2st.shape, lambda b, i: (0, 0)), per_block(gatest)],
        out_specs=pl.BlockSpec((1, Q_PER_STEP * Q_BLOCK, D_ATT), lambda b, i: (b, i, 0)),
        out_shape=jax.ShapeDtypeStruct((batch, seq, D_ATT), F32),
        scratch_shapes=[grp(PIPE, SEL_TILE, COLS), pltpu.VMEM((chains, PIPE, SEL_TILE, COLS), BF16),
                        grp(PIPE, SUBLANES, COLS), grp(PIPE, SUBLANES, COLS), grp(SUBLANES, COLS),
                        grp(V_ROWS, COLS), grp(V_ROWS, COLS),
                        grp(LANES, COLS), grp(WINDOW + Q_BLOCK, COLS),
                        pltpu.VMEM((chains, WINDOW + Q_BLOCK, COLS), BF16)],
        compiler_params=pltpu.CompilerParams(dimension_semantics=("arbitrary", "arbitrary"),
                                             vmem_limit_bytes=VMEM_LIMIT_BYTES),
        name="attn",
    )(qpt, ksel, vst, kw, vwt, kcmp, vcmpt, c2st, gatest)


POST_SUB = 2


def _post_kernel(x_ref, oatt_ref, mconv_ref, p_ref, onag_ref, wo_ref, lnf_ref, wup_ref, fcw_ref, fcb_ref,
                 wdn_ref, lnp_ref, wpg_ref, wpe_ref, out_ref, gbuf, *, tiles_per_seq):
    tm = x_ref.shape[0]
    sub = tm // POST_SUB
    it = pl.program_id(0) % tiles_per_seq
    subs = [slice(k * sub, (k + 1) * sub) for k in range(POST_SUB)]

    @pl.when(it == 0)
    def _():
        gbuf[0:SUBLANES, :] = jnp.zeros((SUBLANES, D_FF), F32)

    h1 = [x_ref[r, :] + _dot(jnp.concatenate([_rms(oatt_ref[r, :], onag_ref[...]).astype(BF16), mconv_ref[r, :]],
                                             axis=1), wo_ref[...]) for r in subs]
    xn = [_rms(h, lnf_ref[...]).astype(BF16) for h in h1]
    h2 = []
    for k, r in enumerate(subs):
        gpre = _dot(xn[k], wup_ref[:, :D_FF])
        up = _dot(xn[k], wup_ref[:, D_FF:])
        lo = SUBLANES + k * sub
        gbuf[lo:lo + sub, :] = gpre
        gate = (fcw_ref[2:3, :] * gpre + fcw_ref[1:2, :] * gbuf[lo - 1:lo - 1 + sub, :]
                + fcw_ref[0:1, :] * gbuf[lo - 2:lo - 2 + sub, :]) + fcb_ref[...]
        act = (gate * jax.nn.sigmoid(gate) * up).astype(BF16)
        h2.append(h1[k] + _dot(act, wdn_ref[...]))
    gbuf[0:SUBLANES, :] = gbuf[tm:tm + SUBLANES, :]
    for k, r in enumerate(subs):
        xn2 = _rms(h2[k], lnp_ref[...]).astype(BF16)
        out_ref[r, :] = (h2[k] + jax.nn.sigmoid(_dot(xn2, wpg_ref[...]))
                         * _dot(p_ref[r, :].astype(BF16), wpe_ref[...]))


def _post(x2, oatt, mconv, p2, onag, wo, lnf, wup, fcw, fcb, wdn, lnp, wpg, wpe, *, seq, tm):
    n = x2.shape[0]
    tps = seq // tm
    row = lambda w: pl.BlockSpec((tm, w), lambda i: (i, 0))
    full = lambda a: pl.BlockSpec(a.shape, lambda i: (0,) * a.ndim, pipeline_mode=pl.Buffered(1))
    return pl.pallas_call(
        functools.partial(_post_kernel, tiles_per_seq=tps),
        grid=(n // tm,),
        in_specs=[row(D_MODEL), row(D_ATT), row(D_CONV), row(D_PLE), full(onag), full(wo), full(lnf), full(wup),
                  full(fcw), full(fcb), full(wdn), full(lnp), full(wpg), full(wpe)],
        out_specs=row(D_MODEL),
        out_shape=jax.ShapeDtypeStruct((n, D_MODEL), F32),
        scratch_shapes=[pltpu.VMEM((tm + SUBLANES, D_FF), F32)],
        compiler_params=pltpu.CompilerParams(dimension_semantics=("arbitrary",),
                                             vmem_limit_bytes=VMEM_LIMIT_BYTES),
        name="post",
    )(x2, oatt, mconv, p2, onag, wo, lnf, wup, fcw, fcb, wdn, lnp, wpg, wpe)


def _rope_tables(pos):
    half = ROT_DIM // 2
    d = np.arange(LANES) % HEAD_DIM
    inv_freq = np.float64(ROPE_THETA) ** (-np.arange(half, dtype=np.float64) * 2.0 / ROT_DIM)
    ang = np.asarray(pos, np.float64)[:, None] * np.tile(inv_freq, LANES // half)[None, :]
    c, sn = np.cos(ang), np.sin(ang)
    first, second = (d < half)[None, :], ((d >= half) & (d < ROT_DIM))[None, :]
    cos = np.where(first | second, c, 1.0)
    sa = np.where(first, -sn, 0.0)
    sb = np.where(second, sn, 0.0)
    return tuple(jnp.asarray(t, F32) for t in (cos, sa, sb))


def _block_diag_mean(width):
    idx = np.arange(width) // HEAD_DIM
    return jnp.asarray((idx[:, None] == idx[None, :]).astype(np.float32) / HEAD_DIM, BF16)


def _cmp_to_slc_t(nch):
    cs = CMP_STRIDE * np.arange(nch)[None, :]
    ss = SLC_BLOCK * np.arange(N_SLC_LANES)[:, None]
    ov = np.clip(np.minimum(cs + CMP_BLOCK, ss + SLC_BLOCK) - np.maximum(cs, ss), 0, None)
    return jnp.asarray(ov.astype(np.float32) / CMP_BLOCK, BF16)


def _layer(h, p_l, ln_mix_g, w_in, qn_g, kn_g, pe_k, pe_v, w_ck1, w_ck2, w_cv1, w_cv2, conv_w, on_att_g,
           on_conv_g, w_o, ln_ffn_g, w_up, ffn_conv_w, ffn_conv_b, w_down, ln_ple_g, w_pg, w_pe):
    batch, seq, _ = h.shape
    assert seq % SEL_TILE == 0 and seq // SLC_BLOCK <= N_SLC_LANES and seq // SLC_BLOCK >= N_SELECT
    assert seq >= WINDOW + Q_BLOCK
    n = batch * seq
    nch = seq // CMP_STRIDE
    x2 = h.reshape(n, D_MODEL)
    row = lambda v: v.reshape(1, -1).astype(F32)

    o_q, o_kv, o_g, o_cv = 0, D_ATT, D_ATT + 6 * KV_W, D_ATT + 6 * KV_W + N_BRANCH * N_Q_HEADS
    wq = w_in[:, o_q:o_kv].astype(BF16)
    wkv = w_in[:, o_kv:o_g].astype(BF16)
    wcv = w_in[:, o_cv:].astype(BF16)
    per_g = Q_PER_KV * N_BRANCH
    wg = jnp.concatenate(
        [jnp.pad(w_in[:, o_g + g * per_g:o_g + (g + 1) * per_g], ((0, 0), (0, LANES - per_g)))
         for g in range(N_KV_HEADS)], axis=1).astype(BF16)
    cos, sa, sb = _rope_tables(np.arange(seq))
    bd512, bd128 = _block_diag_mean(D_ATT), _block_diag_mean(KV_W)
    tile_heads = lambda v, k: jnp.tile(v.astype(F32), k).reshape(1, -1)

    tm = ROW_TILE
    assert seq % ROW_TILE == 0 and seq % POST_TILE == 0
    qpt, kc2, vc2, ksel, vst, kw, vwt, gatest, mconv = _inproj(
        x2, row(ln_mix_g), wq, wkv, wg, wcv, tile_heads(qn_g, N_Q_HEADS),
        jnp.stack([jnp.tile(kn_g[1], N_KV_HEADS), jnp.tile(kn_g[2], N_KV_HEADS)]).astype(F32),
        bd512, bd128, cos, sa, sb, conv_w.astype(F32), row(on_conv_g), batch=batch, seq=seq, tm=tm)

    assert N_KV_HEADS == 2
    half = CMP_BLOCK // 2

    def w1_parts(w1):
        w = w1.astype(BF16)
        z = jnp.zeros_like(w)
        wfull = jnp.concatenate([jnp.concatenate([w, z], axis=2), jnp.concatenate([z, w], axis=2)], axis=1)
        return wfull[:half].reshape(half * KV_W, -1), wfull[half:].reshape(half * KV_W, -1)

    def pe_parts(pe):
        pf = jnp.broadcast_to(pe[:, None, :], (CMP_BLOCK, N_KV_HEADS, HEAD_DIM)).astype(F32)
        return pf[:half].reshape(1, -1), pf[half:].reshape(1, -1)

    def w2bd(w2):
        w = w2.astype(BF16)
        z = jnp.zeros_like(w)
        return jnp.concatenate([jnp.concatenate([w, z], axis=1), jnp.concatenate([z, w], axis=1)], axis=0)
    w1ak, w1bk = w1_parts(w_ck1)
    w1av, w1bv = w1_parts(w_cv1)
    peak, pebk = pe_parts(pe_k)
    peav, pebv = pe_parts(pe_v)
    ccos, csa, csb = _rope_tables(CMP_STRIDE * np.arange(nch) + CMP_BLOCK - 1)
    kcmp, vcmpt = _compress(kc2.reshape(batch, seq, KV_W), vc2.reshape(batch, seq, KV_W),
                            w1ak, w1bk, w1av, w1bv, peak, pebk, peav, pebv, w2bd(w_ck2), w2bd(w_cv2),
                            tile_heads(kn_g[0], N_KV_HEADS), bd128, ccos, csa, csb)

    oatt = _attn(qpt, ksel, vst, kw, vwt, kcmp, vcmpt, _cmp_to_slc_t(nch), gatest)

    out = _post(x2, oatt.reshape(n, D_ATT), mconv, p_l.reshape(n, D_PLE), row(on_att_g), w_o.astype(BF16),
                row(ln_ffn_g), w_up.astype(BF16), ffn_conv_w.astype(F32), row(ffn_conv_b), w_down.astype(BF16),
                row(ln_ple_g), w_pg.astype(BF16), w_pe.astype(BF16), seq=seq, tm=POST_TILE)
    return out.reshape(batch, seq, D_MODEL)


def kernel(x, p, ln_mix_g, w_in, qn_g, kn_g, pe_k, pe_v, w_ck1, w_ck2, w_cv1, w_cv2, conv_w, on_att_g,
           on_conv_g, w_o, ln_ffn_g, w_up, ffn_conv_w, ffn_conv_b, w_down, ln_ple_g, w_pg, w_pe):
    h = x
    for i in range(p.shape[0]):
        h = _layer(h, p[i], ln_mix_g[i], w_in[i], qn_g[i], kn_g[i], pe_k[i], pe_v[i], w_ck1[i], w_ck2[i],
                   w_cv1[i], w_cv2[i], conv_w[i], on_att_g[i], on_conv_g[i], w_o[i], ln_ffn_g[i], w_up[i],
                   ffn_conv_w[i], ffn_conv_b[i], w_down[i], ln_ple_g[i], w_pg[i], w_pe[i])
    return h
```

```python
import functools

import jax
import jax.numpy as jnp
import numpy as np
from jax import lax
from jax.experimental import pallas as pl
from jax.experimental.pallas import tpu as pltpu

D_MODEL = 1024
HEAD_DIM = 64
N_Q_HEADS = 8
N_KV_HEADS = 2
Q_PER_KV = N_Q_HEADS // N_KV_HEADS
D_ATT = N_Q_HEADS * HEAD_DIM
D_CONV = D_MODEL - D_ATT
KV_W = N_KV_HEADS * HEAD_DIM
N_BRANCH = 3
CONV_TAPS = 3
ROT_DIM = HEAD_DIM // 4
ROPE_THETA = 500000.0
CMP_BLOCK = 32
CMP_STRIDE = 16
CMP_HIDDEN = 256
SLC_BLOCK = 64
N_SELECT = 16
WINDOW = 512
Q_BLOCK = 128
D_FF = 2816
D_PLE = 256
EPS = 1e-6
NEG = -1e30
INT32_MIN = -2 ** 31
LOG2E = float(np.log2(np.e))
V_ROWS = HEAD_DIM + 16
GATE_ROWS = 16
SLC_SHIFT = SLC_BLOCK.bit_length() - 1
CMP_SHIFT = CMP_STRIDE.bit_length() - 1
assert 1 << SLC_SHIFT == SLC_BLOCK and 1 << CMP_SHIFT == CMP_STRIDE

LANES = 128
SUBLANES = 8
N_SLC_LANES = LANES
VMEM_LIMIT_BYTES = 56 * 1024 * 1024

F32 = jnp.float32
BF16 = jnp.bfloat16


def _dot(a, b):
    return jnp.dot(a, b, preferred_element_type=F32)


def _rms(x, g):
    return x * lax.rsqrt(jnp.mean(x * x, axis=-1, keepdims=True) + EPS) * g


def _head_rms_rope(x, g, bd, cos, sa, sb):
    w = x.shape[-1]
    msq = _dot((x * x).astype(BF16), bd)
    xn = x * lax.rsqrt(msq + EPS) * g
    return xn * cos + pltpu.roll(xn, w - ROT_DIM // 2, 1) * sa + pltpu.roll(xn, ROT_DIM // 2, 1) * sb


def _dup_halves(x):
    r = pltpu.roll(x, HEAD_DIM, 1)
    lane = lax.broadcasted_iota(jnp.int32, x.shape, 1)
    lo = lane < HEAD_DIM
    return jnp.where(lo, x, r), jnp.where(lo, r, x)


def _inproj_kernel(x_ref, lng_ref, wq_ref, wkv_ref, wg_ref, wcv_ref, qng_ref, kng_ref, bd512_ref, bd128_ref,
                   cos_ref, sa_ref, sb_ref, convw_ref, oncg_ref,
                   q_out, kc_out, vc_out, ksel_out, vs_out, kw_out, vw_out, gate_out, mconv_out,
                   zbuf, *, tiles_per_seq):
    tm = x_ref.shape[0]
    sub = tm // IN_SUB
    it = pl.program_id(0) % tiles_per_seq

    @pl.when(it == 0)
    def _():
        zbuf[0:SUBLANES, :] = jnp.zeros((SUBLANES, D_CONV), F32)

    for k in range(IN_SUB):
        r = slice(k * sub, (k + 1) * sub)
        xn = _rms(x_ref[r, :], lng_ref[...]).astype(BF16)

        cos, sa, sb = cos_ref[r, :], sa_ref[r, :], sb_ref[r, :]
        cos4, sa4, sb4 = (jnp.concatenate([t] * 4, axis=1) for t in (cos, sa, sb))
        q = _dot(xn, wq_ref[...])
        qr = _head_rms_rope(q, qng_ref[...], bd512_ref[...], cos4, sa4, sb4)
        qs = qr * (HEAD_DIM ** -0.5 * LOG2E)
        for blk_i in range(sub // Q_BLOCK):
            rows = slice(blk_i * Q_BLOCK, (blk_i + 1) * Q_BLOCK)
            for pr in range(N_Q_HEADS // 2):
                q_out[0, k * (sub // Q_BLOCK) + blk_i, pr] = qs[rows, pr * LANES:(pr + 1) * LANES].T.astype(BF16)

        kv = _dot(xn, wkv_ref[...])
        kc_out[r, :] = kv[:, 0 * KV_W:1 * KV_W]
        vc_out[r, :] = kv[:, 1 * KV_W:2 * KV_W]
        ks = _head_rms_rope(kv[:, 2 * KV_W:3 * KV_W], kng_ref[0:1, :], bd128_ref[...], cos, sa, sb)
        kw = _head_rms_rope(kv[:, 4 * KV_W:5 * KV_W], kng_ref[1:2, :], bd128_ref[...], cos, sa, sb)
        vs = kv[:, 3 * KV_W:4 * KV_W]
        vw = kv[:, 5 * KV_W:6 * KV_W]
        tpos = it * tm + k * sub + lax.broadcasted_iota(jnp.int32, (sub, N_SLC_LANES), 0)
        blk = lax.broadcasted_iota(jnp.int32, (sub, N_SLC_LANES), 1)
        onehot = jnp.where(lax.shift_right_logical(tpos, SLC_SHIFT) == blk, 1.0, 0.0).astype(BF16)
        ks_d, vs_d, kw_d, vw_d = (_dup_halves(t) for t in (ks, vs, kw, vw))
        vrow = lax.broadcasted_iota(jnp.int32, (V_ROWS, sub), 0)
        for g in range(N_KV_HEADS):
            ksel_out[0, g, r, :] = jnp.concatenate([ks_d[g].astype(BF16), onehot], axis=1)
            kw_out[0, g, r, :] = kw_d[g].astype(BF16)
            vst = jnp.where(vrow < HEAD_DIM, vs_d[g].T[:V_ROWS], 1.0).astype(BF16)
            for c in range(sub // SEL_TILE):
                vs_out[0, g, k * (sub // SEL_TILE) + c] = vst[:, c * SEL_TILE:(c + 1) * SEL_TILE]
            vwt = jnp.where(vrow < HEAD_DIM, vw_d[g].T[:V_ROWS], 1.0).astype(BF16)
            for c in range(sub // LANES):
                vw_out[0, g, k * (sub // LANES) + c] = vwt[:, c * LANES:(c + 1) * LANES]

        gates = jax.nn.sigmoid(_dot(xn, wg_ref[...]))
        for blk_i in range(sub // Q_BLOCK):
            rows = slice(blk_i * Q_BLOCK, (blk_i + 1) * Q_BLOCK)
            for g in range(N_KV_HEADS):
                gate_out[0, k * (sub // Q_BLOCK) + blk_i, g] = gates[rows, g * LANES:(g + 1) * LANES].T[:GATE_ROWS]

        cv = _dot(xn, wcv_ref[...])
        cb, cc, cx = cv[:, :D_CONV], cv[:, D_CONV:2 * D_CONV], cv[:, 2 * D_CONV:]
        z = cc * cx
        lo = SUBLANES + k * sub
        zbuf[lo:lo + sub, :] = z
        y = (convw_ref[2:3, :] * z + convw_ref[1:2, :] * zbuf[lo - 1:lo - 1 + sub, :]
             + convw_ref[0:1, :] * zbuf[lo - 2:lo - 2 + sub, :])
        mconv_out[r, :] = _rms(cb * y, oncg_ref[...]).astype(BF16)
    zbuf[0:SUBLANES, :] = zbuf[tm:tm + SUBLANES, :]


def _inproj(x2, lng, wq, wkv, wg, wcv, qng, kng, bd512, bd128, cos, sa, sb, convw, oncg, *, batch, seq, tm):
    n = batch * seq
    tps = seq // tm
    row = lambda w: pl.BlockSpec((tm, w), lambda i: (i, 0))
    full = lambda a: pl.BlockSpec(a.shape, lambda i: (0,) * a.ndim)
    tab = pl.BlockSpec((tm, LANES), lambda i: (i % tps, 0))
    grp = lambda w: pl.BlockSpec((1, N_KV_HEADS, tm, w), lambda i: (i // tps, 0, i % tps, 0))
    gshape = lambda w: jax.ShapeDtypeStruct((batch, N_KV_HEADS, seq, w), BF16)
    assert tm % (IN_SUB * SEL_TILE) == 0
    vtile = lambda keys: pl.BlockSpec((1, N_KV_HEADS, tm // keys, V_ROWS, keys),
                                      lambda i: (i // tps, 0, i % tps, 0, 0))
    vshape = lambda keys: jax.ShapeDtypeStruct((batch, N_KV_HEADS, seq // keys, V_ROWS, keys), BF16)
    qblk = lambda tile: pl.BlockSpec((1, tm // Q_BLOCK) + tile, lambda i: (i // tps, i % tps) + (0,) * len(tile))
    return pl.pallas_call(
        functools.partial(_inproj_kernel, tiles_per_seq=tps),
        grid=(n // tm,),
        in_specs=[row(D_MODEL), full(lng), full(wq), full(wkv), full(wg), full(wcv), full(qng), full(kng),
                  full(bd512), full(bd128), tab, tab, tab, full(convw), full(oncg)],
        out_specs=[qblk((N_Q_HEADS // 2, LANES, Q_BLOCK)), row(KV_W), row(KV_W), grp(2 * LANES), vtile(SEL_TILE),
                   grp(LANES), vtile(LANES), qblk((N_KV_HEADS, GATE_ROWS, Q_BLOCK)), row(D_CONV)],
        out_shape=[jax.ShapeDtypeStruct((batch, seq // Q_BLOCK, N_Q_HEADS // 2, LANES, Q_BLOCK), BF16),
                   jax.ShapeDtypeStruct((n, KV_W), F32),
                   jax.ShapeDtypeStruct((n, KV_W), F32), gshape(2 * LANES), vshape(SEL_TILE), gshape(LANES),
                   vshape(LANES),
                   jax.ShapeDtypeStruct((batch, seq // Q_BLOCK, N_KV_HEADS, GATE_ROWS, Q_BLOCK), F32),
                   jax.ShapeDtypeStruct((n, D_CONV), BF16)],
        scratch_shapes=[pltpu.VMEM((tm + SUBLANES, D_CONV), F32)],
        compiler_params=pltpu.CompilerParams(dimension_semantics=("arbitrary",),
                                             vmem_limit_bytes=VMEM_LIMIT_BYTES),
        name="inproj",
    )(x2, lng, wq, wkv, wg, wcv, qng, kng, bd512, bd128, cos, sa, sb, convw, oncg)


def _gelu_tanh(x):
    return 0.5 * x * (1.0 + jnp.tanh(np.sqrt(2.0 / np.pi).astype(np.float32) * (x + 0.044715 * (x * x * x))))


def _compress_kernel(kc_ref, vc_ref, w1ak_ref, w1bk_ref, w1av_ref, w1bv_ref, peak_ref, pebk_ref, peav_ref,
                     pebv_ref, w2k_ref, w2v_ref, kng_ref, bd128_ref, cos_ref, sa_ref, sb_ref,
                     kcmp_out, vcmp_out):
    nch = kc_ref.shape[1] // CMP_STRIDE

    def chunks(ref):
        return jnp.concatenate([ref[0, pl.ds(l, nch, stride=CMP_STRIDE), :] for l in range(CMP_STRIDE)], axis=1)

    def compress(x, w1a, w1b, pea, peb, w2):
        a = _dot((x + pea).astype(BF16), w1a)
        b = _dot((x + peb).astype(BF16), w1b)
        hid = _gelu_tanh(a + pltpu.roll(b, nch - 1, 0))
        return _dot(hid.astype(BF16), w2)

    kc = compress(chunks(kc_ref), w1ak_ref[...], w1bk_ref[...], peak_ref[...], pebk_ref[...], w2k_ref[...])
    vc = compress(chunks(vc_ref), w1av_ref[...], w1bv_ref[...], peav_ref[...], pebv_ref[...], w2v_ref[...])
    kc = _head_rms_rope(kc, kng_ref[...], bd128_ref[...], cos_ref[...], sa_ref[...], sb_ref[...])
    kd, vd = _dup_halves(kc), _dup_halves(vc)
    for g in range(N_KV_HEADS):
        kcmp_out[0, g] = kd[g].astype(BF16)
        vcmp_out[0, g] = vd[g].T.astype(BF16)


def _compress(kc3, vc3, w1ak, w1bk, w1av, w1bv, peak, pebk, peav, pebv, w2k, w2v, kng0, bd128, cos, sa, sb):
    batch, seq, wide = kc3.shape
    nch = seq // CMP_STRIDE
    full = lambda a: pl.BlockSpec(a.shape, lambda b: (0,) * a.ndim)
    tok = pl.BlockSpec((1, seq, wide), lambda b: (b, 0, 0))
    kout = pl.BlockSpec((1, N_KV_HEADS, nch, LANES), lambda b: (b, 0, 0, 0))
    vout = pl.BlockSpec((1, N_KV_HEADS, LANES, nch), lambda b: (b, 0, 0, 0))
    consts = (w1ak, w1bk, w1av, w1bv, peak, pebk, peav, pebv, w2k, w2v, kng0, bd128, cos, sa, sb)
    return pl.pallas_call(
        _compress_kernel,
        grid=(batch,),
        in_specs=[tok, tok] + [full(a) for a in consts],
        out_specs=[kout, vout],
        out_shape=[jax.ShapeDtypeStruct((batch, N_KV_HEADS, nch, LANES), BF16),
                   jax.ShapeDtypeStruct((batch, N_KV_HEADS, LANES, nch), BF16)],
        compiler_params=pltpu.CompilerParams(dimension_semantics=("arbitrary",),
                                             vmem_limit_bytes=VMEM_LIMIT_BYTES),
        name="compress",
    )(kc3, vc3, *consts)


SEL_TILE = 256
COLS = Q_PER_KV * Q_BLOCK
ROW_TILE = 1024
IN_SUB = 4
POST_TILE = 512
RADIX_BITS = 1
Q_PER_STEP = 2
PIPE = 2
SWEEP_UNROLL = 2
assert SWEEP_UNROLL % PIPE == 0 and SEL_TILE % (Q_PER_STEP * Q_BLOCK) == 0
WIN_MASK_ROWS = 128
WIN_EXP_ROWS = 128
assert (WINDOW + Q_BLOCK) // WIN_MASK_ROWS + (WINDOW + Q_BLOCK) // WIN_EXP_ROWS <= 32


def _split2(x):
    hi = x.astype(BF16)
    return hi, (x - hi.astype(F32)).astype(BF16)


def _zero_after(x):
    return lax.shift_right_logical(lax.shift_right_logical(x, 16), 16)


def _normalize_dup(acc):
    o = acc[:HEAD_DIM] / acc[HEAD_DIM:HEAD_DIM + 1]
    return jnp.concatenate([o, o], axis=0)


def _attn_kernel(q_ref, ksel_ref, vst_ref, kw_ref, vwt_ref, kcmp_ref, vcmpt_ref, c2st_ref, gate_ref, o_ref,
                 s_all, p_all, a_all, cm_all, m_scr, acc_scr, acct_scr, cw_scr, sw_scr, pw_scr):
    i = pl.program_id(1)
    starts = [(Q_PER_STEP * i + d) * Q_BLOCK for d in range(Q_PER_STEP)]
    kv = lambda g: g % N_KV_HEADS
    qb = lambda g: g // N_KV_HEADS
    nch = kcmp_ref.shape[2]
    last_tile = ksel_ref.shape[2] // SEL_TILE - 1
    groups = range(Q_PER_STEP * N_KV_HEADS)

    lane = lax.broadcasted_iota(jnp.int32, (Q_BLOCK, LANES), 1)
    lo_half = lane < HEAD_DIM
    top_half = lax.broadcasted_iota(jnp.int32, (LANES, Q_BLOCK), 0) < HEAD_DIM
    zero_bf = jnp.zeros((LANES, Q_BLOCK), BF16)
    tqs = [st + (lax.broadcasted_iota(jnp.int32, (1, COLS), 1) & (Q_BLOCK - 1)) for st in starts]
    wk = WINDOW + Q_BLOCK
    bases = [pl.multiple_of(jnp.maximum(st - WINDOW, 0), Q_BLOCK) for st in starts]
    wrow = lax.broadcasted_iota(jnp.int32, (WIN_MASK_ROWS, COLS), 0)
    nrow = lax.broadcasted_iota(jnp.int32, (nch, COLS), 0)
    srow = lax.broadcasted_iota(jnp.int32, (N_SLC_LANES, Q_BLOCK), 0)
    tqqs = [st + lax.broadcasted_iota(jnp.int32, (N_SLC_LANES, Q_BLOCK), 1) for st in starts]
    c2st = c2st_ref[...]
    n_main = starts[0] // SEL_TILE

    qt = [jnp.concatenate([jnp.where(top_half if r % 2 == 0 else ~top_half,
                                     q_ref[0, qb(g), kv(g) * (Q_PER_KV // 2) + r // 2], zero_bf)
                           for r in range(Q_PER_KV)], axis=1) for g in groups]

    def grow(g, b):
        gt = gate_ref[0, qb(g), kv(g)]
        return jnp.concatenate([gt[r * N_BRANCH + b:r * N_BRANCH + b + 1, :] for r in range(Q_PER_KV)], axis=1)

    s_c = [_dot(kcmp_ref[0, kv(g)], qt[g]) for g in groups]
    for g in groups:
        sw_scr[g] = _dot(kw_ref[0, kv(g), pl.ds(bases[qb(g)], wk), :], qt[g])

    last_valid = [lax.shift_right_arithmetic(t - (CMP_BLOCK - 1), CMP_SHIFT) for t in tqs]
    curs = [lax.shift_right_logical(t, SLC_SHIFT) for t in tqqs]
    forced = [(srow == 0) | (srow == c) | (srow == c - 1) for c in curs]
    future = [srow * SLC_BLOCK > t for t in tqqs]
    impt = []
    for g in groups:
        sc = jnp.where(nrow <= last_valid[qb(g)], s_c[g], NEG)
        m_c = jnp.max(sc, axis=0, keepdims=True)
        e_c = jnp.exp2(sc - m_c)
        l_c = jnp.sum(e_c, axis=0, keepdims=True)
        p_c = e_c * jnp.where(m_c > 0.5 * NEG, 1.0 / l_c, 0.0)
        cw_scr[g] = grow(g, 0) * _dot(vcmpt_ref[0, kv(g)], p_c.astype(BF16))
        psum = sum(p_c[:, r * Q_BLOCK:(r + 1) * Q_BLOCK] for r in range(Q_PER_KV))
        imp = sum(_dot(c2st, t) for t in _split2(psum))
        impt.append(jnp.where(forced[qb(g)], 1e9, jnp.where(future[qb(g)], -1e9, imp)))

    def window_mask_piece(g, k, colmax, zero):
        rows = slice(WIN_MASK_ROWS * k, WIN_MASK_ROWS * (k + 1))
        back = (tqs[qb(g)] + zero - bases[qb(g)] - WIN_MASK_ROWS * k) - wrow
        in_window = lax.bitcast_convert_type(back, jnp.uint32) < jnp.uint32(WINDOW)
        sm = jnp.where(in_window, sw_scr[g, rows, :], NEG)
        sw_scr[g, rows, :] = sm
        for c in range(WIN_MASK_ROWS // SUBLANES):
            colmax = jnp.maximum(colmax, sm[c * SUBLANES:(c + 1) * SUBLANES])
        return colmax

    def window_exp_piece(g, k, m_w, zero):
        rows = slice(WIN_EXP_ROWS * k, WIN_EXP_ROWS * (k + 1))
        pw_scr[g, rows, :] = jnp.exp2(sw_scr[g, rows, :] - (m_w + zero.astype(F32))).astype(BF16)

    key_to_float = lambda k: lax.bitcast_convert_type(jnp.where(k >= 0, k, k ^ jnp.int32(0x7FFFFFFF)), F32)
    wrap32 = lambda v: ((v + 2 ** 31) % 2 ** 32) - 2 ** 31
    n_mask, n_exp = wk // WIN_MASK_ROWS, wk // WIN_EXP_ROWS
    thr = [jnp.full((1, Q_BLOCK), INT32_MIN, jnp.int32) for _ in groups]
    colmax = [jnp.full((SUBLANES, COLS), NEG, F32) for _ in groups]
    m_w = [None for _ in groups]
    piece = 0
    for b in range(32 - RADIX_BITS, -1, -RADIX_BITS):
        for g in groups:
            reached = thr[g]
            for mult in range(1, 2 ** RADIX_BITS):
                cand = thr[g] + jnp.int32(wrap32(mult << b))
                n_ge = jnp.sum(jnp.where(impt[g] >= key_to_float(cand), 1.0, 0.0), axis=0, keepdims=True)
                reached = jnp.where(n_ge >= N_SELECT, cand, reached)
            thr[g] = reached
        for _ in range(RADIX_BITS):
            for g in groups:
                zero = jnp.concatenate([_zero_after(thr[g])] * Q_PER_KV, axis=1)
                if piece < n_mask:
                    colmax[g] = window_mask_piece(g, piece, colmax[g], zero)
                elif piece < n_mask + n_exp:
                    if piece == n_mask:
                        m_w[g] = jnp.max(colmax[g], axis=0, keepdims=True)
                    window_exp_piece(g, piece - n_mask, m_w[g], zero)
            piece += 1

    scol = lax.broadcasted_iota(jnp.int32, (N_SLC_LANES, N_SLC_LANES), 1)
    earlier = jnp.where(scol < srow, 1.0, 0.0).astype(BF16)
    widen = lambda x: jnp.concatenate([x.astype(BF16)] * Q_PER_KV, axis=1)
    qt_tail, qt_main = [], []
    for g in groups:
        vw = jnp.concatenate([vwt_ref[0, kv(g), bases[qb(g)] // LANES + c] for c in range(wk // LANES)],
                             axis=1)
        cw_scr[g] += grow(g, 2) * _normalize_dup(_dot(vw, pw_scr[g]))
        kth = key_to_float(thr[g])
        above = impt[g] > kth
        tied = impt[g] == kth
        n_above = jnp.sum(jnp.where(above, 1.0, 0.0), axis=0, keepdims=True)
        tied_before = _dot(earlier, jnp.where(tied, 1.0, 0.0).astype(BF16))
        selected = above | (tied & (tied_before < N_SELECT - n_above))
        bias = jnp.where(selected, 0.0, NEG)
        bias_main = jnp.where(srow >= n_main * (SEL_TILE // SLC_BLOCK), NEG, bias)
        qt_tail.append(jnp.concatenate([qt[g], widen(bias)], axis=0))
        qt_main.append(jnp.concatenate([qt[g], widen(bias_main)], axis=0))

    kt = pl.multiple_of(n_main * SEL_TILE, SEL_TILE)
    krow = lax.broadcasted_iota(jnp.int32, (SEL_TILE, COLS), 0)
    tile0 = slice(0, SEL_TILE)
    for g in groups:
        sw_scr[g, tile0, :] = _dot(ksel_ref[0, kv(g), pl.ds(kt, SEL_TILE), :], qt_tail[g])
    g_sel = [grow(g, 1) for g in groups]

    def scores(g, j, slot):
        k0 = pl.multiple_of(jnp.minimum(j, last_tile) * SEL_TILE, SEL_TILE)
        sv = _dot(ksel_ref[0, kv(g), pl.ds(k0, SEL_TILE), :], qt_main[g])
        s_all[g, slot] = sv
        cm = sv[0:SUBLANES]
        for c in range(1, SEL_TILE // SUBLANES):
            cm = jnp.maximum(cm, sv[c * SUBLANES:(c + 1) * SUBLANES])
        cm_all[g, slot] = cm

    def softmax(g, slot):
        s = s_all[g, slot]
        m_prev = m_scr[g, 0:1, :]
        m_new = jnp.maximum(m_prev, jnp.max(cm_all[g, slot], axis=0, keepdims=True))
        a_all[g, slot] = jnp.broadcast_to(jnp.exp2(m_prev - m_new), (SUBLANES, COLS))
        p_all[g, slot] = jnp.exp2(s - m_new).astype(BF16)
        m_scr[g] = jnp.broadcast_to(m_new, (SUBLANES, COLS))

    def values(g, j, slot):
        acc_scr[g] = (a_all[g, slot, 0:1, :] * acc_scr[g]
                      + _dot(vst_ref[0, kv(g), jnp.minimum(j, last_tile)], p_all[g, slot]))

    m_scr[...] = jnp.full(m_scr.shape, NEG, F32)
    acc_scr[...] = jnp.zeros(acc_scr.shape, F32)
    for k in range(PIPE):
        for g in groups:
            scores(g, k, k)
    m_t = []
    for g in groups:
        s_t = jnp.where(kt + krow <= tqs[qb(g)], sw_scr[g, tile0, :], NEG)
        m_t.append(jnp.max(s_t, axis=0, keepdims=True))
        pw_scr[g, tile0, :] = jnp.exp2(s_t - m_t[g]).astype(BF16)
    for g in groups:
        acct_scr[g] = _dot(vst_ref[0, kv(g), n_main], pw_scr[g, tile0, :])
    for k in range(PIPE // 2):
        for g in groups:
            softmax(g, k)

    def sweep(t, carry):
        for k in range(SWEEP_UNROLL):
            for g in groups:
                scores(g, SWEEP_UNROLL * t + k + PIPE, k % PIPE)
            for g in groups:
                values(g, SWEEP_UNROLL * t + k, k % PIPE)
            for g in groups:
                softmax(g, (k + PIPE // 2) % PIPE)
        return carry

    lax.fori_loop(0, (n_main + SWEEP_UNROLL - 1) // SWEEP_UNROLL, sweep, 0)

    pairs = [[] for _ in range(Q_PER_STEP)]
    for g in groups:
        m_p = m_scr[g, 0:1, :]
        m_tot = jnp.maximum(m_p, m_t[g])
        o_s = _normalize_dup(jnp.exp2(m_p - m_tot) * acc_scr[g] + jnp.exp2(m_t[g] - m_tot) * acct_scr[g])
        comb = cw_scr[g] + g_sel[g] * o_s
        outs = [comb[:, r * Q_BLOCK:(r + 1) * Q_BLOCK].T for r in range(Q_PER_KV)]
        pairs[qb(g)] += [jnp.where(lo_half, outs[0], outs[1]), jnp.where(lo_half, outs[2], outs[3])]
    for d in range(Q_PER_STEP):
        o_ref[0, d * Q_BLOCK:(d + 1) * Q_BLOCK, :] = jnp.concatenate(pairs[d], axis=1)


def _attn(qpt, ksel, vst, kw, vwt, kcmp, vcmpt, c2st, gatest):
    batch, seq = qpt.shape[0], qpt.shape[1] * Q_BLOCK
    per_batch = lambda a: pl.BlockSpec((1,) + a.shape[1:], lambda b, i: (b,) + (0,) * (a.ndim - 1),
                                       pipeline_mode=pl.Buffered(1))
    per_block = lambda a: pl.BlockSpec((1, Q_PER_STEP) + a.shape[2:], lambda b, i: (b, i) + (0,) * (a.ndim - 2))
    chains = Q_PER_STEP * N_KV_HEADS
    grp = lambda *shape: pltpu.VMEM((chains,) + shape, F32)
    return pl.pallas_call(
        _attn_kernel,
        grid=(batch, seq // (Q_PER_STEP * Q_BLOCK)),
        in_specs=[per_block(qpt), per_batch(ksel), per_batch(vst), per_batch(kw), per_batch(vwt), per_batch(kcmp),
                  per_batch(vcmpt), pl.BlockSpec(c2st.shape, lambda b, i: (0, 0)), per_block(gatest)],
        out_specs=pl.BlockSpec((1, Q_PER_STEP * Q_BLOCK, D_ATT), lambda b, i: (b, i, 0)),
        out_shape=jax.ShapeDtypeStruct((batch, seq, D_ATT), F32),
        scratch_shapes=[grp(PIPE, SEL_TILE, COLS), pltpu.VMEM((chains, PIPE, SEL_TILE, COLS), BF16),
                        grp(PIPE, SUBLANES, COLS), grp(PIPE, SUBLANES, COLS), grp(SUBLANES, COLS),
                        grp(V_ROWS, COLS), grp(V_ROWS, COLS),
                        grp(LANES, COLS), grp(WINDOW + Q_BLOCK, COLS),
                        pltpu.VMEM((chains, WINDOW + Q_BLOCK, COLS), BF16)],
        compiler_params=pltpu.CompilerParams(dimension_semantics=("arbitrary", "arbitrary"),
                                             vmem_limit_bytes=VMEM_LIMIT_BYTES),
        name="attn",
    )(qpt, ksel, vst, kw, vwt, kcmp, vcmpt, c2st, gatest)


POST_SUB = 2


def _post_kernel(x_ref, oatt_ref, mconv_ref, p_ref, onag_ref, wo_ref, lnf_ref, wup_ref, fcw_ref, fcb_ref,
                 wdn_ref, lnp_ref, wpg_ref, wpe_ref, out_ref, gbuf, *, tiles_per_seq):
    tm = x_ref.shape[0]
    sub = tm // POST_SUB
    it = pl.program_id(0) % tiles_per_seq
    subs = [slice(k * sub, (k + 1) * sub) for k in range(POST_SUB)]

    @pl.when(it == 0)
    def _():
        gbuf[0:SUBLANES, :] = jnp.zeros((SUBLANES, D_FF), F32)

    h1 = [x_ref[r, :] + _dot(jnp.concatenate([_rms(oatt_ref[r, :], onag_ref[...]).astype(BF16), mconv_ref[r, :]],
                                             axis=1), wo_ref[...]) for r in subs]
    xn = [_rms(h, lnf_ref[...]).astype(BF16) for h in h1]
    h2 = []
    for k, r in enumerate(subs):
        gpre = _dot(xn[k], wup_ref[:, :D_FF])
        up = _dot(xn[k], wup_ref[:, D_FF:])
        lo = SUBLANES + k * sub
        gbuf[lo:lo + sub, :] = gpre
        gate = (fcw_ref[2:3, :] * gpre + fcw_ref[1:2, :] * gbuf[lo - 1:lo - 1 + sub, :]
                + fcw_ref[0:1, :] * gbuf[lo - 2:lo - 2 + sub, :]) + fcb_ref[...]
        act = (gate * jax.nn.sigmoid(gate) * up).astype(BF16)
        h2.append(h1[k] + _dot(act, wdn_ref[...]))
    gbuf[0:SUBLANES, :] = gbuf[tm:tm + SUBLANES, :]
    for k, r in enumerate(subs):
        xn2 = _rms(h2[k], lnp_ref[...]).astype(BF16)
        out_ref[r, :] = (h2[k] + jax.nn.sigmoid(_dot(xn2, wpg_ref[...]))
                         * _dot(p_ref[r, :].astype(BF16), wpe_ref[...]))


def _post(x2, oatt, mconv, p2, onag, wo, lnf, wup, fcw, fcb, wdn, lnp, wpg, wpe, *, seq, tm):
    n = x2.shape[0]
    tps = seq // tm
    row = lambda w: pl.BlockSpec((tm, w), lambda i: (i, 0))
    full = lambda a: pl.BlockSpec(a.shape, lambda i: (0,) * a.ndim, pipeline_mode=pl.Buffered(1))
    return pl.pallas_call(
        functools.partial(_post_kernel, tiles_per_seq=tps),
        grid=(n // tm,),
        in_specs=[row(D_MODEL), row(D_ATT), row(D_CONV), row(D_PLE), full(onag), full(wo), full(lnf), full(wup),
                  full(fcw), full(fcb), full(wdn), full(lnp), full(wpg), full(wpe)],
        out_specs=row(D_MODEL),
        out_shape=jax.ShapeDtypeStruct((n, D_MODEL), F32),
        scratch_shapes=[pltpu.VMEM((tm + SUBLANES, D_FF), F32)],
        compiler_params=pltpu.CompilerParams(dimension_semantics=("arbitrary",),
                                             vmem_limit_bytes=VMEM_LIMIT_BYTES),
        name="post",
    )(x2, oatt, mconv, p2, onag, wo, lnf, wup, fcw, fcb, wdn, lnp, wpg, wpe)


def _rope_tables(pos):
    half = ROT_DIM // 2
    d = np.arange(LANES) % HEAD_DIM
    inv_freq = np.float64(ROPE_THETA) ** (-np.arange(half, dtype=np.float64) * 2.0 / ROT_DIM)
    ang = np.asarray(pos, np.float64)[:, None] * np.tile(inv_freq, LANES // half)[None, :]
    c, sn = np.cos(ang), np.sin(ang)
    first, second = (d < half)[None, :], ((d >= half) & (d < ROT_DIM))[None, :]
    cos = np.where(first | second, c, 1.0)
    sa = np.where(first, -sn, 0.0)
    sb = np.where(second, sn, 0.0)
    return tuple(jnp.asarray(t, F32) for t in (cos, sa, sb))


def _block_diag_mean(width):
    idx = np.arange(width) // HEAD_DIM
    return jnp.asarray((idx[:, None] == idx[None, :]).astype(np.float32) / HEAD_DIM, BF16)


def _cmp_to_slc_t(nch):
    cs = CMP_STRIDE * np.arange(nch)[None, :]
    ss = SLC_BLOCK * np.arange(N_SLC_LANES)[:, None]
    ov = np.clip(np.minimum(cs + CMP_BLOCK, ss + SLC_BLOCK) - np.maximum(cs, ss), 0, None)
    return jnp.asarray(ov.astype(np.float32) / CMP_BLOCK, BF16)


def _layer(h, p_l, ln_mix_g, w_in, qn_g, kn_g, pe_k, pe_v, w_ck1, w_ck2, w_cv1, w_cv2, conv_w, on_att_g,
           on_conv_g, w_o, ln_ffn_g, w_up, ffn_conv_w, ffn_conv_b, w_down, ln_ple_g, w_pg, w_pe):
    batch, seq, _ = h.shape
    assert seq % SEL_TILE == 0 and seq // SLC_BLOCK <= N_SLC_LANES and seq // SLC_BLOCK >= N_SELECT
    assert seq >= WINDOW + Q_BLOCK
    n = batch * seq
    nch = seq // CMP_STRIDE
    x2 = h.reshape(n, D_MODEL)
    row = lambda v: v.reshape(1, -1).astype(F32)

    o_q, o_kv, o_g, o_cv = 0, D_ATT, D_ATT + 6 * KV_W, D_ATT + 6 * KV_W + N_BRANCH * N_Q_HEADS
    wq = w_in[:, o_q:o_kv].astype(BF16)
    wkv = w_in[:, o_kv:o_g].astype(BF16)
    wcv = w_in[:, o_cv:].astype(BF16)
    per_g = Q_PER_KV * N_BRANCH
    wg = jnp.concatenate(
        [jnp.pad(w_in[:, o_g + g * per_g:o_g + (g + 1) * per_g], ((0, 0), (0, LANES - per_g)))
         for g in range(N_KV_HEADS)], axis=1).astype(BF16)
    cos, sa, sb = _rope_tables(np.arange(seq))
    bd512, bd128 = _block_diag_mean(D_ATT), _block_diag_mean(KV_W)
    tile_heads = lambda v, k: jnp.tile(v.astype(F32), k).reshape(1, -1)

    tm = ROW_TILE
    assert seq % ROW_TILE == 0 and seq % POST_TILE == 0
    qpt, kc2, vc2, ksel, vst, kw, vwt, gatest, mconv = _inproj(
        x2, row(ln_mix_g), wq, wkv, wg, wcv, tile_heads(qn_g, N_Q_HEADS),
        jnp.stack([jnp.tile(kn_g[1], N_KV_HEADS), jnp.tile(kn_g[2], N_KV_HEADS)]).astype(F32),
        bd512, bd128, cos, sa, sb, conv_w.astype(F32), row(on_conv_g), batch=batch, seq=seq, tm=tm)

    assert N_KV_HEADS == 2
    half = CMP_BLOCK // 2

    def w1_parts(w1):
        w = w1.astype(BF16)
        z = jnp.zeros_like(w)
        wfull = jnp.concatenate([jnp.concatenate([w, z], axis=2), jnp.concatenate([z, w], axis=2)], axis=1)
        return wfull[:half].reshape(half * KV_W, -1), wfull[half:].reshape(half * KV_W, -1)

    def pe_parts(pe):
        pf = jnp.broadcast_to(pe[:, None, :], (CMP_BLOCK, N_KV_HEADS, HEAD_DIM)).astype(F32)
        return pf[:half].reshape(1, -1), pf[half:].reshape(1, -1)

    def w2bd(w2):
        w = w2.astype(BF16)
        z = jnp.zeros_like(w)
        return jnp.concatenate([jnp.concatenate([w, z], axis=1), jnp.concatenate([z, w], axis=1)], axis=0)
    w1ak, w1bk = w1_parts(w_ck1)
    w1av, w1bv = w1_parts(w_cv1)
    peak, pebk = pe_parts(pe_k)
    peav, pebv = pe_parts(pe_v)
    ccos, csa, csb = _rope_tables(CMP_STRIDE * np.arange(nch) + CMP_BLOCK - 1)
    kcmp, vcmpt = _compress(kc2.reshape(batch, seq, KV_W), vc2.reshape(batch, seq, KV_W),
                            w1ak, w1bk, w1av, w1bv, peak, pebk, peav, pebv, w2bd(w_ck2), w2bd(w_cv2),
                            tile_heads(kn_g[0], N_KV_HEADS), bd128, ccos, csa, csb)

    oatt = _attn(qpt, ksel, vst, kw, vwt, kcmp, vcmpt, _cmp_to_slc_t(nch), gatest)

    out = _post(x2, oatt.reshape(n, D_ATT), mconv, p_l.reshape(n, D_PLE), row(on_att_g), w_o.astype(BF16),
                row(ln_ffn_g), w_up.astype(BF16), ffn_conv_w.astype(F32), row(ffn_conv_b), w_down.astype(BF16),
                row(ln_ple_g), w_pg.astype(BF16), w_pe.astype(BF16), seq=seq, tm=POST_TILE)
    return out.reshape(batch, seq, D_MODEL)


def kernel(x, p, ln_mix_g, w_in, qn_g, kn_g, pe_k, pe_v, w_ck1, w_ck2, w_cv1, w_cv2, conv_w, on_att_g,
           on_conv_g, w_o, ln_ffn_g, w_up, ffn_conv_w, ffn_conv_b, w_down, ln_ple_g, w_pg, w_pe):
    h = x
    for i in range(p.shape[0]):
        h = _layer(h, p[i], ln_mix_g[i], w_in[i], qn_g[i], kn_g[i], pe_k[i], pe_v[i], w_ck1[i], w_ck2[i],
                   w_cv1[i], w_cv2[i], conv_w[i], on_att_g[i], on_conv_g[i], w_o[i], ln_ffn_g[i], w_up[i],
                   ffn_conv_w[i], ffn_conv_b[i], w_down[i], ln_ple_g[i], w_pg[i], w_pe[i])
    return h
```

```python
import functools

import jax
import jax.numpy as jnp
import numpy as np
from jax import lax
from jax.experimental import pallas as pl
from jax.experimental.pallas import tpu as pltpu

D_MODEL = 1024
HEAD_DIM = 64
N_Q_HEADS = 8
N_KV_HEADS = 2
Q_PER_KV = N_Q_HEADS // N_KV_HEADS
D_ATT = N_Q_HEADS * HEAD_DIM
D_CONV = D_MODEL - D_ATT
KV_W = N_KV_HEADS * HEAD_DIM
N_BRANCH = 3
CONV_TAPS = 3
ROT_DIM = HEAD_DIM // 4
ROPE_THETA = 500000.0
CMP_BLOCK = 32
CMP_STRIDE = 16
CMP_HIDDEN = 256
SLC_BLOCK = 64
N_SELECT = 16
WINDOW = 512
Q_BLOCK = 128
D_FF = 2816
D_PLE = 256
EPS = 1e-6
NEG = -1e30
INT32_MIN = -2 ** 31
LOG2E = float(np.log2(np.e))
V_ROWS = HEAD_DIM + 16
GATE_ROWS = 16
SLC_SHIFT = SLC_BLOCK.bit_length() - 1
CMP_SHIFT = CMP_STRIDE.bit_length() - 1
assert 1 << SLC_SHIFT == SLC_BLOCK and 1 << CMP_SHIFT == CMP_STRIDE

LANES = 128
SUBLANES = 8
N_SLC_LANES = LANES
VMEM_LIMIT_BYTES = 56 * 1024 * 1024

F32 = jnp.float32
BF16 = jnp.bfloat16


def _dot(a, b):
    return jnp.dot(a, b, preferred_element_type=F32)


def _rms(x, g):
    return x * lax.rsqrt(jnp.mean(x * x, axis=-1, keepdims=True) + EPS) * g


def _head_rms_rope(x, g, bd, cos, sa, sb):
    w = x.shape[-1]
    msq = _dot((x * x).astype(BF16), bd)
    xn = x * lax.rsqrt(msq + EPS) * g
    return xn * cos + pltpu.roll(xn, w - ROT_DIM // 2, 1) * sa + pltpu.roll(xn, ROT_DIM // 2, 1) * sb


def _dup_halves(x):
    r = pltpu.roll(x, HEAD_DIM, 1)
    lane = lax.broadcasted_iota(jnp.int32, x.shape, 1)
    lo = lane < HEAD_DIM
    return jnp.where(lo, x, r), jnp.where(lo, r, x)


def _inproj_kernel(x_ref, lng_ref, wq_ref, wkv_ref, wg_ref, wcv_ref, qng_ref, kng_ref, bd512_ref, bd128_ref,
                   cos_ref, sa_ref, sb_ref, convw_ref, oncg_ref,
                   q_out, kc_out, vc_out, ksel_out, vs_out, kw_out, vw_out, gate_out, mconv_out,
                   zbuf, *, tiles_per_seq):
    tm = x_ref.shape[0]
    sub = tm // IN_SUB
    it = pl.program_id(0) % tiles_per_seq

    @pl.when(it == 0)
    def _():
        zbuf[0:SUBLANES, :] = jnp.zeros((SUBLANES, D_CONV), F32)

    for k in range(IN_SUB):
        r = slice(k * sub, (k + 1) * sub)
        xn = _rms(x_ref[r, :], lng_ref[...]).astype(BF16)

        cos, sa, sb = cos_ref[r, :], sa_ref[r, :], sb_ref[r, :]
        cos4, sa4, sb4 = (jnp.concatenate([t] * 4, axis=1) for t in (cos, sa, sb))
        q = _dot(xn, wq_ref[...])
        qr = _head_rms_rope(q, qng_ref[...], bd512_ref[...], cos4, sa4, sb4)
        qs = qr * (HEAD_DIM ** -0.5 * LOG2E)
        for blk_i in range(sub // Q_BLOCK):
            rows = slice(blk_i * Q_BLOCK, (blk_i + 1) * Q_BLOCK)
            for pr in range(N_Q_HEADS // 2):
                q_out[0, k * (sub // Q_BLOCK) + blk_i, pr] = qs[rows, pr * LANES:(pr + 1) * LANES].T.astype(BF16)

        kv = _dot(xn, wkv_ref[...])
        kc_out[r, :] = kv[:, 0 * KV_W:1 * KV_W]
        vc_out[r, :] = kv[:, 1 * KV_W:2 * KV_W]
        ks = _head_rms_rope(kv[:, 2 * KV_W:3 * KV_W], kng_ref[0:1, :], bd128_ref[...], cos, sa, sb)
        kw = _head_rms_rope(kv[:, 4 * KV_W:5 * KV_W], kng_ref[1:2, :], bd128_ref[...], cos, sa, sb)
        vs = kv[:, 3 * KV_W:4 * KV_W]
        vw = kv[:, 5 * KV_W:6 * KV_W]
        tpos = it * tm + k * sub + lax.broadcasted_iota(jnp.int32, (sub, N_SLC_LANES), 0)
        blk = lax.broadcasted_iota(jnp.int32, (sub, N_SLC_LANES), 1)
        onehot = jnp.where(lax.shift_right_logical(tpos, SLC_SHIFT) == blk, 1.0, 0.0).astype(BF16)
        ks_d, vs_d, kw_d, vw_d = (_dup_halves(t) for t in (ks, vs, kw, vw))
        vrow = lax.broadcasted_iota(jnp.int32, (V_ROWS, sub), 0)
        for g in range(N_KV_HEADS):
            ksel_out[0, g, r, :] = jnp.concatenate([ks_d[g].astype(BF16), onehot], axis=1)
            kw_out[0, g, r, :] = kw_d[g].astype(BF16)
            vst = jnp.where(vrow < HEAD_DIM, vs_d[g].T[:V_ROWS], 1.0).astype(BF16)
            for c in range(sub // SEL_TILE):
                vs_out[0, g, k * (sub // SEL_TILE) + c] = vst[:, c * SEL_TILE:(c + 1) * SEL_TILE]
            vwt = jnp.where(vrow < HEAD_DIM, vw_d[g].T[:V_ROWS], 1.0).astype(BF16)
            for c in range(sub // LANES):
                vw_out[0, g, k * (sub // LANES) + c] = vwt[:, c * LANES:(c + 1) * LANES]

        gates = jax.nn.sigmoid(_dot(xn, wg_ref[...]))
        for blk_i in range(sub // Q_BLOCK):
            rows = slice(blk_i * Q_BLOCK, (blk_i + 1) * Q_BLOCK)
            for g in range(N_KV_HEADS):
                gate_out[0, k * (sub // Q_BLOCK) + blk_i, g] = gates[rows, g * LANES:(g + 1) * LANES].T[:GATE_ROWS]

        cv = _dot(xn, wcv_ref[...])
        cb, cc, cx = cv[:, :D_CONV], cv[:, D_CONV:2 * D_CONV], cv[:, 2 * D_CONV:]
        z = cc * cx
        lo = SUBLANES + k * sub
        zbuf[lo:lo + sub, :] = z
        y = (convw_ref[2:3, :] * z + convw_ref[1:2, :] * zbuf[lo - 1:lo - 1 + sub, :]
             + convw_ref[0:1, :] * zbuf[lo - 2:lo - 2 + sub, :])
        mconv_out[r, :] = _rms(cb * y, oncg_ref[...]).astype(BF16)
    zbuf[0:SUBLANES, :] = zbuf[tm:tm + SUBLANES, :]


def _inproj(x2, lng, wq, wkv, wg, wcv, qng, kng, bd512, bd128, cos, sa, sb, convw, oncg, *, batch, seq, tm):
    n = batch * seq
    tps = seq // tm
    row = lambda w: pl.BlockSpec((tm, w), lambda i: (i, 0))
    full = lambda a: pl.BlockSpec(a.shape, lambda i: (0,) * a.ndim)
    tab = pl.BlockSpec((tm, LANES), lambda i: (i % tps, 0))
    grp = lambda w: pl.BlockSpec((1, N_KV_HEADS, tm, w), lambda i: (i // tps, 0, i % tps, 0))
    gshape = lambda w: jax.ShapeDtypeStruct((batch, N_KV_HEADS, seq, w), BF16)
    assert tm % (IN_SUB * SEL_TILE) == 0
    vtile = lambda keys: pl.BlockSpec((1, N_KV_HEADS, tm // keys, V_ROWS, keys),
                                      lambda i: (i // tps, 0, i % tps, 0, 0))
    vshape = lambda keys: jax.ShapeDtypeStruct((batch, N_KV_HEADS, seq // keys, V_ROWS, keys), BF16)
    qblk = lambda tile: pl.BlockSpec((1, tm // Q_BLOCK) + tile, lambda i: (i // tps, i % tps) + (0,) * len(tile))
    return pl.pallas_call(
        functools.partial(_inproj_kernel, tiles_per_seq=tps),
        grid=(n // tm,),
        in_specs=[row(D_MODEL), full(lng), full(wq), full(wkv), full(wg), full(wcv), full(qng), full(kng),
                  full(bd512), full(bd128), tab, tab, tab, full(convw), full(oncg)],
        out_specs=[qblk((N_Q_HEADS // 2, LANES, Q_BLOCK)), row(KV_W), row(KV_W), grp(2 * LANES), vtile(SEL_TILE),
                   grp(LANES), vtile(LANES), qblk((N_KV_HEADS, GATE_ROWS, Q_BLOCK)), row(D_CONV)],
        out_shape=[jax.ShapeDtypeStruct((batch, seq // Q_BLOCK, N_Q_HEADS // 2, LANES, Q_BLOCK), BF16),
                   jax.ShapeDtypeStruct((n, KV_W), F32),
                   jax.ShapeDtypeStruct((n, KV_W), F32), gshape(2 * LANES), vshape(SEL_TILE), gshape(LANES),
                   vshape(LANES),
                   jax.ShapeDtypeStruct((batch, seq // Q_BLOCK, N_KV_HEADS, GATE_ROWS, Q_BLOCK), F32),
                   jax.ShapeDtypeStruct((n, D_CONV), BF16)],
        scratch_shapes=[pltpu.VMEM((tm + SUBLANES, D_CONV), F32)],
        compiler_params=pltpu.CompilerParams(dimension_semantics=("arbitrary",),
                                             vmem_limit_bytes=VMEM_LIMIT_BYTES),
        name="inproj",
    )(x2, lng, wq, wkv, wg, wcv, qng, kng, bd512, bd128, cos, sa, sb, convw, oncg)


def _gelu_tanh(x):
    return 0.5 * x * (1.0 + jnp.tanh(np.sqrt(2.0 / np.pi).astype(np.float32) * (x + 0.044715 * (x * x * x))))


def _compress_kernel(kc_ref, vc_ref, w1ak_ref, w1bk_ref, w1av_ref, w1bv_ref, peak_ref, pebk_ref, peav_ref,
                     pebv_ref, w2k_ref, w2v_ref, kng_ref, bd128_ref, cos_ref, sa_ref, sb_ref,
                     kcmp_out, vcmp_out):
    batch = kc_ref.shape[0]
    nch = kc_ref.shape[1] // CMP_STRIDE
    rows = batch * nch

    def chunks(ref):
        return jnp.concatenate(
            [jnp.concatenate([ref[b, pl.ds(l, nch, stride=CMP_STRIDE), :] for l in range(CMP_STRIDE)], axis=1)
             for b in range(batch)], axis=0)

    def compress(x, w1a, w1b, pea, peb, w2):
        a = _dot((x + pea).astype(BF16), w1a)
        b = _dot((x + peb).astype(BF16), w1b)
        hid = _gelu_tanh(a + pltpu.roll(b, rows - 1, 0))
        return _dot(hid.astype(BF16), w2)

    kc = compress(chunks(kc_ref), w1ak_ref[...], w1bk_ref[...], peak_ref[...], pebk_ref[...], w2k_ref[...])
    vc = compress(chunks(vc_ref), w1av_ref[...], w1bv_ref[...], peav_ref[...], pebv_ref[...], w2v_ref[...])
    tab = lambda ref: jnp.concatenate([ref[...]] * batch, axis=0)
    kc = _head_rms_rope(kc, kng_ref[...], bd128_ref[...], tab(cos_ref), tab(sa_ref), tab(sb_ref))
    kd, vd = _dup_halves(kc), _dup_halves(vc)
    for b in range(batch):
        r = slice(b * nch, (b + 1) * nch)
        for g in range(N_KV_HEADS):
            kcmp_out[b, g] = kd[g][r].astype(BF16)
            vcmp_out[b, g] = vd[g][r].T.astype(BF16)


def _compress(kc3, vc3, w1ak, w1bk, w1av, w1bv, peak, pebk, peav, pebv, w2k, w2v, kng0, bd128, cos, sa, sb):
    batch, seq, wide = kc3.shape
    nch = seq // CMP_STRIDE
    full = lambda a: pl.BlockSpec(a.shape, lambda i: (0,) * a.ndim)
    tok = pl.BlockSpec((batch, seq, wide), lambda i: (0, 0, 0))
    kout = pl.BlockSpec((batch, N_KV_HEADS, nch, LANES), lambda i: (0, 0, 0, 0))
    vout = pl.BlockSpec((batch, N_KV_HEADS, LANES, nch), lambda i: (0, 0, 0, 0))
    consts = (w1ak, w1bk, w1av, w1bv, peak, pebk, peav, pebv, w2k, w2v, kng0, bd128, cos, sa, sb)
    return pl.pallas_call(
        _compress_kernel,
        grid=(1,),
        in_specs=[tok, tok] + [full(a) for a in consts],
        out_specs=[kout, vout],
        out_shape=[jax.ShapeDtypeStruct((batch, N_KV_HEADS, nch, LANES), BF16),
                   jax.ShapeDtypeStruct((batch, N_KV_HEADS, LANES, nch), BF16)],
        compiler_params=pltpu.CompilerParams(dimension_semantics=("arbitrary",),
                                             vmem_limit_bytes=VMEM_LIMIT_BYTES),
        name="compress",
    )(kc3, vc3, *consts)


SEL_TILE = 256
COLS = Q_PER_KV * Q_BLOCK
ROW_TILE = 1024
IN_SUB = 4
POST_TILE = 512
RADIX_BITS = 1
Q_PER_STEP = 2
PIPE = 2
SWEEP_UNROLL = 2
assert SWEEP_UNROLL % PIPE == 0 and SEL_TILE % (Q_PER_STEP * Q_BLOCK) == 0
WIN_MASK_ROWS = 128
WIN_EXP_ROWS = 128
assert (WINDOW + Q_BLOCK) // WIN_MASK_ROWS + (WINDOW + Q_BLOCK) // WIN_EXP_ROWS <= 32


def _split2(x):
    hi = x.astype(BF16)
    return hi, (x - hi.astype(F32)).astype(BF16)


def _zero_after(x):
    return lax.shift_right_logical(lax.shift_right_logical(x, 16), 16)


def _normalize_dup(acc):
    o = acc[:HEAD_DIM] / acc[HEAD_DIM:HEAD_DIM + 1]
    return jnp.concatenate([o, o], axis=0)


def _attn_kernel(q_ref, ksel_ref, vst_ref, kw_ref, vwt_ref, kcmp_ref, vcmpt_ref, c2st_ref, gate_ref, o_ref,
                 s_all, p_all, a_all, cm_all, m_scr, acc_scr, acct_scr, cw_scr, sw_scr, pw_scr):
    i = pl.program_id(1)
    starts = [(Q_PER_STEP * i + d) * Q_BLOCK for d in range(Q_PER_STEP)]
    kv = lambda g: g % N_KV_HEADS
    qb = lambda g: g // N_KV_HEADS
    nch = kcmp_ref.shape[2]
    last_tile = ksel_ref.shape[2] // SEL_TILE - 1
    groups = range(Q_PER_STEP * N_KV_HEADS)

    lane = lax.broadcasted_iota(jnp.int32, (Q_BLOCK, LANES), 1)
    lo_half = lane < HEAD_DIM
    top_half = lax.broadcasted_iota(jnp.int32, (LANES, Q_BLOCK), 0) < HEAD_DIM
    zero_bf = jnp.zeros((LANES, Q_BLOCK), BF16)
    tqs = [st + (lax.broadcasted_iota(jnp.int32, (1, COLS), 1) & (Q_BLOCK - 1)) for st in starts]
    wk = WINDOW + Q_BLOCK
    bases = [pl.multiple_of(jnp.maximum(st - WINDOW, 0), Q_BLOCK) for st in starts]
    wrow = lax.broadcasted_iota(jnp.int32, (WIN_MASK_ROWS, COLS), 0)
    nrow = lax.broadcasted_iota(jnp.int32, (nch, COLS), 0)
    srow = lax.broadcasted_iota(jnp.int32, (N_SLC_LANES, Q_BLOCK), 0)
    tqqs = [st + lax.broadcasted_iota(jnp.int32, (N_SLC_LANES, Q_BLOCK), 1) for st in starts]
    c2st = c2st_ref[...]
    n_main = starts[0] // SEL_TILE

    qt = [jnp.concatenate([jnp.where(top_half if r % 2 == 0 else ~top_half,
                                     q_ref[0, qb(g), kv(g) * (Q_PER_KV // 2) + r // 2], zero_bf)
                           for r in range(Q_PER_KV)], axis=1) for g in groups]

    def grow(g, b):
        gt = gate_ref[0, qb(g), kv(g)]
        return jnp.concatenate([gt[r * N_BRANCH + b:r * N_BRANCH + b + 1, :] for r in range(Q_PER_KV)], axis=1)

    s_c = [_dot(kcmp_ref[0, kv(g)], qt[g]) for g in groups]
    for g in groups:
        sw_scr[g] = _dot(kw_ref[0, kv(g), pl.ds(bases[qb(g)], wk), :], qt[g])

    last_valid = [lax.shift_right_arithmetic(t - (CMP_BLOCK - 1), CMP_SHIFT) for t in tqs]
    curs = [lax.shift_right_logical(t, SLC_SHIFT) for t in tqqs]
    forced = [(srow == 0) | (srow == c) | (srow == c - 1) for c in curs]
    future = [srow * SLC_BLOCK > t for t in tqqs]
    impt = []
    for g in groups:
        sc = jnp.where(nrow <= last_valid[qb(g)], s_c[g], NEG)
        m_c = jnp.max(sc, axis=0, keepdims=True)
        e_c = jnp.exp2(sc - m_c)
        l_c = jnp.sum(e_c, axis=0, keepdims=True)
        p_c = e_c * jnp.where(m_c > 0.5 * NEG, 1.0 / l_c, 0.0)
        cw_scr[g] = grow(g, 0) * _dot(vcmpt_ref[0, kv(g)], p_c.astype(BF16))
        psum = sum(p_c[:, r * Q_BLOCK:(r + 1) * Q_BLOCK] for r in range(Q_PER_KV))
        imp = sum(_dot(c2st, t) for t in _split2(psum))
        impt.append(jnp.where(forced[qb(g)], 1e9, jnp.where(future[qb(g)], -1e9, imp)))

    def window_mask_piece(g, k, colmax, zero):
        rows = slice(WIN_MASK_ROWS * k, WIN_MASK_ROWS * (k + 1))
        back = (tqs[qb(g)] + zero - bases[qb(g)] - WIN_MASK_ROWS * k) - wrow
        in_window = lax.bitcast_convert_type(back, jnp.uint32) < jnp.uint32(WINDOW)
        sm = jnp.where(in_window, sw_scr[g, rows, :], NEG)
        sw_scr[g, rows, :] = sm
        for c in range(WIN_MASK_ROWS // SUBLANES):
            colmax = jnp.maximum(colmax, sm[c * SUBLANES:(c + 1) * SUBLANES])
        return colmax

    def window_exp_piece(g, k, m_w, zero):
        rows = slice(WIN_EXP_ROWS * k, WIN_EXP_ROWS * (k + 1))
        pw_scr[g, rows, :] = jnp.exp2(sw_scr[g, rows, :] - (m_w + zero.astype(F32))).astype(BF16)

    key_to_float = lambda k: lax.bitcast_convert_type(jnp.where(k >= 0, k, k ^ jnp.int32(0x7FFFFFFF)), F32)
    wrap32 = lambda v: ((v + 2 ** 31) % 2 ** 32) - 2 ** 31
    n_mask, n_exp = wk // WIN_MASK_ROWS, wk // WIN_EXP_ROWS
    thr = [jnp.full((1, Q_BLOCK), INT32_MIN, jnp.int32) for _ in groups]
    colmax = [jnp.full((SUBLANES, COLS), NEG, F32) for _ in groups]
    m_w = [None for _ in groups]
    piece = 0
    for b in range(32 - RADIX_BITS, -1, -RADIX_BITS):
        for g in groups:
            reached = thr[g]
            for mult in range(1, 2 ** RADIX_BITS):
                cand = thr[g] + jnp.int32(wrap32(mult << b))
                n_ge = jnp.sum(jnp.where(impt[g] >= key_to_float(cand), 1.0, 0.0), axis=0, keepdims=True)
                reached = jnp.where(n_ge >= N_SELECT, cand, reached)
            thr[g] = reached
        for _ in range(RADIX_BITS):
            for g in groups:
                zero = jnp.concatenate([_zero_after(thr[g])] * Q_PER_KV, axis=1)
                if piece < n_mask:
                    colmax[g] = window_mask_piece(g, piece, colmax[g], zero)
                elif piece < n_mask + n_exp:
                    if piece == n_mask:
                        m_w[g] = jnp.max(colmax[g], axis=0, keepdims=True)
                    window_exp_piece(g, piece - n_mask, m_w[g], zero)
            piece += 1

    scol = lax.broadcasted_iota(jnp.int32, (N_SLC_LANES, N_SLC_LANES), 1)
    earlier = jnp.where(scol < srow, 1.0, 0.0).astype(BF16)
    widen = lambda x: jnp.concatenate([x.astype(BF16)] * Q_PER_KV, axis=1)
    qt_tail, qt_main = [], []
    for g in groups:
        vw = jnp.concatenate([vwt_ref[0, kv(g), bases[qb(g)] // LANES + c] for c in range(wk // LANES)],
                             axis=1)
        cw_scr[g] += grow(g, 2) * _normalize_dup(_dot(vw, pw_scr[g]))
        kth = key_to_float(thr[g])
        above = impt[g] > kth
        tied = impt[g] == kth
        n_above = jnp.sum(jnp.where(above, 1.0, 0.0), axis=0, keepdims=True)
        tied_before = _dot(earlier, jnp.where(tied, 1.0, 0.0).astype(BF16))
        selected = above | (tied & (tied_before < N_SELECT - n_above))
        bias = jnp.where(selected, 0.0, NEG)
        bias_main = jnp.where(srow >= n_main * (SEL_TILE // SLC_BLOCK), NEG, bias)
        qt_tail.append(jnp.concatenate([qt[g], widen(bias)], axis=0))
        qt_main.append(jnp.concatenate([qt[g], widen(bias_main)], axis=0))

    kt = pl.multiple_of(n_main * SEL_TILE, SEL_TILE)
    krow = lax.broadcasted_iota(jnp.int32, (SEL_TILE, COLS), 0)
    tile0 = slice(0, SEL_TILE)
    for g in groups:
        sw_scr[g, tile0, :] = _dot(ksel_ref[0, kv(g), pl.ds(kt, SEL_TILE), :], qt_tail[g])
    g_sel = [grow(g, 1) for g in groups]

    def scores(g, j, slot):
        k0 = pl.multiple_of(jnp.minimum(j, last_tile) * SEL_TILE, SEL_TILE)
        sv = _dot(ksel_ref[0, kv(g), pl.ds(k0, SEL_TILE), :], qt_main[g])
        s_all[g, slot] = sv
        cm = sv[0:SUBLANES]
        for c in range(1, SEL_TILE // SUBLANES):
            cm = jnp.maximum(cm, sv[c * SUBLANES:(c + 1) * SUBLANES])
        cm_all[g, slot] = cm

    def softmax(g, slot):
        s = s_all[g, slot]
        m_prev = m_scr[g, 0:1, :]
        m_new = jnp.maximum(m_prev, jnp.max(cm_all[g, slot], axis=0, keepdims=True))
        a_all[g, slot] = jnp.broadcast_to(jnp.exp2(m_prev - m_new), (SUBLANES, COLS))
        p_all[g, slot] = jnp.exp2(s - m_new).astype(BF16)
        m_scr[g] = jnp.broadcast_to(m_new, (SUBLANES, COLS))

    def values(g, j, slot):
        acc_scr[g] = (a_all[g, slot, 0:1, :] * acc_scr[g]
                      + _dot(vst_ref[0, kv(g), jnp.minimum(j, last_tile)], p_all[g, slot]))

    m_scr[...] = jnp.full(m_scr.shape, NEG, F32)
    acc_scr[...] = jnp.zeros(acc_scr.shape, F32)
    for k in range(PIPE):
        for g in groups:
            scores(g, k, k)
    m_t = []
    for g in groups:
        s_t = jnp.where(kt + krow <= tqs[qb(g)], sw_scr[g, tile0, :], NEG)
        m_t.append(jnp.max(s_t, axis=0, keepdims=True))
        pw_scr[g, tile0, :] = jnp.exp2(s_t - m_t[g]).astype(BF16)
    for g in groups:
        acct_scr[g] = _dot(vst_ref[0, kv(g), n_main], pw_scr[g, tile0, :])
    for k in range(PIPE // 2):
        for g in groups:
            softmax(g, k)

    def sweep(t, carry):
        for k in range(SWEEP_UNROLL):
            for g in groups:
                scores(g, SWEEP_UNROLL * t + k + PIPE, k % PIPE)
            for g in groups:
                values(g, SWEEP_UNROLL * t + k, k % PIPE)
            for g in groups:
                softmax(g, (k + PIPE // 2) % PIPE)
        return carry

    lax.fori_loop(0, (n_main + SWEEP_UNROLL - 1) // SWEEP_UNROLL, sweep, 0)

    pairs = [[] for _ in range(Q_PER_STEP)]
    for g in groups:
        m_p = m_scr[g, 0:1, :]
        m_tot = jnp.maximum(m_p, m_t[g])
        o_s = _normalize_dup(jnp.exp2(m_p - m_tot) * acc_scr[g] + jnp.exp2(m_t[g] - m_tot) * acct_scr[g])
        comb = cw_scr[g] + g_sel[g] * o_s
        outs = [comb[:, r * Q_BLOCK:(r + 1) * Q_BLOCK].T for r in range(Q_PER_KV)]
        pairs[qb(g)] += [jnp.where(lo_half, outs[0], outs[1]), jnp.where(lo_half, outs[2], outs[3])]
    for d in range(Q_PER_STEP):
        o_ref[0, d * Q_BLOCK:(d + 1) * Q_BLOCK, :] = jnp.concatenate(pairs[d], axis=1)


def _attn(qpt, ksel, vst, kw, vwt, kcmp, vcmpt, c2st, gatest):
    batch, seq = qpt.shape[0], qpt.shape[1] * Q_BLOCK
    per_batch = lambda a: pl.BlockSpec((1,) + a.shape[1:], lambda b, i: (b,) + (0,) * (a.ndim - 1),
                                       pipeline_mode=pl.Buffered(1))
    per_block = lambda a: pl.BlockSpec((1, Q_PER_STEP) + a.shape[2:], lambda b, i: (b, i) + (0,) * (a.ndim - 2))
    chains = Q_PER_STEP * N_KV_HEADS
    grp = lambda *shape: pltpu.VMEM((chains,) + shape, F32)
    return pl.pallas_call(
        _attn_kernel,
        grid=(batch, seq // (Q_PER_STEP * Q_BLOCK)),
        in_specs=[per_block(qpt), per_batch(ksel), per_batch(vst), per_batch(kw), per_batch(vwt), per_batch(kcmp),
                  per_batch(vcmpt), pl.BlockSpec(c2st.shape, lambda b, i: (0, 0)), per_block(gatest)],
        out_specs=pl.BlockSpec((1, Q_PER_STEP * Q_BLOCK, D_ATT), lambda b, i: (b, i, 0)),
        out_shape=jax.ShapeDtypeStruct((batch, seq, D_ATT), F32),
        scratch_shapes=[grp(PIPE, SEL_TILE, COLS), pltpu.VMEM((chains, PIPE, SEL_TILE, COLS), BF16),
                        grp(PIPE, SUBLANES, COLS), grp(PIPE, SUBLANES, COLS), grp(SUBLANES, COLS),
                        grp(V_ROWS, COLS), grp(V_ROWS, COLS),
                        grp(LANES, COLS), grp(WINDOW + Q_BLOCK, COLS),
                        pltpu.VMEM((chains, WINDOW + Q_BLOCK, COLS), BF16)],
        compiler_params=pltpu.CompilerParams(dimension_semantics=("arbitrary", "arbitrary"),
                                             vmem_limit_bytes=VMEM_LIMIT_BYTES),
        name="attn",
    )(qpt, ksel, vst, kw, vwt, kcmp, vcmpt, c2st, gatest)


POST_SUB = 2


def _post_kernel(x_ref, oatt_ref, mconv_ref, p_ref, onag_ref, wo_ref, lnf_ref, wup_ref, fcw_ref, fcb_ref,
                 wdn_ref, lnp_ref, wpg_ref, wpe_ref, out_ref, gbuf, *, tiles_per_seq):
    tm = x_ref.shape[0]
    sub = tm // POST_SUB
    it = pl.program_id(0) % tiles_per_seq
    subs = [slice(k * sub, (k + 1) * sub) for k in range(POST_SUB)]

    @pl.when(it == 0)
    def _():
        gbuf[0:SUBLANES, :] = jnp.zeros((SUBLANES, D_FF), F32)

    h1 = [x_ref[r, :] + _dot(jnp.concatenate([_rms(oatt_ref[r, :], onag_ref[...]).astype(BF16), mconv_ref[r, :]],
                                             axis=1), wo_ref[...]) for r in subs]
    xn = [_rms(h, lnf_ref[...]).astype(BF16) for h in h1]
    h2 = []
    for k, r in enumerate(subs):
        gpre = _dot(xn[k], wup_ref[:, :D_FF])
        up = _dot(xn[k], wup_ref[:, D_FF:])
        lo = SUBLANES + k * sub
        gbuf[lo:lo + sub, :] = gpre
        gate = (fcw_ref[2:3, :] * gpre + fcw_ref[1:2, :] * gbuf[lo - 1:lo - 1 + sub, :]
                + fcw_ref[0:1, :] * gbuf[lo - 2:lo - 2 + sub, :]) + fcb_ref[...]
        act = (gate * jax.nn.sigmoid(gate) * up).astype(BF16)
        h2.append(h1[k] + _dot(act, wdn_ref[...]))
    gbuf[0:SUBLANES, :] = gbuf[tm:tm + SUBLANES, :]
    for k, r in enumerate(subs):
        xn2 = _rms(h2[k], lnp_ref[...]).astype(BF16)
        out_ref[r, :] = (h2[k] + jax.nn.sigmoid(_dot(xn2, wpg_ref[...]))
                         * _dot(p_ref[r, :].astype(BF16), wpe_ref[...]))


def _post(x2, oatt, mconv, p2, onag, wo, lnf, wup, fcw, fcb, wdn, lnp, wpg, wpe, *, seq, tm):
    n = x2.shape[0]
    tps = seq // tm
    row = lambda w: pl.BlockSpec((tm, w), lambda i: (i, 0))
    full = lambda a: pl.BlockSpec(a.shape, lambda i: (0,) * a.ndim, pipeline_mode=pl.Buffered(1))
    return pl.pallas_call(
        functools.partial(_post_kernel, tiles_per_seq=tps),
        grid=(n // tm,),
        in_specs=[row(D_MODEL), row(D_ATT), row(D_CONV), row(D_PLE), full(onag), full(wo), full(lnf), full(wup),
                  full(fcw), full(fcb), full(wdn), full(lnp), full(wpg), full(wpe)],
        out_specs=row(D_MODEL),
        out_shape=jax.ShapeDtypeStruct((n, D_MODEL), F32),
        scratch_shapes=[pltpu.VMEM((tm + SUBLANES, D_FF), F32)],
        compiler_params=pltpu.CompilerParams(dimension_semantics=("arbitrary",),
                                             vmem_limit_bytes=VMEM_LIMIT_BYTES),
        name="post",
    )(x2, oatt, mconv, p2, onag, wo, lnf, wup, fcw, fcb, wdn, lnp, wpg, wpe)


def _rope_tables(pos):
    half = ROT_DIM // 2
    d = np.arange(LANES) % HEAD_DIM
    inv_freq = np.float64(ROPE_THETA) ** (-np.arange(half, dtype=np.float64) * 2.0 / ROT_DIM)
    ang = np.asarray(pos, np.float64)[:, None] * np.tile(inv_freq, LANES // half)[None, :]
    c, sn = np.cos(ang), np.sin(ang)
    first, second = (d < half)[None, :], ((d >= half) & (d < ROT_DIM))[None, :]
    cos = np.where(first | second, c, 1.0)
    sa = np.where(first, -sn, 0.0)
    sb = np.where(second, sn, 0.0)
    return tuple(jnp.asarray(t, F32) for t in (cos, sa, sb))


def _block_diag_mean(width):
    idx = np.arange(width) // HEAD_DIM
    return jnp.asarray((idx[:, None] == idx[None, :]).astype(np.float32) / HEAD_DIM, BF16)


def _cmp_to_slc_t(nch):
    cs = CMP_STRIDE * np.arange(nch)[None, :]
    ss = SLC_BLOCK * np.arange(N_SLC_LANES)[:, None]
    ov = np.clip(np.minimum(cs + CMP_BLOCK, ss + SLC_BLOCK) - np.maximum(cs, ss), 0, None)
    return jnp.asarray(ov.astype(np.float32) / CMP_BLOCK, BF16)


def _layer(h, p_l, ln_mix_g, w_in, qn_g, kn_g, pe_k, pe_v, w_ck1, w_ck2, w_cv1, w_cv2, conv_w, on_att_g,
           on_conv_g, w_o, ln_ffn_g, w_up, ffn_conv_w, ffn_conv_b, w_down, ln_ple_g, w_pg, w_pe):
    batch, seq, _ = h.shape
    assert seq % SEL_TILE == 0 and seq // SLC_BLOCK <= N_SLC_LANES and seq // SLC_BLOCK >= N_SELECT
    assert seq >= WINDOW + Q_BLOCK
    n = batch * seq
    nch = seq // CMP_STRIDE
    x2 = h.reshape(n, D_MODEL)
    row = lambda v: v.reshape(1, -1).astype(F32)

    o_q, o_kv, o_g, o_cv = 0, D_ATT, D_ATT + 6 * KV_W, D_ATT + 6 * KV_W + N_BRANCH * N_Q_HEADS
    wq = w_in[:, o_q:o_kv].astype(BF16)
    wkv = w_in[:, o_kv:o_g].astype(BF16)
    wcv = w_in[:, o_cv:].astype(BF16)
    per_g = Q_PER_KV * N_BRANCH
    wg = jnp.concatenate(
        [jnp.pad(w_in[:, o_g + g * per_g:o_g + (g + 1) * per_g], ((0, 0), (0, LANES - per_g)))
         for g in range(N_KV_HEADS)], axis=1).astype(BF16)
    cos, sa, sb = _rope_tables(np.arange(seq))
    bd512, bd128 = _block_diag_mean(D_ATT), _block_diag_mean(KV_W)
    tile_heads = lambda v, k: jnp.tile(v.astype(F32), k).reshape(1, -1)

    tm = ROW_TILE
    assert seq % ROW_TILE == 0 and seq % POST_TILE == 0
    qpt, kc2, vc2, ksel, vst, kw, vwt, gatest, mconv = _inproj(
        x2, row(ln_mix_g), wq, wkv, wg, wcv, tile_heads(qn_g, N_Q_HEADS),
        jnp.stack([jnp.tile(kn_g[1], N_KV_HEADS), jnp.tile(kn_g[2], N_KV_HEADS)]).astype(F32),
        bd512, bd128, cos, sa, sb, conv_w.astype(F32), row(on_conv_g), batch=batch, seq=seq, tm=tm)

    assert N_KV_HEADS == 2
    half = CMP_BLOCK // 2

    def w1_parts(w1):
        w = w1.astype(BF16)
        z = jnp.zeros_like(w)
        wfull = jnp.concatenate([jnp.concatenate([w, z], axis=2), jnp.concatenate([z, w], axis=2)], axis=1)
        return wfull[:half].reshape(half * KV_W, -1), wfull[half:].reshape(half * KV_W, -1)

    def pe_parts(pe):
        pf = jnp.broadcast_to(pe[:, None, :], (CMP_BLOCK, N_KV_HEADS, HEAD_DIM)).astype(F32)
        return pf[:half].reshape(1, -1), pf[half:].reshape(1, -1)

    def w2bd(w2):
        w = w2.astype(BF16)
        z = jnp.zeros_like(w)
        return jnp.concatenate([jnp.concatenate([w, z], axis=1), jnp.concatenate([z, w], axis=1)], axis=0)
    w1ak, w1bk = w1_parts(w_ck1)
    w1av, w1bv = w1_parts(w_cv1)
    peak, pebk = pe_parts(pe_k)
    peav, pebv = pe_parts(pe_v)
    ccos, csa, csb = _rope_tables(CMP_STRIDE * np.arange(nch) + CMP_BLOCK - 1)
    kcmp, vcmpt = _compress(kc2.reshape(batch, seq, KV_W), vc2.reshape(batch, seq, KV_W),
                            w1ak, w1bk, w1av, w1bv, peak, pebk, peav, pebv, w2bd(w_ck2), w2bd(w_cv2),
                            tile_heads(kn_g[0], N_KV_HEADS), bd128, ccos, csa, csb)

    oatt = _attn(qpt, ksel, vst, kw, vwt, kcmp, vcmpt, _cmp_to_slc_t(nch), gatest)

    out = _post(x2, oatt.reshape(n, D_ATT), mconv, p_l.reshape(n, D_PLE), row(on_att_g), w_o.astype(BF16),
                row(ln_ffn_g), w_up.astype(BF16), ffn_conv_w.astype(F32), row(ffn_conv_b), w_down.astype(BF16),
                row(ln_ple_g), w_pg.astype(BF16), w_pe.astype(BF16), seq=seq, tm=POST_TILE)
    return out.reshape(batch, seq, D_MODEL)


def kernel(x, p, ln_mix_g, w_in, qn_g, kn_g, pe_k, pe_v, w_ck1, w_ck2, w_cv1, w_cv2, conv_w, on_att_g,
           on_conv_g, w_o, ln_ffn_g, w_up, ffn_conv_w, ffn_conv_b, w_down, ln_ple_g, w_pg, w_pe):
    h = x
    for i in range(p.shape[0]):
        h = _layer(h, p[i], ln_mix_g[i], w_in[i], qn_g[i], kn_g[i], pe_k[i], pe_v[i], w_ck1[i], w_ck2[i],
                   w_cv1[i], w_cv2[i], conv_w[i], on_att_g[i], on_conv_g[i], w_o[i], ln_ffn_g[i], w_up[i],
                   ffn_conv_w[i], ffn_conv_b[i], w_down[i], ln_ple_g[i], w_pg[i], w_pe[i])
    return h
```

```python
import functools

import jax
import jax.numpy as jnp
import numpy as np
from jax import lax
from jax.experimental import pallas as pl
from jax.experimental.pallas import tpu as pltpu

D_MODEL = 1024
HEAD_DIM = 64
N_Q_HEADS = 8
N_KV_HEADS = 2
Q_PER_KV = N_Q_HEADS // N_KV_HEADS
D_ATT = N_Q_HEADS * HEAD_DIM
D_CONV = D_MODEL - D_ATT
KV_W = N_KV_HEADS * HEAD_DIM
N_BRANCH = 3
CONV_TAPS = 3
ROT_DIM = HEAD_DIM // 4
ROPE_THETA = 500000.0
CMP_BLOCK = 32
CMP_STRIDE = 16
CMP_HIDDEN = 256
SLC_BLOCK = 64
N_SELECT = 16
WINDOW = 512
Q_BLOCK = 128
D_FF = 2816
D_PLE = 256
EPS = 1e-6
NEG = -1e30
INT32_MIN = -2 ** 31
LOG2E = float(np.log2(np.e))
V_ROWS = HEAD_DIM + 16
GATE_ROWS = 16
SLC_SHIFT = SLC_BLOCK.bit_length() - 1
CMP_SHIFT = CMP_STRIDE.bit_length() - 1
assert 1 << SLC_SHIFT == SLC_BLOCK and 1 << CMP_SHIFT == CMP_STRIDE

LANES = 128
SUBLANES = 8
N_SLC_LANES = LANES
VMEM_LIMIT_BYTES = 56 * 1024 * 1024

F32 = jnp.float32
BF16 = jnp.bfloat16


def _dot(a, b):
    return jnp.dot(a, b, preferred_element_type=F32)


def _rms(x, g):
    return x * lax.rsqrt(jnp.mean(x * x, axis=-1, keepdims=True) + EPS) * g


def _head_rms_rope(x, g, bd, cos, sa, sb):
    w = x.shape[-1]
    msq = _dot((x * x).astype(BF16), bd)
    xn = x * lax.rsqrt(msq + EPS) * g
    return xn * cos + pltpu.roll(xn, w - ROT_DIM // 2, 1) * sa + pltpu.roll(xn, ROT_DIM // 2, 1) * sb


def _dup_halves(x):
    r = pltpu.roll(x, HEAD_DIM, 1)
    lane = lax.broadcasted_iota(jnp.int32, x.shape, 1)
    lo = lane < HEAD_DIM
    return jnp.where(lo, x, r), jnp.where(lo, r, x)


def _inproj_kernel(x_ref, lng_ref, wq_ref, wkv_ref, wg_ref, wcv_ref, qng_ref, kng_ref, bd512_ref, bd128_ref,
                   cos_ref, sa_ref, sb_ref, convw_ref, oncg_ref,
                   q_out, kc_out, vc_out, ksel_out, vs_out, kw_out, vw_out, gate_out, mconv_out,
                   zbuf, *, tiles_per_seq):
    tm = x_ref.shape[0]
    sub = tm // IN_SUB
    it = pl.program_id(0) % tiles_per_seq

    @pl.when(it == 0)
    def _():
        zbuf[0:SUBLANES, :] = jnp.zeros((SUBLANES, D_CONV), F32)

    for k in range(IN_SUB):
        r = slice(k * sub, (k + 1) * sub)
        xn = _rms(x_ref[r, :], lng_ref[...]).astype(BF16)

        cos, sa, sb = cos_ref[r, :], sa_ref[r, :], sb_ref[r, :]
        cos4, sa4, sb4 = (jnp.concatenate([t] * 4, axis=1) for t in (cos, sa, sb))
        q = _dot(xn, wq_ref[...])
        qr = _head_rms_rope(q, qng_ref[...], bd512_ref[...], cos4, sa4, sb4)
        qs = qr * (HEAD_DIM ** -0.5 * LOG2E)
        for blk_i in range(sub // Q_BLOCK):
            rows = slice(blk_i * Q_BLOCK, (blk_i + 1) * Q_BLOCK)
            for pr in range(N_Q_HEADS // 2):
                q_out[0, k * (sub // Q_BLOCK) + blk_i, pr] = qs[rows, pr * LANES:(pr + 1) * LANES].T.astype(BF16)

        kv = _dot(xn, wkv_ref[...])
        kc_out[r, :] = kv[:, 0 * KV_W:1 * KV_W]
        vc_out[r, :] = kv[:, 1 * KV_W:2 * KV_W]
        ks = _head_rms_rope(kv[:, 2 * KV_W:3 * KV_W], kng_ref[0:1, :], bd128_ref[...], cos, sa, sb)
        kw = _head_rms_rope(kv[:, 4 * KV_W:5 * KV_W], kng_ref[1:2, :], bd128_ref[...], cos, sa, sb)
        vs = kv[:, 3 * KV_W:4 * KV_W]
        vw = kv[:, 5 * KV_W:6 * KV_W]
        tpos = it * tm + k * sub + lax.broadcasted_iota(jnp.int32, (sub, N_SLC_LANES), 0)
        blk = lax.broadcasted_iota(jnp.int32, (sub, N_SLC_LANES), 1)
        onehot = jnp.where(lax.shift_right_logical(tpos, SLC_SHIFT) == blk, 1.0, 0.0).astype(BF16)
        ks_d, vs_d, kw_d, vw_d = (_dup_halves(t) for t in (ks, vs, kw, vw))
        vrow = lax.broadcasted_iota(jnp.int32, (V_ROWS, sub), 0)
        for g in range(N_KV_HEADS):
            ksel_out[0, g, r, :] = jnp.concatenate([ks_d[g].astype(BF16), onehot], axis=1)
            kw_out[0, g, r, :] = kw_d[g].astype(BF16)
            vst = jnp.where(vrow < HEAD_DIM, vs_d[g].T[:V_ROWS], 1.0).astype(BF16)
            for c in range(sub // SEL_TILE):
                vs_out[0, g, k * (sub // SEL_TILE) + c] = vst[:, c * SEL_TILE:(c + 1) * SEL_TILE]
            vwt = jnp.where(vrow < HEAD_DIM, vw_d[g].T[:V_ROWS], 1.0).astype(BF16)
            for c in range(sub // LANES):
                vw_out[0, g, k * (sub // LANES) + c] = vwt[:, c * LANES:(c + 1) * LANES]

        gates = jax.nn.sigmoid(_dot(xn, wg_ref[...]))
        for blk_i in range(sub // Q_BLOCK):
            rows = slice(blk_i * Q_BLOCK, (blk_i + 1) * Q_BLOCK)
            for g in range(N_KV_HEADS):
                gate_out[0, k * (sub // Q_BLOCK) + blk_i, g] = gates[rows, g * LANES:(g + 1) * LANES].T[:GATE_ROWS]

        cv = _dot(xn, wcv_ref[...])
        cb, cc, cx = cv[:, :D_CONV], cv[:, D_CONV:2 * D_CONV], cv[:, 2 * D_CONV:]
        z = cc * cx
        lo = SUBLANES + k * sub
        zbuf[lo:lo + sub, :] = z
        y = (convw_ref[2:3, :] * z + convw_ref[1:2, :] * zbuf[lo - 1:lo - 1 + sub, :]
             + convw_ref[0:1, :] * zbuf[lo - 2:lo - 2 + sub, :])
        mconv_out[r, :] = _rms(cb * y, oncg_ref[...]).astype(BF16)
    zbuf[0:SUBLANES, :] = zbuf[tm:tm + SUBLANES, :]


def _inproj(x2, lng, wq, wkv, wg, wcv, qng, kng, bd512, bd128, cos, sa, sb, convw, oncg, *, batch, seq, tm):
    n = batch * seq
    tps = seq // tm
    row = lambda w: pl.BlockSpec((tm, w), lambda i: (i, 0))
    full = lambda a: pl.BlockSpec(a.shape, lambda i: (0,) * a.ndim)
    tab = pl.BlockSpec((tm, LANES), lambda i: (i % tps, 0))
    grp = lambda w: pl.BlockSpec((1, N_KV_HEADS, tm, w), lambda i: (i // tps, 0, i % tps, 0))
    gshape = lambda w: jax.ShapeDtypeStruct((batch, N_KV_HEADS, seq, w), BF16)
    assert tm % (IN_SUB * SEL_TILE) == 0
    vtile = lambda keys: pl.BlockSpec((1, N_KV_HEADS, tm // keys, V_ROWS, keys),
                                      lambda i: (i // tps, 0, i % tps, 0, 0))
    vshape = lambda keys: jax.ShapeDtypeStruct((batch, N_KV_HEADS, seq // keys, V_ROWS, keys), BF16)
    qblk = lambda tile: pl.BlockSpec((1, tm // Q_BLOCK) + tile, lambda i: (i // tps, i % tps) + (0,) * len(tile))
    return pl.pallas_call(
        functools.partial(_inproj_kernel, tiles_per_seq=tps),
        grid=(n // tm,),
        in_specs=[row(D_MODEL), full(lng), full(wq), full(wkv), full(wg), full(wcv), full(qng), full(kng),
                  full(bd512), full(bd128), tab, tab, tab, full(convw), full(oncg)],
        out_specs=[qblk((N_Q_HEADS // 2, LANES, Q_BLOCK)), row(KV_W), row(KV_W), grp(2 * LANES), vtile(SEL_TILE),
                   grp(LANES), vtile(LANES), qblk((N_KV_HEADS, GATE_ROWS, Q_BLOCK)), row(D_CONV)],
        out_shape=[jax.ShapeDtypeStruct((batch, seq // Q_BLOCK, N_Q_HEADS // 2, LANES, Q_BLOCK), BF16),
                   jax.ShapeDtypeStruct((n, KV_W), F32),
                   jax.ShapeDtypeStruct((n, KV_W), F32), gshape(2 * LANES), vshape(SEL_TILE), gshape(LANES),
                   vshape(LANES),
                   jax.ShapeDtypeStruct((batch, seq // Q_BLOCK, N_KV_HEADS, GATE_ROWS, Q_BLOCK), F32),
                   jax.ShapeDtypeStruct((n, D_CONV), BF16)],
        scratch_shapes=[pltpu.VMEM((tm + SUBLANES, D_CONV), F32)],
        compiler_params=pltpu.CompilerParams(dimension_semantics=("arbitrary",),
                                             vmem_limit_bytes=VMEM_LIMIT_BYTES),
        name="inproj",
    )(x2, lng, wq, wkv, wg, wcv, qng, kng, bd512, bd128, cos, sa, sb, convw, oncg)


def _gelu_tanh(x):
    return 0.5 * x * (1.0 + jnp.tanh(np.sqrt(2.0 / np.pi).astype(np.float32) * (x + 0.044715 * (x * x * x))))


def _compress_kernel(kc_ref, vc_ref, w1ak_ref, w1bk_ref, w1av_ref, w1bv_ref, peak_ref, pebk_ref, peav_ref,
                     pebv_ref, w2k_ref, w2v_ref, kng_ref, bd128_ref, cos_ref, sa_ref, sb_ref,
                     kcmp_out, vcmp_out):
    nch = kc_ref.shape[1] // CMP_STRIDE

    def chunks(ref):
        return jnp.concatenate([ref[0, pl.ds(l, nch, stride=CMP_STRIDE), :] for l in range(CMP_STRIDE)], axis=1)

    def compress(x, w1a, w1b, pea, peb, w2):
        a = _dot((x + pea).astype(BF16), w1a)
        b = _dot((x + peb).astype(BF16), w1b)
        hid = _gelu_tanh(a + pltpu.roll(b, nch - 1, 0))
        return _dot(hid.astype(BF16), w2)

    kc = compress(chunks(kc_ref), w1ak_ref[...], w1bk_ref[...], peak_ref[...], pebk_ref[...], w2k_ref[...])
    vc = compress(chunks(vc_ref), w1av_ref[...], w1bv_ref[...], peav_ref[...], pebv_ref[...], w2v_ref[...])
    kc = _head_rms_rope(kc, kng_ref[...], bd128_ref[...], cos_ref[...], sa_ref[...], sb_ref[...])
    kd, vd = _dup_halves(kc), _dup_halves(vc)
    for g in range(N_KV_HEADS):
        kcmp_out[0, g] = kd[g].astype(BF16)
        vcmp_out[0, g] = vd[g].T.astype(BF16)


def _compress(kc3, vc3, w1ak, w1bk, w1av, w1bv, peak, pebk, peav, pebv, w2k, w2v, kng0, bd128, cos, sa, sb):
    batch, seq, wide = kc3.shape
    nch = seq // CMP_STRIDE
    full = lambda a: pl.BlockSpec(a.shape, lambda b: (0,) * a.ndim)
    tok = pl.BlockSpec((1, seq, wide), lambda b: (b, 0, 0))
    kout = pl.BlockSpec((1, N_KV_HEADS, nch, LANES), lambda b: (b, 0, 0, 0))
    vout = pl.BlockSpec((1, N_KV_HEADS, LANES, nch), lambda b: (b, 0, 0, 0))
    consts = (w1ak, w1bk, w1av, w1bv, peak, pebk, peav, pebv, w2k, w2v, kng0, bd128, cos, sa, sb)
    return pl.pallas_call(
        _compress_kernel,
        grid=(batch,),
        in_specs=[tok, tok] + [full(a) for a in consts],
        out_specs=[kout, vout],
        out_shape=[jax.ShapeDtypeStruct((batch, N_KV_HEADS, nch, LANES), BF16),
                   jax.ShapeDtypeStruct((batch, N_KV_HEADS, LANES, nch), BF16)],
        compiler_params=pltpu.CompilerParams(dimension_semantics=("arbitrary",),
                                             vmem_limit_bytes=VMEM_LIMIT_BYTES),
        name="compress",
    )(kc3, vc3, *consts)


SEL_TILE = 256
COLS = Q_PER_KV * Q_BLOCK
ROW_TILE = 1024
IN_SUB = 4
POST_TILE = 512
RADIX_BITS = 1
Q_PER_STEP = 2
PIPE = 2
SWEEP_UNROLL = 2
assert SWEEP_UNROLL % PIPE == 0 and SEL_TILE % (Q_PER_STEP * Q_BLOCK) == 0
WIN_MASK_ROWS = 128
WIN_EXP_ROWS = 128
assert (WINDOW + Q_BLOCK) // WIN_MASK_ROWS + (WINDOW + Q_BLOCK) // WIN_EXP_ROWS <= 32


def _split2(x):
    hi = x.astype(BF16)
    return hi, (x - hi.astype(F32)).astype(BF16)


def _zero_after(x):
    return lax.shift_right_logical(lax.shift_right_logical(x, 16), 16)


def _normalize_dup(acc):
    o = acc[:HEAD_DIM] / acc[HEAD_DIM:HEAD_DIM + 1]
    return jnp.concatenate([o, o], axis=0)


def _attn_kernel(q_ref, ksel_ref, vst_ref, kw_ref, vwt_ref, kcmp_ref, vcmpt_ref, c2st_ref, gate_ref, o_ref,
                 s_all, p_all, a_all, cm_all, m_scr, acc_scr, acct_scr, cw_scr, sw_scr, pw_scr):
    i = pl.program_id(1)
    starts = [(Q_PER_STEP * i + d) * Q_BLOCK for d in range(Q_PER_STEP)]
    kv = lambda g: g % N_KV_HEADS
    qb = lambda g: g // N_KV_HEADS
    nch = kcmp_ref.shape[2]
    last_tile = ksel_ref.shape[2] // SEL_TILE - 1
    groups = range(Q_PER_STEP * N_KV_HEADS)

    lane = lax.broadcasted_iota(jnp.int32, (Q_BLOCK, LANES), 1)
    lo_half = lane < HEAD_DIM
    top_half = lax.broadcasted_iota(jnp.int32, (LANES, Q_BLOCK), 0) < HEAD_DIM
    zero_bf = jnp.zeros((LANES, Q_BLOCK), BF16)
    tqs = [st + (lax.broadcasted_iota(jnp.int32, (1, COLS), 1) & (Q_BLOCK - 1)) for st in starts]
    wk = WINDOW + Q_BLOCK
    bases = [pl.multiple_of(jnp.maximum(st - WINDOW, 0), Q_BLOCK) for st in starts]
    wrow = lax.broadcasted_iota(jnp.int32, (WIN_MASK_ROWS, COLS), 0)
    nrow = lax.broadcasted_iota(jnp.int32, (nch, COLS), 0)
    srow = lax.broadcasted_iota(jnp.int32, (N_SLC_LANES, Q_BLOCK), 0)
    tqqs = [st + lax.broadcasted_iota(jnp.int32, (N_SLC_LANES, Q_BLOCK), 1) for st in starts]
    c2st = c2st_ref[...]
    n_main = starts[0] // SEL_TILE

    qt = [jnp.concatenate([jnp.where(top_half if r % 2 == 0 else ~top_half,
                                     q_ref[0, qb(g), kv(g) * (Q_PER_KV // 2) + r // 2], zero_bf)
                           for r in range(Q_PER_KV)], axis=1) for g in groups]

    def grow(g, b):
        gt = gate_ref[0, qb(g), kv(g)]
        return jnp.concatenate([gt[r * N_BRANCH + b:r * N_BRANCH + b + 1, :] for r in range(Q_PER_KV)], axis=1)

    s_c = [_dot(kcmp_ref[0, kv(g)], qt[g]) for g in groups]
    for g in groups:
        sw_scr[g] = _dot(kw_ref[0, kv(g), pl.ds(bases[qb(g)], wk), :], qt[g])

    last_valid = [lax.shift_right_arithmetic(t - (CMP_BLOCK - 1), CMP_SHIFT) for t in tqs]
    curs = [lax.shift_right_logical(t, SLC_SHIFT) for t in tqqs]
    forced = [(srow == 0) | (srow == c) | (srow == c - 1) for c in curs]
    future = [srow * SLC_BLOCK > t for t in tqqs]
    impt = []
    for g in groups:
        sc = jnp.where(nrow <= last_valid[qb(g)], s_c[g], NEG)
        m_c = jnp.max(sc, axis=0, keepdims=True)
        e_c = jnp.exp2(sc - m_c)
        l_c = jnp.sum(e_c, axis=0, keepdims=True)
        p_c = e_c * jnp.where(m_c > 0.5 * NEG, 1.0 / l_c, 0.0)
        psum = sum(p_c[:, r * Q_BLOCK:(r + 1) * Q_BLOCK] for r in range(Q_PER_KV))
        imp = sum(_dot(c2st, t) for t in _split2(psum))
        impt.append(jnp.where(forced[qb(g)], 1e9, jnp.where(future[qb(g)], -1e9, imp)))
        cw_scr[g] = grow(g, 0) * _dot(vcmpt_ref[0, kv(g)], p_c.astype(BF16))

    def window_mask_piece(g, k, colmax, zero):
        rows = slice(WIN_MASK_ROWS * k, WIN_MASK_ROWS * (k + 1))
        back = (tqs[qb(g)] + zero - bases[qb(g)] - WIN_MASK_ROWS * k) - wrow
        in_window = lax.bitcast_convert_type(back, jnp.uint32) < jnp.uint32(WINDOW)
        sm = jnp.where(in_window, sw_scr[g, rows, :], NEG)
        sw_scr[g, rows, :] = sm
        for c in range(WIN_MASK_ROWS // SUBLANES):
            colmax = jnp.maximum(colmax, sm[c * SUBLANES:(c + 1) * SUBLANES])
        return colmax

    def window_exp_piece(g, k, m_w, zero):
        rows = slice(WIN_EXP_ROWS * k, WIN_EXP_ROWS * (k + 1))
        pw_scr[g, rows, :] = jnp.exp2(sw_scr[g, rows, :] - (m_w + zero.astype(F32))).astype(BF16)

    key_to_float = lambda k: lax.bitcast_convert_type(jnp.where(k >= 0, k, k ^ jnp.int32(0x7FFFFFFF)), F32)
    wrap32 = lambda v: ((v + 2 ** 31) % 2 ** 32) - 2 ** 31
    n_mask, n_exp = wk // WIN_MASK_ROWS, wk // WIN_EXP_ROWS
    thr = [jnp.full((1, Q_BLOCK), INT32_MIN, jnp.int32) for _ in groups]
    colmax = [jnp.full((SUBLANES, COLS), NEG, F32) for _ in groups]
    m_w = [None for _ in groups]
    piece = 0
    for b in range(32 - RADIX_BITS, -1, -RADIX_BITS):
        for g in groups:
            reached = thr[g]
            for mult in range(1, 2 ** RADIX_BITS):
                cand = thr[g] + jnp.int32(wrap32(mult << b))
                n_ge = jnp.sum(jnp.where(impt[g] >= key_to_float(cand), 1.0, 0.0), axis=0, keepdims=True)
                reached = jnp.where(n_ge >= N_SELECT, cand, reached)
            thr[g] = reached
        for _ in range(RADIX_BITS):
            for g in groups:
                zero = jnp.concatenate([_zero_after(thr[g])] * Q_PER_KV, axis=1)
                if piece < n_mask:
                    colmax[g] = window_mask_piece(g, piece, colmax[g], zero)
                elif piece < n_mask + n_exp:
                    if piece == n_mask:
                        m_w[g] = jnp.max(colmax[g], axis=0, keepdims=True)
                    window_exp_piece(g, piece - n_mask, m_w[g], zero)
            piece += 1

    scol = lax.broadcasted_iota(jnp.int32, (N_SLC_LANES, N_SLC_LANES), 1)
    earlier = jnp.where(scol < srow, 1.0, 0.0).astype(BF16)
    widen = lambda x: jnp.concatenate([x.astype(BF16)] * Q_PER_KV, axis=1)
    qt_tail, qt_main = [], []
    for g in groups:
        vw = jnp.concatenate([vwt_ref[0, kv(g), bases[qb(g)] // LANES + c] for c in range(wk // LANES)],
                             axis=1)
        cw_scr[g] += grow(g, 2) * _normalize_dup(_dot(vw, pw_scr[g]))
        kth = key_to_float(thr[g])
        above = impt[g] > kth
        tied = impt[g] == kth
        n_above = jnp.sum(jnp.where(above, 1.0, 0.0), axis=0, keepdims=True)
        tied_before = _dot(earlier, jnp.where(tied, 1.0, 0.0).astype(BF16))
        selected = above | (tied & (tied_before < N_SELECT - n_above))
        bias = jnp.where(selected, 0.0, NEG)
        bias_main = jnp.where(srow >= n_main * (SEL_TILE // SLC_BLOCK), NEG, bias)
        qt_tail.append(jnp.concatenate([qt[g], widen(bias)], axis=0))
        qt_main.append(jnp.concatenate([qt[g], widen(bias_main)], axis=0))

    kt = pl.multiple_of(n_main * SEL_TILE, SEL_TILE)
    krow = lax.broadcasted_iota(jnp.int32, (SEL_TILE, COLS), 0)
    tile0 = slice(0, SEL_TILE)
    for g in groups:
        sw_scr[g, tile0, :] = _dot(ksel_ref[0, kv(g), pl.ds(kt, SEL_TILE), :], qt_tail[g])
    g_sel = [grow(g, 1) for g in groups]

    def scores(g, j, slot):
        k0 = pl.multiple_of(jnp.minimum(j, last_tile) * SEL_TILE, SEL_TILE)
        sv = _dot(ksel_ref[0, kv(g), pl.ds(k0, SEL_TILE), :], qt_main[g])
        s_all[g, slot] = sv
        cm = sv[0:SUBLANES]
        for c in range(1, SEL_TILE // SUBLANES):
            cm = jnp.maximum(cm, sv[c * SUBLANES:(c + 1) * SUBLANES])
        cm_all[g, slot] = cm

    def softmax(g, slot):
        s = s_all[g, slot]
        m_prev = m_scr[g, 0:1, :]
        m_new = jnp.maximum(m_prev, jnp.max(cm_all[g, slot], axis=0, keepdims=True))
        a_all[g, slot] = jnp.broadcast_to(jnp.exp2(m_prev - m_new), (SUBLANES, COLS))
        p_all[g, slot] = jnp.exp2(s - m_new).astype(BF16)
        m_scr[g] = jnp.broadcast_to(m_new, (SUBLANES, COLS))

    def values(g, j, slot):
        acc_scr[g] = (a_all[g, slot, 0:1, :] * acc_scr[g]
                      + _dot(vst_ref[0, kv(g), jnp.minimum(j, last_tile)], p_all[g, slot]))

    m_scr[...] = jnp.full(m_scr.shape, NEG, F32)
    acc_scr[...] = jnp.zeros(acc_scr.shape, F32)
    for k in range(PIPE):
        for g in groups:
            scores(g, k, k)
    m_t = []
    for g in groups:
        s_t = jnp.where(kt + krow <= tqs[qb(g)], sw_scr[g, tile0, :], NEG)
        m_t.append(jnp.max(s_t, axis=0, keepdims=True))
        pw_scr[g, tile0, :] = jnp.exp2(s_t - m_t[g]).astype(BF16)
    for g in groups:
        acct_scr[g] = _dot(vst_ref[0, kv(g), n_main], pw_scr[g, tile0, :])
    for k in range(PIPE // 2):
        for g in groups:
            softmax(g, k)

    def sweep(t, carry):
        for k in range(SWEEP_UNROLL):
            for g in groups:
                scores(g, SWEEP_UNROLL * t + k + PIPE, k % PIPE)
            for g in groups:
                values(g, SWEEP_UNROLL * t + k, k % PIPE)
            for g in groups:
                softmax(g, (k + PIPE // 2) % PIPE)
        return carry

    lax.fori_loop(0, (n_main + SWEEP_UNROLL - 1) // SWEEP_UNROLL, sweep, 0)

    pairs = [[] for _ in range(Q_PER_STEP)]
    for g in groups:
        m_p = m_scr[g, 0:1, :]
        m_tot = jnp.maximum(m_p, m_t[g])
        o_s = _normalize_dup(jnp.exp2(m_p - m_tot) * acc_scr[g] + jnp.exp2(m_t[g] - m_tot) * acct_scr[g])
        comb = cw_scr[g] + g_sel[g] * o_s
        outs = [comb[:, r * Q_BLOCK:(r + 1) * Q_BLOCK].T for r in range(Q_PER_KV)]
        pairs[qb(g)] += [jnp.where(lo_half, outs[0], outs[1]), jnp.where(lo_half, outs[2], outs[3])]
    for d in range(Q_PER_STEP):
        o_ref[0, d * Q_BLOCK:(d + 1) * Q_BLOCK, :] = jnp.concatenate(pairs[d], axis=1)


def _attn(qpt, ksel, vst, kw, vwt, kcmp, vcmpt, c2st, gatest):
    batch, seq = qpt.shape[0], qpt.shape[1] * Q_BLOCK
    per_batch = lambda a: pl.BlockSpec((1,) + a.shape[1:], lambda b, i: (b,) + (0,) * (a.ndim - 1),
                                       pipeline_mode=pl.Buffered(1))
    per_block = lambda a: pl.BlockSpec((1, Q_PER_STEP) + a.shape[2:], lambda b, i: (b, i) + (0,) * (a.ndim - 2))
    chains = Q_PER_STEP * N_KV_HEADS
    grp = lambda *shape: pltpu.VMEM((chains,) + shape, F32)
    return pl.pallas_call(
        _attn_kernel,
        grid=(batch, seq // (Q_PER_STEP * Q_BLOCK)),
        in_specs=[per_block(qpt), per_batch(ksel), per_batch(vst), per_batch(kw), per_batch(vwt), per_batch(kcmp),
                  per_batch(vcmpt), pl.BlockSpec(c2st.shape, lambda b, i: (0, 0)), per_block(gatest)],
        out_specs=pl.BlockSpec((1, Q_PER_STEP * Q_BLOCK, D_ATT), lambda b, i: (b, i, 0)),
        out_shape=jax.ShapeDtypeStruct((batch, seq, D_ATT), F32),
        scratch_shapes=[grp(PIPE, SEL_TILE, COLS), pltpu.VMEM((chains, PIPE, SEL_TILE, COLS), BF16),
                        grp(PIPE, SUBLANES, COLS), grp(PIPE, SUBLANES, COLS), grp(SUBLANES, COLS),
                        grp(V_ROWS, COLS), grp(V_ROWS, COLS),
                        grp(LANES, COLS), grp(WINDOW + Q_BLOCK, COLS),
                        pltpu.VMEM((chains, WINDOW + Q_BLOCK, COLS), BF16)],
        compiler_params=pltpu.CompilerParams(dimension_semantics=("arbitrary", "arbitrary"),
                                             vmem_limit_bytes=VMEM_LIMIT_BYTES),
        name="attn",
    )(qpt, ksel, vst, kw, vwt, kcmp, vcmpt, c2st, gatest)


POST_SUB = 2


def _post_kernel(x_ref, oatt_ref, mconv_ref, p_ref, onag_ref, wo_ref, lnf_ref, wup_ref, fcw_ref, fcb_ref,
                 wdn_ref, lnp_ref, wpg_ref, wpe_ref, out_ref, gbuf, *, tiles_per_seq):
    tm = x_ref.shape[0]
    sub = tm // POST_SUB
    it = pl.program_id(0) % tiles_per_seq
    subs = [slice(k * sub, (k + 1) * sub) for k in range(POST_SUB)]

    @pl.when(it == 0)
    def _():
        gbuf[0:SUBLANES, :] = jnp.zeros((SUBLANES, D_FF), F32)

    h1 = [x_ref[r, :] + _dot(jnp.concatenate([_rms(oatt_ref[r, :], onag_ref[...]).astype(BF16), mconv_ref[r, :]],
                                             axis=1), wo_ref[...]) for r in subs]
    xn = [_rms(h, lnf_ref[...]).astype(BF16) for h in h1]
    h2 = []
    for k, r in enumerate(subs):
        gpre = _dot(xn[k], wup_ref[:, :D_FF])
        up = _dot(xn[k], wup_ref[:, D_FF:])
        lo = SUBLANES + k * sub
        gbuf[lo:lo + sub, :] = gpre
        gate = (fcw_ref[2:3, :] * gpre + fcw_ref[1:2, :] * gbuf[lo - 1:lo - 1 + sub, :]
                + fcw_ref[0:1, :] * gbuf[lo - 2:lo - 2 + sub, :]) + fcb_ref[...]
        act = (gate * jax.nn.sigmoid(gate) * up).astype(BF16)
        h2.append(h1[k] + _dot(act, wdn_ref[...]))
    gbuf[0:SUBLANES, :] = gbuf[tm:tm + SUBLANES, :]
    for k, r in enumerate(subs):
        xn2 = _rms(h2[k], lnp_ref[...]).astype(BF16)
        out_ref[r, :] = (h2[k] + jax.nn.sigmoid(_dot(xn2, wpg_ref[...]))
                         * _dot(p_ref[r, :].astype(BF16), wpe_ref[...]))


def _post(x2, oatt, mconv, p2, onag, wo, lnf, wup, fcw, fcb, wdn, lnp, wpg, wpe, *, seq, tm):
    n = x2.shape[0]
    tps = seq // tm
    row = lambda w: pl.BlockSpec((tm, w), lambda i: (i, 0))
    full = lambda a: pl.BlockSpec(a.shape, lambda i: (0,) * a.ndim, pipeline_mode=pl.Buffered(1))
    return pl.pallas_call(
        functools.partial(_post_kernel, tiles_per_seq=tps),
        grid=(n // tm,),
        in_specs=[row(D_MODEL), row(D_ATT), row(D_CONV), row(D_PLE), full(onag), full(wo), full(lnf), full(wup),
                  full(fcw), full(fcb), full(wdn), full(lnp), full(wpg), full(wpe)],
        out_specs=row(D_MODEL),
        out_shape=jax.ShapeDtypeStruct((n, D_MODEL), F32),
        scratch_shapes=[pltpu.VMEM((tm + SUBLANES, D_FF), F32)],
        compiler_params=pltpu.CompilerParams(dimension_semantics=("arbitrary",),
                                             vmem_limit_bytes=VMEM_LIMIT_BYTES),
        name="post",
    )(x2, oatt, mconv, p2, onag, wo, lnf, wup, fcw, fcb, wdn, lnp, wpg, wpe)


def _rope_tables(pos):
    half = ROT_DIM // 2
    d = np.arange(LANES) % HEAD_DIM
    inv_freq = np.float64(ROPE_THETA) ** (-np.arange(half, dtype=np.float64) * 2.0 / ROT_DIM)
    ang = np.asarray(pos, np.float64)[:, None] * np.tile(inv_freq, LANES // half)[None, :]
    c, sn = np.cos(ang), np.sin(ang)
    first, second = (d < half)[None, :], ((d >= half) & (d < ROT_DIM))[None, :]
    cos = np.where(first | second, c, 1.0)
    sa = np.where(first, -sn, 0.0)
    sb = np.where(second, sn, 0.0)
    return tuple(jnp.asarray(t, F32) for t in (cos, sa, sb))


def _block_diag_mean(width):
    idx = np.arange(width) // HEAD_DIM
    return jnp.asarray((idx[:, None] == idx[None, :]).astype(np.float32) / HEAD_DIM, BF16)


def _cmp_to_slc_t(nch):
    cs = CMP_STRIDE * np.arange(nch)[None, :]
    ss = SLC_BLOCK * np.arange(N_SLC_LANES)[:, None]
    ov = np.clip(np.minimum(cs + CMP_BLOCK, ss + SLC_BLOCK) - np.maximum(cs, ss), 0, None)
    return jnp.asarray(ov.astype(np.float32) / CMP_BLOCK, BF16)


def _layer(h, p_l, ln_mix_g, w_in, qn_g, kn_g, pe_k, pe_v, w_ck1, w_ck2, w_cv1, w_cv2, conv_w, on_att_g,
           on_conv_g, w_o, ln_ffn_g, w_up, ffn_conv_w, ffn_conv_b, w_down, ln_ple_g, w_pg, w_pe):
    batch, seq, _ = h.shape
    assert seq % SEL_TILE == 0 and seq // SLC_BLOCK <= N_SLC_LANES and seq // SLC_BLOCK >= N_SELECT
    assert seq >= WINDOW + Q_BLOCK
    n = batch * seq
    nch = seq // CMP_STRIDE
    x2 = h.reshape(n, D_MODEL)
    row = lambda v: v.reshape(1, -1).astype(F32)

    o_q, o_kv, o_g, o_cv = 0, D_ATT, D_ATT + 6 * KV_W, D_ATT + 6 * KV_W + N_BRANCH * N_Q_HEADS
    wq = w_in[:, o_q:o_kv].astype(BF16)
    wkv = w_in[:, o_kv:o_g].astype(BF16)
    wcv = w_in[:, o_cv:].astype(BF16)
    per_g = Q_PER_KV * N_BRANCH
    wg = jnp.concatenate(
        [jnp.pad(w_in[:, o_g + g * per_g:o_g + (g + 1) * per_g], ((0, 0), (0, LANES - per_g)))
         for g in range(N_KV_HEADS)], axis=1).astype(BF16)
    cos, sa, sb = _rope_tables(np.arange(seq))
    bd512, bd128 = _block_diag_mean(D_ATT), _block_diag_mean(KV_W)
    tile_heads = lambda v, k: jnp.tile(v.astype(F32), k).reshape(1, -1)

    tm = ROW_TILE
    assert seq % ROW_TILE == 0 and seq % POST_TILE == 0
    qpt, kc2, vc2, ksel, vst, kw, vwt, gatest, mconv = _inproj(
        x2, row(ln_mix_g), wq, wkv, wg, wcv, tile_heads(qn_g, N_Q_HEADS),
        jnp.stack([jnp.tile(kn_g[1], N_KV_HEADS), jnp.tile(kn_g[2], N_KV_HEADS)]).astype(F32),
        bd512, bd128, cos, sa, sb, conv_w.astype(F32), row(on_conv_g), batch=batch, seq=seq, tm=tm)

    assert N_KV_HEADS == 2
    half = CMP_BLOCK // 2

    def w1_parts(w1):
        w = w1.astype(BF16)
        z = jnp.zeros_like(w)
        wfull = jnp.concatenate([jnp.concatenate([w, z], axis=2), jnp.concatenate([z, w], axis=2)], axis=1)
        return wfull[:half].reshape(half * KV_W, -1), wfull[half:].reshape(half * KV_W, -1)

    def pe_parts(pe):
        pf = jnp.broadcast_to(pe[:, None, :], (CMP_BLOCK, N_KV_HEADS, HEAD_DIM)).astype(F32)
        return pf[:half].reshape(1, -1), pf[half:].reshape(1, -1)

    def w2bd(w2):
        w = w2.astype(BF16)
        z = jnp.zeros_like(w)
        return jnp.concatenate([jnp.concatenate([w, z], axis=1), jnp.concatenate([z, w], axis=1)], axis=0)
    w1ak, w1bk = w1_parts(w_ck1)
    w1av, w1bv = w1_parts(w_cv1)
    peak, pebk = pe_parts(pe_k)
    peav, pebv = pe_parts(pe_v)
    ccos, csa, csb = _rope_tables(CMP_STRIDE * np.arange(nch) + CMP_BLOCK - 1)
    kcmp, vcmpt = _compress(kc2.reshape(batch, seq, KV_W), vc2.reshape(batch, seq, KV_W),
                            w1ak, w1bk, w1av, w1bv, peak, pebk, peav, pebv, w2bd(w_ck2), w2bd(w_cv2),
                            tile_heads(kn_g[0], N_KV_HEADS), bd128, ccos, csa, csb)

    oatt = _attn(qpt, ksel, vst, kw, vwt, kcmp, vcmpt, _cmp_to_slc_t(nch), gatest)

    out = _post(x2, oatt.reshape(n, D_ATT), mconv, p_l.reshape(n, D_PLE), row(on_att_g), w_o.astype(BF16),
                row(ln_ffn_g), w_up.astype(BF16), ffn_conv_w.astype(F32), row(ffn_conv_b), w_down.astype(BF16),
                row(ln_ple_g), w_pg.astype(BF16), w_pe.astype(BF16), seq=seq, tm=POST_TILE)
    return out.reshape(batch, seq, D_MODEL)


def kernel(x, p, ln_mix_g, w_in, qn_g, kn_g, pe_k, pe_v, w_ck1, w_ck2, w_cv1, w_cv2, conv_w, on_att_g,
           on_conv_g, w_o, ln_ffn_g, w_up, ffn_conv_w, ffn_conv_b, w_down, ln_ple_g, w_pg, w_pe):
    h = x
    for i in range(p.shape[0]):
        h = _layer(h, p[i], ln_mix_g[i], w_in[i], qn_g[i], kn_g[i], pe_k[i], pe_v[i], w_ck1[i], w_ck2[i],
                   w_cv1[i], w_cv2[i], conv_w[i], on_att_g[i], on_conv_g[i], w_o[i], ln_ffn_g[i], w_up[i],
                   ffn_conv_w[i], ffn_conv_b[i], w_down[i], ln_ple_g[i], w_pg[i], w_pe[i])
    return h
```
